```python
import math
import jax, jax.numpy as jnp
from jax import lax
import numpy as np


D_MODEL = 1024
BATCH = 8
SEQ = 2048
DEPTH = 1
DEC_BATCH = 128
DEC_SEQ = 4
PAST_LEN = 16384
PAGE_SIZE = 128

M_HEADS = 4
M_DK = D_MODEL // 8
M_DV = D_MODEL // 8
M_WIDTH = M_HEADS * M_DV
M_CHUNK = 128
A_HEADS = 8
A_KV_HEADS = 2
A_GROUP = A_HEADS // A_KV_HEADS
A_HEAD_DIM = 64
A_WIDTH = A_HEADS * A_HEAD_DIM
A_KV_WIDTH = A_KV_HEADS * A_HEAD_DIM
WINDOW = 128
ROT_DIM = A_HEAD_DIM // 4
ROPE_THETA = 500000.0
NORM_EPS = 1e-6
IN_SIZES = (M_HEADS * M_DK, M_HEADS * M_DK, M_WIDTH, M_HEADS, M_HEADS, M_WIDTH, M_WIDTH,
            A_WIDTH, A_KV_WIDTH, A_KV_WIDTH, A_WIDTH, D_MODEL, D_MODEL)
IN_COLS = sum(IN_SIZES)

kernel_name = 'hybrid_mlstm_swa_sink_step'


def rmsnorm(x, g):
    xf = x.astype(jnp.float32)
    y = xf * lax.rsqrt(jnp.mean(xf * xf, axis=-1, keepdims=True) + NORM_EPS)
    return (y * g.astype(jnp.float32)).astype(x.dtype)


def split_in(z):
    out, start = [], 0
    for size in IN_SIZES:
        out.append(z[..., start:start + size])
        start += size
    return out


def adaln_in(x, c, w_ada, b_ada, g_norm):
    mod = jax.nn.silu(c) @ w_ada + b_ada
    shift, scale, gate = jnp.split(mod, 3, axis=-1)
    h = rmsnorm(x, g_norm) * (1 + scale[:, None, :]) + shift[:, None, :]
    return h, gate[:, None, :]


def partial_rope(x, pos):
    half = ROT_DIM // 2
    inv = ROPE_THETA ** (-jnp.arange(0, ROT_DIM, 2, dtype=jnp.float32) / ROT_DIM)
    ang = pos.astype(jnp.float32)[:, None] * inv
    cos, sin = jnp.cos(ang)[:, None, :], jnp.sin(ang)[:, None, :]
    xf = x.astype(jnp.float32)
    x1, x2 = xf[..., :half], xf[..., half:ROT_DIM]
    out = jnp.concatenate([x1 * cos - x2 * sin, x2 * cos + x1 * sin, xf[..., ROT_DIM:]], axis=-1)
    return out.astype(x.dtype)


def mlstm_chunk(carry, inp):
    C, n, m = carry
    q, k, v, ig, lf = inp
    L = q.shape[2]
    b = jnp.cumsum(lf, axis=-1)
    causal = jnp.tril(jnp.ones((L, L), dtype=bool))
    dmat = jnp.where(causal, b[..., :, None] - b[..., None, :] + ig[..., None, :], -jnp.inf)
    inter = b + m[..., None]
    m_hat = jnp.maximum(inter, jnp.max(dmat, axis=-1))
    s = jnp.einsum('bhtd,bhsd->bhts', q, k) * jnp.exp(dmat - m_hat[..., None])
    w_inter = jnp.exp(inter - m_hat)
    num = jnp.einsum('bhts,bhse->bhte', s, v) + w_inter[..., None] * jnp.einsum('bhtd,bhde->bhte', q, C)
    dot = jnp.sum(s, axis=-1) + w_inter * jnp.einsum('bhtd,bhd->bht', q, n)
    h = num / jnp.maximum(jnp.abs(dot), jnp.exp(-m_hat))[..., None]
    b_last = b[..., -1]
    d_end = b_last[..., None] - b + ig
    m_new = jnp.maximum(b_last + m, jnp.max(d_end, axis=-1))
    w_end = jnp.exp(d_end - m_new[..., None])
    decay = jnp.exp(b_last + m - m_new)
    C_new = decay[..., None, None] * C + jnp.einsum('bhs,bhsd,bhse->bhde', w_end, k, v)
    n_new = decay[..., None] * n + jnp.einsum('bhs,bhsd->bhd', w_end, k)
    return (C_new, n_new, m_new), h


def mlstm_branch(mq, mk, mv, mi, mf, mo, mz, C0, n0, m0, b_i, b_f, g_mn):
    f32 = jnp.float32
    B, T, _ = mq.shape

    def heads(t, d):
        return t.reshape(B, T, M_HEADS, d).transpose(0, 2, 1, 3).astype(f32)

    q = heads(mq, M_DK) * (M_DK ** -0.5)
    k = heads(mk, M_DK)
    v = heads(mv, M_DV)
    ig = (mi.astype(f32) + b_i.astype(f32)).transpose(0, 2, 1)
    lf = jax.nn.log_sigmoid(mf.astype(f32) + b_f.astype(f32)).transpose(0, 2, 1)
    L = M_CHUNK if T % M_CHUNK == 0 else T
    nc = T // L

    def chunks(t):
        return jnp.moveaxis(t.reshape(t.shape[:2] + (nc, L) + t.shape[3:]), 2, 0)

    (C, n, m), h = lax.scan(mlstm_chunk, (C0.astype(f32), n0.astype(f32), m0.astype(f32)),
                            (chunks(q), chunks(k), chunks(v), chunks(ig), chunks(lf)))
    h = jnp.moveaxis(h, 0, 2).reshape(B, M_HEADS, T, M_DV).transpose(0, 2, 1, 3)
    h = h * lax.rsqrt(jnp.mean(h * h, axis=-1, keepdims=True) + NORM_EPS)
    h = h.reshape(B, T, M_WIDTH) * g_mn.astype(f32)
    h = h * jax.nn.sigmoid(mo.astype(f32)) * jax.nn.silu(mz.astype(f32))
    return h.astype(mq.dtype), C, n, m


def sink_attend(q, k, v, q_pos, k_pos, sinks):
    f32 = jnp.float32
    B, N, Tq = q.shape[:3]
    qg = q.reshape(B, N, Tq, A_KV_HEADS, A_GROUP, A_HEAD_DIM).astype(f32)
    s = jnp.einsum('bnqkgd,bnskd->bnkgqs', qg, k.astype(f32)) * (A_HEAD_DIM ** -0.5)
    dist = q_pos[:, :, None] - k_pos[:, None, :]
    mask = (dist >= 0) & (dist < WINDOW) & (k_pos[:, None, :] >= 0)
    s = jnp.where(mask[None, :, None, None], s, -jnp.inf)
    sink = sinks.astype(f32).reshape(A_KV_HEADS, A_GROUP)[None, None, :, :, None, None]
    mx = jnp.maximum(jnp.max(s, axis=-1, keepdims=True), sink)
    p = jnp.exp(s - mx)
    den = jnp.sum(p, axis=-1, keepdims=True) + jnp.exp(sink - mx)
    o = jnp.einsum('bnkgqs,bnskd->bnqkgd', p / den, v.astype(f32))
    return o.reshape(B, N, Tq, A_WIDTH)


def attn_prompt(q, k, v, sinks):
    B, T = q.shape[:2]
    nb = T // WINDOW
    qb = q.reshape(B, nb, WINDOW, A_HEADS, A_HEAD_DIM)

    def with_prev(t):
        tb = t.reshape(B, nb, WINDOW, A_KV_HEADS, A_HEAD_DIM)
        prev = jnp.pad(tb, ((0, 0), (1, 0), (0, 0), (0, 0), (0, 0)))[:, :-1]
        return jnp.concatenate([prev, tb], axis=2)

    pos = jnp.arange(T, dtype=jnp.int32).reshape(nb, WINDOW)
    k_pos = jnp.concatenate([pos - WINDOW, pos], axis=1)
    o = sink_attend(qb, with_prev(k), with_prev(v), pos, k_pos, sinks)
    return o.reshape(B, T, A_WIDTH)


def layer(x, c, C0, n0, m0, k_past, v_past, w_ada, b_ada, g_norm, w_in, b_i, b_f, g_mn,
          sinks, w_mo, w_ao, w_o):
    B, T, _ = x.shape
    h, gate = adaln_in(x, c, w_ada, b_ada, g_norm)
    (mq, mk, mv, mi, mf, mo, mz, aq, ak, av, az, gm, ga) = split_in(h @ w_in)
    y_m, C, n, m = mlstm_branch(mq, mk, mv, mi, mf, mo, mz, C0, n0, m0, b_i, b_f, g_mn)
    if k_past is None:
        pos = jnp.arange(T, dtype=jnp.int32)
    else:
        pos = PAST_LEN + jnp.arange(T, dtype=jnp.int32)
    q = partial_rope(aq.reshape(B, T, A_HEADS, A_HEAD_DIM), pos)
    k = partial_rope(ak.reshape(B, T, A_KV_HEADS, A_HEAD_DIM), pos)
    v = av.reshape(B, T, A_KV_HEADS, A_HEAD_DIM)
    if k_past is None:
        o = attn_prompt(q, k, v, sinks)
        k_keep, v_keep = k[:, -WINDOW:], v[:, -WINDOW:]
    else:
        cw = k_past.shape[1]
        kk = jnp.concatenate([k_past.astype(k.dtype), k], axis=1)
        vv = jnp.concatenate([v_past.astype(v.dtype), v], axis=1)
        k_pos = jnp.concatenate([PAST_LEN - cw + jnp.arange(cw, dtype=jnp.int32), pos])
        o = sink_attend(q[:, None], kk[:, None], vv[:, None], pos[None], k_pos[None], sinks)[:, 0]
        k_keep, v_keep = kk[:, -cw:], vv[:, -cw:]
    y_a = o.astype(x.dtype) * jax.nn.silu(az)
    u = jax.nn.sigmoid(gm) * (y_m @ w_mo) + jax.nn.sigmoid(ga) * (y_a @ w_ao)
    x = x + gate * (u @ w_o)
    return x, C, n, m, k_keep, v_keep


def setup_inputs(seed: int = 0) -> dict:
    key = jax.random.key(seed)
    ks = jax.random.split(key, 24)
    f32 = jnp.float32
    cw = min(WINDOW, PAST_LEN)
    nrm = lambda k, shape, s: jax.random.normal(k, shape, f32) * s
    b_f = jnp.linspace(3.0, 6.0, M_HEADS, dtype=f32)[None, :] + nrm(ks[17], (DEPTH, M_HEADS), 0.1)
    return {
        'x_prompt': nrm(ks[0], (BATCH, SEQ, D_MODEL), 1.0),
        'x_sample': nrm(ks[1], (DEC_BATCH, DEC_SEQ, D_MODEL), 1.0),
        'state_C': nrm(ks[2], (DEPTH, DEC_BATCH, M_HEADS, M_DK, M_DV), 1.0),
        'state_n': nrm(ks[3], (DEPTH, DEC_BATCH, M_HEADS, M_DK), 1.0),
        'state_m': jax.random.uniform(ks[4], (DEPTH, DEC_BATCH, M_HEADS), f32, 0.0, 2.0),
        'cache_k': nrm(ks[5], (DEPTH, DEC_BATCH, cw, A_KV_HEADS, A_HEAD_DIM), 1.0),
        'cache_v': nrm(ks[6], (DEPTH, DEC_BATCH, cw, A_KV_HEADS, A_HEAD_DIM), 1.0),
        'c_prompt': nrm(ks[7], (BATCH, D_MODEL), 1.0),
        'c_sample': nrm(ks[8], (DEC_BATCH, D_MODEL), 1.0),
        'w_ada': nrm(ks[9], (DEPTH, D_MODEL, 3 * D_MODEL), 0.5 * D_MODEL ** -0.5),
        'b_ada': nrm(ks[10], (DEPTH, 3 * D_MODEL), 0.02),
        'g_norm': 1.0 + nrm(ks[11], (DEPTH, D_MODEL), 0.02),
        'w_in': nrm(ks[12], (DEPTH, D_MODEL, IN_COLS), D_MODEL ** -0.5),
        'b_igate': nrm(ks[13], (DEPTH, M_HEADS), 0.1) - 1.0,
        'b_fgate': b_f,
        'g_mnorm': 1.0 + nrm(ks[14], (DEPTH, M_WIDTH), 0.02),
        'sinks': nrm(ks[15], (DEPTH, A_HEADS), 1.0),
        'w_m_out': nrm(ks[16], (DEPTH, M_WIDTH, D_MODEL), M_WIDTH ** -0.5),
        'w_a_out': nrm(ks[18], (DEPTH, A_WIDTH, D_MODEL), A_WIDTH ** -0.5),
        'w_out': nrm(ks[19], (DEPTH, D_MODEL, D_MODEL), D_MODEL ** -0.5),
        'g_final': 1.0 + nrm(ks[20], (D_MODEL,), 0.02),
    }


def reference(x_prompt, x_sample, state_C, state_n, state_m, cache_k, cache_v, c_prompt, c_sample,
              w_ada, b_ada, g_norm, w_in, b_igate, b_fgate, g_mnorm, sinks, w_m_out, w_a_out,
              w_out, g_final):
    f32 = jnp.float32
    bp = x_prompt.shape[0]
    yp, ys = x_prompt, x_sample
    sp, ss = [], []
    for l in range(DEPTH):
        zC = jnp.zeros((bp, M_HEADS, M_DK, M_DV), f32)
        zn = jnp.zeros((bp, M_HEADS, M_DK), f32)
        zm = jnp.zeros((bp, M_HEADS), f32)
        yp, Cp, np_, mp, kp, vp = layer(yp, c_prompt, zC, zn, zm, None, None,
                                        w_ada[l], b_ada[l], g_norm[l], w_in[l], b_igate[l], b_fgate[l],
                                        g_mnorm[l], sinks[l], w_m_out[l], w_a_out[l], w_out[l])
        ys, Cs, ns, ms, ksm, vsm = layer(ys, c_sample, state_C[l], state_n[l], state_m[l],
                                         cache_k[l], cache_v[l],
                                         w_ada[l], b_ada[l], g_norm[l], w_in[l], b_igate[l], b_fgate[l],
                                         g_mnorm[l], sinks[l], w_m_out[l], w_a_out[l], w_out[l])
        sp.append((Cp, np_, mp, kp, vp))
        ss.append((Cs, ns, ms, ksm, vsm))
    y_prompt = rmsnorm(yp, g_final)
    y_sample = rmsnorm(ys, g_final)
    C_p = jnp.stack([s[0] for s in sp])
    n_p = jnp.stack([s[1] for s in sp])
    m_p = jnp.stack([s[2] for s in sp])
    k_p = jnp.stack([s[3] for s in sp])
    v_p = jnp.stack([s[4] for s in sp])
    C_s = jnp.stack([s[0] for s in ss])
    n_s = jnp.stack([s[1] for s in ss])
    m_s = jnp.stack([s[2] for s in ss])
    k_s = jnp.stack([s[3] for s in ss])
    v_s = jnp.stack([s[4] for s in ss])
    return (y_prompt, y_sample, C_p, n_p, m_p, k_p, v_p, C_s, n_s, m_s, k_s, v_s)
```

```python
import functools

import jax
import jax.numpy as jnp
from jax import lax
from jax.experimental import pallas as pl
from jax.experimental.pallas import tpu as pltpu

F32 = jnp.float32
BF16 = jnp.bfloat16

D_MODEL = 1024
M_HEADS = 4
M_DK = 128
M_DV = 128
M_WIDTH = M_HEADS * M_DV
M_CHUNK = 128
A_HEADS = 8
A_KV_HEADS = 2
A_GROUP = A_HEADS // A_KV_HEADS
A_HEAD_DIM = 64
A_WIDTH = A_HEADS * A_HEAD_DIM
A_KV_WIDTH = A_KV_HEADS * A_HEAD_DIM
WINDOW = 128
ROT_DIM = A_HEAD_DIM // 4
ROPE_THETA = 500000.0
NORM_EPS = 1e-6
PAST_LEN = 16384
IN_SIZES = (M_HEADS * M_DK, M_HEADS * M_DK, M_WIDTH, M_HEADS, M_HEADS, M_WIDTH, M_WIDTH,
            A_WIDTH, A_KV_WIDTH, A_KV_WIDTH, A_WIDTH, D_MODEL, D_MODEL)

LANES = 128

C_GM = 0
C_GA = C_GM + D_MODEL
C_MQ = C_GA + D_MODEL
C_MK = C_MQ + M_WIDTH
C_MV = C_MK + M_WIDTH
C_MO = C_MV + M_WIDTH
C_MZ = C_MO + M_WIDTH
C_AQ = C_MZ + M_WIDTH
C_AZ = C_AQ + A_WIDTH
C_AK = C_AZ + A_WIDTH
C_AV = C_AK + A_KV_WIDTH
C_GI = C_AV + A_KV_WIDTH
C_GF = C_GI + LANES
N_PACK = C_GF + LANES

PROMPT_TILE = 256
SAMPLE_GROUP = 8
VMEM_LIMIT = 56 * 1024 * 1024


def _sigmoid(x):
    return 1.0 / (1.0 + jnp.exp(-x))


def _silu(x):
    return x * _sigmoid(x)


def _log_sigmoid(x):
    return jnp.minimum(x, 0.0) - jnp.log1p(jnp.exp(-jnp.abs(x)))


def _dot(a, b):
    return jnp.dot(a, b, preferred_element_type=F32)


def _dot_nt(a, b):
    return lax.dot_general(a, b, (((1,), (1,)), ((), ())), preferred_element_type=F32)


def _dot_tn(a, b):
    return lax.dot_general(a, b, (((0,), (0,)), ((), ())), preferred_element_type=F32)


def _dot_exact01(m01, x):
    x1 = x.astype(BF16)
    r1 = x - x1.astype(F32)
    x2 = r1.astype(BF16)
    x3 = (r1 - x2.astype(F32)).astype(BF16)
    return _dot(m01, x1) + _dot(m01, x2) + _dot(m01, x3)


def _rms(x):
    return x * lax.rsqrt(jnp.mean(x * x, axis=-1, keepdims=True) + NORM_EPS)


def _rope(blk, ra, rb, rc):
    return blk * ra + pltpu.roll(blk, 8, 1) * rb + pltpu.roll(blk, LANES - 8, 1) * rc


def _out_stage(x, gate, sgm, sga, ym, ya, wmo_ref, wao_ref, wo_ref, gf):
    pm = _dot(ym, wmo_ref[...])
    pa = _dot(ya, wao_ref[...])
    u = sgm * pm + sga * pa
    r = _dot(u.astype(BF16), wo_ref[...])
    return _rms(x + gate * r) * gf


def _adaln_kernel(cp_ref, cs_ref, w_ref, b_ref, op_ref, os_ref):
    w = w_ref[...].astype(BF16)
    b = b_ref[...]
    op_ref[...] = _dot(_silu(cp_ref[...]).astype(BF16), w) + b
    os_ref[...] = _dot(_silu(cs_ref[...]).astype(BF16), w) + b


def _adaln(c_p, c_s, w_ada, b_ada):
    bp, d = c_p.shape
    bs = c_s.shape[0]
    n = w_ada.shape[1]
    tn = 512
    return pl.pallas_call(
        _adaln_kernel,
        grid=(n // tn,),
        in_specs=[
            pl.BlockSpec((bp, d), lambda j: (0, 0)),
            pl.BlockSpec((bs, d), lambda j: (0, 0)),
            pl.BlockSpec((d, tn), lambda j: (0, j)),
            pl.BlockSpec((1, tn), lambda j: (0, j)),
        ],
        out_specs=[
            pl.BlockSpec((bp, tn), lambda j: (0, j)),
            pl.BlockSpec((bs, tn), lambda j: (0, j)),
        ],
        out_shape=[jax.ShapeDtypeStruct((bp, n), F32), jax.ShapeDtypeStruct((bs, n), F32)],
        compiler_params=pltpu.CompilerParams(dimension_semantics=("arbitrary",)),
        name="adaln",
    )(c_p, c_s, w_ada, b_ada.reshape(1, n))


def _prompt_kernel(sinks_ref, x_ref, mod_ref, gn_ref, gf_ref, gmn_ref, bg_ref, w_ref, wmo_ref, wao_ref, wo_ref,
                   ra_ref, rb_ref, rc_ref,
                   y_ref, c_out_ref, n_out_ref, m_out_ref, kk_ref, vk_ref,
                   hb_s, q_s, k_s, vaug_s, og_s, qa_s, kf_s, vf_s, sz_s, sgm_s, sga_s, g_s,
                   kbuf_s, vbuf_s, c_s, m_s, ym_s, ya_s, *, tile):
    c = pl.program_id(1)
    n_chunks = tile // M_CHUNK
    L = M_CHUNK

    @pl.when(c == 0)
    def _init():
        c_s[...] = jnp.zeros_like(c_s)
        m_s[...] = jnp.zeros_like(m_s)
        kbuf_s[...] = jnp.zeros_like(kbuf_s)
        vbuf_s[...] = jnp.zeros_like(vbuf_s)
        lane = lax.broadcasted_iota(jnp.int32, (tile, LANES), 1)
        ones_col = jnp.where(lane == 0, 1.0, 0.0).astype(BF16)
        for h in range(M_HEADS):
            vaug_s[h, :, LANES:2 * LANES] = ones_col

    x = x_ref[0]
    mod = mod_ref[0]
    shift = mod[:, 0:D_MODEL]
    scale = mod[:, D_MODEL:2 * D_MODEL]
    gate = mod[:, 2 * D_MODEL:3 * D_MODEL]
    h = _rms(x) * gn_ref[...]
    h = h * (1.0 + scale) + shift
    hb_s[...] = h.astype(BF16)

    def proj(lo, n):
        return _dot(hb_s[...], w_ref[:, lo:lo + n])

    sgm_s[...] = _sigmoid(proj(C_GM, D_MODEL))
    sga_s[...] = _sigmoid(proj(C_GA, D_MODEL))
    q_s[...] = (proj(C_MQ, M_WIDTH) * (M_DK ** -0.5)).astype(BF16)
    k_s[...] = proj(C_MK, M_WIDTH)
    zv = proj(C_MV, M_WIDTH)
    for hd in range(M_HEADS):
        vaug_s[hd, :, 0:LANES] = zv[:, hd * M_DV:(hd + 1) * M_DV].astype(BF16)
    og_s[...] = _sigmoid(proj(C_MO, M_WIDTH)) * _silu(proj(C_MZ, M_WIDTH))
    ra = ra_ref[...]
    rb = rb_ref[...]
    rc = rc_ref[...]
    za = proj(C_AQ, A_WIDTH)
    for j in range(A_WIDTH // LANES):
        blk = _rope(za[:, j * LANES:(j + 1) * LANES], ra, rb, rc)
        qa_s[:, j * LANES:(j + 1) * LANES] = (blk * (A_HEAD_DIM ** -0.5)).astype(BF16)
    sz_s[...] = _silu(proj(C_AZ, A_WIDTH))
    kf_s[...] = _rope(proj(C_AK, A_KV_WIDTH), ra, rb, rc)
    vf_s[...] = proj(C_AV, A_KV_WIDTH)
    g_s[...] = proj(C_GI, 2 * LANES) + bg_ref[...]

    row = lax.broadcasted_iota(jnp.int32, (L, L), 0)
    col = lax.broadcasted_iota(jnp.int32, (L, L), 1)
    causal = row >= col
    tril = jnp.where(causal, 1.0, 0.0).astype(BF16)
    lo_half = col < A_HEAD_DIM
    row2 = lax.broadcasted_iota(jnp.int32, (L, 2 * L), 0)
    col2 = lax.broadcasted_iota(jnp.int32, (L, 2 * L), 1)
    no_prev = jnp.where(c > 0, 0, 4 * L)

    for j in range(n_chunks):
        rs = slice(j * L, (j + 1) * L)

        g = g_s[rs, :]
        ig_all = g[:, 0:LANES]
        lf_all = _log_sigmoid(g[:, LANES:2 * LANES])
        b_all = _dot_exact01(tril, lf_all)
        b_last = b_all[L - 1:L, :]
        m_old = m_s[...]
        d_end = b_last - b_all + ig_all
        m_new = jnp.maximum(b_last + m_old, jnp.max(d_end, axis=0, keepdims=True))
        w_end = jnp.exp(d_end - m_new)
        decay = jnp.exp(b_last + m_old - m_new)
        inter = b_all + m_old
        r_t = jnp.transpose(ig_all - b_all)
        for hd in range(M_HEADS):
            hs = slice(hd * M_DK, (hd + 1) * M_DK)
            dmat = jnp.where(causal, b_all[:, hd:hd + 1] + r_t[hd:hd + 1, :], -jnp.inf)
            inter_h = inter[:, hd:hd + 1]
            m_hat = jnp.maximum(inter_h, jnp.max(dmat, axis=-1, keepdims=True))
            qh = q_s[rs, hs]
            kh = k_s[rs, hs]
            s = _dot_nt(qh, kh.astype(BF16)) * jnp.exp(dmat - m_hat)
            w_inter = jnp.exp(inter_h - m_hat)
            va = vaug_s[hd, rs, :]
            c_old = c_s[hd]
            tot = _dot(s.astype(BF16), va) + w_inter * _dot(qh, c_old.astype(BF16))
            num = tot[:, 0:M_DV]
            den = tot[:, M_DV:M_DV + 1]
            hh = num / jnp.maximum(jnp.abs(den), jnp.exp(-m_hat))
            ym_s[rs, hs] = (_rms(hh) * gmn_ref[:, hs] * og_s[rs, hs]).astype(BF16)
            kw = (kh * w_end[:, hd:hd + 1]).astype(BF16)
            c_s[hd] = decay[:, hd:hd + 1] * c_old + _dot_tn(kw, va)
        m_s[...] = m_new

        slot = j % 2
        srows = slice(slot * L, (slot + 1) * L)
        kc = kf_s[rs, :]
        vc = vf_s[rs, :]
        kr = pltpu.roll(kc, A_HEAD_DIM, 1)
        vr = pltpu.roll(vc, A_HEAD_DIM, 1)
        kbuf_s[0, srows, :] = jnp.where(lo_half, kc, 0.0).astype(BF16)
        kbuf_s[1, srows, :] = jnp.where(lo_half, 0.0, kr).astype(BF16)
        kbuf_s[2, srows, :] = jnp.where(lo_half, kr, 0.0).astype(BF16)
        kbuf_s[3, srows, :] = jnp.where(lo_half, 0.0, kc).astype(BF16)
        vbuf_s[0, srows, :] = jnp.where(lo_half, vc, 0.0).astype(BF16)
        vbuf_s[1, srows, :] = jnp.where(lo_half, 0.0, vr).astype(BF16)
        vbuf_s[2, srows, :] = jnp.where(lo_half, vr, 0.0).astype(BF16)
        vbuf_s[3, srows, :] = jnp.where(lo_half, 0.0, vc).astype(BF16)
        if slot == 0:
            cur = (col2 < L) & (col2 <= row2)
            prv = (col2 >= L) & (col2 - L > row2 + (no_prev if j == 0 else 0))
        else:
            cur = (col2 >= L) & (col2 - L <= row2)
            prv = (col2 < L) & (col2 > row2 + (no_prev if j == 0 else 0))
        mask = cur | prv
        for kv in range(A_KV_HEADS):
            qcat = jnp.concatenate([qa_s[rs, (2 * kv) * LANES:(2 * kv + 1) * LANES],
                                    qa_s[rs, (2 * kv + 1) * LANES:(2 * kv + 2) * LANES]], axis=0)
            sc = [_dot_nt(qcat, kbuf_s[2 * kv + half]) for half in range(2)]
            for bi in range(2):
                o = None
                for half in range(2):
                    head = A_GROUP * kv + 2 * bi + half
                    sh = jnp.where(mask, sc[half][bi * L:(bi + 1) * L, :], -jnp.inf)
                    sink = sinks_ref[head]
                    mx = jnp.maximum(jnp.max(sh, axis=-1, keepdims=True), sink)
                    p = jnp.exp(sh - mx)
                    den = jnp.sum(p, axis=-1, keepdims=True) + jnp.exp(sink - mx)
                    pv = _dot((p / den).astype(BF16), vbuf_s[2 * kv + half])
                    o = pv if o is None else o + pv
                cs = slice((2 * kv + bi) * LANES, (2 * kv + bi + 1) * LANES)
                ya_s[rs, cs] = (o * sz_s[rs, cs]).astype(BF16)

    y_ref[0] = _out_stage(x_ref[0], gate, sgm_s[...], sga_s[...], ym_s[...], ya_s[...],
                          wmo_ref, wao_ref, wo_ref, gf_ref[...])

    @pl.when(c == pl.num_programs(1) - 1)
    def _final():
        for hd in range(M_HEADS):
            cf = c_s[hd]
            c_out_ref[0, hd] = cf[:, 0:M_DV]
            n_out_ref[0, hd:hd + 1, :] = jnp.transpose(cf[:, M_DV:2 * M_DV])[0:1, :]
        m_out_ref[0] = m_s[...]
        kk_ref[0] = kf_s[tile - WINDOW:tile, :]
        vk_ref[0] = vf_s[tile - WINDOW:tile, :]


def _rope_tables(pos):
    half = ROT_DIM // 2
    inv = ROPE_THETA ** (-jnp.arange(0, ROT_DIM, 2, dtype=F32) / ROT_DIM)
    ang = pos.astype(F32)[:, None] * inv
    cos, sin = jnp.cos(ang), jnp.sin(ang)
    t = pos.shape[0]
    zh = jnp.zeros((t, half), F32)
    rest = A_HEAD_DIM - ROT_DIM
    a = jnp.concatenate([cos, cos, jnp.ones((t, rest), F32)], axis=1)
    b = jnp.concatenate([zh, sin, jnp.zeros((t, rest), F32)], axis=1)
    cc = jnp.concatenate([-sin, zh, jnp.zeros((t, rest), F32)], axis=1)
    two = lambda v: jnp.concatenate([v, v], axis=1)
    return two(a), two(b), two(cc)


def _full(shape):
    return pl.BlockSpec(shape, lambda *_: (0,) * len(shape))


def _prompt_layer(x, mod, sinks, gn, gf, gmn, bias_g, wp, wmo, wao, wo):
    bsz, seq, d = x.shape
    tile = PROMPT_TILE
    assert seq % tile == 0 and tile % (2 * M_CHUNK) == 0 and d == D_MODEL
    ra, rb, rc = _rope_tables(jnp.arange(seq, dtype=jnp.int32))
    rope_spec = pl.BlockSpec((tile, LANES), lambda b, c: (c, 0))
    in_specs = [
        pl.BlockSpec(memory_space=pltpu.SMEM),
        pl.BlockSpec((1, tile, d), lambda b, c: (b, c, 0)),
        pl.BlockSpec((1, 1, 3 * d), lambda b, c: (b, 0, 0)),
        _full((1, d)), _full((1, d)), _full((1, M_WIDTH)), _full((1, 2 * LANES)),
        _full((d, N_PACK)), _full((M_WIDTH, d)), _full((A_WIDTH, d)), _full((d, d)),
        rope_spec, rope_spec, rope_spec,
    ]
    out_specs = [
        pl.BlockSpec((1, tile, d), lambda b, c: (b, c, 0)),
        pl.BlockSpec((1, M_HEADS, M_DK, M_DV), lambda b, c: (b, 0, 0, 0)),
        pl.BlockSpec((1, M_HEADS, M_DK), lambda b, c: (b, 0, 0)),
        pl.BlockSpec((1, 1, LANES), lambda b, c: (b, 0, 0)),
        pl.BlockSpec((1, WINDOW, A_KV_WIDTH), lambda b, c: (b, 0, 0)),
        pl.BlockSpec((1, WINDOW, A_KV_WIDTH), lambda b, c: (b, 0, 0)),
    ]
    out_shape = [
        jax.ShapeDtypeStruct((bsz, seq, d), F32),
        jax.ShapeDtypeStruct((bsz, M_HEADS, M_DK, M_DV), F32),
        jax.ShapeDtypeStruct((bsz, M_HEADS, M_DK), F32),
        jax.ShapeDtypeStruct((bsz, 1, LANES), F32),
        jax.ShapeDtypeStruct((bsz, WINDOW, A_KV_WIDTH), F32),
        jax.ShapeDtypeStruct((bsz, WINDOW, A_KV_WIDTH), F32),
    ]
    scratch = [
        pltpu.VMEM((tile, d), BF16),
        pltpu.VMEM((tile, M_WIDTH), BF16),
        pltpu.VMEM((tile, M_WIDTH), F32),
        pltpu.VMEM((M_HEADS, tile, 2 * LANES), BF16),
        pltpu.VMEM((tile, M_WIDTH), F32),
        pltpu.VMEM((tile, A_WIDTH), BF16),
        pltpu.VMEM((tile, A_KV_WIDTH), F32),
        pltpu.VMEM((tile, A_KV_WIDTH), F32),
        pltpu.VMEM((tile, A_WIDTH), F32),
        pltpu.VMEM((tile, d), F32),
        pltpu.VMEM((tile, d), F32),
        pltpu.VMEM((tile, 2 * LANES), F32),
        pltpu.VMEM((2 * A_KV_HEADS, 2 * WINDOW, LANES), BF16),
        pltpu.VMEM((2 * A_KV_HEADS, 2 * WINDOW, LANES), BF16),
        pltpu.VMEM((M_HEADS, M_DK, 2 * LANES), F32),
        pltpu.VMEM((1, LANES), F32),
        pltpu.VMEM((tile, M_WIDTH), BF16),
        pltpu.VMEM((tile, A_WIDTH), BF16),
    ]
    return pl.pallas_call(
        functools.partial(_prompt_kernel, tile=tile),
        grid=(bsz, seq // tile),
        in_specs=in_specs,
        out_specs=out_specs,
        out_shape=out_shape,
        scratch_shapes=scratch,
        compiler_params=pltpu.CompilerParams(dimension_semantics=("arbitrary", "arbitrary"),
                                             vmem_limit_bytes=VMEM_LIMIT),
        name="prompt_layer",
    )(sinks, x, mod.reshape(bsz, 1, 3 * d), gn, gf, gmn, bias_g, wp, wmo, wao, wo, ra, rb, rc)


def _sample_proj_kernel(x_ref, mod_ref, gn_ref, w_ref, z_ref, hb_s):
    @pl.when(pl.program_id(0) == 0)
    def _norm():
        mod = mod_ref[...]
        h = _rms(x_ref[...]) * gn_ref[...]
        hb_s[...] = (h * (1.0 + mod[:, D_MODEL:2 * D_MODEL]) + mod[:, 0:D_MODEL]).astype(BF16)

    z_ref[...] = _dot(hb_s[...], w_ref[...])


def _sample_proj(x2, mod_rows, gn, wp):
    rows, d = x2.shape
    tn = 512
    return pl.pallas_call(
        _sample_proj_kernel,
        grid=(N_PACK // tn,),
        in_specs=[
            _full((rows, d)),
            pl.BlockSpec((rows, 2 * d), lambda j: (0, 0)),
            _full((1, d)),
            pl.BlockSpec((d, tn), lambda j: (0, j)),
        ],
        out_specs=pl.BlockSpec((rows, tn), lambda j: (0, j)),
        out_shape=jax.ShapeDtypeStruct((rows, N_PACK), F32),
        scratch_shapes=[pltpu.VMEM((rows, d), BF16)],
        compiler_params=pltpu.CompilerParams(dimension_semantics=("arbitrary",)),
        name="sample_proj",
    )(x2, mod_rows, gn, wp)


def _sample_state_kernel(z_ref, c_ref, n_ref, m_ref, ck_ref, cv_ref, ra_ref, rb_ref, rc_ref, bg_ref, gmn_ref,
                         sink_ref,
                         cn_ref, nn_ref, mn_ref, ym_ref, ya_ref, kn_ref, vn_ref, *, group, steps):
    T = steps
    R = group * T

    def z(lo, n):
        return z_ref[:, lo:lo + n]

    q_all = z(C_MQ, M_WIDTH) * (M_DK ** -0.5)
    k_all = z(C_MK, M_WIDTH)
    v_all = z(C_MV, M_WIDTH)
    og_all = _sigmoid(z(C_MO, M_WIDTH)) * _silu(z(C_MZ, M_WIDTH))
    ra = ra_ref[...]
    rb = rb_ref[...]
    rc = rc_ref[...]
    qa_blocks = [_rope(z(C_AQ + j * LANES, LANES), ra, rb, rc) * (A_HEAD_DIM ** -0.5)
                 for j in range(A_WIDTH // LANES)]
    kn_all = _rope(z(C_AK, A_KV_WIDTH), ra, rb, rc)
    vn_all = z(C_AV, A_KV_WIDTH)
    sz_all = _silu(z(C_AZ, A_WIDTH))
    g = z(C_GI, 2 * LANES) + bg_ref[...]
    ig_all = g[:, 0:LANES]
    lf_all = _log_sigmoid(g[:, LANES:2 * LANES])
    kn_ref[...] = kn_all
    vn_ref[...] = vn_all

    row_t = lax.broadcasted_iota(jnp.int32, (T, T), 0)
    col_t = lax.broadcasted_iota(jnp.int32, (T, T), 1)
    causal = row_t >= col_t
    lane_t = lax.broadcasted_iota(jnp.int32, (T, LANES), 1)
    rows_a = A_HEADS * T
    tok_c = lax.broadcasted_iota(jnp.int32, (rows_a, WINDOW), 0) % T
    key_c = lax.broadcasted_iota(jnp.int32, (rows_a, WINDOW), 1)
    mask_c = key_c > tok_c
    tok_n = lax.broadcasted_iota(jnp.int32, (rows_a, T), 0) % T
    key_n = lax.broadcasted_iota(jnp.int32, (rows_a, T), 1)
    mask_n = key_n <= tok_n
    sink = sink_ref[...][:, 0:1]
    gmn = gmn_ref[...]

    for b in range(group):
        rs = slice(b * T, (b + 1) * T)

        ig = ig_all[rs, 0:M_HEADS]
        lf = lf_all[rs, 0:M_HEADS]
        cum = [lf[0:1, :]]
        for t in range(1, T):
            cum.append(cum[-1] + lf[t:t + 1, :])
        b_all = jnp.concatenate(cum, axis=0)
        b_last = cum[-1]
        m_old = m_ref[b:b + 1, :]
        d_end = b_last - b_all + ig
        m_new = jnp.maximum(b_last + m_old, jnp.max(d_end, axis=0, keepdims=True))
        w_end = jnp.exp(d_end - m_new)
        decay = jnp.exp(b_last + m_old - m_new)
        inter = b_all + m_old
        r_t = jnp.transpose(ig - b_all)
        mn_ref[b:b + 1, :] = m_new
        for hd in range(M_HEADS):
            hs = slice(hd * M_DK, (hd + 1) * M_DK)
            dmat = jnp.where(causal, b_all[:, hd:hd + 1] + r_t[hd:hd + 1, :], -jnp.inf)
            inter_h = inter[:, hd:hd + 1]
            m_hat = jnp.maximum(inter_h, jnp.max(dmat, axis=-1, keepdims=True))
            qh = q_all[rs, hs]
            kh = k_all[rs, hs]
            vh = v_all[rs, hs].astype(BF16)
            qb = qh.astype(BF16)
            s = _dot_nt(qb, kh.astype(BF16)) * jnp.exp(dmat - m_hat)
            w_inter = jnp.exp(inter_h - m_hat)
            c_old = c_ref[b, hd]
            n_old = n_ref[b:b + 1, hs]
            num = _dot(s.astype(BF16), vh) + w_inter * _dot(qb, c_old.astype(BF16))
            den = jnp.sum(s, axis=-1, keepdims=True) + w_inter * jnp.sum(qh * n_old, axis=-1, keepdims=True)
            hh = num / jnp.maximum(jnp.abs(den), jnp.exp(-m_hat))
            ym_ref[rs, hs] = _rms(hh) * gmn[:, hs] * og_all[rs, hs]
            kw = kh * w_end[:, hd:hd + 1]
            dec = decay[:, hd:hd + 1]
            cn_ref[b, hd] = dec * c_old + _dot_tn(kw.astype(BF16), vh)
            nn_ref[b:b + 1, hs] = dec * n_old + jnp.sum(kw, axis=0, keepdims=True)

        qrows = []
        for head in range(A_HEADS):
            kv = head // A_GROUP
            blk = qa_blocks[head // 2][rs, :]
            if head % 2 != kv:
                blk = pltpu.roll(blk, A_HEAD_DIM, 1)
            keep = (lane_t < A_HEAD_DIM) if kv == 0 else (lane_t >= A_HEAD_DIM)
            qrows.append(jnp.where(keep, blk, 0.0))
        qs = jnp.concatenate(qrows, axis=0).astype(BF16)
        kn = kn_all[rs, :].astype(BF16)
        vn = vn_all[rs, :].astype(BF16)
        sc = jnp.where(mask_c, _dot_nt(qs, ck_ref[b].astype(BF16)), -jnp.inf)
        sn = jnp.where(mask_n, _dot_nt(qs, kn), -jnp.inf)
        mx = jnp.maximum(jnp.maximum(jnp.max(sc, axis=-1, keepdims=True), jnp.max(sn, axis=-1, keepdims=True)), sink)
        pc = jnp.exp(sc - mx)
        pn = jnp.exp(sn - mx)
        den = jnp.sum(pc, axis=-1, keepdims=True) + jnp.sum(pn, axis=-1, keepdims=True) + jnp.exp(sink - mx)
        o = _dot((pc / den).astype(BF16), cv_ref[b].astype(BF16)) + _dot((pn / den).astype(BF16), vn)
        for j in range(A_WIDTH // LANES):
            pieces = []
            for half in range(2):
                head = 2 * j + half
                piece = o[head * T:(head + 1) * T, :]
                if half != head // A_GROUP:
                    piece = pltpu.roll(piece, A_HEAD_DIM, 1)
                pieces.append(piece)
            cs = slice(j * LANES, (j + 1) * LANES)
            ya_ref[rs, cs] = jnp.where(lane_t < A_HEAD_DIM, pieces[0], pieces[1]) * sz_all[rs, cs]


def _sample_state(zs, state_c, state_n2, state_m, cache_k, cache_v, ra, rb, rc, bias_g, gmn, sink_rows, steps):
    bsz = state_c.shape[0]
    group = SAMPLE_GROUP
    assert bsz % group == 0
    rows = group * steps
    row_spec = lambda n: pl.BlockSpec((rows, n), lambda i: (i, 0))
    in_specs = [
        row_spec(N_PACK),
        pl.BlockSpec((group, M_HEADS, M_DK, M_DV), lambda i: (i, 0, 0, 0)),
        pl.BlockSpec((group, M_WIDTH), lambda i: (i, 0)),
        pl.BlockSpec((group, M_HEADS), lambda i: (i, 0)),
        pl.BlockSpec((group, WINDOW, A_KV_WIDTH), lambda i: (i, 0, 0)),
        pl.BlockSpec((group, WINDOW, A_KV_WIDTH), lambda i: (i, 0, 0)),
        _full((rows, LANES)), _full((rows, LANES)), _full((rows, LANES)),
        _full((1, 2 * LANES)), _full((1, M_WIDTH)), _full((A_HEADS * steps, LANES)),
    ]
    out_specs = [
        pl.BlockSpec((group, M_HEADS, M_DK, M_DV), lambda i: (i, 0, 0, 0)),
        pl.BlockSpec((group, M_WIDTH), lambda i: (i, 0)),
        pl.BlockSpec((group, M_HEADS), lambda i: (i, 0)),
        row_spec(M_WIDTH), row_spec(A_WIDTH), row_spec(A_KV_WIDTH), row_spec(A_KV_WIDTH),
    ]
    out_shape = [
        jax.ShapeDtypeStruct((bsz, M_HEADS, M_DK, M_DV), F32),
        jax.ShapeDtypeStruct((bsz, M_WIDTH), F32),
        jax.ShapeDtypeStruct((bsz, M_HEADS), F32),
        jax.ShapeDtypeStruct((bsz * steps, M_WIDTH), F32),
        jax.ShapeDtypeStruct((bsz * steps, A_WIDTH), F32),
        jax.ShapeDtypeStruct((bsz * steps, A_KV_WIDTH), F32),
        jax.ShapeDtypeStruct((bsz * steps, A_KV_WIDTH), F32),
    ]
    return pl.pallas_call(
        functools.partial(_sample_state_kernel, group=group, steps=steps),
        grid=(bsz // group,),
        in_specs=in_specs,
        out_specs=out_specs,
        out_shape=out_shape,
        compiler_params=pltpu.CompilerParams(dimension_semantics=("arbitrary",), vmem_limit_bytes=VMEM_LIMIT),
        name="sample_state",
    )(zs, state_c, state_n2, state_m, cache_k, cache_v, ra, rb, rc, bias_g, gmn, sink_rows)


def _sample_out_kernel(x_ref, gate_ref, z_ref, ym_ref, ya_ref, wmo_ref, wao_ref, wo_ref, gf_ref, y_ref):
    zz = z_ref[...]
    y_ref[...] = _out_stage(x_ref[...], gate_ref[...], _sigmoid(zz[:, 0:D_MODEL]), _sigmoid(zz[:, D_MODEL:2 * D_MODEL]),
                            ym_ref[...].astype(BF16), ya_ref[...].astype(BF16), wmo_ref, wao_ref, wo_ref, gf_ref[...])


def _sample_out(x2, mod_rows, zs, ym, ya, wmo, wao, wo, gf):
    rows, d = x2.shape
    return pl.pallas_call(
        _sample_out_kernel,
        grid=(1,),
        in_specs=[
            _full((rows, d)),
            pl.BlockSpec((rows, d), lambda i: (0, 2)),
            pl.BlockSpec((rows, 2 * d), lambda i: (0, 0)),
            _full((rows, M_WIDTH)), _full((rows, A_WIDTH)),
            _full((M_WIDTH, d)), _full((A_WIDTH, d)), _full((d, d)), _full((1, d)),
        ],
        out_specs=_full((rows, d)),
        out_shape=jax.ShapeDtypeStruct((rows, d), F32),
        compiler_params=pltpu.CompilerParams(dimension_semantics=("arbitrary",), vmem_limit_bytes=VMEM_LIMIT),
        name="sample_out",
    )(x2, mod_rows, zs, ym, ya, wmo, wao, wo, gf)


def _pack_w_in(w_in):
    parts, start = [], 0
    for size in IN_SIZES:
        parts.append(w_in[:, start:start + size])
        start += size
    mq, mk, mv, mi, mf, mo, mz, aq, ak, av, az, gm, ga = parts
    pad = jnp.zeros((w_in.shape[0], LANES - M_HEADS), w_in.dtype)
    return jnp.concatenate([gm, ga, mq, mk, mv, mo, mz, aq, az, ak, av, mi, pad, mf, pad], axis=1).astype(BF16)


def kernel(x_prompt, x_sample, state_C, state_n, state_m, cache_k, cache_v, c_prompt, c_sample, w_ada, b_ada, g_norm, w_in, b_igate, b_fgate, g_mnorm, sinks, w_m_out, w_a_out, w_out, g_final):
    assert w_in.shape[0] == 1, "single-layer step"
    bsz_s, steps, d = x_sample.shape
    cw = cache_k.shape[2]
    assert cw == WINDOW

    mod_p, mod_s = _adaln(c_prompt, c_sample, w_ada[0], b_ada[0])
    wp = _pack_w_in(w_in[0])
    wmo = w_m_out[0].astype(BF16)
    wao = w_a_out[0].astype(BF16)
    wo = w_out[0].astype(BF16)
    gn = g_norm[0].reshape(1, d)
    gf = g_final.reshape(1, d)
    gmn = g_mnorm[0].reshape(1, M_WIDTH)
    zpad = jnp.zeros((LANES - M_HEADS,), F32)
    bias_g = jnp.concatenate([b_igate[0], zpad, b_fgate[0], zpad]).reshape(1, 2 * LANES)
    sk = sinks[0]

    y_p, c_p, n_p, m_p, k_p, v_p = _prompt_layer(x_prompt, mod_p, sk, gn, gf, gmn, bias_g, wp, wmo, wao, wo)
    bsz_p = x_prompt.shape[0]
    m_p = m_p[:, 0, 0:M_HEADS]
    k_p = k_p.reshape(bsz_p, WINDOW, A_KV_HEADS, A_HEAD_DIM)
    v_p = v_p.reshape(bsz_p, WINDOW, A_KV_HEADS, A_HEAD_DIM)

    rows = bsz_s * steps
    x2 = x_sample.reshape(rows, d)
    mod_rows = jnp.repeat(mod_s, steps, axis=0)
    zs = _sample_proj(x2, mod_rows, gn, wp)
    ra, rb, rc = _rope_tables(PAST_LEN + jnp.arange(steps, dtype=jnp.int32))
    tile_rows = lambda t: jnp.tile(t, (SAMPLE_GROUP, 1))
    sink_rows = jnp.broadcast_to(jnp.repeat(sk, steps)[:, None], (A_HEADS * steps, LANES))
    ck = cache_k[0].reshape(bsz_s, cw, A_KV_WIDTH)
    cv = cache_v[0].reshape(bsz_s, cw, A_KV_WIDTH)
    c_s, n_s, m_s, ym, ya, k_new, v_new = _sample_state(
        zs, state_C[0], state_n[0].reshape(bsz_s, M_WIDTH), state_m[0], ck, cv,
        tile_rows(ra), tile_rows(rb), tile_rows(rc), bias_g, gmn, sink_rows, steps)
    y_s = _sample_out(x2, mod_rows, zs, ym, ya, wmo, wao, wo, gf).reshape(bsz_s, steps, d)
    k_s = jnp.concatenate([ck[:, steps:], k_new.reshape(bsz_s, steps, A_KV_WIDTH)], axis=1)
    v_s = jnp.concatenate([cv[:, steps:], v_new.reshape(bsz_s, steps, A_KV_WIDTH)], axis=1)
    k_s = k_s.reshape(bsz_s, cw, A_KV_HEADS, A_HEAD_DIM)
    v_s = v_s.reshape(bsz_s, cw, A_KV_HEADS, A_HEAD_DIM)

    return (y_p, y_s, c_p[None], n_p[None], m_p[None], k_p[None], v_p[None],
            c_s[None], n_s.reshape(bsz_s, M_HEADS, M_DK)[None], m_s[None], k_s[None], v_s[None])
```

```python
import functools

import jax
import jax.numpy as jnp
from jax import lax
from jax.experimental import pallas as pl
from jax.experimental.pallas import tpu as pltpu

F32 = jnp.float32
BF16 = jnp.bfloat16

D_MODEL = 1024
M_HEADS = 4
M_DK = 128
M_DV = 128
M_WIDTH = M_HEADS * M_DV
M_CHUNK = 128
A_HEADS = 8
A_KV_HEADS = 2
A_GROUP = A_HEADS // A_KV_HEADS
A_HEAD_DIM = 64
A_WIDTH = A_HEADS * A_HEAD_DIM
A_KV_WIDTH = A_KV_HEADS * A_HEAD_DIM
WINDOW = 128
ROT_DIM = A_HEAD_DIM // 4
ROPE_THETA = 500000.0
NORM_EPS = 1e-6
PAST_LEN = 16384
IN_SIZES = (M_HEADS * M_DK, M_HEADS * M_DK, M_WIDTH, M_HEADS, M_HEADS, M_WIDTH, M_WIDTH,
            A_WIDTH, A_KV_WIDTH, A_KV_WIDTH, A_WIDTH, D_MODEL, D_MODEL)

LANES = 128

C_GM = 0
C_GA = C_GM + D_MODEL
C_MQ = C_GA + D_MODEL
C_MK = C_MQ + M_WIDTH
C_MV = C_MK + M_WIDTH
C_MO = C_MV + M_WIDTH
C_MZ = C_MO + M_WIDTH
C_AQ = C_MZ + M_WIDTH
C_AZ = C_AQ + A_WIDTH
C_AK = C_AZ + A_WIDTH
C_AV = C_AK + A_KV_WIDTH
C_GI = C_AV + A_KV_WIDTH
C_GF = C_GI + LANES
N_PACK = C_GF + LANES

PROMPT_TILE = 256
SAMPLE_GROUP = 8
VMEM_LIMIT = 56 * 1024 * 1024


def _sigmoid(x):
    return 1.0 / (1.0 + jnp.exp(-x))


def _silu(x):
    return x * _sigmoid(x)


def _log_sigmoid(x):
    return jnp.minimum(x, 0.0) - jnp.log1p(jnp.exp(-jnp.abs(x)))


def _dot(a, b):
    return jnp.dot(a, b, preferred_element_type=F32)


def _dot_nt(a, b):
    return lax.dot_general(a, b, (((1,), (1,)), ((), ())), preferred_element_type=F32)


def _dot_tn(a, b):
    return lax.dot_general(a, b, (((0,), (0,)), ((), ())), preferred_element_type=F32)


def _dot_exact01(m01, x):
    x1 = x.astype(BF16)
    r1 = x - x1.astype(F32)
    x2 = r1.astype(BF16)
    x3 = (r1 - x2.astype(F32)).astype(BF16)
    return _dot(m01, x1) + _dot(m01, x2) + _dot(m01, x3)


def _rms(x):
    return x * lax.rsqrt(jnp.mean(x * x, axis=-1, keepdims=True) + NORM_EPS)


def _rope(blk, ra, rb, rc):
    return blk * ra + pltpu.roll(blk, 8, 1) * rb + pltpu.roll(blk, LANES - 8, 1) * rc


def _out_stage(x, gate, sgm, sga, ym, ya, wmo_ref, wao_ref, wo_ref, gf):
    pm = _dot(ym, wmo_ref[...])
    pa = _dot(ya, wao_ref[...])
    u = sgm * pm + sga * pa
    r = _dot(u.astype(BF16), wo_ref[...])
    return _rms(x + gate * r) * gf


def _adaln_kernel(cp_ref, cs_ref, w_ref, b_ref, op_ref, os_ref):
    w = w_ref[...].astype(BF16)
    b = b_ref[...]
    op_ref[...] = _dot(_silu(cp_ref[...]).astype(BF16), w) + b
    os_ref[...] = _dot(_silu(cs_ref[...]).astype(BF16), w) + b


def _adaln(c_p, c_s, w_ada, b_ada):
    bp, d = c_p.shape
    bs = c_s.shape[0]
    n = w_ada.shape[1]
    tn = 512
    return pl.pallas_call(
        _adaln_kernel,
        grid=(n // tn,),
        in_specs=[
            pl.BlockSpec((bp, d), lambda j: (0, 0)),
            pl.BlockSpec((bs, d), lambda j: (0, 0)),
            pl.BlockSpec((d, tn), lambda j: (0, j)),
            pl.BlockSpec((1, tn), lambda j: (0, j)),
        ],
        out_specs=[
            pl.BlockSpec((bp, tn), lambda j: (0, j)),
            pl.BlockSpec((bs, tn), lambda j: (0, j)),
        ],
        out_shape=[jax.ShapeDtypeStruct((bp, n), F32), jax.ShapeDtypeStruct((bs, n), F32)],
        compiler_params=pltpu.CompilerParams(dimension_semantics=("arbitrary",)),
        name="adaln",
    )(c_p, c_s, w_ada, b_ada.reshape(1, n))


_SET_FIELDS = ("hb", "q", "k", "vaug", "og", "qa", "kf", "vf", "sz", "sgm", "sga", "g")


def _set_shapes(tile):
    return [
        pltpu.VMEM((tile, D_MODEL), BF16),
        pltpu.VMEM((tile, M_WIDTH), BF16),
        pltpu.VMEM((tile, M_WIDTH), F32),
        pltpu.VMEM((M_HEADS, tile, 2 * LANES), BF16),
        pltpu.VMEM((tile, M_WIDTH), F32),
        pltpu.VMEM((tile, A_WIDTH), BF16),
        pltpu.VMEM((tile, A_KV_WIDTH), F32),
        pltpu.VMEM((tile, A_KV_WIDTH), F32),
        pltpu.VMEM((tile, A_WIDTH), F32),
        pltpu.VMEM((tile, D_MODEL), F32),
        pltpu.VMEM((tile, D_MODEL), F32),
        pltpu.VMEM((tile, 2 * LANES), F32),
    ]


def _project_pieces(x_ref, mod_ref, rope_refs, st, gn_ref, bg_ref, w_ref):
    def norm():
        mod = mod_ref[0]
        h = _rms(x_ref[0]) * gn_ref[...]
        st["hb"][...] = (h * (1.0 + mod[:, D_MODEL:2 * D_MODEL]) + mod[:, 0:D_MODEL]).astype(BF16)

    def proj(lo, n):
        return _dot(st["hb"][...], w_ref[:, lo:lo + n])

    def rope(v):
        ra_ref, rb_ref, rc_ref = rope_refs
        return _rope(v, ra_ref[...], rb_ref[...], rc_ref[...])

    def gates_q():
        st["g"][...] = proj(C_GI, 2 * LANES) + bg_ref[...]
        st["q"][...] = (proj(C_MQ, M_WIDTH) * (M_DK ** -0.5)).astype(BF16)

    def keys():
        st["k"][...] = proj(C_MK, M_WIDTH)

    def values():
        zv = proj(C_MV, M_WIDTH)
        for hd in range(M_HEADS):
            st["vaug"][hd, :, 0:LANES] = zv[:, hd * M_DV:(hd + 1) * M_DV].astype(BF16)

    def attn_kv():
        st["kf"][...] = rope(proj(C_AK, A_KV_WIDTH))
        st["vf"][...] = proj(C_AV, A_KV_WIDTH)

    def attn_q():
        za = proj(C_AQ, A_WIDTH)
        for j in range(A_WIDTH // LANES):
            blk = rope(za[:, j * LANES:(j + 1) * LANES])
            st["qa"][:, j * LANES:(j + 1) * LANES] = (blk * (A_HEAD_DIM ** -0.5)).astype(BF16)

    def out_gates():
        st["og"][...] = _sigmoid(proj(C_MO, M_WIDTH)) * _silu(proj(C_MZ, M_WIDTH))

    def attn_gate():
        st["sz"][...] = _silu(proj(C_AZ, A_WIDTH))

    def merge_gate(name, lo, part):
        def piece():
            cs = slice(part * M_WIDTH, (part + 1) * M_WIDTH)
            st[name][:, cs] = _sigmoid(proj(lo + part * M_WIDTH, M_WIDTH))
        return piece

    return [norm, gates_q, keys, values, attn_kv, attn_q, out_gates, attn_gate,
            merge_gate("sgm", C_GM, 0), merge_gate("sgm", C_GM, 1),
            merge_gate("sga", C_GA, 0), merge_gate("sga", C_GA, 1)]


def _consume_pieces(st, first_tile, sinks_ref, gmn_ref, kbuf_s, vbuf_s, c_s, m_s, ym_s, ya_s, tile):
    L = M_CHUNK
    pieces = []
    ctx = {}

    def causal_mask():
        row = lax.broadcasted_iota(jnp.int32, (L, L), 0)
        col = lax.broadcasted_iota(jnp.int32, (L, L), 1)
        return row >= col

    def gate_piece(j):
        def piece():
            rs = slice(j * L, (j + 1) * L)
            g = st["g"][rs, :]
            ig_all = g[:, 0:LANES]
            lf_all = _log_sigmoid(g[:, LANES:2 * LANES])
            tril = jnp.where(causal_mask(), 1.0, 0.0).astype(BF16)
            b_all = _dot_exact01(tril, lf_all)
            b_last = b_all[L - 1:L, :]
            m_old = m_s[...]
            d_end = b_last - b_all + ig_all
            m_new = jnp.maximum(b_last + m_old, jnp.max(d_end, axis=0, keepdims=True))
            m_s[...] = m_new
            ctx[j] = dict(b_all=b_all, w_end=jnp.exp(d_end - m_new), decay=jnp.exp(b_last + m_old - m_new),
                          inter=b_all + m_old, r_t=jnp.transpose(ig_all - b_all))
        return piece

    def head_piece(j, hd):
        def piece():
            rs = slice(j * L, (j + 1) * L)
            hs = slice(hd * M_DK, (hd + 1) * M_DK)
            gt = ctx[j]
            dmat = jnp.where(causal_mask(), gt["b_all"][:, hd:hd + 1] + gt["r_t"][hd:hd + 1, :], -jnp.inf)
            inter_h = gt["inter"][:, hd:hd + 1]
            m_hat = jnp.maximum(inter_h, jnp.max(dmat, axis=-1, keepdims=True))
            qh = st["q"][rs, hs]
            kh = st["k"][rs, hs]
            s = _dot_nt(qh, kh.astype(BF16)) * jnp.exp(dmat - m_hat)
            w_inter = jnp.exp(inter_h - m_hat)
            va = st["vaug"][hd, rs, :]
            c_old = c_s[hd]
            tot = _dot(s.astype(BF16), va) + w_inter * _dot(qh, c_old.astype(BF16))
            num = tot[:, 0:M_DV]
            den = tot[:, M_DV:M_DV + 1]
            hh = num / jnp.maximum(jnp.abs(den), jnp.exp(-m_hat))
            ym_s[rs, hs] = (_rms(hh) * gmn_ref[:, hs] * st["og"][rs, hs]).astype(BF16)
            kw = (kh * gt["w_end"][:, hd:hd + 1]).astype(BF16)
            c_s[hd] = gt["decay"][:, hd:hd + 1] * c_old + _dot_tn(kw, va)
        return piece

    def kv_piece(j):
        def piece():
            rs = slice(j * L, (j + 1) * L)
            srows = slice((j % 2) * L, (j % 2 + 1) * L)
            lo_half = lax.broadcasted_iota(jnp.int32, (L, L), 1) < A_HEAD_DIM
            kc = st["kf"][rs, :]
            vc = st["vf"][rs, :]
            kr = pltpu.roll(kc, A_HEAD_DIM, 1)
            vr = pltpu.roll(vc, A_HEAD_DIM, 1)
            kbuf_s[0, srows, :] = jnp.where(lo_half, kc, 0.0).astype(BF16)
            kbuf_s[1, srows, :] = jnp.where(lo_half, 0.0, kr).astype(BF16)
            kbuf_s[2, srows, :] = jnp.where(lo_half, kr, 0.0).astype(BF16)
            kbuf_s[3, srows, :] = jnp.where(lo_half, 0.0, kc).astype(BF16)
            vbuf_s[0, srows, :] = jnp.where(lo_half, vc, 0.0).astype(BF16)
            vbuf_s[1, srows, :] = jnp.where(lo_half, 0.0, vr).astype(BF16)
            vbuf_s[2, srows, :] = jnp.where(lo_half, vr, 0.0).astype(BF16)
            vbuf_s[3, srows, :] = jnp.where(lo_half, 0.0, vc).astype(BF16)
        return piece

    def attn_piece(j, kv):
        def piece():
            rs = slice(j * L, (j + 1) * L)
            row2 = lax.broadcasted_iota(jnp.int32, (L, 2 * L), 0)
            col2 = lax.broadcasted_iota(jnp.int32, (L, 2 * L), 1)
            no_prev = jnp.where(first_tile, 4 * L, 0) if j == 0 else 0
            if j % 2 == 0:
                mask = ((col2 < L) & (col2 <= row2)) | ((col2 >= L) & (col2 - L > row2 + no_prev))
            else:
                mask = ((col2 >= L) & (col2 - L <= row2)) | ((col2 < L) & (col2 > row2 + no_prev))
            qcat = jnp.concatenate([st["qa"][rs, (2 * kv) * LANES:(2 * kv + 1) * LANES],
                                    st["qa"][rs, (2 * kv + 1) * LANES:(2 * kv + 2) * LANES]], axis=0)
            sc = [_dot_nt(qcat, kbuf_s[2 * kv + half]) for half in range(2)]
            for bi in range(2):
                o = None
                for half in range(2):
                    head = A_GROUP * kv + 2 * bi + half
                    sh = jnp.where(mask, sc[half][bi * L:(bi + 1) * L, :], -jnp.inf)
                    sink = sinks_ref[head]
                    mx = jnp.maximum(jnp.max(sh, axis=-1, keepdims=True), sink)
                    p = jnp.exp(sh - mx)
                    den = jnp.sum(p, axis=-1, keepdims=True) + jnp.exp(sink - mx)
                    pv = _dot((p / den).astype(BF16), vbuf_s[2 * kv + half])
                    o = pv if o is None else o + pv
                cs = slice((2 * kv + bi) * LANES, (2 * kv + bi + 1) * LANES)
                ya_s[rs, cs] = (o * st["sz"][rs, cs]).astype(BF16)
        return piece

    for j in range(tile // L):
        pieces.append(gate_piece(j))
        pieces.append(kv_piece(j))
        for hd in range(M_HEADS):
            pieces.append(head_piece(j, hd))
            if hd % 2 == 1:
                pieces.append(attn_piece(j, hd // 2))
    return pieces


def _output_pieces(st, x_ref, rows, gate_ref, ym_s, ya_s, u_s, wmo_ref, wao_ref, wo_ref, gf_ref, y_ref):
    def merge():
        u = st["sgm"][...] * _dot(ym_s[...], wmo_ref[...]) + st["sga"][...] * _dot(ya_s[...], wao_ref[...])
        u_s[...] = u.astype(BF16)

    def out():
        gate = gate_ref[0][:, 2 * D_MODEL:3 * D_MODEL]
        r = _dot(u_s[...], wo_ref[...])
        y_ref[0, rows, :] = _rms(x_ref[0, rows, :] + gate * r) * gf_ref[...]

    return [merge, out]


def _interleave(primary, filler):
    done = 0
    for i, piece in enumerate(primary):
        piece()
        upto = ((i + 1) * len(filler)) // len(primary)
        for f in filler[done:upto]:
            f()
        done = upto


def _prompt_kernel(sinks_ref, x0_ref, xa_ref, xb_ref, xres_ref, mod0_ref, moda_ref, modb_ref,
                   gn_ref, gf_ref, gmn_ref, bg_ref, w_ref, wmo_ref, wao_ref, wo_ref,
                   ra0_ref, rb0_ref, rc0_ref, raa_ref, rba_ref, rca_ref, rab_ref, rbb_ref, rcb_ref,
                   y_ref, c_out_ref, n_out_ref, m_out_ref, kk_ref, vk_ref, *scratch, tile, steps_per_seq):
    nset = len(_SET_FIELDS)
    set_a = dict(zip(_SET_FIELDS, scratch[0:nset]))
    set_b = dict(zip(_SET_FIELDS, scratch[nset:2 * nset]))
    kbuf_s, vbuf_s, c_s, m_s, yma_s, yaa_s, ua_s, ymb_s, yab_s, ub_s = scratch[2 * nset:]
    k = pl.program_id(0)
    first = (k % steps_per_seq) == 0
    project = lambda x_ref, mod_ref, rope, st: _project_pieces(x_ref, mod_ref, rope, st, gn_ref, bg_ref, w_ref)

    @pl.when(k == 0)
    def _prologue():
        lane = lax.broadcasted_iota(jnp.int32, (tile, LANES), 1)
        ones_col = jnp.where(lane == 0, 1.0, 0.0).astype(BF16)
        for st in (set_a, set_b):
            for hd in range(M_HEADS):
                st["vaug"][hd, :, LANES:2 * LANES] = ones_col
        for piece in project(x0_ref, mod0_ref, (ra0_ref, rb0_ref, rc0_ref), set_a):
            piece()

    @pl.when(first)
    def _init():
        c_s[...] = jnp.zeros_like(c_s)
        m_s[...] = jnp.zeros_like(m_s)
        kbuf_s[...] = jnp.zeros_like(kbuf_s)
        vbuf_s[...] = jnp.zeros_like(vbuf_s)

    state = (sinks_ref, gmn_ref, kbuf_s, vbuf_s, c_s, m_s)
    out_w = (wmo_ref, wao_ref, wo_ref, gf_ref, y_ref)
    _interleave(_consume_pieces(set_a, first, *state, yma_s, yaa_s, tile),
                project(xa_ref, moda_ref, (raa_ref, rba_ref, rca_ref), set_b))
    _interleave(_consume_pieces(set_b, False, *state, ymb_s, yab_s, tile),
                _output_pieces(set_a, xres_ref, slice(0, tile), moda_ref, yma_s, yaa_s, ua_s, *out_w)
                + project(xb_ref, modb_ref, (rab_ref, rbb_ref, rcb_ref), set_a))
    for piece in _output_pieces(set_b, xres_ref, slice(tile, 2 * tile), moda_ref, ymb_s, yab_s, ub_s, *out_w):
        piece()

    @pl.when((k % steps_per_seq) == steps_per_seq - 1)
    def _final():
        for hd in range(M_HEADS):
            cf = c_s[hd]
            c_out_ref[0, hd] = cf[:, 0:M_DV]
            n_out_ref[0, hd:hd + 1, :] = jnp.transpose(cf[:, M_DV:2 * M_DV])[0:1, :]
        m_out_ref[0] = m_s[...]
        kk_ref[0] = set_b["kf"][tile - WINDOW:tile, :]
        vk_ref[0] = set_b["vf"][tile - WINDOW:tile, :]


def _rope_tables(pos):
    half = ROT_DIM // 2
    inv = ROPE_THETA ** (-jnp.arange(0, ROT_DIM, 2, dtype=F32) / ROT_DIM)
    ang = pos.astype(F32)[:, None] * inv
    cos, sin = jnp.cos(ang), jnp.sin(ang)
    t = pos.shape[0]
    zh = jnp.zeros((t, half), F32)
    rest = A_HEAD_DIM - ROT_DIM
    a = jnp.concatenate([cos, cos, jnp.ones((t, rest), F32)], axis=1)
    b = jnp.concatenate([zh, sin, jnp.zeros((t, rest), F32)], axis=1)
    cc = jnp.concatenate([-sin, zh, jnp.zeros((t, rest), F32)], axis=1)
    two = lambda v: jnp.concatenate([v, v], axis=1)
    return two(a), two(b), two(cc)


def _full(shape, single_buffer=False):
    mode = dict(pipeline_mode=pl.Buffered(1)) if single_buffer else {}
    return pl.BlockSpec(shape, lambda *_: (0,) * len(shape), **mode)


def _prompt_layer(x, mod, sinks, gn, gf, gmn, bias_g, wp, wmo, wao, wo):
    bsz, seq, d = x.shape
    tile = PROMPT_TILE
    assert seq % (2 * tile) == 0 and tile % (2 * M_CHUNK) == 0 and d == D_MODEL
    tiles_per_seq = seq // tile
    steps_per_seq = tiles_per_seq // 2
    n_tiles = bsz * tiles_per_seq
    n_steps = n_tiles // 2
    ra, rb, rc = _rope_tables(jnp.arange(seq, dtype=jnp.int32))
    xt = x.reshape(n_tiles, tile, d)
    xp = x.reshape(n_steps, 2 * tile, d)
    mod3 = mod.reshape(bsz, 1, 3 * d)

    tile_0 = lambda k: 0
    tile_a = lambda k: 2 * k + 1
    tile_b = lambda k: jnp.minimum(2 * k + 2, n_tiles - 1)
    x_spec = lambda f: pl.BlockSpec((1, tile, d), lambda k: (f(k), 0, 0))
    mod_spec = lambda f: pl.BlockSpec((1, 1, 3 * d), lambda k: (f(k) // tiles_per_seq, 0, 0))
    rope_spec = lambda f: pl.BlockSpec((tile, LANES), lambda k: (f(k) % tiles_per_seq, 0))
    in_specs = [
        pl.BlockSpec(memory_space=pltpu.SMEM),
        x_spec(tile_0), x_spec(tile_a), x_spec(tile_b),
        pl.BlockSpec((1, 2 * tile, d), lambda k: (k, 0, 0)),
        mod_spec(tile_0), mod_spec(tile_a), mod_spec(tile_b),
        _full((1, d)), _full((1, d)), _full((1, M_WIDTH)), _full((1, 2 * LANES)),
        _full((d, N_PACK), True), _full((M_WIDTH, d), True), _full((A_WIDTH, d), True), _full((d, d), True),
        rope_spec(tile_0), rope_spec(tile_0), rope_spec(tile_0),
        rope_spec(tile_a), rope_spec(tile_a), rope_spec(tile_a),
        rope_spec(tile_b), rope_spec(tile_b), rope_spec(tile_b),
    ]
    seq_of = lambda k: k // steps_per_seq
    out_specs = [
        pl.BlockSpec((1, 2 * tile, d), lambda k: (k, 0, 0)),
        pl.BlockSpec((1, M_HEADS, M_DK, M_DV), lambda k: (seq_of(k), 0, 0, 0)),
        pl.BlockSpec((1, M_HEADS, M_DK), lambda k: (seq_of(k), 0, 0)),
        pl.BlockSpec((1, 1, LANES), lambda k: (seq_of(k), 0, 0)),
        pl.BlockSpec((1, WINDOW, A_KV_WIDTH), lambda k: (seq_of(k), 0, 0)),
        pl.BlockSpec((1, WINDOW, A_KV_WIDTH), lambda k: (seq_of(k), 0, 0)),
    ]
    out_shape = [
        jax.ShapeDtypeStruct((n_steps, 2 * tile, d), F32),
        jax.ShapeDtypeStruct((bsz, M_HEADS, M_DK, M_DV), F32),
        jax.ShapeDtypeStruct((bsz, M_HEADS, M_DK), F32),
        jax.ShapeDtypeStruct((bsz, 1, LANES), F32),
        jax.ShapeDtypeStruct((bsz, WINDOW, A_KV_WIDTH), F32),
        jax.ShapeDtypeStruct((bsz, WINDOW, A_KV_WIDTH), F32),
    ]
    scratch = _set_shapes(tile) + _set_shapes(tile) + [
        pltpu.VMEM((2 * A_KV_HEADS, 2 * WINDOW, LANES), BF16),
        pltpu.VMEM((2 * A_KV_HEADS, 2 * WINDOW, LANES), BF16),
        pltpu.VMEM((M_HEADS, M_DK, 2 * LANES), F32),
        pltpu.VMEM((1, LANES), F32),
        pltpu.VMEM((tile, M_WIDTH), BF16),
        pltpu.VMEM((tile, A_WIDTH), BF16),
        pltpu.VMEM((tile, D_MODEL), BF16),
        pltpu.VMEM((tile, M_WIDTH), BF16),
        pltpu.VMEM((tile, A_WIDTH), BF16),
        pltpu.VMEM((tile, D_MODEL), BF16),
    ]
    outs = pl.pallas_call(
        functools.partial(_prompt_kernel, tile=tile, steps_per_seq=steps_per_seq),
        grid=(n_steps,),
        in_specs=in_specs,
        out_specs=out_specs,
        out_shape=out_shape,
        scratch_shapes=scratch,
        compiler_params=pltpu.CompilerParams(dimension_semantics=("arbitrary",), vmem_limit_bytes=VMEM_LIMIT),
        name="prompt_layer",
    )(sinks, xt, xt, xt, xp, mod3, mod3, mod3, gn, gf, gmn, bias_g, wp, wmo, wao, wo,
      ra, rb, rc, ra, rb, rc, ra, rb, rc)
    return (outs[0].reshape(bsz, seq, d),) + tuple(outs[1:])


def _sample_proj_kernel(x_ref, mod_ref, gn_ref, w_ref, z_ref, hb_s):
    @pl.when(pl.program_id(0) == 0)
    def _norm():
        mod = mod_ref[...]
        h = _rms(x_ref[...]) * gn_ref[...]
        hb_s[...] = (h * (1.0 + mod[:, D_MODEL:2 * D_MODEL]) + mod[:, 0:D_MODEL]).astype(BF16)

    z_ref[...] = _dot(hb_s[...], w_ref[...])


def _sample_proj(x2, mod_rows, gn, wp):
    rows, d = x2.shape
    tn = 512
    return pl.pallas_call(
        _sample_proj_kernel,
        grid=(N_PACK // tn,),
        in_specs=[
            _full((rows, d)),
            pl.BlockSpec((rows, 2 * d), lambda j: (0, 0)),
            _full((1, d)),
            pl.BlockSpec((d, tn), lambda j: (0, j)),
        ],
        out_specs=pl.BlockSpec((rows, tn), lambda j: (0, j)),
        out_shape=jax.ShapeDtypeStruct((rows, N_PACK), F32),
        scratch_shapes=[pltpu.VMEM((rows, d), BF16)],
        compiler_params=pltpu.CompilerParams(dimension_semantics=("arbitrary",)),
        name="sample_proj",
    )(x2, mod_rows, gn, wp)


def _sample_state_kernel(z_ref, c_ref, n_ref, m_ref, ck_ref, cv_ref, ra_ref, rb_ref, rc_ref, bg_ref, gmn_ref,
                         sink_ref,
                         cn_ref, nn_ref, mn_ref, ym_ref, ya_ref, kn_ref, vn_ref, *, group, steps):
    T = steps
    R = group * T

    def z(lo, n):
        return z_ref[:, lo:lo + n]

    q_all = z(C_MQ, M_WIDTH) * (M_DK ** -0.5)
    k_all = z(C_MK, M_WIDTH)
    v_all = z(C_MV, M_WIDTH)
    og_all = _sigmoid(z(C_MO, M_WIDTH)) * _silu(z(C_MZ, M_WIDTH))
    ra = ra_ref[...]
    rb = rb_ref[...]
    rc = rc_ref[...]
    qa_blocks = [_rope(z(C_AQ + j * LANES, LANES), ra, rb, rc) * (A_HEAD_DIM ** -0.5)
                 for j in range(A_WIDTH // LANES)]
    kn_all = _rope(z(C_AK, A_KV_WIDTH), ra, rb, rc)
    vn_all = z(C_AV, A_KV_WIDTH)
    sz_all = _silu(z(C_AZ, A_WIDTH))
    g = z(C_GI, 2 * LANES) + bg_ref[...]
    ig_all = g[:, 0:LANES]
    lf_all = _log_sigmoid(g[:, LANES:2 * LANES])
    kn_ref[...] = kn_all
    vn_ref[...] = vn_all

    row_t = lax.broadcasted_iota(jnp.int32, (T, T), 0)
    col_t = lax.broadcasted_iota(jnp.int32, (T, T), 1)
    causal = row_t >= col_t
    lane_t = lax.broadcasted_iota(jnp.int32, (T, LANES), 1)
    rows_a = A_HEADS * T
    tok_c = lax.broadcasted_iota(jnp.int32, (rows_a, WINDOW), 0) % T
    key_c = lax.broadcasted_iota(jnp.int32, (rows_a, WINDOW), 1)
    mask_c = key_c > tok_c
    tok_n = lax.broadcasted_iota(jnp.int32, (rows_a, T), 0) % T
    key_n = lax.broadcasted_iota(jnp.int32, (rows_a, T), 1)
    mask_n = key_n <= tok_n
    sink = sink_ref[...][:, 0:1]
    gmn = gmn_ref[...]

    for b in range(group):
        rs = slice(b * T, (b + 1) * T)

        ig = ig_all[rs, 0:M_HEADS]
        lf = lf_all[rs, 0:M_HEADS]
        cum = [lf[0:1, :]]
        for t in range(1, T):
            cum.append(cum[-1] + lf[t:t + 1, :])
        b_all = jnp.concatenate(cum, axis=0)
        b_last = cum[-1]
        m_old = m_ref[b:b + 1, :]
        d_end = b_last - b_all + ig
        m_new = jnp.maximum(b_last + m_old, jnp.max(d_end, axis=0, keepdims=True))
        w_end = jnp.exp(d_end - m_new)
        decay = jnp.exp(b_last + m_old - m_new)
        inter = b_all + m_old
        r_t = jnp.transpose(ig - b_all)
        mn_ref[b:b + 1, :] = m_new
        for hd in range(M_HEADS):
            hs = slice(hd * M_DK, (hd + 1) * M_DK)
            dmat = jnp.where(causal, b_all[:, hd:hd + 1] + r_t[hd:hd + 1, :], -jnp.inf)
            inter_h = inter[:, hd:hd + 1]
            m_hat = jnp.maximum(inter_h, jnp.max(dmat, axis=-1, keepdims=True))
            qh = q_all[rs, hs]
            kh = k_all[rs, hs]
            vh = v_all[rs, hs].astype(BF16)
            qb = qh.astype(BF16)
            s = _dot_nt(qb, kh.astype(BF16)) * jnp.exp(dmat - m_hat)
            w_inter = jnp.exp(inter_h - m_hat)
            c_old = c_ref[b, hd]
            n_old = n_ref[b:b + 1, hs]
            num = _dot(s.astype(BF16), vh) + w_inter * _dot(qb, c_old.astype(BF16))
            den = jnp.sum(s, axis=-1, keepdims=True) + w_inter * jnp.sum(qh * n_old, axis=-1, keepdims=True)
            hh = num / jnp.maximum(jnp.abs(den), jnp.exp(-m_hat))
            ym_ref[rs, hs] = _rms(hh) * gmn[:, hs] * og_all[rs, hs]
            kw = kh * w_end[:, hd:hd + 1]
            dec = decay[:, hd:hd + 1]
            cn_ref[b, hd] = dec * c_old + _dot_tn(kw.astype(BF16), vh)
            nn_ref[b:b + 1, hs] = dec * n_old + jnp.sum(kw, axis=0, keepdims=True)

        qrows = []
        for head in range(A_HEADS):
            kv = head // A_GROUP
            blk = qa_blocks[head // 2][rs, :]
            if head % 2 != kv:
                blk = pltpu.roll(blk, A_HEAD_DIM, 1)
            keep = (lane_t < A_HEAD_DIM) if kv == 0 else (lane_t >= A_HEAD_DIM)
            qrows.append(jnp.where(keep, blk, 0.0))
        qs = jnp.concatenate(qrows, axis=0).astype(BF16)
        kn = kn_all[rs, :].astype(BF16)
        vn = vn_all[rs, :].astype(BF16)
        sc = jnp.where(mask_c, _dot_nt(qs, ck_ref[b].astype(BF16)), -jnp.inf)
        sn = jnp.where(mask_n, _dot_nt(qs, kn), -jnp.inf)
        mx = jnp.maximum(jnp.maximum(jnp.max(sc, axis=-1, keepdims=True), jnp.max(sn, axis=-1, keepdims=True)), sink)
        pc = jnp.exp(sc - mx)
        pn = jnp.exp(sn - mx)
        den = jnp.sum(pc, axis=-1, keepdims=True) + jnp.sum(pn, axis=-1, keepdims=True) + jnp.exp(sink - mx)
        o = _dot((pc / den).astype(BF16), cv_ref[b].astype(BF16)) + _dot((pn / den).astype(BF16), vn)
        for j in range(A_WIDTH // LANES):
            pieces = []
            for half in range(2):
                head = 2 * j + half
                piece = o[head * T:(head + 1) * T, :]
                if half != head // A_GROUP:
                    piece = pltpu.roll(piece, A_HEAD_DIM, 1)
                pieces.append(piece)
            cs = slice(j * LANES, (j + 1) * LANES)
            ya_ref[rs, cs] = jnp.where(lane_t < A_HEAD_DIM, pieces[0], pieces[1]) * sz_all[rs, cs]


def _sample_state(zs, state_c, state_n2, state_m, cache_k, cache_v, ra, rb, rc, bias_g, gmn, sink_rows, steps):
    bsz = state_c.shape[0]
    group = SAMPLE_GROUP
    assert bsz % group == 0
    rows = group * steps
    row_spec = lambda n: pl.BlockSpec((rows, n), lambda i: (i, 0))
    in_specs = [
        row_spec(N_PACK),
        pl.BlockSpec((group, M_HEADS, M_DK, M_DV), lambda i: (i, 0, 0, 0)),
        pl.BlockSpec((group, M_WIDTH), lambda i: (i, 0)),
        pl.BlockSpec((group, M_HEADS), lambda i: (i, 0)),
        pl.BlockSpec((group, WINDOW, A_KV_WIDTH), lambda i: (i, 0, 0)),
        pl.BlockSpec((group, WINDOW, A_KV_WIDTH), lambda i: (i, 0, 0)),
        _full((rows, LANES)), _full((rows, LANES)), _full((rows, LANES)),
        _full((1, 2 * LANES)), _full((1, M_WIDTH)), _full((A_HEADS * steps, LANES)),
    ]
    out_specs = [
        pl.BlockSpec((group, M_HEADS, M_DK, M_DV), lambda i: (i, 0, 0, 0)),
        pl.BlockSpec((group, M_WIDTH), lambda i: (i, 0)),
        pl.BlockSpec((group, M_HEADS), lambda i: (i, 0)),
        row_spec(M_WIDTH), row_spec(A_WIDTH), row_spec(A_KV_WIDTH), row_spec(A_KV_WIDTH),
    ]
    out_shape = [
        jax.ShapeDtypeStruct((bsz, M_HEADS, M_DK, M_DV), F32),
        jax.ShapeDtypeStruct((bsz, M_WIDTH), F32),
        jax.ShapeDtypeStruct((bsz, M_HEADS), F32),
        jax.ShapeDtypeStruct((bsz * steps, M_WIDTH), F32),
        jax.ShapeDtypeStruct((bsz * steps, A_WIDTH), F32),
        jax.ShapeDtypeStruct((bsz * steps, A_KV_WIDTH), F32),
        jax.ShapeDtypeStruct((bsz * steps, A_KV_WIDTH), F32),
    ]
    return pl.pallas_call(
        functools.partial(_sample_state_kernel, group=group, steps=steps),
        grid=(bsz // group,),
        in_specs=in_specs,
        out_specs=out_specs,
        out_shape=out_shape,
        compiler_params=pltpu.CompilerParams(dimension_semantics=("arbitrary",), vmem_limit_bytes=VMEM_LIMIT),
        name="sample_state",
    )(zs, state_c, state_n2, state_m, cache_k, cache_v, ra, rb, rc, bias_g, gmn, sink_rows)


def _sample_out_kernel(x_ref, gate_ref, z_ref, ym_ref, ya_ref, wmo_ref, wao_ref, wo_ref, gf_ref, y_ref):
    zz = z_ref[...]
    y_ref[...] = _out_stage(x_ref[...], gate_ref[...], _sigmoid(zz[:, 0:D_MODEL]), _sigmoid(zz[:, D_MODEL:2 * D_MODEL]),
                            ym_ref[...].astype(BF16), ya_ref[...].astype(BF16), wmo_ref, wao_ref, wo_ref, gf_ref[...])


def _sample_out(x2, mod_rows, zs, ym, ya, wmo, wao, wo, gf):
    rows, d = x2.shape
    return pl.pallas_call(
        _sample_out_kernel,
        grid=(1,),
        in_specs=[
            _full((rows, d)),
            pl.BlockSpec((rows, d), lambda i: (0, 2)),
            pl.BlockSpec((rows, 2 * d), lambda i: (0, 0)),
            _full((rows, M_WIDTH)), _full((rows, A_WIDTH)),
            _full((M_WIDTH, d)), _full((A_WIDTH, d)), _full((d, d)), _full((1, d)),
        ],
        out_specs=_full((rows, d)),
        out_shape=jax.ShapeDtypeStruct((rows, d), F32),
        compiler_params=pltpu.CompilerParams(dimension_semantics=("arbitrary",), vmem_limit_bytes=VMEM_LIMIT),
        name="sample_out",
    )(x2, mod_rows, zs, ym, ya, wmo, wao, wo, gf)


def _pack_w_in(w_in):
    parts, start = [], 0
    for size in IN_SIZES:
        parts.append(w_in[:, start:start + size])
        start += size
    mq, mk, mv, mi, mf, mo, mz, aq, ak, av, az, gm, ga = parts
    pad = jnp.zeros((w_in.shape[0], LANES - M_HEADS), w_in.dtype)
    return jnp.concatenate([gm, ga, mq, mk, mv, mo, mz, aq, az, ak, av, mi, pad, mf, pad], axis=1).astype(BF16)


def kernel(x_prompt, x_sample, state_C, state_n, state_m, cache_k, cache_v, c_prompt, c_sample, w_ada, b_ada, g_norm, w_in, b_igate, b_fgate, g_mnorm, sinks, w_m_out, w_a_out, w_out, g_final):
    assert w_in.shape[0] == 1, "single-layer step"
    bsz_s, steps, d = x_sample.shape
    cw = cache_k.shape[2]
    assert cw == WINDOW

    mod_p, mod_s = _adaln(c_prompt, c_sample, w_ada[0], b_ada[0])
    wp = _pack_w_in(w_in[0])
    wmo = w_m_out[0].astype(BF16)
    wao = w_a_out[0].astype(BF16)
    wo = w_out[0].astype(BF16)
    gn = g_norm[0].reshape(1, d)
    gf = g_final.reshape(1, d)
    gmn = g_mnorm[0].reshape(1, M_WIDTH)
    zpad = jnp.zeros((LANES - M_HEADS,), F32)
    bias_g = jnp.concatenate([b_igate[0], zpad, b_fgate[0], zpad]).reshape(1, 2 * LANES)
    sk = sinks[0]

    y_p, c_p, n_p, m_p, k_p, v_p = _prompt_layer(x_prompt, mod_p, sk, gn, gf, gmn, bias_g, wp, wmo, wao, wo)
    bsz_p = x_prompt.shape[0]
    m_p = m_p[:, 0, 0:M_HEADS]
    k_p = k_p.reshape(bsz_p, WINDOW, A_KV_HEADS, A_HEAD_DIM)
    v_p = v_p.reshape(bsz_p, WINDOW, A_KV_HEADS, A_HEAD_DIM)

    rows = bsz_s * steps
    x2 = x_sample.reshape(rows, d)
    mod_rows = jnp.repeat(mod_s, steps, axis=0)
    zs = _sample_proj(x2, mod_rows, gn, wp)
    ra, rb, rc = _rope_tables(PAST_LEN + jnp.arange(steps, dtype=jnp.int32))
    tile_rows = lambda t: jnp.tile(t, (SAMPLE_GROUP, 1))
    sink_rows = jnp.broadcast_to(jnp.repeat(sk, steps)[:, None], (A_HEADS * steps, LANES))
    ck = cache_k[0].reshape(bsz_s, cw, A_KV_WIDTH)
    cv = cache_v[0].reshape(bsz_s, cw, A_KV_WIDTH)
    c_s, n_s, m_s, ym, ya, k_new, v_new = _sample_state(
        zs, state_C[0], state_n[0].reshape(bsz_s, M_WIDTH), state_m[0], ck, cv,
        tile_rows(ra), tile_rows(rb), tile_rows(rc), bias_g, gmn, sink_rows, steps)
    y_s = _sample_out(x2, mod_rows, zs, ym, ya, wmo, wao, wo, gf).reshape(bsz_s, steps, d)
    k_s = jnp.concatenate([ck[:, steps:], k_new.reshape(bsz_s, steps, A_KV_WIDTH)], axis=1)
    v_s = jnp.concatenate([cv[:, steps:], v_new.reshape(bsz_s, steps, A_KV_WIDTH)], axis=1)
    k_s = k_s.reshape(bsz_s, cw, A_KV_HEADS, A_HEAD_DIM)
    v_s = v_s.reshape(bsz_s, cw, A_KV_HEADS, A_HEAD_DIM)

    return (y_p, y_s, c_p[None], n_p[None], m_p[None], k_p[None], v_p[None],
            c_s[None], n_s.reshape(bsz_s, M_HEADS, M_DK)[None], m_s[None], k_s[None], v_s[None])
```

```python
import functools

import jax
import jax.numpy as jnp
from jax import lax
from jax.experimental import pallas as pl
from jax.experimental.pallas import tpu as pltpu

F32 = jnp.float32
BF16 = jnp.bfloat16

D_MODEL = 1024
M_HEADS = 4
M_DK = 128
M_DV = 128
M_WIDTH = M_HEADS * M_DV
M_CHUNK = 128
A_HEADS = 8
A_KV_HEADS = 2
A_GROUP = A_HEADS // A_KV_HEADS
A_HEAD_DIM = 64
A_WIDTH = A_HEADS * A_HEAD_DIM
A_KV_WIDTH = A_KV_HEADS * A_HEAD_DIM
WINDOW = 128
ROT_DIM = A_HEAD_DIM // 4
ROPE_THETA = 500000.0
NORM_EPS = 1e-6
PAST_LEN = 16384
IN_SIZES = (M_HEADS * M_DK, M_HEADS * M_DK, M_WIDTH, M_HEADS, M_HEADS, M_WIDTH, M_WIDTH,
            A_WIDTH, A_KV_WIDTH, A_KV_WIDTH, A_WIDTH, D_MODEL, D_MODEL)

LANES = 128

C_GM = 0
C_GA = C_GM + D_MODEL
C_MQ = C_GA + D_MODEL
C_MK = C_MQ + M_WIDTH
C_MV = C_MK + M_WIDTH
C_MO = C_MV + M_WIDTH
C_MZ = C_MO + M_WIDTH
C_AQ = C_MZ + M_WIDTH
C_AZ = C_AQ + A_WIDTH
C_AK = C_AZ + A_WIDTH
C_AV = C_AK + A_KV_WIDTH
C_GI = C_AV + A_KV_WIDTH
C_GF = C_GI + LANES
N_PACK = C_GF + LANES

PROMPT_TILE = 256
SAMPLE_GROUP = 8
VMEM_LIMIT = 56 * 1024 * 1024


def _sigmoid(x):
    return 1.0 / (1.0 + jnp.exp(-x))


def _silu(x):
    return x * _sigmoid(x)


def _log_sigmoid(x):
    return jnp.minimum(x, 0.0) - jnp.log1p(jnp.exp(-jnp.abs(x)))


def _dot(a, b):
    return jnp.dot(a, b, preferred_element_type=F32)


def _dot_nt(a, b):
    return lax.dot_general(a, b, (((1,), (1,)), ((), ())), preferred_element_type=F32)


def _dot_tn(a, b):
    return lax.dot_general(a, b, (((0,), (0,)), ((), ())), preferred_element_type=F32)


def _dot_exact01(m01, x):
    x1 = x.astype(BF16)
    r1 = x - x1.astype(F32)
    x2 = r1.astype(BF16)
    x3 = (r1 - x2.astype(F32)).astype(BF16)
    return _dot(m01, x1) + _dot(m01, x2) + _dot(m01, x3)


def _rms(x):
    return x * lax.rsqrt(jnp.mean(x * x, axis=-1, keepdims=True) + NORM_EPS)


def _rope(blk, ra, rb, rc):
    return blk * ra + pltpu.roll(blk, 8, 1) * rb + pltpu.roll(blk, LANES - 8, 1) * rc


def _out_stage(x, gate, sgm, sga, ym, ya, wmo_ref, wao_ref, wo_ref, gf):
    pm = _dot(ym, wmo_ref[...])
    pa = _dot(ya, wao_ref[...])
    u = sgm * pm + sga * pa
    r = _dot(u.astype(BF16), wo_ref[...])
    return _rms(x + gate * r) * gf


def _adaln_kernel(cp_ref, cs_ref, w_ref, b_ref, op_ref, os_ref):
    w = w_ref[...].astype(BF16)
    b = b_ref[...]
    op_ref[...] = _dot(_silu(cp_ref[...]).astype(BF16), w) + b
    os_ref[...] = _dot(_silu(cs_ref[...]).astype(BF16), w) + b


def _adaln(c_p, c_s, w_ada, b_ada):
    bp, d = c_p.shape
    bs = c_s.shape[0]
    n = w_ada.shape[1]
    tn = 512
    return pl.pallas_call(
        _adaln_kernel,
        grid=(n // tn,),
        in_specs=[
            pl.BlockSpec((bp, d), lambda j: (0, 0)),
            pl.BlockSpec((bs, d), lambda j: (0, 0)),
            pl.BlockSpec((d, tn), lambda j: (0, j)),
            pl.BlockSpec((1, tn), lambda j: (0, j)),
        ],
        out_specs=[
            pl.BlockSpec((bp, tn), lambda j: (0, j)),
            pl.BlockSpec((bs, tn), lambda j: (0, j)),
        ],
        out_shape=[jax.ShapeDtypeStruct((bp, n), F32), jax.ShapeDtypeStruct((bs, n), F32)],
        compiler_params=pltpu.CompilerParams(dimension_semantics=("arbitrary",)),
        name="adaln",
    )(c_p, c_s, w_ada, b_ada.reshape(1, n))


_SET_FIELDS = ("hb", "q", "k", "vaug", "og", "qa", "kf", "vf", "sz", "sgm", "sga", "g")


def _set_shapes(tile):
    return [
        pltpu.VMEM((tile, D_MODEL), BF16),
        pltpu.VMEM((tile, M_WIDTH), BF16),
        pltpu.VMEM((tile, M_WIDTH), F32),
        pltpu.VMEM((M_HEADS, tile, 2 * LANES), BF16),
        pltpu.VMEM((tile, M_WIDTH), F32),
        pltpu.VMEM((tile, A_WIDTH), BF16),
        pltpu.VMEM((tile, A_KV_WIDTH), F32),
        pltpu.VMEM((tile, A_KV_WIDTH), F32),
        pltpu.VMEM((tile, A_WIDTH), F32),
        pltpu.VMEM((tile, D_MODEL), F32),
        pltpu.VMEM((tile, D_MODEL), F32),
        pltpu.VMEM((tile, 2 * LANES), F32),
    ]


def _project_pieces(x_ref, mod_ref, rope_refs, st, gn_ref, bg_ref, w_ref):
    def norm():
        mod = mod_ref[0]
        h = _rms(x_ref[0]) * gn_ref[...]
        st["hb"][...] = (h * (1.0 + mod[:, D_MODEL:2 * D_MODEL]) + mod[:, 0:D_MODEL]).astype(BF16)

    def proj(lo, n):
        return _dot(st["hb"][...], w_ref[:, lo:lo + n])

    def rope(v):
        ra_ref, rb_ref, rc_ref = rope_refs
        return _rope(v, ra_ref[...], rb_ref[...], rc_ref[...])

    def gates_q():
        st["g"][...] = proj(C_GI, 2 * LANES) + bg_ref[...]
        st["q"][...] = (proj(C_MQ, M_WIDTH) * (M_DK ** -0.5)).astype(BF16)

    def keys():
        st["k"][...] = proj(C_MK, M_WIDTH)

    def values():
        zv = proj(C_MV, M_WIDTH)
        for hd in range(M_HEADS):
            st["vaug"][hd, :, 0:LANES] = zv[:, hd * M_DV:(hd + 1) * M_DV].astype(BF16)

    def attn_kv():
        st["kf"][...] = rope(proj(C_AK, A_KV_WIDTH))
        st["vf"][...] = proj(C_AV, A_KV_WIDTH)

    def attn_q():
        za = proj(C_AQ, A_WIDTH)
        for j in range(A_WIDTH // LANES):
            blk = rope(za[:, j * LANES:(j + 1) * LANES])
            st["qa"][:, j * LANES:(j + 1) * LANES] = (blk * (A_HEAD_DIM ** -0.5)).astype(BF16)

    def out_gates():
        st["og"][...] = _sigmoid(proj(C_MO, M_WIDTH)) * _silu(proj(C_MZ, M_WIDTH))

    def attn_gate():
        st["sz"][...] = _silu(proj(C_AZ, A_WIDTH))

    def merge_gate(name, lo, part):
        def piece():
            cs = slice(part * M_WIDTH, (part + 1) * M_WIDTH)
            st[name][:, cs] = _sigmoid(proj(lo + part * M_WIDTH, M_WIDTH))
        return piece

    return [norm, gates_q, keys, values, attn_kv, attn_q, out_gates, attn_gate,
            merge_gate("sgm", C_GM, 0), merge_gate("sgm", C_GM, 1),
            merge_gate("sga", C_GA, 0), merge_gate("sga", C_GA, 1)]


def _consume_pieces(st, first_tile, sinks_ref, gmn_ref, kbuf_s, vbuf_s, c_s, m_s, ym_s, ya_s, tile):
    L = M_CHUNK
    pieces = []
    ctx = {}

    def causal_mask():
        row = lax.broadcasted_iota(jnp.int32, (L, L), 0)
        col = lax.broadcasted_iota(jnp.int32, (L, L), 1)
        return row >= col

    def gate_piece(j):
        def piece():
            rs = slice(j * L, (j + 1) * L)
            g = st["g"][rs, :]
            ig_all = g[:, 0:LANES]
            lf_all = _log_sigmoid(g[:, LANES:2 * LANES])
            tril = jnp.where(causal_mask(), 1.0, 0.0).astype(BF16)
            b_all = _dot_exact01(tril, lf_all)
            b_last = b_all[L - 1:L, :]
            m_old = m_s[...]
            d_end = b_last - b_all + ig_all
            m_new = jnp.maximum(b_last + m_old, jnp.max(d_end, axis=0, keepdims=True))
            m_s[...] = m_new
            ctx[j] = dict(b_all=b_all, w_end=jnp.exp(d_end - m_new), decay=jnp.exp(b_last + m_old - m_new),
                          inter=b_all + m_old, r_t=jnp.transpose(ig_all - b_all))
        return piece

    def head_piece(j, hd):
        def piece():
            rs = slice(j * L, (j + 1) * L)
            hs = slice(hd * M_DK, (hd + 1) * M_DK)
            gt = ctx[j]
            dmat = jnp.where(causal_mask(), gt["b_all"][:, hd:hd + 1] + gt["r_t"][hd:hd + 1, :], -jnp.inf)
            inter_h = gt["inter"][:, hd:hd + 1]
            m_hat = jnp.maximum(inter_h, jnp.max(dmat, axis=-1, keepdims=True))
            qh = st["q"][rs, hs]
            kh = st["k"][rs, hs]
            s = _dot_nt(qh, kh.astype(BF16)) * jnp.exp(dmat - m_hat)
            w_inter = jnp.exp(inter_h - m_hat)
            va = st["vaug"][hd, rs, :]
            c_old = c_s[hd]
            tot = _dot(s.astype(BF16), va) + w_inter * _dot(qh, c_old.astype(BF16))
            num = tot[:, 0:M_DV]
            den = tot[:, M_DV:M_DV + 1]
            hh = num / jnp.maximum(jnp.abs(den), jnp.exp(-m_hat))
            ym_s[rs, hs] = (_rms(hh) * gmn_ref[:, hs] * st["og"][rs, hs]).astype(BF16)
            kw = (kh * gt["w_end"][:, hd:hd + 1]).astype(BF16)
            c_s[hd] = gt["decay"][:, hd:hd + 1] * c_old + _dot_tn(kw, va)
        return piece

    def kv_piece(j):
        def piece():
            rs = slice(j * L, (j + 1) * L)
            srows = slice((j % 2) * L, (j % 2 + 1) * L)
            lo_half = lax.broadcasted_iota(jnp.int32, (L, L), 1) < A_HEAD_DIM
            kc = st["kf"][rs, :]
            vc = st["vf"][rs, :]
            kr = pltpu.roll(kc, A_HEAD_DIM, 1)
            vr = pltpu.roll(vc, A_HEAD_DIM, 1)
            kbuf_s[0, srows, :] = jnp.where(lo_half, kc, 0.0).astype(BF16)
            kbuf_s[1, srows, :] = jnp.where(lo_half, 0.0, kr).astype(BF16)
            kbuf_s[2, srows, :] = jnp.where(lo_half, kr, 0.0).astype(BF16)
            kbuf_s[3, srows, :] = jnp.where(lo_half, 0.0, kc).astype(BF16)
            vbuf_s[0, srows, :] = jnp.where(lo_half, vc, 0.0).astype(BF16)
            vbuf_s[1, srows, :] = jnp.where(lo_half, 0.0, vr).astype(BF16)
            vbuf_s[2, srows, :] = jnp.where(lo_half, vr, 0.0).astype(BF16)
            vbuf_s[3, srows, :] = jnp.where(lo_half, 0.0, vc).astype(BF16)
        return piece

    def attn_piece(j, kv):
        def piece():
            rs = slice(j * L, (j + 1) * L)
            row2 = lax.broadcasted_iota(jnp.int32, (L, 2 * L), 0)
            col2 = lax.broadcasted_iota(jnp.int32, (L, 2 * L), 1)
            no_prev = jnp.where(first_tile, 4 * L, 0) if j == 0 else 0
            if j % 2 == 0:
                mask = ((col2 < L) & (col2 <= row2)) | ((col2 >= L) & (col2 - L > row2 + no_prev))
            else:
                mask = ((col2 >= L) & (col2 - L <= row2)) | ((col2 < L) & (col2 > row2 + no_prev))
            qcat = jnp.concatenate([st["qa"][rs, (2 * kv) * LANES:(2 * kv + 1) * LANES],
                                    st["qa"][rs, (2 * kv + 1) * LANES:(2 * kv + 2) * LANES]], axis=0)
            sc = [_dot_nt(qcat, kbuf_s[2 * kv + half]) for half in range(2)]
            for bi in range(2):
                o = None
                for half in range(2):
                    head = A_GROUP * kv + 2 * bi + half
                    sh = jnp.where(mask, sc[half][bi * L:(bi + 1) * L, :], -jnp.inf)
                    sink = sinks_ref[head]
                    mx = jnp.maximum(jnp.max(sh, axis=-1, keepdims=True), sink)
                    p = jnp.exp(sh - mx)
                    den = jnp.sum(p, axis=-1, keepdims=True) + jnp.exp(sink - mx)
                    pv = _dot((p / den).astype(BF16), vbuf_s[2 * kv + half])
                    o = pv if o is None else o + pv
                cs = slice((2 * kv + bi) * LANES, (2 * kv + bi + 1) * LANES)
                ya_s[rs, cs] = (o * st["sz"][rs, cs]).astype(BF16)
        return piece

    for j in range(tile // L):
        pieces.append(gate_piece(j))
        pieces.append(kv_piece(j))
        for hd in range(M_HEADS):
            pieces.append(head_piece(j, hd))
            if hd % 2 == 1:
                pieces.append(attn_piece(j, hd // 2))
    return pieces


def _output_pieces(st, x_ref, rows, gate_ref, ym_s, ya_s, u_s, wmo_ref, wao_ref, wo_ref, gf_ref, y_ref):
    def merge():
        u = st["sgm"][...] * _dot(ym_s[...], wmo_ref[...]) + st["sga"][...] * _dot(ya_s[...], wao_ref[...])
        u_s[...] = u.astype(BF16)

    def out():
        gate = gate_ref[0][:, 2 * D_MODEL:3 * D_MODEL]
        r = _dot(u_s[...], wo_ref[...])
        y_ref[0, rows, :] = _rms(x_ref[0, rows, :] + gate * r) * gf_ref[...]

    return [merge, out]


def _interleave(primary, filler):
    done = 0
    for i, piece in enumerate(primary):
        piece()
        upto = ((i + 1) * len(filler)) // len(primary)
        for f in filler[done:upto]:
            f()
        done = upto


def _prompt_kernel(sinks_ref, x0_ref, xa_ref, xb_ref, xres_ref, mod0_ref, moda_ref, modb_ref,
                   gn_ref, gf_ref, gmn_ref, bg_ref, w_ref, wmo_ref, wao_ref, wo_ref,
                   ra0_ref, rb0_ref, rc0_ref, raa_ref, rba_ref, rca_ref, rab_ref, rbb_ref, rcb_ref,
                   y_ref, c_out_ref, n_out_ref, m_out_ref, kk_ref, vk_ref, *scratch, tile, steps_per_seq):
    nset = len(_SET_FIELDS)
    set_a = dict(zip(_SET_FIELDS, scratch[0:nset]))
    set_b = dict(zip(_SET_FIELDS, scratch[nset:2 * nset]))
    kbuf_s, vbuf_s, c_s, m_s, yma_s, yaa_s, ua_s, ymb_s, yab_s, ub_s = scratch[2 * nset:]
    k = pl.program_id(0)
    first = (k % steps_per_seq) == 0
    project = lambda x_ref, mod_ref, rope, st: _project_pieces(x_ref, mod_ref, rope, st, gn_ref, bg_ref, w_ref)

    @pl.when(k == 0)
    def _prologue():
        lane = lax.broadcasted_iota(jnp.int32, (tile, LANES), 1)
        ones_col = jnp.where(lane == 0, 1.0, 0.0).astype(BF16)
        for st in (set_a, set_b):
            for hd in range(M_HEADS):
                st["vaug"][hd, :, LANES:2 * LANES] = ones_col
        for piece in project(x0_ref, mod0_ref, (ra0_ref, rb0_ref, rc0_ref), set_a):
            piece()

    @pl.when(first)
    def _init():
        c_s[...] = jnp.zeros_like(c_s)
        m_s[...] = jnp.zeros_like(m_s)
        kbuf_s[...] = jnp.zeros_like(kbuf_s)
        vbuf_s[...] = jnp.zeros_like(vbuf_s)

    state = (sinks_ref, gmn_ref, kbuf_s, vbuf_s, c_s, m_s)
    out_w = (wmo_ref, wao_ref, wo_ref, gf_ref, y_ref)
    _interleave(_consume_pieces(set_a, first, *state, yma_s, yaa_s, tile),
                project(xa_ref, moda_ref, (raa_ref, rba_ref, rca_ref), set_b))
    _interleave(_consume_pieces(set_b, False, *state, ymb_s, yab_s, tile),
                _output_pieces(set_a, xres_ref, slice(0, tile), moda_ref, yma_s, yaa_s, ua_s, *out_w)
                + project(xb_ref, modb_ref, (rab_ref, rbb_ref, rcb_ref), set_a))
    for piece in _output_pieces(set_b, xres_ref, slice(tile, 2 * tile), moda_ref, ymb_s, yab_s, ub_s, *out_w):
        piece()

    @pl.when((k % steps_per_seq) == steps_per_seq - 1)
    def _final():
        for hd in range(M_HEADS):
            cf = c_s[hd]
            c_out_ref[0, hd] = cf[:, 0:M_DV]
            n_out_ref[0, hd:hd + 1, :] = jnp.transpose(cf[:, M_DV:2 * M_DV])[0:1, :]
        m_out_ref[0] = m_s[...]
        kk_ref[0] = set_b["kf"][tile - WINDOW:tile, :]
        vk_ref[0] = set_b["vf"][tile - WINDOW:tile, :]


def _rope_tables(pos):
    half = ROT_DIM // 2
    inv = ROPE_THETA ** (-jnp.arange(0, ROT_DIM, 2, dtype=F32) / ROT_DIM)
    ang = pos.astype(F32)[:, None] * inv
    cos, sin = jnp.cos(ang), jnp.sin(ang)
    t = pos.shape[0]
    zh = jnp.zeros((t, half), F32)
    rest = A_HEAD_DIM - ROT_DIM
    a = jnp.concatenate([cos, cos, jnp.ones((t, rest), F32)], axis=1)
    b = jnp.concatenate([zh, sin, jnp.zeros((t, rest), F32)], axis=1)
    cc = jnp.concatenate([-sin, zh, jnp.zeros((t, rest), F32)], axis=1)
    two = lambda v: jnp.concatenate([v, v], axis=1)
    return two(a), two(b), two(cc)


def _full(shape, single_buffer=False):
    mode = dict(pipeline_mode=pl.Buffered(1)) if single_buffer else {}
    return pl.BlockSpec(shape, lambda *_: (0,) * len(shape), **mode)


def _prompt_layer(x, mod, sinks, gn, gf, gmn, bias_g, wp, wmo, wao, wo):
    bsz, seq, d = x.shape
    tile = PROMPT_TILE
    assert seq % (2 * tile) == 0 and tile % (2 * M_CHUNK) == 0 and d == D_MODEL
    tiles_per_seq = seq // tile
    steps_per_seq = tiles_per_seq // 2
    n_tiles = bsz * tiles_per_seq
    n_steps = n_tiles // 2
    ra, rb, rc = _rope_tables(jnp.arange(seq, dtype=jnp.int32))
    xt = x.reshape(n_tiles, tile, d)
    xp = x.reshape(n_steps, 2 * tile, d)
    mod3 = mod.reshape(bsz, 1, 3 * d)

    tile_0 = lambda k: 0
    tile_a = lambda k: 2 * k + 1
    tile_b = lambda k: jnp.minimum(2 * k + 2, n_tiles - 1)
    x_spec = lambda f: pl.BlockSpec((1, tile, d), lambda k: (f(k), 0, 0))
    mod_spec = lambda f: pl.BlockSpec((1, 1, 3 * d), lambda k: (f(k) // tiles_per_seq, 0, 0))
    rope_spec = lambda f: pl.BlockSpec((tile, LANES), lambda k: (f(k) % tiles_per_seq, 0))
    in_specs = [
        pl.BlockSpec(memory_space=pltpu.SMEM),
        x_spec(tile_0), x_spec(tile_a), x_spec(tile_b),
        pl.BlockSpec((1, 2 * tile, d), lambda k: (k, 0, 0)),
        mod_spec(tile_0), mod_spec(tile_a), mod_spec(tile_b),
        _full((1, d)), _full((1, d)), _full((1, M_WIDTH)), _full((1, 2 * LANES)),
        _full((d, N_PACK), True), _full((M_WIDTH, d), True), _full((A_WIDTH, d), True), _full((d, d), True),
        rope_spec(tile_0), rope_spec(tile_0), rope_spec(tile_0),
        rope_spec(tile_a), rope_spec(tile_a), rope_spec(tile_a),
        rope_spec(tile_b), rope_spec(tile_b), rope_spec(tile_b),
    ]
    seq_of = lambda k: k // steps_per_seq
    out_specs = [
        pl.BlockSpec((1, 2 * tile, d), lambda k: (k, 0, 0)),
        pl.BlockSpec((1, M_HEADS, M_DK, M_DV), lambda k: (seq_of(k), 0, 0, 0)),
        pl.BlockSpec((1, M_HEADS, M_DK), lambda k: (seq_of(k), 0, 0)),
        pl.BlockSpec((1, 1, LANES), lambda k: (seq_of(k), 0, 0)),
        pl.BlockSpec((1, WINDOW, A_KV_WIDTH), lambda k: (seq_of(k), 0, 0)),
        pl.BlockSpec((1, WINDOW, A_KV_WIDTH), lambda k: (seq_of(k), 0, 0)),
    ]
    out_shape = [
        jax.ShapeDtypeStruct((n_steps, 2 * tile, d), F32),
        jax.ShapeDtypeStruct((bsz, M_HEADS, M_DK, M_DV), F32),
        jax.ShapeDtypeStruct((bsz, M_HEADS, M_DK), F32),
        jax.ShapeDtypeStruct((bsz, 1, LANES), F32),
        jax.ShapeDtypeStruct((bsz, WINDOW, A_KV_WIDTH), F32),
        jax.ShapeDtypeStruct((bsz, WINDOW, A_KV_WIDTH), F32),
    ]
    scratch = _set_shapes(tile) + _set_shapes(tile) + [
        pltpu.VMEM((2 * A_KV_HEADS, 2 * WINDOW, LANES), BF16),
        pltpu.VMEM((2 * A_KV_HEADS, 2 * WINDOW, LANES), BF16),
        pltpu.VMEM((M_HEADS, M_DK, 2 * LANES), F32),
        pltpu.VMEM((1, LANES), F32),
        pltpu.VMEM((tile, M_WIDTH), BF16),
        pltpu.VMEM((tile, A_WIDTH), BF16),
        pltpu.VMEM((tile, D_MODEL), BF16),
        pltpu.VMEM((tile, M_WIDTH), BF16),
        pltpu.VMEM((tile, A_WIDTH), BF16),
        pltpu.VMEM((tile, D_MODEL), BF16),
    ]
    outs = pl.pallas_call(
        functools.partial(_prompt_kernel, tile=tile, steps_per_seq=steps_per_seq),
        grid=(n_steps,),
        in_specs=in_specs,
        out_specs=out_specs,
        out_shape=out_shape,
        scratch_shapes=scratch,
        compiler_params=pltpu.CompilerParams(dimension_semantics=("arbitrary",), vmem_limit_bytes=VMEM_LIMIT),
        name="prompt_layer",
    )(sinks, xt, xt, xt, xp, mod3, mod3, mod3, gn, gf, gmn, bias_g, wp, wmo, wao, wo,
      ra, rb, rc, ra, rb, rc, ra, rb, rc)
    return (outs[0].reshape(bsz, seq, d),) + tuple(outs[1:])


def _sample_proj_kernel(x_ref, mod_ref, gn_ref, w_ref, z_ref, hb_s):
    bsz, steps, _ = x_ref.shape

    @pl.when(pl.program_id(0) == 0)
    def _norm():
        mod = mod_ref[...]
        for t in range(steps):
            h = _rms(x_ref[:, t, :]) * gn_ref[...]
            hb_s[t * bsz:(t + 1) * bsz, :] = (h * (1.0 + mod[:, D_MODEL:2 * D_MODEL]) + mod[:, 0:D_MODEL]).astype(BF16)

    z = _dot(hb_s[...], w_ref[...])
    for t in range(steps):
        z_ref[t] = z[t * bsz:(t + 1) * bsz, :]


def _sample_proj(x3, mod_s, gn, wp):
    bsz, steps, d = x3.shape
    tn = 512
    return pl.pallas_call(
        _sample_proj_kernel,
        grid=(N_PACK // tn,),
        in_specs=[
            _full((bsz, steps, d)),
            pl.BlockSpec((bsz, 2 * d), lambda j: (0, 0)),
            _full((1, d)),
            pl.BlockSpec((d, tn), lambda j: (0, j)),
        ],
        out_specs=pl.BlockSpec((steps, bsz, tn), lambda j: (0, 0, j)),
        out_shape=jax.ShapeDtypeStruct((steps, bsz, N_PACK), F32),
        scratch_shapes=[pltpu.VMEM((steps * bsz, d), BF16)],
        compiler_params=pltpu.CompilerParams(dimension_semantics=("arbitrary",)),
        name="sample_proj",
    )(x3, mod_s, gn, wp)


def _sample_state_kernel(sinks_ref, z_ref, c_ref, n_ref, m_ref, ck_ref, cv_ref, rope_ref, bg_ref, gmn_ref,
                         cn_ref, nn_ref, mn_ref, ym_ref, ya_ref, ko_ref, vo_ref, *, group, steps):
    T, G = steps, group
    toks = range(T)

    def z(t, lo, n):
        return z_ref[t, :, lo:lo + n]

    def rope_t(t, blk):
        return _rope(blk, rope_ref[0, t:t + 1, :], rope_ref[1, t:t + 1, :], rope_ref[2, t:t + 1, :])

    lane = lax.broadcasted_iota(jnp.int32, (G, LANES), 1)
    lo_half = lane < A_HEAD_DIM
    gmn = gmn_ref[...]

    q = [z(t, C_MQ, M_WIDTH) * (M_DK ** -0.5) for t in toks]
    k = [z(t, C_MK, M_WIDTH) for t in toks]
    v = [z(t, C_MV, M_WIDTH) for t in toks]
    og = [_sigmoid(z(t, C_MO, M_WIDTH)) * _silu(z(t, C_MZ, M_WIDTH)) for t in toks]
    qa = [[rope_t(t, z(t, C_AQ + j * LANES, LANES)) * (A_HEAD_DIM ** -0.5) for j in range(A_WIDTH // LANES)]
          for t in toks]
    kn = [rope_t(t, z(t, C_AK, A_KV_WIDTH)) for t in toks]
    vn = [z(t, C_AV, A_KV_WIDTH) for t in toks]
    sz = [_silu(z(t, C_AZ, A_WIDTH)) for t in toks]
    gt = [z(t, C_GI, 2 * LANES) + bg_ref[...] for t in toks]
    ig = [gt[t][:, 0:LANES] for t in toks]
    lf = [_log_sigmoid(gt[t][:, LANES:2 * LANES]) for t in toks]


    m_all = m_ref[...]
    rep = lambda a, h: jnp.broadcast_to(a[:, h:h + 1], (G, LANES))
    gates = []
    for hd in range(M_HEADS):
        ig_h = [rep(ig[t], hd) for t in toks]
        lf_h = [rep(lf[t], hd) for t in toks]
        m_old = rep(m_all, hd)
        bsum = [lf_h[0]]
        for t in range(1, T):
            bsum.append(bsum[-1] + lf_h[t])
        b_last = bsum[-1]
        d_end = [b_last - bsum[s] + ig_h[s] for s in toks]
        m_new = b_last + m_old
        for s in toks:
            m_new = jnp.maximum(m_new, d_end[s])
        mn_ref[:, hd:hd + 1] = m_new[:, 0:1]
        gd = dict(w_end=[jnp.exp(d_end[s] - m_new) for s in toks], decay=jnp.exp(b_last + m_old - m_new),
                  e=[], w_inter=[], emh=[])
        for t in toks:
            inter = bsum[t] + m_old
            dm = [bsum[t] - bsum[s] + ig_h[s] for s in range(t + 1)]
            m_hat = inter
            for s in range(t + 1):
                m_hat = jnp.maximum(m_hat, dm[s])
            gd["e"].append([jnp.exp(dm[s] - m_hat) for s in range(t + 1)])
            gd["w_inter"].append(jnp.exp(inter - m_hat))
            gd["emh"].append(jnp.exp(-m_hat))
        gates.append(gd)

    heads = [slice(hd * M_DK, (hd + 1) * M_DK) for hd in range(M_HEADS)]
    n_old = [n_ref[:, hd, :] for hd in range(M_HEADS)]
    qk = {(hd, t, s): jnp.sum(q[t][:, heads[hd]] * k[s][:, heads[hd]], axis=-1, keepdims=True)
          for hd in range(M_HEADS) for t in toks for s in range(t + 1)}
    qn = {(hd, t): jnp.sum(q[t][:, heads[hd]] * n_old[hd], axis=-1, keepdims=True)
          for hd in range(M_HEADS) for t in toks}

    lo_f = jnp.where(lo_half, 1.0, 0.0)
    hi_f = 1.0 - lo_f
    dup = lambda a, kv: (jnp.where(lo_half, a, pltpu.roll(a, A_HEAD_DIM, 1)) if kv == 0 else
                         jnp.where(lo_half, pltpu.roll(a, A_HEAD_DIM, 1), a))
    kn_x = [[dup(kn[s], kv) for s in toks] for kv in range(A_KV_HEADS)]
    vn_x = [[dup(vn[s], kv) for s in toks] for kv in range(A_KV_HEADS)]
    s_new = {}
    for kv in range(A_KV_HEADS):
        for blk in range(2):
            for t in toks:
                for s in range(t + 1):
                    prod = qa[t][2 * kv + blk] * kn_x[kv][s]
                    s_new[kv, 0, blk, t, s] = jnp.sum(prod * lo_f, axis=-1, keepdims=True)
                    s_new[kv, 1, blk, t, s] = jnp.sum(prod * hi_f, axis=-1, keepdims=True)

    rows_tg = lax.broadcasted_iota(jnp.int32, (T * G, LANES), 0) % G
    own_rows = [rows_tg == b for b in range(G)]
    rows64 = lax.broadcasted_iota(jnp.int32, (2 * T * G, LANES), 0) % G

    qc = []
    for hd in range(M_HEADS):
        hs = heads[hd]
        gd = gates[hd]
        q32 = jnp.concatenate([q[t][:, hs] for t in toks], axis=0)
        kw = [k[s][:, hs] * gd["w_end"][s] for s in toks]
        kw_t = jnp.transpose(jnp.concatenate(kw, axis=0)).astype(BF16)
        v32 = jnp.concatenate([v[s][:, hs] for s in toks], axis=0)
        acc = None
        for b in range(G):
            c_old = c_ref[b, hd]
            part = _dot(jnp.where(own_rows[b], q32, 0.0).astype(BF16), c_old.astype(BF16))
            acc = part if acc is None else acc + part
            upd = _dot(kw_t, jnp.where(own_rows[b], v32, 0.0).astype(BF16))
            cn_ref[b, hd] = gd["decay"][b:b + 1, 0:1] * c_old + upd
        qc.append(acc)
        nn_ref[:, hd, :] = gd["decay"] * n_old[hd] + (kw[0] + kw[1] + kw[2] + kw[3])

    zeros_kt = jnp.zeros((A_HEAD_DIM, WINDOW), BF16)
    sc = {}
    for kv in range(A_KV_HEADS):
        l64 = jnp.concatenate([qa[t][2 * kv + blk] for blk in range(2) for t in toks], axis=0)
        own64 = [jnp.where(rows64 == b, l64, 0.0).astype(BF16) for b in range(G)]
        kt = [ck_ref[b, kv].astype(BF16) for b in range(G)]
        for par in range(2):
            acc = None
            for b in range(G):
                rhs = jnp.concatenate([kt[b], zeros_kt] if par == 0 else [zeros_kt, kt[b]], axis=0)
                part = _dot(own64[b], rhs)
                acc = part if acc is None else acc + part
            sc[kv, par] = acc

    hh = {}
    for hd in range(M_HEADS):
        gd = gates[hd]
        for t in toks:
            sv, ssum = None, None
            for s in range(t + 1):
                s_ts = qk[hd, t, s] * gd["e"][t][s]
                sv = s_ts * v[s][:, heads[hd]] if sv is None else sv + s_ts * v[s][:, heads[hd]]
                ssum = s_ts if ssum is None else ssum + s_ts
            wi = gd["w_inter"][t]
            num = sv + wi * qc[hd][t * G:(t + 1) * G, :]
            den = ssum + wi * qn[hd, t]
            hh[hd, t] = num / jnp.maximum(jnp.abs(den), gd["emh"][t])
    ms = {key: jnp.mean(val * val, axis=-1, keepdims=True) for key, val in hh.items()}
    for (hd, t), val in hh.items():
        hs = heads[hd]
        ym_ref[t, :, hs] = val * lax.rsqrt(ms[hd, t] + NORM_EPS) * gmn[:, hs] * og[t][:, hs]

    items = [(kv, par, blk, t) for kv in range(A_KV_HEADS) for par in range(2) for blk in range(2) for t in toks]
    s_c = {it: jnp.where(lane > it[3], sc[it[0], it[1]][(it[2] * T + it[3]) * G:(it[2] * T + it[3] + 1) * G, :], -jnp.inf)
           for it in items}
    mx_c = {it: jnp.max(s_c[it], axis=-1, keepdims=True) for it in items}
    p_c, p_n, sink_t = {}, {}, {}
    for it in items:
        kv, par, blk, t = it
        sink = sinks_ref[A_GROUP * kv + 2 * blk + par]
        mx = jnp.maximum(mx_c[it], sink)
        for s in range(t + 1):
            mx = jnp.maximum(mx, s_new[kv, par, blk, t, s])
        p_c[it] = jnp.exp(s_c[it] - mx)
        p_n[it] = [jnp.exp(s_new[kv, par, blk, t, s] - mx) for s in range(t + 1)]
        sink_t[it] = jnp.exp(sink - mx)
    sum_c = {it: jnp.sum(p_c[it], axis=-1, keepdims=True) for it in items}
    fresh = {}
    probs = {(kv, par): [] for kv in range(A_KV_HEADS) for par in range(2)}
    for it in items:
        kv, par, blk, t = it
        den = sum_c[it] + sink_t[it]
        for s in range(t + 1):
            den = den + p_n[it][s]
        r = 1.0 / den
        probs[kv, par].append(p_c[it] * r)
        acc = None
        for s in range(t + 1):
            term = (p_n[it][s] * r) * vn_x[kv][s]
            acc = term if acc is None else acc + term
        fresh[it] = acc

    for kv in range(A_KV_HEADS):
        vt = [cv_ref[b, kv].astype(BF16) for b in range(G)]
        o = []
        for par in range(2):
            pstack = jnp.concatenate(probs[kv, par], axis=0)
            acc = None
            for b in range(G):
                rhs = jnp.concatenate([vt[b], zeros_kt] if par == 0 else [zeros_kt, vt[b]], axis=0)
                part = _dot_nt(jnp.where(rows64 == b, pstack, 0.0).astype(BF16), rhs)
                acc = part if acc is None else acc + part
            o.append(acc)
        for blk in range(2):
            cs = slice((2 * kv + blk) * LANES, (2 * kv + blk + 1) * LANES)
            for t in toks:
                rs = slice((blk * T + t) * G, (blk * T + t + 1) * G)
                new_v = jnp.where(lo_half, fresh[kv, 0, blk, t], fresh[kv, 1, blk, t])
                ya_ref[t, :, cs] = (o[0][rs, :] + o[1][rs, :] + new_v) * sz[t][:, cs]

    n_rows = T * G
    src = lax.broadcasted_iota(jnp.int32, (n_rows, n_rows), 1)
    dst = lax.broadcasted_iota(jnp.int32, (n_rows, n_rows), 0)
    perm = jnp.where(src == (dst % T) * G + dst // T, 1.0, 0.0).astype(BF16)
    lane_d = lax.broadcasted_iota(jnp.int32, (A_HEAD_DIM, WINDOW), 1)
    for new, cache_ref, out_ref in ((kn, ck_ref, ko_ref), (vn, cv_ref, vo_ref)):
        stack = _dot_exact01(perm, jnp.concatenate(new, axis=0))
        stack = jnp.concatenate([stack, jnp.zeros((LANES - n_rows, LANES), F32)], axis=0)
        new_t = jnp.transpose(stack)
        for b in range(G):
            placed = pltpu.roll(new_t, (WINDOW - T - T * b) % LANES, 1)
            for kv in range(A_KV_HEADS):
                slid = pltpu.roll(cache_ref[b, kv], WINDOW - T, 1)
                out_ref[b, kv] = jnp.where(lane_d >= WINDOW - T, placed[kv * A_HEAD_DIM:(kv + 1) * A_HEAD_DIM, :], slid)


def _sample_state(zs, state_c, state_n, state_m, ck_t, cv_t, rope, bias_g, gmn, sinks):
    steps, bsz, _ = zs.shape
    group = SAMPLE_GROUP
    assert bsz % group == 0
    in_specs = [
        pl.BlockSpec(memory_space=pltpu.SMEM),
        pl.BlockSpec((steps, group, N_PACK), lambda i: (0, i, 0)),
        pl.BlockSpec((group, M_HEADS, M_DK, M_DV), lambda i: (i, 0, 0, 0)),
        pl.BlockSpec((group, M_HEADS, M_DK), lambda i: (i, 0, 0)),
        pl.BlockSpec((group, M_HEADS), lambda i: (i, 0)),
        pl.BlockSpec((group, A_KV_HEADS, A_HEAD_DIM, WINDOW), lambda i: (i, 0, 0, 0)),
        pl.BlockSpec((group, A_KV_HEADS, A_HEAD_DIM, WINDOW), lambda i: (i, 0, 0, 0)),
        _full((3, steps, LANES)), _full((1, 2 * LANES)), _full((1, M_WIDTH)),
    ]
    out_specs = [
        pl.BlockSpec((group, M_HEADS, M_DK, M_DV), lambda i: (i, 0, 0, 0)),
        pl.BlockSpec((group, M_HEADS, M_DK), lambda i: (i, 0, 0)),
        pl.BlockSpec((group, M_HEADS), lambda i: (i, 0)),
        pl.BlockSpec((steps, group, M_WIDTH), lambda i: (0, i, 0)),
        pl.BlockSpec((steps, group, A_WIDTH), lambda i: (0, i, 0)),
        pl.BlockSpec((group, A_KV_HEADS, A_HEAD_DIM, WINDOW), lambda i: (i, 0, 0, 0)),
        pl.BlockSpec((group, A_KV_HEADS, A_HEAD_DIM, WINDOW), lambda i: (i, 0, 0, 0)),
    ]
    out_shape = [
        jax.ShapeDtypeStruct((bsz, M_HEADS, M_DK, M_DV), F32),
        jax.ShapeDtypeStruct((bsz, M_HEADS, M_DK), F32),
        jax.ShapeDtypeStruct((bsz, M_HEADS), F32),
        jax.ShapeDtypeStruct((steps, bsz, M_WIDTH), F32),
        jax.ShapeDtypeStruct((steps, bsz, A_WIDTH), F32),
        jax.ShapeDtypeStruct((bsz, A_KV_HEADS, A_HEAD_DIM, WINDOW), F32),
        jax.ShapeDtypeStruct((bsz, A_KV_HEADS, A_HEAD_DIM, WINDOW), F32),
    ]
    return pl.pallas_call(
        functools.partial(_sample_state_kernel, group=group, steps=steps),
        grid=(bsz // group,),
        in_specs=in_specs,
        out_specs=out_specs,
        out_shape=out_shape,
        compiler_params=pltpu.CompilerParams(dimension_semantics=("arbitrary",), vmem_limit_bytes=VMEM_LIMIT),
        name="sample_state",
    )(sinks, zs, state_c, state_n, state_m, ck_t, cv_t, rope, bias_g, gmn)


def _sample_out_kernel(x_ref, gate_ref, z_ref, ym_ref, ya_ref, wmo_ref, wao_ref, wo_ref, gf_ref, y_ref):
    bsz, steps, _ = x_ref.shape
    stack = lambda ref, cs: jnp.concatenate([ref[t, :, cs] for t in range(steps)], axis=0)
    full = slice(None)
    x = jnp.concatenate([x_ref[:, t, :] for t in range(steps)], axis=0)
    gate = jnp.concatenate([gate_ref[...]] * steps, axis=0)
    y = _out_stage(x, gate, _sigmoid(stack(z_ref, slice(0, D_MODEL))), _sigmoid(stack(z_ref, slice(D_MODEL, 2 * D_MODEL))),
                   stack(ym_ref, full).astype(BF16), stack(ya_ref, full).astype(BF16), wmo_ref, wao_ref, wo_ref, gf_ref[...])
    for t in range(steps):
        y_ref[:, t, :] = y[t * bsz:(t + 1) * bsz, :]


def _sample_out(x3, mod_s, zs, ym, ya, wmo, wao, wo, gf):
    bsz, steps, d = x3.shape
    return pl.pallas_call(
        _sample_out_kernel,
        grid=(1,),
        in_specs=[
            _full((bsz, steps, d)),
            pl.BlockSpec((bsz, d), lambda i: (0, 2)),
            pl.BlockSpec((steps, bsz, 2 * d), lambda i: (0, 0, 0)),
            _full((steps, bsz, M_WIDTH)), _full((steps, bsz, A_WIDTH)),
            _full((M_WIDTH, d)), _full((A_WIDTH, d)), _full((d, d)), _full((1, d)),
        ],
        out_specs=_full((bsz, steps, d)),
        out_shape=jax.ShapeDtypeStruct((bsz, steps, d), F32),
        compiler_params=pltpu.CompilerParams(dimension_semantics=("arbitrary",), vmem_limit_bytes=VMEM_LIMIT),
        name="sample_out",
    )(x3, mod_s, zs, ym, ya, wmo, wao, wo, gf)


def _pack_w_in(w_in):
    parts, start = [], 0
    for size in IN_SIZES:
        parts.append(w_in[:, start:start + size])
        start += size
    mq, mk, mv, mi, mf, mo, mz, aq, ak, av, az, gm, ga = parts
    pad = jnp.zeros((w_in.shape[0], LANES - M_HEADS), w_in.dtype)
    return jnp.concatenate([gm, ga, mq, mk, mv, mo, mz, aq, az, ak, av, mi, pad, mf, pad], axis=1).astype(BF16)


def kernel(x_prompt, x_sample, state_C, state_n, state_m, cache_k, cache_v, c_prompt, c_sample, w_ada, b_ada, g_norm, w_in, b_igate, b_fgate, g_mnorm, sinks, w_m_out, w_a_out, w_out, g_final):
    assert w_in.shape[0] == 1, "single-layer step"
    bsz_s, steps, d = x_sample.shape
    assert cache_k.shape[2] == WINDOW

    mod_p, mod_s = _adaln(c_prompt, c_sample, w_ada[0], b_ada[0])
    wp = _pack_w_in(w_in[0])
    wmo = w_m_out[0].astype(BF16)
    wao = w_a_out[0].astype(BF16)
    wo = w_out[0].astype(BF16)
    gn = g_norm[0].reshape(1, d)
    gf = g_final.reshape(1, d)
    gmn = g_mnorm[0].reshape(1, M_WIDTH)
    zpad = jnp.zeros((LANES - M_HEADS,), F32)
    bias_g = jnp.concatenate([b_igate[0], zpad, b_fgate[0], zpad]).reshape(1, 2 * LANES)
    sk = sinks[0]

    y_p, c_p, n_p, m_p, k_p, v_p = _prompt_layer(x_prompt, mod_p, sk, gn, gf, gmn, bias_g, wp, wmo, wao, wo)
    bsz_p = x_prompt.shape[0]
    m_p = m_p[:, 0, 0:M_HEADS]
    k_p = k_p.reshape(bsz_p, WINDOW, A_KV_HEADS, A_HEAD_DIM)
    v_p = v_p.reshape(bsz_p, WINDOW, A_KV_HEADS, A_HEAD_DIM)

    zs = _sample_proj(x_sample, mod_s, gn, wp)
    rope = jnp.stack(_rope_tables(PAST_LEN + jnp.arange(steps, dtype=jnp.int32)))
    dims_keys = lambda c: jnp.transpose(c[0], (0, 2, 3, 1))
    c_s, n_s, m_s, ym, ya, k_t, v_t = _sample_state(zs, state_C[0], state_n[0], state_m[0], dims_keys(cache_k),
                                                    dims_keys(cache_v), rope, bias_g, gmn, sk)
    y_s = _sample_out(x_sample, mod_s, zs, ym, ya, wmo, wao, wo, gf)
    keys_dims = lambda c: jnp.transpose(c, (0, 3, 1, 2))[None]

    return (y_p, y_s, c_p[None], n_p[None], m_p[None], k_p[None], v_p[None],
            c_s[None], n_s[None], m_s[None], keys_dims(k_t), keys_dims(v_t))
```

```python
import functools

import jax
import jax.numpy as jnp
from jax import lax
from jax.experimental import pallas as pl
from jax.experimental.pallas import tpu as pltpu

F32 = jnp.float32
BF16 = jnp.bfloat16

D_MODEL = 1024
M_HEADS = 4
M_DK = 128
M_DV = 128
M_WIDTH = M_HEADS * M_DV
M_CHUNK = 128
A_HEADS = 8
A_KV_HEADS = 2
A_GROUP = A_HEADS // A_KV_HEADS
A_HEAD_DIM = 64
A_WIDTH = A_HEADS * A_HEAD_DIM
A_KV_WIDTH = A_KV_HEADS * A_HEAD_DIM
WINDOW = 128
ROT_DIM = A_HEAD_DIM // 4
ROPE_THETA = 500000.0
NORM_EPS = 1e-6
PAST_LEN = 16384
IN_SIZES = (M_HEADS * M_DK, M_HEADS * M_DK, M_WIDTH, M_HEADS, M_HEADS, M_WIDTH, M_WIDTH,
            A_WIDTH, A_KV_WIDTH, A_KV_WIDTH, A_WIDTH, D_MODEL, D_MODEL)

LANES = 128

C_GM = 0
C_GA = C_GM + D_MODEL
C_MQ = C_GA + D_MODEL
C_MK = C_MQ + M_WIDTH
C_MV = C_MK + M_WIDTH
C_MO = C_MV + M_WIDTH
C_MZ = C_MO + M_WIDTH
C_AQ = C_MZ + M_WIDTH
C_AZ = C_AQ + A_WIDTH
C_AK = C_AZ + A_WIDTH
C_AV = C_AK + A_KV_WIDTH
C_GI = C_AV + A_KV_WIDTH
C_GF = C_GI + LANES
N_PACK = C_GF + LANES

PROMPT_TILE = 256
SAMPLE_GROUP = 8
VMEM_LIMIT = 56 * 1024 * 1024


def _sigmoid(x):
    return 1.0 / (1.0 + jnp.exp(-x))


def _silu(x):
    return x * _sigmoid(x)


def _log_sigmoid(x):
    return jnp.minimum(x, 0.0) - jnp.log1p(jnp.exp(-jnp.abs(x)))


def _dot(a, b):
    return jnp.dot(a, b, preferred_element_type=F32)


def _dot_nt(a, b):
    return lax.dot_general(a, b, (((1,), (1,)), ((), ())), preferred_element_type=F32)


def _dot_tn(a, b):
    return lax.dot_general(a, b, (((0,), (0,)), ((), ())), preferred_element_type=F32)


def _dot_exact01(m01, x):
    x1 = x.astype(BF16)
    r1 = x - x1.astype(F32)
    x2 = r1.astype(BF16)
    x3 = (r1 - x2.astype(F32)).astype(BF16)
    return _dot(m01, x1) + _dot(m01, x2) + _dot(m01, x3)


def _rms(x):
    return x * lax.rsqrt(jnp.mean(x * x, axis=-1, keepdims=True) + NORM_EPS)


def _rope(blk, ra, rb, rc):
    return blk * ra + pltpu.roll(blk, 8, 1) * rb + pltpu.roll(blk, LANES - 8, 1) * rc


def _out_stage(x, gate, sgm, sga, ym, ya, wmo_ref, wao_ref, wo_ref, gf):
    pm = _dot(ym, wmo_ref[...])
    pa = _dot(ya, wao_ref[...])
    u = sgm * pm + sga * pa
    r = _dot(u.astype(BF16), wo_ref[...])
    return _rms(x + gate * r) * gf


def _adaln_kernel(cp_ref, cs_ref, w_ref, b_ref, op_ref, os_ref):
    w = w_ref[...].astype(BF16)
    b = b_ref[...]
    op_ref[...] = _dot(_silu(cp_ref[...]).astype(BF16), w) + b
    os_ref[...] = _dot(_silu(cs_ref[...]).astype(BF16), w) + b


def _adaln(c_p, c_s, w_ada, b_ada):
    bp, d = c_p.shape
    bs = c_s.shape[0]
    n = w_ada.shape[1]
    tn = 512
    return pl.pallas_call(
        _adaln_kernel,
        grid=(n // tn,),
        in_specs=[
            pl.BlockSpec((bp, d), lambda j: (0, 0)),
            pl.BlockSpec((bs, d), lambda j: (0, 0)),
            pl.BlockSpec((d, tn), lambda j: (0, j)),
            pl.BlockSpec((1, tn), lambda j: (0, j)),
        ],
        out_specs=[
            pl.BlockSpec((bp, tn), lambda j: (0, j)),
            pl.BlockSpec((bs, tn), lambda j: (0, j)),
        ],
        out_shape=[jax.ShapeDtypeStruct((bp, n), F32), jax.ShapeDtypeStruct((bs, n), F32)],
        compiler_params=pltpu.CompilerParams(dimension_semantics=("arbitrary",)),
        name="adaln",
    )(c_p, c_s, w_ada, b_ada.reshape(1, n))


_SET_FIELDS = ("hb", "q", "k", "vaug", "og", "qa", "kf", "vf", "sz", "sgm", "sga", "g")


def _set_shapes(tile):
    return [
        pltpu.VMEM((tile, D_MODEL), BF16),
        pltpu.VMEM((tile, M_WIDTH), BF16),
        pltpu.VMEM((tile, M_WIDTH), F32),
        pltpu.VMEM((M_HEADS, tile, 2 * LANES), BF16),
        pltpu.VMEM((tile, M_WIDTH), F32),
        pltpu.VMEM((tile, A_WIDTH), BF16),
        pltpu.VMEM((tile, A_KV_WIDTH), F32),
        pltpu.VMEM((tile, A_KV_WIDTH), F32),
        pltpu.VMEM((tile, A_WIDTH), F32),
        pltpu.VMEM((tile, D_MODEL), F32),
        pltpu.VMEM((tile, D_MODEL), F32),
        pltpu.VMEM((tile, 2 * LANES), F32),
    ]


def _project_pieces(x_ref, mod_ref, rope_refs, st, gn_ref, bg_ref, w_ref):
    def norm():
        mod = mod_ref[0]
        h = _rms(x_ref[0]) * gn_ref[...]
        st["hb"][...] = (h * (1.0 + mod[:, D_MODEL:2 * D_MODEL]) + mod[:, 0:D_MODEL]).astype(BF16)

    def proj(lo, n):
        return _dot(st["hb"][...], w_ref[:, lo:lo + n])

    def rope(v):
        ra_ref, rb_ref, rc_ref = rope_refs
        return _rope(v, ra_ref[...], rb_ref[...], rc_ref[...])

    def gates_q():
        st["g"][...] = proj(C_GI, 2 * LANES) + bg_ref[...]
        st["q"][...] = (proj(C_MQ, M_WIDTH) * (M_DK ** -0.5)).astype(BF16)

    def keys():
        st["k"][...] = proj(C_MK, M_WIDTH)

    def values():
        zv = proj(C_MV, M_WIDTH)
        for hd in range(M_HEADS):
            st["vaug"][hd, :, 0:LANES] = zv[:, hd * M_DV:(hd + 1) * M_DV].astype(BF16)

    def attn_kv():
        st["kf"][...] = rope(proj(C_AK, A_KV_WIDTH))
        st["vf"][...] = proj(C_AV, A_KV_WIDTH)

    def attn_q():
        za = proj(C_AQ, A_WIDTH)
        for j in range(A_WIDTH // LANES):
            blk = rope(za[:, j * LANES:(j + 1) * LANES])
            st["qa"][:, j * LANES:(j + 1) * LANES] = (blk * (A_HEAD_DIM ** -0.5)).astype(BF16)

    def out_gates():
        st["og"][...] = _sigmoid(proj(C_MO, M_WIDTH)) * _silu(proj(C_MZ, M_WIDTH))

    def attn_gate():
        st["sz"][...] = _silu(proj(C_AZ, A_WIDTH))

    def merge_gate(name, lo, part):
        def piece():
            cs = slice(part * M_WIDTH, (part + 1) * M_WIDTH)
            st[name][:, cs] = _sigmoid(proj(lo + part * M_WIDTH, M_WIDTH))
        return piece

    return [norm, gates_q, keys, values, attn_kv, attn_q, out_gates, attn_gate,
            merge_gate("sgm", C_GM, 0), merge_gate("sgm", C_GM, 1),
            merge_gate("sga", C_GA, 0), merge_gate("sga", C_GA, 1)]


def _consume_pieces(st, first_tile, sinks_ref, gmn_ref, kbuf_s, vbuf_s, c_s, m_s, ym_s, ya_s, tile):
    L = M_CHUNK
    heads = [slice(hd * M_DK, (hd + 1) * M_DK) for hd in range(M_HEADS)]
    a_heads = [(kv, bi, half) for kv in range(A_KV_HEADS) for bi in range(2) for half in range(2)]

    def causal_mask():
        row = lax.broadcasted_iota(jnp.int32, (L, L), 0)
        col = lax.broadcasted_iota(jnp.int32, (L, L), 1)
        return row >= col

    def chunk_stages(j):
        rs = slice(j * L, (j + 1) * L)
        cx = {}

        def gates():
            g = st["g"][rs, :]
            ig_all = g[:, 0:LANES]
            lf_all = _log_sigmoid(g[:, LANES:2 * LANES])
            tril = jnp.where(causal_mask(), 1.0, 0.0).astype(BF16)
            b_all = _dot_exact01(tril, lf_all)
            b_last = b_all[L - 1:L, :]
            m_old = m_s[...]
            d_end = b_last - b_all + ig_all
            m_new = jnp.maximum(b_last + m_old, jnp.max(d_end, axis=0, keepdims=True))
            m_s[...] = m_new
            cx.update(b_all=b_all, w_end=jnp.exp(d_end - m_new), decay=jnp.exp(b_last + m_old - m_new),
                      inter=b_all + m_old, r_t=jnp.transpose(ig_all - b_all))

        def kv_buffers():
            srows = slice((j % 2) * L, (j % 2 + 1) * L)
            lo_half = lax.broadcasted_iota(jnp.int32, (L, L), 1) < A_HEAD_DIM
            kc = st["kf"][rs, :]
            vc = st["vf"][rs, :]
            kr = pltpu.roll(kc, A_HEAD_DIM, 1)
            vr = pltpu.roll(vc, A_HEAD_DIM, 1)
            kbuf_s[0, srows, :] = jnp.where(lo_half, kc, 0.0).astype(BF16)
            kbuf_s[1, srows, :] = jnp.where(lo_half, 0.0, kr).astype(BF16)
            kbuf_s[2, srows, :] = jnp.where(lo_half, kr, 0.0).astype(BF16)
            kbuf_s[3, srows, :] = jnp.where(lo_half, 0.0, kc).astype(BF16)
            vbuf_s[0, srows, :] = jnp.where(lo_half, vc, 0.0).astype(BF16)
            vbuf_s[1, srows, :] = jnp.where(lo_half, 0.0, vr).astype(BF16)
            vbuf_s[2, srows, :] = jnp.where(lo_half, vr, 0.0).astype(BF16)
            vbuf_s[3, srows, :] = jnp.where(lo_half, 0.0, vc).astype(BF16)

        def decay_matrices():
            cx["dmat"] = [jnp.where(causal_mask(), cx["b_all"][:, hd:hd + 1] + cx["r_t"][hd:hd + 1, :], -jnp.inf)
                          for hd in range(M_HEADS)]
            cx["m_hat"] = [jnp.maximum(cx["inter"][:, hd:hd + 1], jnp.max(cx["dmat"][hd], axis=-1, keepdims=True))
                           for hd in range(M_HEADS)]
            cx["kw"] = [(st["k"][rs, heads[hd]] * cx["w_end"][:, hd:hd + 1]).astype(BF16) for hd in range(M_HEADS)]

        def state_matmuls():
            cx["qk"], cx["qc"] = [], []
            for hd in range(M_HEADS):
                qh = st["q"][rs, heads[hd]]
                c_old = c_s[hd]
                cx["qk"].append(_dot_nt(qh, st["k"][rs, heads[hd]].astype(BF16)))
                cx["qc"].append(_dot(qh, c_old.astype(BF16)))
                va = st["vaug"][hd, rs, :]
                c_s[hd] = cx["decay"][:, hd:hd + 1] * c_old + _dot_tn(cx["kw"][hd], va)

        def attn_scores():
            row2 = lax.broadcasted_iota(jnp.int32, (L, 2 * L), 0)
            col2 = lax.broadcasted_iota(jnp.int32, (L, 2 * L), 1)
            no_prev = jnp.where(first_tile, 4 * L, 0) if j == 0 else 0
            if j % 2 == 0:
                mask = ((col2 < L) & (col2 <= row2)) | ((col2 >= L) & (col2 - L > row2 + no_prev))
            else:
                mask = ((col2 >= L) & (col2 - L <= row2)) | ((col2 < L) & (col2 > row2 + no_prev))
            cx["sh"] = {}
            for kv in range(A_KV_HEADS):
                qcat = jnp.concatenate([st["qa"][rs, (2 * kv) * LANES:(2 * kv + 1) * LANES],
                                        st["qa"][rs, (2 * kv + 1) * LANES:(2 * kv + 2) * LANES]], axis=0)
                for half in range(2):
                    sc = _dot_nt(qcat, kbuf_s[2 * kv + half])
                    for bi in range(2):
                        cx["sh"][kv, bi, half] = jnp.where(mask, sc[bi * L:(bi + 1) * L, :], -jnp.inf)

        def attn_max():
            cx["mx"] = {hk: jnp.maximum(jnp.max(cx["sh"][hk], axis=-1, keepdims=True),
                                        sinks_ref[A_GROUP * hk[0] + 2 * hk[1] + hk[2]]) for hk in a_heads}

        def stabilised_scores():
            cx["s"] = [(cx["qk"][hd] * jnp.exp(cx["dmat"][hd] - cx["m_hat"][hd])).astype(BF16)
                       for hd in range(M_HEADS)]

        def score_value_matmuls():
            cx["sv"] = [_dot(cx["s"][hd], st["vaug"][hd, rs, :]) for hd in range(M_HEADS)]

        def attn_exp():
            cx["p"] = {hk: jnp.exp(cx["sh"][hk] - cx["mx"][hk]) for hk in a_heads}
            cx["den"] = {hk: jnp.sum(cx["p"][hk], axis=-1, keepdims=True)
                         + jnp.exp(sinks_ref[A_GROUP * hk[0] + 2 * hk[1] + hk[2]] - cx["mx"][hk]) for hk in a_heads}

        def attn_normalise():
            cx["pn"] = {hk: (cx["p"][hk] / cx["den"][hk]).astype(BF16) for hk in a_heads}

        def attn_value_matmuls():
            for kv in range(A_KV_HEADS):
                for bi in range(2):
                    o = (_dot(cx["pn"][kv, bi, 0], vbuf_s[2 * kv]) + _dot(cx["pn"][kv, bi, 1], vbuf_s[2 * kv + 1]))
                    cs = slice((2 * kv + bi) * LANES, (2 * kv + bi + 1) * LANES)
                    ya_s[rs, cs] = (o * st["sz"][rs, cs]).astype(BF16)

        def mlstm_outputs():
            hh = []
            for hd in range(M_HEADS):
                m_hat = cx["m_hat"][hd]
                w_inter = jnp.exp(cx["inter"][:, hd:hd + 1] - m_hat)
                tot = cx["sv"][hd] + w_inter * cx["qc"][hd]
                hh.append(tot[:, 0:M_DV] / jnp.maximum(jnp.abs(tot[:, M_DV:M_DV + 1]), jnp.exp(-m_hat)))
            ms = [jnp.mean(h * h, axis=-1, keepdims=True) for h in hh]
            for hd in range(M_HEADS):
                hs = heads[hd]
                ym_s[rs, hs] = (hh[hd] * lax.rsqrt(ms[hd] + NORM_EPS) * gmn_ref[:, hs] * st["og"][rs, hs]).astype(BF16)

        mlstm = [gates, decay_matrices, state_matmuls, stabilised_scores, score_value_matmuls, mlstm_outputs]
        attn = [kv_buffers, attn_scores, attn_max, attn_exp, attn_normalise, attn_value_matmuls]
        return mlstm, attn

    assert tile == 2 * L
    (m0, a0), (m1, a1) = chunk_stages(0), chunk_stages(1)
    merged = []
    for m_stage, a_stage in zip(m0 + m1, a0 + a1):
        merged += [m_stage, a_stage]
    return merged


def _output_pieces(st, x_ref, rows, gate_ref, ym_s, ya_s, u_s, wmo_ref, wao_ref, wo_ref, gf_ref, y_ref):
    def merge():
        u = st["sgm"][...] * _dot(ym_s[...], wmo_ref[...]) + st["sga"][...] * _dot(ya_s[...], wao_ref[...])
        u_s[...] = u.astype(BF16)

    def out():
        gate = gate_ref[0][:, 2 * D_MODEL:3 * D_MODEL]
        r = _dot(u_s[...], wo_ref[...])
        y_ref[0, rows, :] = _rms(x_ref[0, rows, :] + gate * r) * gf_ref[...]

    return [merge, out]


def _interleave(primary, filler):
    done = 0
    for i, piece in enumerate(primary):
        piece()
        upto = ((i + 1) * len(filler)) // len(primary)
        for f in filler[done:upto]:
            f()
        done = upto


def _prompt_kernel(sinks_ref, x0_ref, xa_ref, xb_ref, xres_ref, mod0_ref, moda_ref, modb_ref,
                   gn_ref, gf_ref, gmn_ref, bg_ref, w_ref, wmo_ref, wao_ref, wo_ref,
                   ra0_ref, rb0_ref, rc0_ref, raa_ref, rba_ref, rca_ref, rab_ref, rbb_ref, rcb_ref,
                   y_ref, c_out_ref, n_out_ref, m_out_ref, kk_ref, vk_ref, *scratch, tile, steps_per_seq):
    nset = len(_SET_FIELDS)
    set_a = dict(zip(_SET_FIELDS, scratch[0:nset]))
    set_b = dict(zip(_SET_FIELDS, scratch[nset:2 * nset]))
    kbuf_s, vbuf_s, c_s, m_s, yma_s, yaa_s, ua_s, ymb_s, yab_s, ub_s = scratch[2 * nset:]
    k = pl.program_id(0)
    first = (k % steps_per_seq) == 0
    project = lambda x_ref, mod_ref, rope, st: _project_pieces(x_ref, mod_ref, rope, st, gn_ref, bg_ref, w_ref)

    @pl.when(k == 0)
    def _prologue():
        lane = lax.broadcasted_iota(jnp.int32, (tile, LANES), 1)
        ones_col = jnp.where(lane == 0, 1.0, 0.0).astype(BF16)
        for st in (set_a, set_b):
            for hd in range(M_HEADS):
                st["vaug"][hd, :, LANES:2 * LANES] = ones_col
        for piece in project(x0_ref, mod0_ref, (ra0_ref, rb0_ref, rc0_ref), set_a):
            piece()

    @pl.when(first)
    def _init():
        c_s[...] = jnp.zeros_like(c_s)
        m_s[...] = jnp.zeros_like(m_s)
        kbuf_s[...] = jnp.zeros_like(kbuf_s)
        vbuf_s[...] = jnp.zeros_like(vbuf_s)

    state = (sinks_ref, gmn_ref, kbuf_s, vbuf_s, c_s, m_s)
    out_w = (wmo_ref, wao_ref, wo_ref, gf_ref, y_ref)
    _interleave(_consume_pieces(set_a, first, *state, yma_s, yaa_s, tile),
                project(xa_ref, moda_ref, (raa_ref, rba_ref, rca_ref), set_b))
    _interleave(_consume_pieces(set_b, False, *state, ymb_s, yab_s, tile),
                _output_pieces(set_a, xres_ref, slice(0, tile), moda_ref, yma_s, yaa_s, ua_s, *out_w)
                + project(xb_ref, modb_ref, (rab_ref, rbb_ref, rcb_ref), set_a))
    for piece in _output_pieces(set_b, xres_ref, slice(tile, 2 * tile), moda_ref, ymb_s, yab_s, ub_s, *out_w):
        piece()

    @pl.when((k % steps_per_seq) == steps_per_seq - 1)
    def _final():
        for hd in range(M_HEADS):
            cf = c_s[hd]
            c_out_ref[0, hd] = cf[:, 0:M_DV]
            n_out_ref[0, hd:hd + 1, :] = jnp.transpose(cf[:, M_DV:2 * M_DV])[0:1, :]
        m_out_ref[0] = m_s[...]
        kk_ref[0] = set_b["kf"][tile - WINDOW:tile, :]
        vk_ref[0] = set_b["vf"][tile - WINDOW:tile, :]


def _rope_tables(pos):
    half = ROT_DIM // 2
    dim = jnp.arange(LANES, dtype=jnp.int32) % A_HEAD_DIM
    inv = ROPE_THETA ** (-(2 * (dim % half)).astype(F32) / ROT_DIM)
    ang = pos.astype(F32)[:, None] * inv[None, :]
    cos, sin = jnp.cos(ang), jnp.sin(ang)
    ra = jnp.where(dim < ROT_DIM, cos, 1.0)
    rb = jnp.where((dim >= half) & (dim < ROT_DIM), sin, 0.0)
    rc = jnp.where(dim < half, -sin, 0.0)
    return ra, rb, rc


def _full(shape, single_buffer=False):
    mode = dict(pipeline_mode=pl.Buffered(1)) if single_buffer else {}
    return pl.BlockSpec(shape, lambda *_: (0,) * len(shape), **mode)


def _prompt_layer(x, mod, sinks, gn, gf, gmn, bias_g, wp, wmo, wao, wo):
    bsz, seq, d = x.shape
    tile = PROMPT_TILE
    assert seq % (2 * tile) == 0 and tile % (2 * M_CHUNK) == 0 and d == D_MODEL
    tiles_per_seq = seq // tile
    steps_per_seq = tiles_per_seq // 2
    n_tiles = bsz * tiles_per_seq
    n_steps = n_tiles // 2
    ra, rb, rc = _rope_tables(jnp.arange(seq, dtype=jnp.int32))
    xt = x.reshape(n_tiles, tile, d)
    xp = x.reshape(n_steps, 2 * tile, d)
    mod3 = mod.reshape(bsz, 1, 3 * d)

    tile_0 = lambda k: 0
    tile_a = lambda k: 2 * k + 1
    tile_b = lambda k: jnp.minimum(2 * k + 2, n_tiles - 1)
    x_spec = lambda f: pl.BlockSpec((1, tile, d), lambda k: (f(k), 0, 0))
    mod_spec = lambda f: pl.BlockSpec((1, 1, 3 * d), lambda k: (f(k) // tiles_per_seq, 0, 0))
    rope_spec = lambda f: pl.BlockSpec((tile, LANES), lambda k: (f(k) % tiles_per_seq, 0))
    in_specs = [
        pl.BlockSpec(memory_space=pltpu.SMEM),
        x_spec(tile_0), x_spec(tile_a), x_spec(tile_b),
        pl.BlockSpec((1, 2 * tile, d), lambda k: (k, 0, 0)),
        mod_spec(tile_0), mod_spec(tile_a), mod_spec(tile_b),
        _full((1, d)), _full((1, d)), _full((1, M_WIDTH)), _full((1, 2 * LANES)),
        _full((d, N_PACK), True), _full((M_WIDTH, d), True), _full((A_WIDTH, d), True), _full((d, d), True),
        rope_spec(tile_0), rope_spec(tile_0), rope_spec(tile_0),
        rope_spec(tile_a), rope_spec(tile_a), rope_spec(tile_a),
        rope_spec(tile_b), rope_spec(tile_b), rope_spec(tile_b),
    ]
    seq_of = lambda k: k // steps_per_seq
    out_specs = [
        pl.BlockSpec((1, 2 * tile, d), lambda k: (k, 0, 0)),
        pl.BlockSpec((1, M_HEADS, M_DK, M_DV), lambda k: (seq_of(k), 0, 0, 0)),
        pl.BlockSpec((1, M_HEADS, M_DK), lambda k: (seq_of(k), 0, 0)),
        pl.BlockSpec((1, 1, LANES), lambda k: (seq_of(k), 0, 0)),
        pl.BlockSpec((1, WINDOW, A_KV_WIDTH), lambda k: (seq_of(k), 0, 0)),
        pl.BlockSpec((1, WINDOW, A_KV_WIDTH), lambda k: (seq_of(k), 0, 0)),
    ]
    out_shape = [
        jax.ShapeDtypeStruct((n_steps, 2 * tile, d), F32),
        jax.ShapeDtypeStruct((bsz, M_HEADS, M_DK, M_DV), F32),
        jax.ShapeDtypeStruct((bsz, M_HEADS, M_DK), F32),
        jax.ShapeDtypeStruct((bsz, 1, LANES), F32),
        jax.ShapeDtypeStruct((bsz, WINDOW, A_KV_WIDTH), F32),
        jax.ShapeDtypeStruct((bsz, WINDOW, A_KV_WIDTH), F32),
    ]
    scratch = _set_shapes(tile) + _set_shapes(tile) + [
        pltpu.VMEM((2 * A_KV_HEADS, 2 * WINDOW, LANES), BF16),
        pltpu.VMEM((2 * A_KV_HEADS, 2 * WINDOW, LANES), BF16),
        pltpu.VMEM((M_HEADS, M_DK, 2 * LANES), F32),
        pltpu.VMEM((1, LANES), F32),
        pltpu.VMEM((tile, M_WIDTH), BF16),
        pltpu.VMEM((tile, A_WIDTH), BF16),
        pltpu.VMEM((tile, D_MODEL), BF16),
        pltpu.VMEM((tile, M_WIDTH), BF16),
        pltpu.VMEM((tile, A_WIDTH), BF16),
        pltpu.VMEM((tile, D_MODEL), BF16),
    ]
    outs = pl.pallas_call(
        functools.partial(_prompt_kernel, tile=tile, steps_per_seq=steps_per_seq),
        grid=(n_steps,),
        in_specs=in_specs,
        out_specs=out_specs,
        out_shape=out_shape,
        scratch_shapes=scratch,
        compiler_params=pltpu.CompilerParams(dimension_semantics=("arbitrary",), vmem_limit_bytes=VMEM_LIMIT),
        name="prompt_layer",
    )(sinks, xt, xt, xt, xp, mod3, mod3, mod3, gn, gf, gmn, bias_g, wp, wmo, wao, wo,
      ra, rb, rc, ra, rb, rc, ra, rb, rc)
    return (outs[0].reshape(bsz, seq, d),) + tuple(outs[1:])


def _sample_proj_kernel(x_ref, mod_ref, gn_ref, w_ref, z_ref, hb_s):
    bsz, steps, _ = x_ref.shape

    @pl.when(pl.program_id(0) == 0)
    def _norm():
        mod = mod_ref[...]
        for t in range(steps):
            h = _rms(x_ref[:, t, :]) * gn_ref[...]
            hb_s[t * bsz:(t + 1) * bsz, :] = (h * (1.0 + mod[:, D_MODEL:2 * D_MODEL]) + mod[:, 0:D_MODEL]).astype(BF16)

    z = _dot(hb_s[...], w_ref[...])
    for t in range(steps):
        z_ref[t] = z[t * bsz:(t + 1) * bsz, :]


def _sample_proj(x3, mod_s, gn, wp):
    bsz, steps, d = x3.shape
    tn = 512
    return pl.pallas_call(
        _sample_proj_kernel,
        grid=(N_PACK // tn,),
        in_specs=[
            _full((bsz, steps, d)),
            pl.BlockSpec((bsz, 2 * d), lambda j: (0, 0)),
            _full((1, d)),
            pl.BlockSpec((d, tn), lambda j: (0, j)),
        ],
        out_specs=pl.BlockSpec((steps, bsz, tn), lambda j: (0, 0, j)),
        out_shape=jax.ShapeDtypeStruct((steps, bsz, N_PACK), F32),
        scratch_shapes=[pltpu.VMEM((steps * bsz, d), BF16)],
        compiler_params=pltpu.CompilerParams(dimension_semantics=("arbitrary",)),
        name="sample_proj",
    )(x3, mod_s, gn, wp)


def _sample_state_kernel(sinks_ref, z_ref, c_ref, n_ref, m_ref, ck_ref, cv_ref, rope_ref, bg_ref, gmn_ref,
                         cn_ref, nn_ref, mn_ref, ym_ref, ya_ref, ko_ref, vo_ref, *, group, steps):
    T, G = steps, group
    toks = range(T)

    def z(t, lo, n):
        return z_ref[t, :, lo:lo + n]

    def rope_t(t, blk):
        return _rope(blk, rope_ref[0, t:t + 1, :], rope_ref[1, t:t + 1, :], rope_ref[2, t:t + 1, :])

    lane = lax.broadcasted_iota(jnp.int32, (G, LANES), 1)
    lo_half = lane < A_HEAD_DIM
    gmn = gmn_ref[...]

    q = [z(t, C_MQ, M_WIDTH) * (M_DK ** -0.5) for t in toks]
    k = [z(t, C_MK, M_WIDTH) for t in toks]
    v = [z(t, C_MV, M_WIDTH) for t in toks]
    og = [_sigmoid(z(t, C_MO, M_WIDTH)) * _silu(z(t, C_MZ, M_WIDTH)) for t in toks]
    qa = [[rope_t(t, z(t, C_AQ + j * LANES, LANES)) * (A_HEAD_DIM ** -0.5) for j in range(A_WIDTH // LANES)]
          for t in toks]
    kn = [rope_t(t, z(t, C_AK, A_KV_WIDTH)) for t in toks]
    vn = [z(t, C_AV, A_KV_WIDTH) for t in toks]
    sz = [_silu(z(t, C_AZ, A_WIDTH)) for t in toks]
    gt = [z(t, C_GI, 2 * LANES) + bg_ref[...] for t in toks]
    ig = [gt[t][:, 0:LANES] for t in toks]
    lf = [_log_sigmoid(gt[t][:, LANES:2 * LANES]) for t in toks]


    m_all = m_ref[...]
    rep = lambda a, h: jnp.broadcast_to(a[:, h:h + 1], (G, LANES))
    gates = []
    for hd in range(M_HEADS):
        ig_h = [rep(ig[t], hd) for t in toks]
        lf_h = [rep(lf[t], hd) for t in toks]
        m_old = rep(m_all, hd)
        bsum = [lf_h[0]]
        for t in range(1, T):
            bsum.append(bsum[-1] + lf_h[t])
        b_last = bsum[-1]
        d_end = [b_last - bsum[s] + ig_h[s] for s in toks]
        m_new = b_last + m_old
        for s in toks:
            m_new = jnp.maximum(m_new, d_end[s])
        mn_ref[:, hd:hd + 1] = m_new[:, 0:1]
        gd = dict(w_end=[jnp.exp(d_end[s] - m_new) for s in toks], decay=jnp.exp(b_last + m_old - m_new),
                  e=[], w_inter=[], emh=[])
        for t in toks:
            inter = bsum[t] + m_old
            dm = [bsum[t] - bsum[s] + ig_h[s] for s in range(t + 1)]
            m_hat = inter
            for s in range(t + 1):
                m_hat = jnp.maximum(m_hat, dm[s])
            gd["e"].append([jnp.exp(dm[s] - m_hat) for s in range(t + 1)])
            gd["w_inter"].append(jnp.exp(inter - m_hat))
            gd["emh"].append(jnp.exp(-m_hat))
        gates.append(gd)

    heads = [slice(hd * M_DK, (hd + 1) * M_DK) for hd in range(M_HEADS)]
    n_old = [n_ref[:, hd, :] for hd in range(M_HEADS)]
    qk = {(hd, t, s): jnp.sum(q[t][:, heads[hd]] * k[s][:, heads[hd]], axis=-1, keepdims=True)
          for hd in range(M_HEADS) for t in toks for s in range(t + 1)}
    qn = {(hd, t): jnp.sum(q[t][:, heads[hd]] * n_old[hd], axis=-1, keepdims=True)
          for hd in range(M_HEADS) for t in toks}

    lo_f = jnp.where(lo_half, 1.0, 0.0)
    hi_f = 1.0 - lo_f
    dup = lambda a, kv: (jnp.where(lo_half, a, pltpu.roll(a, A_HEAD_DIM, 1)) if kv == 0 else
                         jnp.where(lo_half, pltpu.roll(a, A_HEAD_DIM, 1), a))
    kn_x = [[dup(kn[s], kv) for s in toks] for kv in range(A_KV_HEADS)]
    vn_x = [[dup(vn[s], kv) for s in toks] for kv in range(A_KV_HEADS)]
    s_new = {}
    for kv in range(A_KV_HEADS):
        for blk in range(2):
            for t in toks:
                for s in range(t + 1):
                    prod = qa[t][2 * kv + blk] * kn_x[kv][s]
                    s_new[kv, 0, blk, t, s] = jnp.sum(prod * lo_f, axis=-1, keepdims=True)
                    s_new[kv, 1, blk, t, s] = jnp.sum(prod * hi_f, axis=-1, keepdims=True)

    rows_tg = lax.broadcasted_iota(jnp.int32, (T * G, LANES), 0) % G
    own_rows = [rows_tg == b for b in range(G)]
    rows64 = lax.broadcasted_iota(jnp.int32, (2 * T * G, LANES), 0) % G

    qc = []
    for hd in range(M_HEADS):
        hs = heads[hd]
        gd = gates[hd]
        q32 = jnp.concatenate([q[t][:, hs] for t in toks], axis=0)
        kw = [k[s][:, hs] * gd["w_end"][s] for s in toks]
        kw_t = jnp.transpose(jnp.concatenate(kw, axis=0)).astype(BF16)
        v32 = jnp.concatenate([v[s][:, hs] for s in toks], axis=0)
        acc = None
        for b in range(G):
            c_old = c_ref[b, hd]
            part = _dot(jnp.where(own_rows[b], q32, 0.0).astype(BF16), c_old.astype(BF16))
            acc = part if acc is None else acc + part
            upd = _dot(kw_t, jnp.where(own_rows[b], v32, 0.0).astype(BF16))
            cn_ref[b, hd] = gd["decay"][b:b + 1, 0:1] * c_old + upd
        qc.append(acc)
        nn_ref[:, hd, :] = gd["decay"] * n_old[hd] + (kw[0] + kw[1] + kw[2] + kw[3])

    zeros_kt = jnp.zeros((A_HEAD_DIM, WINDOW), BF16)
    sc = {}
    for kv in range(A_KV_HEADS):
        l64 = jnp.concatenate([qa[t][2 * kv + blk] for blk in range(2) for t in toks], axis=0)
        own64 = [jnp.where(rows64 == b, l64, 0.0).astype(BF16) for b in range(G)]
        kt = [ck_ref[b, kv].astype(BF16) for b in range(G)]
        for par in range(2):
            acc = None
            for b in range(G):
                rhs = jnp.concatenate([kt[b], zeros_kt] if par == 0 else [zeros_kt, kt[b]], axis=0)
                part = _dot(own64[b], rhs)
                acc = part if acc is None else acc + part
            sc[kv, par] = acc

    hh = {}
    for hd in range(M_HEADS):
        gd = gates[hd]
        for t in toks:
            sv, ssum = None, None
            for s in range(t + 1):
                s_ts = qk[hd, t, s] * gd["e"][t][s]
                sv = s_ts * v[s][:, heads[hd]] if sv is None else sv + s_ts * v[s][:, heads[hd]]
                ssum = s_ts if ssum is None else ssum + s_ts
            wi = gd["w_inter"][t]
            num = sv + wi * qc[hd][t * G:(t + 1) * G, :]
            den = ssum + wi * qn[hd, t]
            hh[hd, t] = num / jnp.maximum(jnp.abs(den), gd["emh"][t])
    ms = {key: jnp.mean(val * val, axis=-1, keepdims=True) for key, val in hh.items()}
    for (hd, t), val in hh.items():
        hs = heads[hd]
        ym_ref[t, :, hs] = val * lax.rsqrt(ms[hd, t] + NORM_EPS) * gmn[:, hs] * og[t][:, hs]

    items = [(kv, par, blk, t) for kv in range(A_KV_HEADS) for par in range(2) for blk in range(2) for t in toks]
    s_c = {it: jnp.where(lane > it[3], sc[it[0], it[1]][(it[2] * T + it[3]) * G:(it[2] * T + it[3] + 1) * G, :], -jnp.inf)
           for it in items}
    mx_c = {it: jnp.max(s_c[it], axis=-1, keepdims=True) for it in items}
    p_c, p_n, sink_t = {}, {}, {}
    for it in items:
        kv, par, blk, t = it
        sink = sinks_ref[A_GROUP * kv + 2 * blk + par]
        mx = jnp.maximum(mx_c[it], sink)
        for s in range(t + 1):
            mx = jnp.maximum(mx, s_new[kv, par, blk, t, s])
        p_c[it] = jnp.exp(s_c[it] - mx)
        p_n[it] = [jnp.exp(s_new[kv, par, blk, t, s] - mx) for s in range(t + 1)]
        sink_t[it] = jnp.exp(sink - mx)
    sum_c = {it: jnp.sum(p_c[it], axis=-1, keepdims=True) for it in items}
    fresh = {}
    probs = {(kv, par): [] for kv in range(A_KV_HEADS) for par in range(2)}
    for it in items:
        kv, par, blk, t = it
        den = sum_c[it] + sink_t[it]
        for s in range(t + 1):
            den = den + p_n[it][s]
        r = 1.0 / den
        probs[kv, par].append(p_c[it] * r)
        acc = None
        for s in range(t + 1):
            term = (p_n[it][s] * r) * vn_x[kv][s]
            acc = term if acc is None else acc + term
        fresh[it] = acc

    for kv in range(A_KV_HEADS):
        vt = [cv_ref[b, kv].astype(BF16) for b in range(G)]
        o = []
        for par in range(2):
            pstack = jnp.concatenate(probs[kv, par], axis=0)
            acc = None
            for b in range(G):
                rhs = jnp.concatenate([vt[b], zeros_kt] if par == 0 else [zeros_kt, vt[b]], axis=0)
                part = _dot_nt(jnp.where(rows64 == b, pstack, 0.0).astype(BF16), rhs)
                acc = part if acc is None else acc + part
            o.append(acc)
        for blk in range(2):
            cs = slice((2 * kv + blk) * LANES, (2 * kv + blk + 1) * LANES)
            for t in toks:
                rs = slice((blk * T + t) * G, (blk * T + t + 1) * G)
                new_v = jnp.where(lo_half, fresh[kv, 0, blk, t], fresh[kv, 1, blk, t])
                ya_ref[t, :, cs] = (o[0][rs, :] + o[1][rs, :] + new_v) * sz[t][:, cs]

    n_rows = T * G
    src = lax.broadcasted_iota(jnp.int32, (n_rows, n_rows), 1)
    dst = lax.broadcasted_iota(jnp.int32, (n_rows, n_rows), 0)
    perm = jnp.where(src == (dst % T) * G + dst // T, 1.0, 0.0).astype(BF16)
    lane_d = lax.broadcasted_iota(jnp.int32, (A_HEAD_DIM, WINDOW), 1)
    for new, cache_ref, out_ref in ((kn, ck_ref, ko_ref), (vn, cv_ref, vo_ref)):
        stack = _dot_exact01(perm, jnp.concatenate(new, axis=0))
        stack = jnp.concatenate([stack, jnp.zeros((LANES - n_rows, LANES), F32)], axis=0)
        new_t = jnp.transpose(stack)
        for b in range(G):
            placed = pltpu.roll(new_t, (WINDOW - T - T * b) % LANES, 1)
            for kv in range(A_KV_HEADS):
                slid = pltpu.roll(cache_ref[b, kv], WINDOW - T, 1)
                out_ref[b, kv] = jnp.where(lane_d >= WINDOW - T, placed[kv * A_HEAD_DIM:(kv + 1) * A_HEAD_DIM, :], slid)


def _sample_state(zs, state_c, state_n, state_m, ck_t, cv_t, rope, bias_g, gmn, sinks):
    steps, bsz, _ = zs.shape
    group = SAMPLE_GROUP
    assert bsz % group == 0
    in_specs = [
        pl.BlockSpec(memory_space=pltpu.SMEM),
        pl.BlockSpec((steps, group, N_PACK), lambda i: (0, i, 0)),
        pl.BlockSpec((group, M_HEADS, M_DK, M_DV), lambda i: (i, 0, 0, 0)),
        pl.BlockSpec((group, M_HEADS, M_DK), lambda i: (i, 0, 0)),
        pl.BlockSpec((group, M_HEADS), lambda i: (i, 0)),
        pl.BlockSpec((group, A_KV_HEADS, A_HEAD_DIM, WINDOW), lambda i: (i, 0, 0, 0)),
        pl.BlockSpec((group, A_KV_HEADS, A_HEAD_DIM, WINDOW), lambda i: (i, 0, 0, 0)),
        _full((3, steps, LANES)), _full((1, 2 * LANES)), _full((1, M_WIDTH)),
    ]
    out_specs = [
        pl.BlockSpec((group, M_HEADS, M_DK, M_DV), lambda i: (i, 0, 0, 0)),
        pl.BlockSpec((group, M_HEADS, M_DK), lambda i: (i, 0, 0)),
        pl.BlockSpec((group, M_HEADS), lambda i: (i, 0)),
        pl.BlockSpec((steps, group, M_WIDTH), lambda i: (0, i, 0)),
        pl.BlockSpec((steps, group, A_WIDTH), lambda i: (0, i, 0)),
        pl.BlockSpec((group, A_KV_HEADS, A_HEAD_DIM, WINDOW), lambda i: (i, 0, 0, 0)),
        pl.BlockSpec((group, A_KV_HEADS, A_HEAD_DIM, WINDOW), lambda i: (i, 0, 0, 0)),
    ]
    out_shape = [
        jax.ShapeDtypeStruct((bsz, M_HEADS, M_DK, M_DV), F32),
        jax.ShapeDtypeStruct((bsz, M_HEADS, M_DK), F32),
        jax.ShapeDtypeStruct((bsz, M_HEADS), F32),
        jax.ShapeDtypeStruct((steps, bsz, M_WIDTH), F32),
        jax.ShapeDtypeStruct((steps, bsz, A_WIDTH), F32),
        jax.ShapeDtypeStruct((bsz, A_KV_HEADS, A_HEAD_DIM, WINDOW), F32),
        jax.ShapeDtypeStruct((bsz, A_KV_HEADS, A_HEAD_DIM, WINDOW), F32),
    ]
    return pl.pallas_call(
        functools.partial(_sample_state_kernel, group=group, steps=steps),
        grid=(bsz // group,),
        in_specs=in_specs,
        out_specs=out_specs,
        out_shape=out_shape,
        compiler_params=pltpu.CompilerParams(dimension_semantics=("arbitrary",), vmem_limit_bytes=VMEM_LIMIT),
        name="sample_state",
    )(sinks, zs, state_c, state_n, state_m, ck_t, cv_t, rope, bias_g, gmn)


def _sample_out_kernel(x_ref, gate_ref, z_ref, ym_ref, ya_ref, wmo_ref, wao_ref, wo_ref, gf_ref, y_ref):
    bsz, steps, _ = x_ref.shape
    stack = lambda ref, cs: jnp.concatenate([ref[t, :, cs] for t in range(steps)], axis=0)
    full = slice(None)
    x = jnp.concatenate([x_ref[:, t, :] for t in range(steps)], axis=0)
    gate = jnp.concatenate([gate_ref[...]] * steps, axis=0)
    y = _out_stage(x, gate, _sigmoid(stack(z_ref, slice(0, D_MODEL))), _sigmoid(stack(z_ref, slice(D_MODEL, 2 * D_MODEL))),
                   stack(ym_ref, full).astype(BF16), stack(ya_ref, full).astype(BF16), wmo_ref, wao_ref, wo_ref, gf_ref[...])
    for t in range(steps):
        y_ref[:, t, :] = y[t * bsz:(t + 1) * bsz, :]


def _sample_out(x3, mod_s, zs, ym, ya, wmo, wao, wo, gf):
    bsz, steps, d = x3.shape
    return pl.pallas_call(
        _sample_out_kernel,
        grid=(1,),
        in_specs=[
            _full((bsz, steps, d)),
            pl.BlockSpec((bsz, d), lambda i: (0, 2)),
            pl.BlockSpec((steps, bsz, 2 * d), lambda i: (0, 0, 0)),
            _full((steps, bsz, M_WIDTH)), _full((steps, bsz, A_WIDTH)),
            _full((M_WIDTH, d)), _full((A_WIDTH, d)), _full((d, d)), _full((1, d)),
        ],
        out_specs=_full((bsz, steps, d)),
        out_shape=jax.ShapeDtypeStruct((bsz, steps, d), F32),
        compiler_params=pltpu.CompilerParams(dimension_semantics=("arbitrary",), vmem_limit_bytes=VMEM_LIMIT),
        name="sample_out",
    )(x3, mod_s, zs, ym, ya, wmo, wao, wo, gf)


PACK_BLOCK = 256


def _pack_sources():
    names = ("mq", "mk", "mv", "mi", "mf", "mo", "mz", "aq", "ak", "av", "az", "gm", "ga")
    start, pos = {}, 0
    for name, size in zip(names, IN_SIZES):
        start[name] = pos
        pos += size
    assert start["av"] == start["ak"] + A_KV_WIDTH and start["mf"] == start["mi"] + M_HEADS
    order = (("gm", C_GM, D_MODEL), ("ga", C_GA, D_MODEL), ("mq", C_MQ, M_WIDTH), ("mk", C_MK, M_WIDTH),
             ("mv", C_MV, M_WIDTH), ("mo", C_MO, M_WIDTH), ("mz", C_MZ, M_WIDTH), ("aq", C_AQ, A_WIDTH),
             ("az", C_AZ, A_WIDTH), ("ak", C_AK, 2 * A_KV_WIDTH))
    src = []
    for name, col, width in order:
        assert col == len(src) * PACK_BLOCK and width % PACK_BLOCK == 0
        src += [start[name] + i * PACK_BLOCK for i in range(width // PACK_BLOCK)]
    assert C_GI == len(src) * PACK_BLOCK and C_GF == C_GI + LANES and N_PACK == C_GI + PACK_BLOCK
    return src + [start["mi"]]


def _pack_kernel(src_ref, wt_ref, o_ref):
    j = pl.program_id(0)
    last = pl.num_programs(0) - 1

    @pl.when(j < last)
    def _block():
        o_ref[...] = jnp.transpose(wt_ref[...]).astype(BF16)

    @pl.when(j == last)
    def _gates():
        x = wt_ref[...]
        pad = jnp.zeros((LANES - M_HEADS, x.shape[1]), F32)
        rows = jnp.concatenate([x[0:M_HEADS], pad, x[M_HEADS:2 * M_HEADS], pad], axis=0)
        o_ref[...] = jnp.transpose(rows).astype(BF16)


def _pack_w_in(w_in):
    d = w_in.shape[0]
    src = jnp.asarray(_pack_sources(), jnp.int32)
    return pl.pallas_call(
        _pack_kernel,
        grid_spec=pltpu.PrefetchScalarGridSpec(
            num_scalar_prefetch=1, grid=(N_PACK // PACK_BLOCK,),
            in_specs=[pl.BlockSpec((pl.Element(PACK_BLOCK), pl.Element(d)),
                                   lambda j, src_ref: (pl.multiple_of(src_ref[j], 8), 0))],
            out_specs=pl.BlockSpec((d, PACK_BLOCK), lambda j, src_ref: (0, j))),
        out_shape=jax.ShapeDtypeStruct((d, N_PACK), BF16),
        compiler_params=pltpu.CompilerParams(dimension_semantics=("arbitrary",)),
        name="pack_w_in",
    )(src, jnp.transpose(w_in))


def kernel(x_prompt, x_sample, state_C, state_n, state_m, cache_k, cache_v, c_prompt, c_sample, w_ada, b_ada, g_norm, w_in, b_igate, b_fgate, g_mnorm, sinks, w_m_out, w_a_out, w_out, g_final):
    assert w_in.shape[0] == 1, "single-layer step"
    bsz_s, steps, d = x_sample.shape
    assert cache_k.shape[2] == WINDOW

    mod_p, mod_s = _adaln(c_prompt, c_sample, w_ada[0], b_ada[0])
    wp = _pack_w_in(w_in[0])
    wmo = w_m_out[0].astype(BF16)
    wao = w_a_out[0].astype(BF16)
    wo = w_out[0].astype(BF16)
    gn = g_norm[0].reshape(1, d)
    gf = g_final.reshape(1, d)
    gmn = g_mnorm[0].reshape(1, M_WIDTH)
    zpad = jnp.zeros((LANES - M_HEADS,), F32)
    bias_g = jnp.concatenate([b_igate[0], zpad, b_fgate[0], zpad]).reshape(1, 2 * LANES)
    sk = sinks[0]

    y_p, c_p, n_p, m_p, k_p, v_p = _prompt_layer(x_prompt, mod_p, sk, gn, gf, gmn, bias_g, wp, wmo, wao, wo)
    bsz_p = x_prompt.shape[0]
    m_p = m_p[:, 0, 0:M_HEADS]
    k_p = k_p.reshape(bsz_p, WINDOW, A_KV_HEADS, A_HEAD_DIM)
    v_p = v_p.reshape(bsz_p, WINDOW, A_KV_HEADS, A_HEAD_DIM)

    zs = _sample_proj(x_sample, mod_s, gn, wp)
    rope = jnp.stack(_rope_tables(PAST_LEN + jnp.arange(steps, dtype=jnp.int32)))
    dims_keys = lambda c: jnp.transpose(c[0], (0, 2, 3, 1))
    c_s, n_s, m_s, ym, ya, k_t, v_t = _sample_state(zs, state_C[0], state_n[0], state_m[0], dims_keys(cache_k),
                                                    dims_keys(cache_v), rope, bias_g, gmn, sk)
    y_s = _sample_out(x_sample, mod_s, zs, ym, ya, wmo, wao, wo, gf)
    keys_dims = lambda c: jnp.transpose(c, (0, 3, 1, 2))[None]

    return (y_p, y_s, c_p[None], n_p[None], m_p[None], k_p[None], v_p[None],
            c_s[None], n_s[None], m_s[None], keys_dims(k_t), keys_dims(v_t))
```

```python
import functools

import jax
import jax.numpy as jnp
from jax import lax
from jax.experimental import pallas as pl
from jax.experimental.pallas import tpu as pltpu

F32 = jnp.float32
BF16 = jnp.bfloat16

D_MODEL = 1024
M_HEADS = 4
M_DK = 128
M_DV = 128
M_WIDTH = M_HEADS * M_DV
M_CHUNK = 128
A_HEADS = 8
A_KV_HEADS = 2
A_GROUP = A_HEADS // A_KV_HEADS
A_HEAD_DIM = 64
A_WIDTH = A_HEADS * A_HEAD_DIM
A_KV_WIDTH = A_KV_HEADS * A_HEAD_DIM
WINDOW = 128
ROT_DIM = A_HEAD_DIM // 4
ROPE_THETA = 500000.0
NORM_EPS = 1e-6
PAST_LEN = 16384
IN_SIZES = (M_HEADS * M_DK, M_HEADS * M_DK, M_WIDTH, M_HEADS, M_HEADS, M_WIDTH, M_WIDTH,
            A_WIDTH, A_KV_WIDTH, A_KV_WIDTH, A_WIDTH, D_MODEL, D_MODEL)

LANES = 128

C_GM = 0
C_GA = C_GM + D_MODEL
C_MQ = C_GA + D_MODEL
C_MK = C_MQ + M_WIDTH
C_MV = C_MK + M_WIDTH
C_MO = C_MV + M_WIDTH
C_MZ = C_MO + M_WIDTH
C_AQ = C_MZ + M_WIDTH
C_AZ = C_AQ + A_WIDTH
C_AK = C_AZ + A_WIDTH
C_AV = C_AK + A_KV_WIDTH
C_GI = C_AV + A_KV_WIDTH
C_GF = C_GI + LANES
N_PACK = C_GF + LANES

PROMPT_TILE = 256
SAMPLE_GROUP = 8
VMEM_LIMIT = 56 * 1024 * 1024


def _sigmoid(x):
    return 1.0 / (1.0 + jnp.exp(-x))


def _silu(x):
    return x * _sigmoid(x)


def _log_sigmoid(x):
    return jnp.minimum(x, 0.0) - jnp.log1p(jnp.exp(-jnp.abs(x)))


def _dot(a, b):
    return jnp.dot(a, b, preferred_element_type=F32)


def _dot_nt(a, b):
    return lax.dot_general(a, b, (((1,), (1,)), ((), ())), preferred_element_type=F32)


def _dot_tn(a, b):
    return lax.dot_general(a, b, (((0,), (0,)), ((), ())), preferred_element_type=F32)


def _dot_exact01(m01, x):
    x1 = x.astype(BF16)
    r1 = x - x1.astype(F32)
    x2 = r1.astype(BF16)
    x3 = (r1 - x2.astype(F32)).astype(BF16)
    return _dot(m01, x1) + _dot(m01, x2) + _dot(m01, x3)


def _rms(x):
    return x * lax.rsqrt(jnp.mean(x * x, axis=-1, keepdims=True) + NORM_EPS)


def _rope(blk, ra, rb, rc):
    return blk * ra + pltpu.roll(blk, 8, 1) * rb + pltpu.roll(blk, LANES - 8, 1) * rc


def _out_stage(x, gate, sgm, sga, ym, ya, wmo_ref, wao_ref, wo_ref, gf):
    pm = _dot(ym, wmo_ref[...])
    pa = _dot(ya, wao_ref[...])
    u = sgm * pm + sga * pa
    r = _dot(u.astype(BF16), wo_ref[...])
    return _rms(x + gate * r) * gf


def _adaln_kernel(cp_ref, cs_ref, w_ref, b_ref, op_ref, os_ref):
    w = w_ref[...].astype(BF16)
    b = b_ref[...]
    op_ref[...] = _dot(_silu(cp_ref[...]).astype(BF16), w) + b
    os_ref[...] = _dot(_silu(cs_ref[...]).astype(BF16), w) + b


def _adaln(c_p, c_s, w_ada, b_ada):
    bp, d = c_p.shape
    bs = c_s.shape[0]
    n = w_ada.shape[1]
    tn = 512
    return pl.pallas_call(
        _adaln_kernel,
        grid=(n // tn,),
        in_specs=[
            pl.BlockSpec((bp, d), lambda j: (0, 0)),
            pl.BlockSpec((bs, d), lambda j: (0, 0)),
            pl.BlockSpec((d, tn), lambda j: (0, j)),
            pl.BlockSpec((1, tn), lambda j: (0, j)),
        ],
        out_specs=[
            pl.BlockSpec((bp, tn), lambda j: (0, j)),
            pl.BlockSpec((bs, tn), lambda j: (0, j)),
        ],
        out_shape=[jax.ShapeDtypeStruct((bp, n), F32), jax.ShapeDtypeStruct((bs, n), F32)],
        compiler_params=pltpu.CompilerParams(dimension_semantics=("arbitrary",)),
        name="adaln",
    )(c_p, c_s, w_ada, b_ada.reshape(1, n))


_SET_FIELDS = ("hb", "q", "k", "vaug", "og", "qa", "kf", "vf", "sz", "sgm", "sga", "g")


def _set_shapes(tile):
    return [
        pltpu.VMEM((tile, D_MODEL), BF16),
        pltpu.VMEM((tile, M_WIDTH), BF16),
        pltpu.VMEM((tile, M_WIDTH), F32),
        pltpu.VMEM((M_HEADS, tile, 2 * LANES), BF16),
        pltpu.VMEM((tile, M_WIDTH), F32),
        pltpu.VMEM((tile, A_WIDTH), BF16),
        pltpu.VMEM((tile, A_KV_WIDTH), F32),
        pltpu.VMEM((tile, A_KV_WIDTH), F32),
        pltpu.VMEM((tile, A_WIDTH), F32),
        pltpu.VMEM((tile, D_MODEL), F32),
        pltpu.VMEM((tile, D_MODEL), F32),
        pltpu.VMEM((tile, 2 * LANES), F32),
    ]


N_TILE = 256


def _project_pieces(x_ref, mod_ref, rope_refs, st, gn_ref, bg_ref, w_ref):
    def norm():
        mod = mod_ref[0]
        h = _rms(x_ref[0]) * gn_ref[...]
        st["hb"][...] = (h * (1.0 + mod[:, D_MODEL:2 * D_MODEL]) + mod[:, 0:D_MODEL]).astype(BF16)

    def proj(lo):
        return _dot(st["hb"][...], w_ref[:, lo:lo + N_TILE])

    def rope(v):
        ra_ref, rb_ref, rc_ref = rope_refs
        return _rope(v, ra_ref[...], rb_ref[...], rc_ref[...])

    def cols(i):
        return slice(i * N_TILE, (i + 1) * N_TILE)

    def gates():
        st["g"][...] = proj(C_GI) + bg_ref[...]

    def queries(i):
        def piece():
            st["q"][:, cols(i)] = (proj(C_MQ + i * N_TILE) * (M_DK ** -0.5)).astype(BF16)
        return piece

    def keys(i):
        def piece():
            st["k"][:, cols(i)] = proj(C_MK + i * N_TILE)
        return piece

    def values(i):
        def piece():
            zv = proj(C_MV + i * N_TILE)
            for j in range(N_TILE // M_DV):
                st["vaug"][i * (N_TILE // M_DV) + j, :, 0:LANES] = zv[:, j * M_DV:(j + 1) * M_DV].astype(BF16)
        return piece

    def attn_kv():
        z = proj(C_AK)
        st["kf"][...] = rope(z[:, 0:A_KV_WIDTH])
        st["vf"][...] = z[:, A_KV_WIDTH:2 * A_KV_WIDTH]

    def attn_q(i):
        def piece():
            za = proj(C_AQ + i * N_TILE)
            for j in range(N_TILE // LANES):
                blk = rope(za[:, j * LANES:(j + 1) * LANES])
                lo = i * N_TILE + j * LANES
                st["qa"][:, lo:lo + LANES] = (blk * (A_HEAD_DIM ** -0.5)).astype(BF16)
        return piece

    def out_gate(i):
        def piece():
            st["og"][:, cols(i)] = _sigmoid(proj(C_MO + i * N_TILE))
        return piece

    def out_silu(i):
        def piece():
            st["og"][:, cols(i)] = st["og"][:, cols(i)] * _silu(proj(C_MZ + i * N_TILE))
        return piece

    def attn_gate(i):
        def piece():
            st["sz"][:, cols(i)] = _silu(proj(C_AZ + i * N_TILE))
        return piece

    def merge_gate(name, lo, i):
        def piece():
            st[name][:, cols(i)] = _sigmoid(proj(lo + i * N_TILE))
        return piece

    assert C_AV == C_AK + A_KV_WIDTH and 2 * A_KV_WIDTH == N_TILE and C_GF == C_GI + LANES
    half, full = range(M_WIDTH // N_TILE), range(D_MODEL // N_TILE)
    pieces = [norm]
    for i in full:
        pieces += [merge_gate("sgm", C_GM, i), merge_gate("sga", C_GA, i)]
    for i in half:
        pieces += [out_gate(i), out_silu(i), attn_gate(i)]
    pieces += [gates, attn_kv]
    for i in half:
        pieces += [attn_q(i), queries(i), keys(i), values(i)]
    return pieces


def _consume_pieces(st, first_tile, sinks_ref, gmn_ref, kbuf_s, vbuf_s, c_s, m_s, ym_s, ya_s, tile):
    L = M_CHUNK
    heads = [slice(hd * M_DK, (hd + 1) * M_DK) for hd in range(M_HEADS)]
    a_heads = [(kv, bi, half) for kv in range(A_KV_HEADS) for bi in range(2) for half in range(2)]

    def causal_mask():
        row = lax.broadcasted_iota(jnp.int32, (L, L), 0)
        col = lax.broadcasted_iota(jnp.int32, (L, L), 1)
        return row >= col

    def chunk_stages(j):
        rs = slice(j * L, (j + 1) * L)
        cx = {}

        def gates():
            g = st["g"][rs, :]
            ig_all = g[:, 0:LANES]
            lf_all = _log_sigmoid(g[:, LANES:2 * LANES])
            tril = jnp.where(causal_mask(), 1.0, 0.0).astype(BF16)
            b_all = _dot_exact01(tril, lf_all)
            b_last = b_all[L - 1:L, :]
            m_old = m_s[...]
            d_end = b_last - b_all + ig_all
            m_new = jnp.maximum(b_last + m_old, jnp.max(d_end, axis=0, keepdims=True))
            m_s[...] = m_new
            cx.update(b_all=b_all, w_end=jnp.exp(d_end - m_new), decay=jnp.exp(b_last + m_old - m_new),
                      inter=b_all + m_old, r_t=jnp.transpose(ig_all - b_all))

        def kv_buffers():
            srows = slice((j % 2) * L, (j % 2 + 1) * L)
            lo_half = lax.broadcasted_iota(jnp.int32, (L, L), 1) < A_HEAD_DIM
            kc = st["kf"][rs, :]
            vc = st["vf"][rs, :]
            kr = pltpu.roll(kc, A_HEAD_DIM, 1)
            vr = pltpu.roll(vc, A_HEAD_DIM, 1)
            kbuf_s[0, srows, :] = jnp.where(lo_half, kc, 0.0).astype(BF16)
            kbuf_s[1, srows, :] = jnp.where(lo_half, 0.0, kr).astype(BF16)
            kbuf_s[2, srows, :] = jnp.where(lo_half, kr, 0.0).astype(BF16)
            kbuf_s[3, srows, :] = jnp.where(lo_half, 0.0, kc).astype(BF16)
            lane = lax.broadcasted_iota(jnp.int32, (L, L), 1)
            ones_lo = jnp.where(lane == 0, 1.0, 0.0)
            ones_hi = jnp.where(lane == A_HEAD_DIM, 1.0, 0.0)
            vbuf_s[0, srows, :] = jnp.where(lo_half, vc, ones_hi).astype(BF16)
            vbuf_s[1, srows, :] = jnp.where(lo_half, ones_lo, vr).astype(BF16)
            vbuf_s[2, srows, :] = jnp.where(lo_half, vr, ones_hi).astype(BF16)
            vbuf_s[3, srows, :] = jnp.where(lo_half, ones_lo, vc).astype(BF16)

        def decay_matrices():
            cx["dmat"] = [jnp.where(causal_mask(), cx["b_all"][:, hd:hd + 1] + cx["r_t"][hd:hd + 1, :], -jnp.inf)
                          for hd in range(M_HEADS)]
            cx["m_hat"] = [jnp.maximum(cx["inter"][:, hd:hd + 1], jnp.max(cx["dmat"][hd], axis=-1, keepdims=True))
                           for hd in range(M_HEADS)]
            cx["kw"] = [(st["k"][rs, heads[hd]] * cx["w_end"][:, hd:hd + 1]).astype(BF16) for hd in range(M_HEADS)]

        def state_matmuls():
            cx["qk"], cx["qc"] = [], []
            for hd in range(M_HEADS):
                qh = st["q"][rs, heads[hd]]
                c_old = c_s[hd]
                cx["qk"].append(_dot_nt(qh, st["k"][rs, heads[hd]].astype(BF16)))
                cx["qc"].append(_dot(qh, c_old.astype(BF16)))
                va = st["vaug"][hd, rs, :]
                c_s[hd] = cx["decay"][:, hd:hd + 1] * c_old + _dot_tn(cx["kw"][hd], va)

        def attn_scores():
            row2 = lax.broadcasted_iota(jnp.int32, (L, 2 * L), 0)
            col2 = lax.broadcasted_iota(jnp.int32, (L, 2 * L), 1)
            no_prev = jnp.where(first_tile, 4 * L, 0) if j == 0 else 0
            if j % 2 == 0:
                mask = ((col2 < L) & (col2 <= row2)) | ((col2 >= L) & (col2 - L > row2 + no_prev))
            else:
                mask = ((col2 >= L) & (col2 - L <= row2)) | ((col2 < L) & (col2 > row2 + no_prev))
            cx["sh"] = {}
            for kv in range(A_KV_HEADS):
                qcat = jnp.concatenate([st["qa"][rs, (2 * kv) * LANES:(2 * kv + 1) * LANES],
                                        st["qa"][rs, (2 * kv + 1) * LANES:(2 * kv + 2) * LANES]], axis=0)
                for half in range(2):
                    sc = _dot_nt(qcat, kbuf_s[2 * kv + half])
                    for bi in range(2):
                        cx["sh"][kv, bi, half] = jnp.where(mask, sc[bi * L:(bi + 1) * L, :], -jnp.inf)

        def attn_max():
            cx["mx"] = {hk: jnp.maximum(jnp.max(cx["sh"][hk], axis=-1, keepdims=True),
                                        sinks_ref[A_GROUP * hk[0] + 2 * hk[1] + hk[2]]) for hk in a_heads}

        def stabilised_scores():
            cx["s"] = [(cx["qk"][hd] * jnp.exp(cx["dmat"][hd] - cx["m_hat"][hd])).astype(BF16)
                       for hd in range(M_HEADS)]

        def score_value_matmuls():
            cx["sv"] = [_dot(cx["s"][hd], st["vaug"][hd, rs, :]) for hd in range(M_HEADS)]

        def attn_exp():
            cx["p"] = {hk: jnp.exp(cx["sh"][hk] - cx["mx"][hk]).astype(BF16) for hk in a_heads}
            cx["sink"] = {hk: jnp.exp(sinks_ref[A_GROUP * hk[0] + 2 * hk[1] + hk[2]] - cx["mx"][hk]) for hk in a_heads}

        def attn_value_matmuls():
            lo_half = lax.broadcasted_iota(jnp.int32, (L, L), 1) < A_HEAD_DIM
            for kv in range(A_KV_HEADS):
                for bi in range(2):
                    o_lo = _dot(cx["p"][kv, bi, 0], vbuf_s[2 * kv])
                    o_hi = _dot(cx["p"][kv, bi, 1], vbuf_s[2 * kv + 1])
                    den_lo = o_lo[:, A_HEAD_DIM:A_HEAD_DIM + 1] + cx["sink"][kv, bi, 0]
                    den_hi = o_hi[:, 0:1] + cx["sink"][kv, bi, 1]
                    o = jnp.where(lo_half, o_lo / den_lo, o_hi / den_hi)
                    cs = slice((2 * kv + bi) * LANES, (2 * kv + bi + 1) * LANES)
                    ya_s[rs, cs] = (o * st["sz"][rs, cs]).astype(BF16)

        def mlstm_outputs():
            hh = []
            for hd in range(M_HEADS):
                m_hat = cx["m_hat"][hd]
                w_inter = jnp.exp(cx["inter"][:, hd:hd + 1] - m_hat)
                tot = cx["sv"][hd] + w_inter * cx["qc"][hd]
                hh.append(tot[:, 0:M_DV] / jnp.maximum(jnp.abs(tot[:, M_DV:M_DV + 1]), jnp.exp(-m_hat)))
            ms = [jnp.mean(h * h, axis=-1, keepdims=True) for h in hh]
            for hd in range(M_HEADS):
                hs = heads[hd]
                ym_s[rs, hs] = (hh[hd] * lax.rsqrt(ms[hd] + NORM_EPS) * gmn_ref[:, hs] * st["og"][rs, hs]).astype(BF16)

        mlstm = [gates, decay_matrices, state_matmuls, stabilised_scores, score_value_matmuls, mlstm_outputs]
        attn = [kv_buffers, attn_scores, attn_max, attn_exp, attn_value_matmuls]
        return mlstm, attn

    assert tile == 2 * L
    (m0, a0), (m1, a1) = chunk_stages(0), chunk_stages(1)
    return _merge(m0 + m1, a0 + a1)


def _output_pieces(st, x_ref, rows, gate_ref, ym_s, ya_s, u_s, wmo_ref, wao_ref, wo_ref, gf_ref, y_ref):
    def cols(i):
        return slice(i * N_TILE, (i + 1) * N_TILE)

    def merge(i):
        def piece():
            u = (st["sgm"][:, cols(i)] * _dot(ym_s[...], wmo_ref[:, cols(i)])
                 + st["sga"][:, cols(i)] * _dot(ya_s[...], wao_ref[:, cols(i)]))
            u_s[:, cols(i)] = u.astype(BF16)
        return piece

    def out(i):
        def piece():
            gate = gate_ref[0][:, 2 * D_MODEL + i * N_TILE:2 * D_MODEL + (i + 1) * N_TILE]
            y_ref[0, rows, cols(i)] = x_ref[0, rows, cols(i)] + gate * _dot(u_s[...], wo_ref[:, cols(i)])
        return piece

    def final_norm():
        y_ref[0, rows, :] = _rms(y_ref[0, rows, :]) * gf_ref[...]

    tiles = range(D_MODEL // N_TILE)
    return [merge(i) for i in tiles] + [out(i) for i in tiles] + [final_norm]


def _merge(primary, filler, lead=0):
    merged, done = list(filler[:lead]), lead
    rest = len(filler) - lead
    for i, piece in enumerate(primary):
        merged.append(piece)
        upto = lead + ((i + 1) * rest) // len(primary)
        merged += filler[done:upto]
        done = upto
    return merged


def _prompt_kernel(sinks_ref, x0_ref, xa_ref, xb_ref, xres_ref, mod0_ref, moda_ref, modb_ref,
                   gn_ref, gf_ref, gmn_ref, bg_ref, w_ref, wmo_ref, wao_ref, wo_ref,
                   ra0_ref, rb0_ref, rc0_ref, raa_ref, rba_ref, rca_ref, rab_ref, rbb_ref, rcb_ref,
                   y_ref, c_out_ref, n_out_ref, m_out_ref, kk_ref, vk_ref, *scratch, tile, steps_per_seq):
    nset = len(_SET_FIELDS)
    set_a = dict(zip(_SET_FIELDS, scratch[0:nset]))
    set_b = dict(zip(_SET_FIELDS, scratch[nset:2 * nset]))
    kbuf_s, vbuf_s, c_s, m_s, yma_s, yaa_s, ua_s, ymb_s, yab_s, ub_s = scratch[2 * nset:]
    k = pl.program_id(0)
    first = (k % steps_per_seq) == 0
    project = lambda x_ref, mod_ref, rope, st: _project_pieces(x_ref, mod_ref, rope, st, gn_ref, bg_ref, w_ref)

    @pl.when(k == 0)
    def _prologue():
        lane = lax.broadcasted_iota(jnp.int32, (tile, LANES), 1)
        ones_col = jnp.where(lane == 0, 1.0, 0.0).astype(BF16)
        for st in (set_a, set_b):
            for hd in range(M_HEADS):
                st["vaug"][hd, :, LANES:2 * LANES] = ones_col
        for piece in project(x0_ref, mod0_ref, (ra0_ref, rb0_ref, rc0_ref), set_a):
            piece()

    @pl.when(first)
    def _init():
        c_s[...] = jnp.zeros_like(c_s)
        m_s[...] = jnp.zeros_like(m_s)
        kbuf_s[...] = jnp.zeros_like(kbuf_s)
        vbuf_s[...] = jnp.zeros_like(vbuf_s)

    state = (sinks_ref, gmn_ref, kbuf_s, vbuf_s, c_s, m_s)
    out_w = (wmo_ref, wao_ref, wo_ref, gf_ref, y_ref)
    schedule = (_merge(_consume_pieces(set_a, first, *state, yma_s, yaa_s, tile),
                       project(xa_ref, moda_ref, (raa_ref, rba_ref, rca_ref), set_b), lead=2)
                + _merge(_consume_pieces(set_b, False, *state, ymb_s, yab_s, tile),
                         _output_pieces(set_a, xres_ref, slice(0, tile), moda_ref, yma_s, yaa_s, ua_s, *out_w)
                         + project(xb_ref, modb_ref, (rab_ref, rbb_ref, rcb_ref), set_a), lead=4)
                + _output_pieces(set_b, xres_ref, slice(tile, 2 * tile), moda_ref, ymb_s, yab_s, ub_s, *out_w))
    for piece in schedule:
        piece()

    @pl.when((k % steps_per_seq) == steps_per_seq - 1)
    def _final():
        for hd in range(M_HEADS):
            cf = c_s[hd]
            c_out_ref[0, hd] = cf[:, 0:M_DV]
            n_out_ref[0, hd:hd + 1, :] = jnp.transpose(cf[:, M_DV:2 * M_DV])[0:1, :]
        m_out_ref[0] = m_s[...]
        kk_ref[0] = set_b["kf"][tile - WINDOW:tile, :]
        vk_ref[0] = set_b["vf"][tile - WINDOW:tile, :]


def _rope_tables(pos):
    half = ROT_DIM // 2
    dim = jnp.arange(LANES, dtype=jnp.int32) % A_HEAD_DIM
    inv = ROPE_THETA ** (-(2 * (dim % half)).astype(F32) / ROT_DIM)
    ang = pos.astype(F32)[:, None] * inv[None, :]
    cos, sin = jnp.cos(ang), jnp.sin(ang)
    ra = jnp.where(dim < ROT_DIM, cos, 1.0)
    rb = jnp.where((dim >= half) & (dim < ROT_DIM), sin, 0.0)
    rc = jnp.where(dim < half, -sin, 0.0)
    return ra, rb, rc


def _full(shape, single_buffer=False):
    mode = dict(pipeline_mode=pl.Buffered(1)) if single_buffer else {}
    return pl.BlockSpec(shape, lambda *_: (0,) * len(shape), **mode)


def _prompt_layer(x, mod, sinks, gn, gf, gmn, bias_g, wp, wmo, wao, wo):
    bsz, seq, d = x.shape
    tile = PROMPT_TILE
    assert seq % (2 * tile) == 0 and tile % (2 * M_CHUNK) == 0 and d == D_MODEL
    tiles_per_seq = seq // tile
    steps_per_seq = tiles_per_seq // 2
    n_tiles = bsz * tiles_per_seq
    n_steps = n_tiles // 2
    ra, rb, rc = _rope_tables(jnp.arange(seq, dtype=jnp.int32))
    xt = x.reshape(n_tiles, tile, d)
    xp = x.reshape(n_steps, 2 * tile, d)
    mod3 = mod.reshape(bsz, 1, 3 * d)

    tile_0 = lambda k: 0
    tile_a = lambda k: 2 * k + 1
    tile_b = lambda k: jnp.minimum(2 * k + 2, n_tiles - 1)
    x_spec = lambda f: pl.BlockSpec((1, tile, d), lambda k: (f(k), 0, 0))
    mod_spec = lambda f: pl.BlockSpec((1, 1, 3 * d), lambda k: (f(k) // tiles_per_seq, 0, 0))
    rope_spec = lambda f: pl.BlockSpec((tile, LANES), lambda k: (f(k) % tiles_per_seq, 0))
    in_specs = [
        pl.BlockSpec(memory_space=pltpu.SMEM),
        x_spec(tile_0), x_spec(tile_a), x_spec(tile_b),
        pl.BlockSpec((1, 2 * tile, d), lambda k: (k, 0, 0)),
        mod_spec(tile_0), mod_spec(tile_a), mod_spec(tile_b),
        _full((1, d)), _full((1, d)), _full((1, M_WIDTH)), _full((1, 2 * LANES)),
        _full((d, N_PACK), True), _full((M_WIDTH, d), True), _full((A_WIDTH, d), True), _full((d, d), True),
        rope_spec(tile_0), rope_spec(tile_0), rope_spec(tile_0),
        rope_spec(tile_a), rope_spec(tile_a), rope_spec(tile_a),
        rope_spec(tile_b), rope_spec(tile_b), rope_spec(tile_b),
    ]
    seq_of = lambda k: k // steps_per_seq
    out_specs = [
        pl.BlockSpec((1, 2 * tile, d), lambda k: (k, 0, 0)),
        pl.BlockSpec((1, M_HEADS, M_DK, M_DV), lambda k: (seq_of(k), 0, 0, 0)),
        pl.BlockSpec((1, M_HEADS, M_DK), lambda k: (seq_of(k), 0, 0)),
        pl.BlockSpec((1, 1, LANES), lambda k: (seq_of(k), 0, 0)),
        pl.BlockSpec((1, WINDOW, A_KV_WIDTH), lambda k: (seq_of(k), 0, 0)),
        pl.BlockSpec((1, WINDOW, A_KV_WIDTH), lambda k: (seq_of(k), 0, 0)),
    ]
    out_shape = [
        jax.ShapeDtypeStruct((n_steps, 2 * tile, d), F32),
        jax.ShapeDtypeStruct((bsz, M_HEADS, M_DK, M_DV), F32),
        jax.ShapeDtypeStruct((bsz, M_HEADS, M_DK), F32),
        jax.ShapeDtypeStruct((bsz, 1, LANES), F32),
        jax.ShapeDtypeStruct((bsz, WINDOW, A_KV_WIDTH), F32),
        jax.ShapeDtypeStruct((bsz, WINDOW, A_KV_WIDTH), F32),
    ]
    scratch = _set_shapes(tile) + _set_shapes(tile) + [
        pltpu.VMEM((2 * A_KV_HEADS, 2 * WINDOW, LANES), BF16),
        pltpu.VMEM((2 * A_KV_HEADS, 2 * WINDOW, LANES), BF16),
        pltpu.VMEM((M_HEADS, M_DK, 2 * LANES), F32),
        pltpu.VMEM((1, LANES), F32),
        pltpu.VMEM((tile, M_WIDTH), BF16),
        pltpu.VMEM((tile, A_WIDTH), BF16),
        pltpu.VMEM((tile, D_MODEL), BF16),
        pltpu.VMEM((tile, M_WIDTH), BF16),
        pltpu.VMEM((tile, A_WIDTH), BF16),
        pltpu.VMEM((tile, D_MODEL), BF16),
    ]
    outs = pl.pallas_call(
        functools.partial(_prompt_kernel, tile=tile, steps_per_seq=steps_per_seq),
        grid=(n_steps,),
        in_specs=in_specs,
        out_specs=out_specs,
        out_shape=out_shape,
        scratch_shapes=scratch,
        compiler_params=pltpu.CompilerParams(dimension_semantics=("arbitrary",), vmem_limit_bytes=VMEM_LIMIT),
        name="prompt_layer",
    )(sinks, xt, xt, xt, xp, mod3, mod3, mod3, gn, gf, gmn, bias_g, wp, wmo, wao, wo,
      ra, rb, rc, ra, rb, rc, ra, rb, rc)
    return (outs[0].reshape(bsz, seq, d),) + tuple(outs[1:])


def _sample_proj_kernel(x_ref, mod_ref, gn_ref, w_ref, z_ref, hb_s):
    bsz, steps, _ = x_ref.shape

    @pl.when(pl.program_id(0) == 0)
    def _norm():
        mod = mod_ref[...]
        for t in range(steps):
            h = _rms(x_ref[:, t, :]) * gn_ref[...]
            hb_s[t * bsz:(t + 1) * bsz, :] = (h * (1.0 + mod[:, D_MODEL:2 * D_MODEL]) + mod[:, 0:D_MODEL]).astype(BF16)

    z = _dot(hb_s[...], w_ref[...])
    for t in range(steps):
        z_ref[t] = z[t * bsz:(t + 1) * bsz, :]


def _sample_proj(x3, mod_s, gn, wp):
    bsz, steps, d = x3.shape
    tn = 512
    return pl.pallas_call(
        _sample_proj_kernel,
        grid=(N_PACK // tn,),
        in_specs=[
            _full((bsz, steps, d)),
            pl.BlockSpec((bsz, 2 * d), lambda j: (0, 0)),
            _full((1, d)),
            pl.BlockSpec((d, tn), lambda j: (0, j)),
        ],
        out_specs=pl.BlockSpec((steps, bsz, tn), lambda j: (0, 0, j)),
        out_shape=jax.ShapeDtypeStruct((steps, bsz, N_PACK), F32),
        scratch_shapes=[pltpu.VMEM((steps * bsz, d), BF16)],
        compiler_params=pltpu.CompilerParams(dimension_semantics=("arbitrary",)),
        name="sample_proj",
    )(x3, mod_s, gn, wp)


def _sample_state_kernel(sinks_ref, z_ref, c_ref, n_ref, m_ref, ck_ref, cv_ref, rope_ref, bg_ref, gmn_ref,
                         cn_ref, nn_ref, mn_ref, ym_ref, ya_ref, ko_ref, vo_ref, *, group, steps):
    T, G = steps, group
    toks = range(T)

    def z(t, lo, n):
        return z_ref[t, :, lo:lo + n]

    def rope_t(t, blk):
        return _rope(blk, rope_ref[0, t:t + 1, :], rope_ref[1, t:t + 1, :], rope_ref[2, t:t + 1, :])

    lane = lax.broadcasted_iota(jnp.int32, (G, LANES), 1)
    lo_half = lane < A_HEAD_DIM
    gmn = gmn_ref[...]

    q = [z(t, C_MQ, M_WIDTH) * (M_DK ** -0.5) for t in toks]
    k = [z(t, C_MK, M_WIDTH) for t in toks]
    v = [z(t, C_MV, M_WIDTH) for t in toks]
    og = [_sigmoid(z(t, C_MO, M_WIDTH)) * _silu(z(t, C_MZ, M_WIDTH)) for t in toks]
    qa = [[rope_t(t, z(t, C_AQ + j * LANES, LANES)) * (A_HEAD_DIM ** -0.5) for j in range(A_WIDTH // LANES)]
          for t in toks]
    kn = [rope_t(t, z(t, C_AK, A_KV_WIDTH)) for t in toks]
    vn = [z(t, C_AV, A_KV_WIDTH) for t in toks]
    sz = [_silu(z(t, C_AZ, A_WIDTH)) for t in toks]
    gt = [z(t, C_GI, 2 * LANES) + bg_ref[...] for t in toks]
    ig = [gt[t][:, 0:LANES] for t in toks]
    lf = [_log_sigmoid(gt[t][:, LANES:2 * LANES]) for t in toks]


    m_all = m_ref[...]
    rep = lambda a, h: jnp.broadcast_to(a[:, h:h + 1], (G, LANES))
    gates = []
    for hd in range(M_HEADS):
        ig_h = [rep(ig[t], hd) for t in toks]
        lf_h = [rep(lf[t], hd) for t in toks]
        m_old = rep(m_all, hd)
        bsum = [lf_h[0]]
        for t in range(1, T):
            bsum.append(bsum[-1] + lf_h[t])
        b_last = bsum[-1]
        d_end = [b_last - bsum[s] + ig_h[s] for s in toks]
        m_new = b_last + m_old
        for s in toks:
            m_new = jnp.maximum(m_new, d_end[s])
        mn_ref[:, hd:hd + 1] = m_new[:, 0:1]
        gd = dict(w_end=[jnp.exp(d_end[s] - m_new) for s in toks], decay=jnp.exp(b_last + m_old - m_new),
                  e=[], w_inter=[], emh=[])
        for t in toks:
            inter = bsum[t] + m_old
            dm = [bsum[t] - bsum[s] + ig_h[s] for s in range(t + 1)]
            m_hat = inter
            for s in range(t + 1):
                m_hat = jnp.maximum(m_hat, dm[s])
            gd["e"].append([jnp.exp(dm[s] - m_hat) for s in range(t + 1)])
            gd["w_inter"].append(jnp.exp(inter - m_hat))
            gd["emh"].append(jnp.exp(-m_hat))
        gates.append(gd)

    heads = [slice(hd * M_DK, (hd + 1) * M_DK) for hd in range(M_HEADS)]
    n_old = [n_ref[:, hd, :] for hd in range(M_HEADS)]
    qk = {(hd, t, s): jnp.sum(q[t][:, heads[hd]] * k[s][:, heads[hd]], axis=-1, keepdims=True)
          for hd in range(M_HEADS) for t in toks for s in range(t + 1)}
    qn = {(hd, t): jnp.sum(q[t][:, heads[hd]] * n_old[hd], axis=-1, keepdims=True)
          for hd in range(M_HEADS) for t in toks}

    lo_f = jnp.where(lo_half, 1.0, 0.0)
    hi_f = 1.0 - lo_f
    dup = lambda a, kv: (jnp.where(lo_half, a, pltpu.roll(a, A_HEAD_DIM, 1)) if kv == 0 else
                         jnp.where(lo_half, pltpu.roll(a, A_HEAD_DIM, 1), a))
    kn_x = [[dup(kn[s], kv) for s in toks] for kv in range(A_KV_HEADS)]
    vn_x = [[dup(vn[s], kv) for s in toks] for kv in range(A_KV_HEADS)]
    s_new = {}
    for kv in range(A_KV_HEADS):
        for blk in range(2):
            for t in toks:
                for s in range(t + 1):
                    prod = qa[t][2 * kv + blk] * kn_x[kv][s]
                    s_new[kv, 0, blk, t, s] = jnp.sum(prod * lo_f, axis=-1, keepdims=True)
                    s_new[kv, 1, blk, t, s] = jnp.sum(prod * hi_f, axis=-1, keepdims=True)

    rows_tg = lax.broadcasted_iota(jnp.int32, (T * G, LANES), 0) % G
    own_rows = [rows_tg == b for b in range(G)]
    rows64 = lax.broadcasted_iota(jnp.int32, (2 * T * G, LANES), 0) % G

    qc = []
    for hd in range(M_HEADS):
        hs = heads[hd]
        gd = gates[hd]
        q32 = jnp.concatenate([q[t][:, hs] for t in toks], axis=0)
        kw = [k[s][:, hs] * gd["w_end"][s] for s in toks]
        kw_t = jnp.transpose(jnp.concatenate(kw, axis=0)).astype(BF16)
        v32 = jnp.concatenate([v[s][:, hs] for s in toks], axis=0)
        acc = None
        for b in range(G):
            c_old = c_ref[b, hd]
            part = _dot(jnp.where(own_rows[b], q32, 0.0).astype(BF16), c_old.astype(BF16))
            acc = part if acc is None else acc + part
            upd = _dot(kw_t, jnp.where(own_rows[b], v32, 0.0).astype(BF16))
            cn_ref[b, hd] = gd["decay"][b:b + 1, 0:1] * c_old + upd
        qc.append(acc)
        nn_ref[:, hd, :] = gd["decay"] * n_old[hd] + (kw[0] + kw[1] + kw[2] + kw[3])

    zeros_kt = jnp.zeros((A_HEAD_DIM, WINDOW), BF16)
    sc = {}
    for kv in range(A_KV_HEADS):
        l64 = jnp.concatenate([qa[t][2 * kv + blk] for blk in range(2) for t in toks], axis=0)
        own64 = [jnp.where(rows64 == b, l64, 0.0).astype(BF16) for b in range(G)]
        kt = [ck_ref[b, kv].astype(BF16) for b in range(G)]
        for par in range(2):
            acc = None
            for b in range(G):
                rhs = jnp.concatenate([kt[b], zeros_kt] if par == 0 else [zeros_kt, kt[b]], axis=0)
                part = _dot(own64[b], rhs)
                acc = part if acc is None else acc + part
            sc[kv, par] = acc

    hh = {}
    for hd in range(M_HEADS):
        gd = gates[hd]
        for t in toks:
            sv, ssum = None, None
            for s in range(t + 1):
                s_ts = qk[hd, t, s] * gd["e"][t][s]
                sv = s_ts * v[s][:, heads[hd]] if sv is None else sv + s_ts * v[s][:, heads[hd]]
                ssum = s_ts if ssum is None else ssum + s_ts
            wi = gd["w_inter"][t]
            num = sv + wi * qc[hd][t * G:(t + 1) * G, :]
            den = ssum + wi * qn[hd, t]
            hh[hd, t] = num / jnp.maximum(jnp.abs(den), gd["emh"][t])
    ms = {key: jnp.mean(val * val, axis=-1, keepdims=True) for key, val in hh.items()}
    for (hd, t), val in hh.items():
        hs = heads[hd]
        ym_ref[t, :, hs] = val * lax.rsqrt(ms[hd, t] + NORM_EPS) * gmn[:, hs] * og[t][:, hs]

    items = [(kv, par, blk, t) for kv in range(A_KV_HEADS) for par in range(2) for blk in range(2) for t in toks]
    s_c = {it: jnp.where(lane > it[3], sc[it[0], it[1]][(it[2] * T + it[3]) * G:(it[2] * T + it[3] + 1) * G, :], -jnp.inf)
           for it in items}
    mx_c = {it: jnp.max(s_c[it], axis=-1, keepdims=True) for it in items}
    p_c, p_n, sink_t = {}, {}, {}
    for it in items:
        kv, par, blk, t = it
        sink = sinks_ref[A_GROUP * kv + 2 * blk + par]
        mx = jnp.maximum(mx_c[it], sink)
        for s in range(t + 1):
            mx = jnp.maximum(mx, s_new[kv, par, blk, t, s])
        p_c[it] = jnp.exp(s_c[it] - mx)
        p_n[it] = [jnp.exp(s_new[kv, par, blk, t, s] - mx) for s in range(t + 1)]
        sink_t[it] = jnp.exp(sink - mx)
    sum_c = {it: jnp.sum(p_c[it], axis=-1, keepdims=True) for it in items}
    fresh = {}
    probs = {(kv, par): [] for kv in range(A_KV_HEADS) for par in range(2)}
    for it in items:
        kv, par, blk, t = it
        den = sum_c[it] + sink_t[it]
        for s in range(t + 1):
            den = den + p_n[it][s]
        r = 1.0 / den
        probs[kv, par].append(p_c[it] * r)
        acc = None
        for s in range(t + 1):
            term = (p_n[it][s] * r) * vn_x[kv][s]
            acc = term if acc is None else acc + term
        fresh[it] = acc

    for kv in range(A_KV_HEADS):
        vt = [cv_ref[b, kv].astype(BF16) for b in range(G)]
        o = []
        for par in range(2):
            pstack = jnp.concatenate(probs[kv, par], axis=0)
            acc = None
            for b in range(G):
                rhs = jnp.concatenate([vt[b], zeros_kt] if par == 0 else [zeros_kt, vt[b]], axis=0)
                part = _dot_nt(jnp.where(rows64 == b, pstack, 0.0).astype(BF16), rhs)
                acc = part if acc is None else acc + part
            o.append(acc)
        for blk in range(2):
            cs = slice((2 * kv + blk) * LANES, (2 * kv + blk + 1) * LANES)
            for t in toks:
                rs = slice((blk * T + t) * G, (blk * T + t + 1) * G)
                new_v = jnp.where(lo_half, fresh[kv, 0, blk, t], fresh[kv, 1, blk, t])
                ya_ref[t, :, cs] = (o[0][rs, :] + o[1][rs, :] + new_v) * sz[t][:, cs]

    n_rows = T * G
    src = lax.broadcasted_iota(jnp.int32, (n_rows, n_rows), 1)
    dst = lax.broadcasted_iota(jnp.int32, (n_rows, n_rows), 0)
    perm = jnp.where(src == (dst % T) * G + dst // T, 1.0, 0.0).astype(BF16)
    lane_d = lax.broadcasted_iota(jnp.int32, (A_HEAD_DIM, WINDOW), 1)
    for new, cache_ref, out_ref in ((kn, ck_ref, ko_ref), (vn, cv_ref, vo_ref)):
        stack = _dot_exact01(perm, jnp.concatenate(new, axis=0))
        stack = jnp.concatenate([stack, jnp.zeros((LANES - n_rows, LANES), F32)], axis=0)
        new_t = jnp.transpose(stack)
        for b in range(G):
            placed = pltpu.roll(new_t, (WINDOW - T - T * b) % LANES, 1)
            for kv in range(A_KV_HEADS):
                slid = pltpu.roll(cache_ref[b, kv], WINDOW - T, 1)
                out_ref[b, kv] = jnp.where(lane_d >= WINDOW - T, placed[kv * A_HEAD_DIM:(kv + 1) * A_HEAD_DIM, :], slid)


def _sample_state(zs, state_c, state_n, state_m, ck_t, cv_t, rope, bias_g, gmn, sinks):
    steps, bsz, _ = zs.shape
    group = SAMPLE_GROUP
    assert bsz % group == 0
    in_specs = [
        pl.BlockSpec(memory_space=pltpu.SMEM),
        pl.BlockSpec((steps, group, N_PACK), lambda i: (0, i, 0)),
        pl.BlockSpec((group, M_HEADS, M_DK, M_DV), lambda i: (i, 0, 0, 0)),
        pl.BlockSpec((group, M_HEADS, M_DK), lambda i: (i, 0, 0)),
        pl.BlockSpec((group, M_HEADS), lambda i: (i, 0)),
        pl.BlockSpec((group, A_KV_HEADS, A_HEAD_DIM, WINDOW), lambda i: (i, 0, 0, 0)),
        pl.BlockSpec((group, A_KV_HEADS, A_HEAD_DIM, WINDOW), lambda i: (i, 0, 0, 0)),
        _full((3, steps, LANES)), _full((1, 2 * LANES)), _full((1, M_WIDTH)),
    ]
    out_specs = [
        pl.BlockSpec((group, M_HEADS, M_DK, M_DV), lambda i: (i, 0, 0, 0)),
        pl.BlockSpec((group, M_HEADS, M_DK), lambda i: (i, 0, 0)),
        pl.BlockSpec((group, M_HEADS), lambda i: (i, 0)),
        pl.BlockSpec((steps, group, M_WIDTH), lambda i: (0, i, 0)),
        pl.BlockSpec((steps, group, A_WIDTH), lambda i: (0, i, 0)),
        pl.BlockSpec((group, A_KV_HEADS, A_HEAD_DIM, WINDOW), lambda i: (i, 0, 0, 0)),
        pl.BlockSpec((group, A_KV_HEADS, A_HEAD_DIM, WINDOW), lambda i: (i, 0, 0, 0)),
    ]
    out_shape = [
        jax.ShapeDtypeStruct((bsz, M_HEADS, M_DK, M_DV), F32),
        jax.ShapeDtypeStruct((bsz, M_HEADS, M_DK), F32),
        jax.ShapeDtypeStruct((bsz, M_HEADS), F32),
        jax.ShapeDtypeStruct((steps, bsz, M_WIDTH), F32),
        jax.ShapeDtypeStruct((steps, bsz, A_WIDTH), F32),
        jax.ShapeDtypeStruct((bsz, A_KV_HEADS, A_HEAD_DIM, WINDOW), F32),
        jax.ShapeDtypeStruct((bsz, A_KV_HEADS, A_HEAD_DIM, WINDOW), F32),
    ]
    return pl.pallas_call(
        functools.partial(_sample_state_kernel, group=group, steps=steps),
        grid=(bsz // group,),
        in_specs=in_specs,
        out_specs=out_specs,
        out_shape=out_shape,
        compiler_params=pltpu.CompilerParams(dimension_semantics=("arbitrary",), vmem_limit_bytes=VMEM_LIMIT),
        name="sample_state",
    )(sinks, zs, state_c, state_n, state_m, ck_t, cv_t, rope, bias_g, gmn)


def _sample_out_kernel(x_ref, gate_ref, z_ref, ym_ref, ya_ref, wmo_ref, wao_ref, wo_ref, gf_ref, y_ref):
    bsz, steps, _ = x_ref.shape
    stack = lambda ref, cs: jnp.concatenate([ref[t, :, cs] for t in range(steps)], axis=0)
    full = slice(None)
    x = jnp.concatenate([x_ref[:, t, :] for t in range(steps)], axis=0)
    gate = jnp.concatenate([gate_ref[...]] * steps, axis=0)
    y = _out_stage(x, gate, _sigmoid(stack(z_ref, slice(0, D_MODEL))), _sigmoid(stack(z_ref, slice(D_MODEL, 2 * D_MODEL))),
                   stack(ym_ref, full).astype(BF16), stack(ya_ref, full).astype(BF16), wmo_ref, wao_ref, wo_ref, gf_ref[...])
    for t in range(steps):
        y_ref[:, t, :] = y[t * bsz:(t + 1) * bsz, :]


def _sample_out(x3, mod_s, zs, ym, ya, wmo, wao, wo, gf):
    bsz, steps, d = x3.shape
    return pl.pallas_call(
        _sample_out_kernel,
        grid=(1,),
        in_specs=[
            _full((bsz, steps, d)),
            pl.BlockSpec((bsz, d), lambda i: (0, 2)),
            pl.BlockSpec((steps, bsz, 2 * d), lambda i: (0, 0, 0)),
            _full((steps, bsz, M_WIDTH)), _full((steps, bsz, A_WIDTH)),
            _full((M_WIDTH, d)), _full((A_WIDTH, d)), _full((d, d)), _full((1, d)),
        ],
        out_specs=_full((bsz, steps, d)),
        out_shape=jax.ShapeDtypeStruct((bsz, steps, d), F32),
        compiler_params=pltpu.CompilerParams(dimension_semantics=("arbitrary",), vmem_limit_bytes=VMEM_LIMIT),
        name="sample_out",
    )(x3, mod_s, zs, ym, ya, wmo, wao, wo, gf)


PACK_BLOCK = 256


def _pack_sources():
    names = ("mq", "mk", "mv", "mi", "mf", "mo", "mz", "aq", "ak", "av", "az", "gm", "ga")
    start, pos = {}, 0
    for name, size in zip(names, IN_SIZES):
        start[name] = pos
        pos += size
    assert start["av"] == start["ak"] + A_KV_WIDTH and start["mf"] == start["mi"] + M_HEADS
    order = (("gm", C_GM, D_MODEL), ("ga", C_GA, D_MODEL), ("mq", C_MQ, M_WIDTH), ("mk", C_MK, M_WIDTH),
             ("mv", C_MV, M_WIDTH), ("mo", C_MO, M_WIDTH), ("mz", C_MZ, M_WIDTH), ("aq", C_AQ, A_WIDTH),
             ("az", C_AZ, A_WIDTH), ("ak", C_AK, 2 * A_KV_WIDTH))
    src = []
    for name, col, width in order:
        assert col == len(src) * PACK_BLOCK and width % PACK_BLOCK == 0
        src += [start[name] + i * PACK_BLOCK for i in range(width // PACK_BLOCK)]
    assert C_GI == len(src) * PACK_BLOCK and C_GF == C_GI + LANES and N_PACK == C_GI + PACK_BLOCK
    return src + [start["mi"]]


def _pack_kernel(src_ref, wt_ref, o_ref):
    j = pl.program_id(0)
    last = pl.num_programs(0) - 1

    @pl.when(j < last)
    def _block():
        o_ref[...] = jnp.transpose(wt_ref[...]).astype(BF16)

    @pl.when(j == last)
    def _gates():
        x = wt_ref[...]
        pad = jnp.zeros((LANES - M_HEADS, x.shape[1]), F32)
        rows = jnp.concatenate([x[0:M_HEADS], pad, x[M_HEADS:2 * M_HEADS], pad], axis=0)
        o_ref[...] = jnp.transpose(rows).astype(BF16)


def _pack_w_in(w_in):
    d = w_in.shape[0]
    src = jnp.asarray(_pack_sources(), jnp.int32)
    return pl.pallas_call(
        _pack_kernel,
        grid_spec=pltpu.PrefetchScalarGridSpec(
            num_scalar_prefetch=1, grid=(N_PACK // PACK_BLOCK,),
            in_specs=[pl.BlockSpec((pl.Element(PACK_BLOCK), pl.Element(d)),
                                   lambda j, src_ref: (pl.multiple_of(src_ref[j], 8), 0))],
            out_specs=pl.BlockSpec((d, PACK_BLOCK), lambda j, src_ref: (0, j))),
        out_shape=jax.ShapeDtypeStruct((d, N_PACK), BF16),
        compiler_params=pltpu.CompilerParams(dimension_semantics=("arbitrary",)),
        name="pack_w_in",
    )(src, jnp.transpose(w_in))


def kernel(x_prompt, x_sample, state_C, state_n, state_m, cache_k, cache_v, c_prompt, c_sample, w_ada, b_ada, g_norm, w_in, b_igate, b_fgate, g_mnorm, sinks, w_m_out, w_a_out, w_out, g_final):
    assert w_in.shape[0] == 1, "single-layer step"
    bsz_s, steps, d = x_sample.shape
    assert cache_k.shape[2] == WINDOW

    mod_p, mod_s = _adaln(c_prompt, c_sample, w_ada[0], b_ada[0])
    wp = _pack_w_in(w_in[0])
    wmo = w_m_out[0].astype(BF16)
    wao = w_a_out[0].astype(BF16)
    wo = w_out[0].astype(BF16)
    gn = g_norm[0].reshape(1, d)
    gf = g_final.reshape(1, d)
    gmn = g_mnorm[0].reshape(1, M_WIDTH)
    zpad = jnp.zeros((LANES - M_HEADS,), F32)
    bias_g = jnp.concatenate([b_igate[0], zpad, b_fgate[0], zpad]).reshape(1, 2 * LANES)
    sk = sinks[0]

    y_p, c_p, n_p, m_p, k_p, v_p = _prompt_layer(x_prompt, mod_p, sk, gn, gf, gmn, bias_g, wp, wmo, wao, wo)
    bsz_p = x_prompt.shape[0]
    m_p = m_p[:, 0, 0:M_HEADS]
    k_p = k_p.reshape(bsz_p, WINDOW, A_KV_HEADS, A_HEAD_DIM)
    v_p = v_p.reshape(bsz_p, WINDOW, A_KV_HEADS, A_HEAD_DIM)

    zs = _sample_proj(x_sample, mod_s, gn, wp)
    rope = jnp.stack(_rope_tables(PAST_LEN + jnp.arange(steps, dtype=jnp.int32)))
    dims_keys = lambda c: jnp.transpose(c[0], (0, 2, 3, 1))
    c_s, n_s, m_s, ym, ya, k_t, v_t = _sample_state(zs, state_C[0], state_n[0], state_m[0], dims_keys(cache_k),
                                                    dims_keys(cache_v), rope, bias_g, gmn, sk)
    y_s = _sample_out(x_sample, mod_s, zs, ym, ya, wmo, wao, wo, gf)
    keys_dims = lambda c: jnp.transpose(c, (0, 3, 1, 2))[None]

    return (y_p, y_s, c_p[None], n_p[None], m_p[None], k_p[None], v_p[None],
            c_s[None], n_s[None], m_s[None], keys_dims(k_t), keys_dims(v_t))
```

```python
import functools

import jax
import jax.numpy as jnp
from jax import lax
from jax.experimental import pallas as pl
from jax.experimental.pallas import tpu as pltpu

F32 = jnp.float32
BF16 = jnp.bfloat16

D_MODEL = 1024
M_HEADS = 4
M_DK = 128
M_DV = 128
M_WIDTH = M_HEADS * M_DV
M_CHUNK = 128
A_HEADS = 8
A_KV_HEADS = 2
A_GROUP = A_HEADS // A_KV_HEADS
A_HEAD_DIM = 64
A_WIDTH = A_HEADS * A_HEAD_DIM
A_KV_WIDTH = A_KV_HEADS * A_HEAD_DIM
WINDOW = 128
ROT_DIM = A_HEAD_DIM // 4
ROPE_THETA = 500000.0
NORM_EPS = 1e-6
PAST_LEN = 16384
IN_SIZES = (M_HEADS * M_DK, M_HEADS * M_DK, M_WIDTH, M_HEADS, M_HEADS, M_WIDTH, M_WIDTH,
            A_WIDTH, A_KV_WIDTH, A_KV_WIDTH, A_WIDTH, D_MODEL, D_MODEL)

LANES = 128

C_GM = 0
C_GA = C_GM + D_MODEL
C_MQ = C_GA + D_MODEL
C_MK = C_MQ + M_WIDTH
C_MV = C_MK + M_WIDTH
C_MO = C_MV + M_WIDTH
C_MZ = C_MO + M_WIDTH
C_AQ = C_MZ + M_WIDTH
C_AZ = C_AQ + A_WIDTH
C_AK = C_AZ + A_WIDTH
C_AV = C_AK + A_KV_WIDTH
C_GI = C_AV + A_KV_WIDTH
C_GF = C_GI + LANES
N_PACK = C_GF + LANES

PROMPT_TILE = 256
SAMPLE_GROUP = 16
VMEM_LIMIT = 56 * 1024 * 1024


def _sigmoid(x):
    return 1.0 / (1.0 + jnp.exp(-x))


def _silu(x):
    return x * _sigmoid(x)


def _log_sigmoid(x):
    return jnp.minimum(x, 0.0) - jnp.log1p(jnp.exp(-jnp.abs(x)))


def _dot(a, b):
    return jnp.dot(a, b, preferred_element_type=F32)


def _dot_nt(a, b):
    return lax.dot_general(a, b, (((1,), (1,)), ((), ())), preferred_element_type=F32)


def _dot_tn(a, b):
    return lax.dot_general(a, b, (((0,), (0,)), ((), ())), preferred_element_type=F32)


def _dot_exact01(m01, x):
    x1 = x.astype(BF16)
    r1 = x - x1.astype(F32)
    x2 = r1.astype(BF16)
    x3 = (r1 - x2.astype(F32)).astype(BF16)
    return _dot(m01, x1) + _dot(m01, x2) + _dot(m01, x3)


def _rms(x):
    return x * lax.rsqrt(jnp.mean(x * x, axis=-1, keepdims=True) + NORM_EPS)


def _rope(blk, ra, rb, rc):
    return blk * ra + pltpu.roll(blk, 8, 1) * rb + pltpu.roll(blk, LANES - 8, 1) * rc


def _out_stage(x, gate, sgm, sga, ym, ya, wmo_ref, wao_ref, wo_ref, gf):
    pm = _dot(ym, wmo_ref[...])
    pa = _dot(ya, wao_ref[...])
    u = sgm * pm + sga * pa
    r = _dot(u.astype(BF16), wo_ref[...])
    return _rms(x + gate * r) * gf


def _adaln_kernel(cp_ref, cs_ref, w_ref, b_ref, op_ref, os_ref):
    w = w_ref[...].astype(BF16)
    b = b_ref[...]
    op_ref[...] = _dot(_silu(cp_ref[...]).astype(BF16), w) + b
    os_ref[...] = _dot(_silu(cs_ref[...]).astype(BF16), w) + b


def _adaln(c_p, c_s, w_ada, b_ada):
    bp, d = c_p.shape
    bs = c_s.shape[0]
    n = w_ada.shape[1]
    tn = 1024
    return pl.pallas_call(
        _adaln_kernel,
        grid=(n // tn,),
        in_specs=[
            pl.BlockSpec((bp, d), lambda j: (0, 0)),
            pl.BlockSpec((bs, d), lambda j: (0, 0)),
            pl.BlockSpec((d, tn), lambda j: (0, j)),
            pl.BlockSpec((1, tn), lambda j: (0, j)),
        ],
        out_specs=[
            pl.BlockSpec((bp, tn), lambda j: (0, j)),
            pl.BlockSpec((bs, tn), lambda j: (0, j)),
        ],
        out_shape=[jax.ShapeDtypeStruct((bp, n), F32), jax.ShapeDtypeStruct((bs, n), F32)],
        compiler_params=pltpu.CompilerParams(dimension_semantics=("arbitrary",)),
        name="adaln",
    )(c_p, c_s, w_ada, b_ada.reshape(1, n))


_SET_FIELDS = ("hb", "q", "k", "vaug", "og", "qa", "kf", "vf", "sz", "sgm", "sga", "g")


def _set_shapes(tile):
    return [
        pltpu.VMEM((tile, D_MODEL), BF16),
        pltpu.VMEM((tile, M_WIDTH), BF16),
        pltpu.VMEM((tile, M_WIDTH), F32),
        pltpu.VMEM((M_HEADS, tile, 2 * LANES), BF16),
        pltpu.VMEM((tile, M_WIDTH), F32),
        pltpu.VMEM((tile, A_WIDTH), BF16),
        pltpu.VMEM((tile, A_KV_WIDTH), F32),
        pltpu.VMEM((tile, A_KV_WIDTH), F32),
        pltpu.VMEM((tile, A_WIDTH), F32),
        pltpu.VMEM((tile, D_MODEL), F32),
        pltpu.VMEM((tile, D_MODEL), F32),
        pltpu.VMEM((tile, 2 * LANES), F32),
    ]


def _project_pieces(x_ref, mod_ref, rope_refs, st, gn_ref, bg_ref, w_ref):
    def norm():
        mod = mod_ref[0]
        h = _rms(x_ref[0]) * gn_ref[...]
        st["hb"][...] = (h * (1.0 + mod[:, D_MODEL:2 * D_MODEL]) + mod[:, 0:D_MODEL]).astype(BF16)

    def proj(lo, n):
        return _dot(st["hb"][...], w_ref[:, lo:lo + n])

    def rope(v):
        ra_ref, rb_ref, rc_ref = rope_refs
        return _rope(v, ra_ref[...], rb_ref[...], rc_ref[...])

    def gates_q():
        st["g"][...] = proj(C_GI, 2 * LANES) + bg_ref[...]
        st["q"][...] = (proj(C_MQ, M_WIDTH) * (M_DK ** -0.5)).astype(BF16)

    def keys():
        st["k"][...] = proj(C_MK, M_WIDTH)

    def values():
        zv = proj(C_MV, M_WIDTH)
        for hd in range(M_HEADS):
            st["vaug"][hd, :, 0:LANES] = zv[:, hd * M_DV:(hd + 1) * M_DV].astype(BF16)

    def attn_kv():
        st["kf"][...] = rope(proj(C_AK, A_KV_WIDTH))
        st["vf"][...] = proj(C_AV, A_KV_WIDTH)

    def attn_q():
        za = proj(C_AQ, A_WIDTH)
        for j in range(A_WIDTH // LANES):
            blk = rope(za[:, j * LANES:(j + 1) * LANES])
            st["qa"][:, j * LANES:(j + 1) * LANES] = (blk * (A_HEAD_DIM ** -0.5)).astype(BF16)

    def out_gates():
        st["og"][...] = _sigmoid(proj(C_MO, M_WIDTH)) * _silu(proj(C_MZ, M_WIDTH))

    def attn_gate():
        st["sz"][...] = _silu(proj(C_AZ, A_WIDTH))

    def merge_gate(name, lo, part):
        def piece():
            cs = slice(part * M_WIDTH, (part + 1) * M_WIDTH)
            st[name][:, cs] = _sigmoid(proj(lo + part * M_WIDTH, M_WIDTH))
        return piece

    return [norm, gates_q, keys, values, attn_kv, attn_q, out_gates, attn_gate,
            merge_gate("sgm", C_GM, 0), merge_gate("sgm", C_GM, 1),
            merge_gate("sga", C_GA, 0), merge_gate("sga", C_GA, 1)]


def _consume_pieces(st, first_tile, sinks_ref, gmn_ref, kbuf_s, vbuf_s, c_s, m_s, ym_s, ya_s, tile):
    L = M_CHUNK
    heads = [slice(hd * M_DK, (hd + 1) * M_DK) for hd in range(M_HEADS)]
    a_heads = [(kv, bi, half) for kv in range(A_KV_HEADS) for bi in range(2) for half in range(2)]

    def causal_mask():
        row = lax.broadcasted_iota(jnp.int32, (L, L), 0)
        col = lax.broadcasted_iota(jnp.int32, (L, L), 1)
        return row >= col

    def chunk_stages(j):
        rs = slice(j * L, (j + 1) * L)
        cx = {}

        def gates():
            g = st["g"][rs, :]
            ig_all = g[:, 0:LANES]
            lf_all = _log_sigmoid(g[:, LANES:2 * LANES])
            tril = jnp.where(causal_mask(), 1.0, 0.0).astype(BF16)
            b_all = _dot_exact01(tril, lf_all)
            b_last = b_all[L - 1:L, :]
            m_old = m_s[...]
            d_end = b_last - b_all + ig_all
            m_new = jnp.maximum(b_last + m_old, jnp.max(d_end, axis=0, keepdims=True))
            m_s[...] = m_new
            cx.update(b_all=b_all, w_end=jnp.exp(d_end - m_new), decay=jnp.exp(b_last + m_old - m_new),
                      inter=b_all + m_old, r_t=jnp.transpose(ig_all - b_all))

        def kv_buffers():
            srows = slice((j % 2) * L, (j % 2 + 1) * L)
            lo_half = lax.broadcasted_iota(jnp.int32, (L, L), 1) < A_HEAD_DIM
            kc = st["kf"][rs, :]
            vc = st["vf"][rs, :]
            kr = pltpu.roll(kc, A_HEAD_DIM, 1)
            vr = pltpu.roll(vc, A_HEAD_DIM, 1)
            kbuf_s[0, srows, :] = jnp.where(lo_half, kc, 0.0).astype(BF16)
            kbuf_s[1, srows, :] = jnp.where(lo_half, 0.0, kr).astype(BF16)
            kbuf_s[2, srows, :] = jnp.where(lo_half, kr, 0.0).astype(BF16)
            kbuf_s[3, srows, :] = jnp.where(lo_half, 0.0, kc).astype(BF16)
            vbuf_s[0, srows, :] = jnp.where(lo_half, vc, 0.0).astype(BF16)
            vbuf_s[1, srows, :] = jnp.where(lo_half, 0.0, vr).astype(BF16)
            vbuf_s[2, srows, :] = jnp.where(lo_half, vr, 0.0).astype(BF16)
            vbuf_s[3, srows, :] = jnp.where(lo_half, 0.0, vc).astype(BF16)

        def decay_matrices():
            cx["dmat"] = [jnp.where(causal_mask(), cx["b_all"][:, hd:hd + 1] + cx["r_t"][hd:hd + 1, :], -jnp.inf)
                          for hd in range(M_HEADS)]
            cx["m_hat"] = [jnp.maximum(cx["inter"][:, hd:hd + 1], jnp.max(cx["dmat"][hd], axis=-1, keepdims=True))
                           for hd in range(M_HEADS)]
            cx["kw"] = [(st["k"][rs, heads[hd]] * cx["w_end"][:, hd:hd + 1]).astype(BF16) for hd in range(M_HEADS)]

        def state_matmuls():
            cx["qk"], cx["qc"] = [], []
            for hd in range(M_HEADS):
                qh = st["q"][rs, heads[hd]]
                c_old = c_s[hd]
                cx["qk"].append(_dot_nt(qh, st["k"][rs, heads[hd]].astype(BF16)))
                cx["qc"].append(_dot(qh, c_old.astype(BF16)))
                va = st["vaug"][hd, rs, :]
                c_s[hd] = cx["decay"][:, hd:hd + 1] * c_old + _dot_tn(cx["kw"][hd], va)

        def attn_scores():
            row2 = lax.broadcasted_iota(jnp.int32, (L, 2 * L), 0)
            col2 = lax.broadcasted_iota(jnp.int32, (L, 2 * L), 1)
            no_prev = jnp.where(first_tile, 4 * L, 0) if j == 0 else 0
            if j % 2 == 0:
                mask = ((col2 < L) & (col2 <= row2)) | ((col2 >= L) & (col2 - L > row2 + no_prev))
            else:
                mask = ((col2 >= L) & (col2 - L <= row2)) | ((col2 < L) & (col2 > row2 + no_prev))
            cx["sh"] = {}
            for kv in range(A_KV_HEADS):
                qcat = jnp.concatenate([st["qa"][rs, (2 * kv) * LANES:(2 * kv + 1) * LANES],
                                        st["qa"][rs, (2 * kv + 1) * LANES:(2 * kv + 2) * LANES]], axis=0)
                for half in range(2):
                    sc = _dot_nt(qcat, kbuf_s[2 * kv + half])
                    for bi in range(2):
                        cx["sh"][kv, bi, half] = jnp.where(mask, sc[bi * L:(bi + 1) * L, :], -jnp.inf)

        def attn_max():
            cx["mx"] = {hk: jnp.maximum(jnp.max(cx["sh"][hk], axis=-1, keepdims=True),
                                        sinks_ref[A_GROUP * hk[0] + 2 * hk[1] + hk[2]]) for hk in a_heads}

        def stabilised_scores():
            cx["s"] = [(cx["qk"][hd] * jnp.exp(cx["dmat"][hd] - cx["m_hat"][hd])).astype(BF16)
                       for hd in range(M_HEADS)]

        def score_value_matmuls():
            cx["sv"] = [_dot(cx["s"][hd], st["vaug"][hd, rs, :]) for hd in range(M_HEADS)]

        def attn_exp():
            cx["p"] = {hk: jnp.exp(cx["sh"][hk] - cx["mx"][hk]) for hk in a_heads}
            cx["den"] = {hk: jnp.sum(cx["p"][hk], axis=-1, keepdims=True)
                         + jnp.exp(sinks_ref[A_GROUP * hk[0] + 2 * hk[1] + hk[2]] - cx["mx"][hk]) for hk in a_heads}

        def attn_normalise():
            cx["pn"] = {hk: (cx["p"][hk] / cx["den"][hk]).astype(BF16) for hk in a_heads}

        def attn_value_matmuls():
            for kv in range(A_KV_HEADS):
                for bi in range(2):
                    o = (_dot(cx["pn"][kv, bi, 0], vbuf_s[2 * kv]) + _dot(cx["pn"][kv, bi, 1], vbuf_s[2 * kv + 1]))
                    cs = slice((2 * kv + bi) * LANES, (2 * kv + bi + 1) * LANES)
                    ya_s[rs, cs] = (o * st["sz"][rs, cs]).astype(BF16)

        def mlstm_outputs():
            hh = []
            for hd in range(M_HEADS):
                m_hat = cx["m_hat"][hd]
                w_inter = jnp.exp(cx["inter"][:, hd:hd + 1] - m_hat)
                tot = cx["sv"][hd] + w_inter * cx["qc"][hd]
                hh.append(tot[:, 0:M_DV] / jnp.maximum(jnp.abs(tot[:, M_DV:M_DV + 1]), jnp.exp(-m_hat)))
            ms = [jnp.mean(h * h, axis=-1, keepdims=True) for h in hh]
            for hd in range(M_HEADS):
                hs = heads[hd]
                ym_s[rs, hs] = (hh[hd] * lax.rsqrt(ms[hd] + NORM_EPS) * gmn_ref[:, hs] * st["og"][rs, hs]).astype(BF16)

        mlstm = [gates, decay_matrices, state_matmuls, stabilised_scores, score_value_matmuls, mlstm_outputs]
        attn = [kv_buffers, attn_scores, attn_max, attn_exp, attn_normalise, attn_value_matmuls]
        return mlstm, attn

    assert tile == 2 * L
    (m0, a0), (m1, a1) = chunk_stages(0), chunk_stages(1)
    merged = []
    for m_stage, a_stage in zip(m0 + m1, a0 + a1):
        merged += [m_stage, a_stage]
    return merged


def _output_pieces(st, x_ref, rows, gate_ref, ym_s, ya_s, u_s, wmo_ref, wao_ref, wo_ref, gf_ref, y_ref):
    def merge():
        u = st["sgm"][...] * _dot(ym_s[...], wmo_ref[...]) + st["sga"][...] * _dot(ya_s[...], wao_ref[...])
        u_s[...] = u.astype(BF16)

    def out():
        gate = gate_ref[0][:, 2 * D_MODEL:3 * D_MODEL]
        r = _dot(u_s[...], wo_ref[...])
        y_ref[0, rows, :] = _rms(x_ref[0, rows, :] + gate * r) * gf_ref[...]

    return [merge, out]


def _interleave(primary, filler):
    done = 0
    for i, piece in enumerate(primary):
        piece()
        upto = ((i + 1) * len(filler)) // len(primary)
        for f in filler[done:upto]:
            f()
        done = upto


def _prompt_kernel(sinks_ref, x0_ref, xa_ref, xb_ref, xres_ref, mod0_ref, moda_ref, modb_ref,
                   gn_ref, gf_ref, gmn_ref, bg_ref, w_ref, wmo_ref, wao_ref, wo_ref,
                   ra0_ref, rb0_ref, rc0_ref, raa_ref, rba_ref, rca_ref, rab_ref, rbb_ref, rcb_ref,
                   y_ref, c_out_ref, n_out_ref, m_out_ref, kk_ref, vk_ref, *scratch, tile, steps_per_seq):
    nset = len(_SET_FIELDS)
    set_a = dict(zip(_SET_FIELDS, scratch[0:nset]))
    set_b = dict(zip(_SET_FIELDS, scratch[nset:2 * nset]))
    kbuf_s, vbuf_s, c_s, m_s, yma_s, yaa_s, ua_s, ymb_s, yab_s, ub_s = scratch[2 * nset:]
    k = pl.program_id(0)
    first = (k % steps_per_seq) == 0
    project = lambda x_ref, mod_ref, rope, st: _project_pieces(x_ref, mod_ref, rope, st, gn_ref, bg_ref, w_ref)

    @pl.when(k == 0)
    def _prologue():
        lane = lax.broadcasted_iota(jnp.int32, (tile, LANES), 1)
        ones_col = jnp.where(lane == 0, 1.0, 0.0).astype(BF16)
        for st in (set_a, set_b):
            for hd in range(M_HEADS):
                st["vaug"][hd, :, LANES:2 * LANES] = ones_col
        for piece in project(x0_ref, mod0_ref, (ra0_ref, rb0_ref, rc0_ref), set_a):
            piece()

    @pl.when(first)
    def _init():
        c_s[...] = jnp.zeros_like(c_s)
        m_s[...] = jnp.zeros_like(m_s)
        kbuf_s[...] = jnp.zeros_like(kbuf_s)
        vbuf_s[...] = jnp.zeros_like(vbuf_s)

    state = (sinks_ref, gmn_ref, kbuf_s, vbuf_s, c_s, m_s)
    out_w = (wmo_ref, wao_ref, wo_ref, gf_ref, y_ref)
    _interleave(_consume_pieces(set_a, first, *state, yma_s, yaa_s, tile),
                project(xa_ref, moda_ref, (raa_ref, rba_ref, rca_ref), set_b))
    _interleave(_consume_pieces(set_b, False, *state, ymb_s, yab_s, tile),
                _output_pieces(set_a, xres_ref, slice(0, tile), moda_ref, yma_s, yaa_s, ua_s, *out_w)
                + project(xb_ref, modb_ref, (rab_ref, rbb_ref, rcb_ref), set_a))
    for piece in _output_pieces(set_b, xres_ref, slice(tile, 2 * tile), moda_ref, ymb_s, yab_s, ub_s, *out_w):
        piece()

    @pl.when((k % steps_per_seq) == steps_per_seq - 1)
    def _final():
        for hd in range(M_HEADS):
            cf = c_s[hd]
            c_out_ref[0, hd] = cf[:, 0:M_DV]
            n_out_ref[0, hd:hd + 1, :] = jnp.transpose(cf[:, M_DV:2 * M_DV])[0:1, :]
        m_out_ref[0] = m_s[...]
        k_t = jnp.transpose(set_b["kf"][tile - WINDOW:tile, :])
        v_t = jnp.transpose(set_b["vf"][tile - WINDOW:tile, :])
        for kv in range(A_KV_HEADS):
            kk_ref[0, kv] = k_t[kv * A_HEAD_DIM:(kv + 1) * A_HEAD_DIM, :]
            vk_ref[0, kv] = v_t[kv * A_HEAD_DIM:(kv + 1) * A_HEAD_DIM, :]


def _rope_tables(pos):
    half = ROT_DIM // 2
    dim = jnp.arange(LANES, dtype=jnp.int32) % A_HEAD_DIM
    inv = ROPE_THETA ** (-(2 * (dim % half)).astype(F32) / ROT_DIM)
    ang = pos.astype(F32)[:, None] * inv[None, :]
    cos, sin = jnp.cos(ang), jnp.sin(ang)
    ra = jnp.where(dim < ROT_DIM, cos, 1.0)
    rb = jnp.where((dim >= half) & (dim < ROT_DIM), sin, 0.0)
    rc = jnp.where(dim < half, -sin, 0.0)
    return ra, rb, rc


def _full(shape, single_buffer=False):
    mode = dict(pipeline_mode=pl.Buffered(1)) if single_buffer else {}
    return pl.BlockSpec(shape, lambda *_: (0,) * len(shape), **mode)


def _prompt_layer(x, mod, sinks, gn, gf, gmn, bias_g, wp, wmo, wao, wo):
    bsz, seq, d = x.shape
    tile = PROMPT_TILE
    assert seq % (2 * tile) == 0 and tile % (2 * M_CHUNK) == 0 and d == D_MODEL
    tiles_per_seq = seq // tile
    steps_per_seq = tiles_per_seq // 2
    n_tiles = bsz * tiles_per_seq
    n_steps = n_tiles // 2
    ra, rb, rc = _rope_tables(jnp.arange(seq, dtype=jnp.int32))
    xt = x.reshape(n_tiles, tile, d)
    xp = x.reshape(n_steps, 2 * tile, d)
    mod3 = mod.reshape(bsz, 1, 3 * d)

    tile_0 = lambda k: 0
    tile_a = lambda k: 2 * k + 1
    tile_b = lambda k: jnp.minimum(2 * k + 2, n_tiles - 1)
    x_spec = lambda f: pl.BlockSpec((1, tile, d), lambda k: (f(k), 0, 0))
    mod_spec = lambda f: pl.BlockSpec((1, 1, 3 * d), lambda k: (f(k) // tiles_per_seq, 0, 0))
    rope_spec = lambda f: pl.BlockSpec((tile, LANES), lambda k: (f(k) % tiles_per_seq, 0))
    in_specs = [
        pl.BlockSpec(memory_space=pltpu.SMEM),
        x_spec(tile_0), x_spec(tile_a), x_spec(tile_b),
        pl.BlockSpec((1, 2 * tile, d), lambda k: (k, 0, 0)),
        mod_spec(tile_0), mod_spec(tile_a), mod_spec(tile_b),
        _full((1, d)), _full((1, d)), _full((1, M_WIDTH)), _full((1, 2 * LANES)),
        _full((d, N_PACK), True), _full((M_WIDTH, d), True), _full((A_WIDTH, d), True), _full((d, d), True),
        rope_spec(tile_0), rope_spec(tile_0), rope_spec(tile_0),
        rope_spec(tile_a), rope_spec(tile_a), rope_spec(tile_a),
        rope_spec(tile_b), rope_spec(tile_b), rope_spec(tile_b),
    ]
    seq_of = lambda k: k // steps_per_seq
    out_specs = [
        pl.BlockSpec((1, 2 * tile, d), lambda k: (k, 0, 0)),
        pl.BlockSpec((1, M_HEADS, M_DK, M_DV), lambda k: (seq_of(k), 0, 0, 0)),
        pl.BlockSpec((1, M_HEADS, M_DK), lambda k: (seq_of(k), 0, 0)),
        pl.BlockSpec((1, 1, LANES), lambda k: (seq_of(k), 0, 0)),
        pl.BlockSpec((1, A_KV_HEADS, A_HEAD_DIM, WINDOW), lambda k: (seq_of(k), 0, 0, 0)),
        pl.BlockSpec((1, A_KV_HEADS, A_HEAD_DIM, WINDOW), lambda k: (seq_of(k), 0, 0, 0)),
    ]
    out_shape = [
        jax.ShapeDtypeStruct((n_steps, 2 * tile, d), F32),
        jax.ShapeDtypeStruct((bsz, M_HEADS, M_DK, M_DV), F32),
        jax.ShapeDtypeStruct((bsz, M_HEADS, M_DK), F32),
        jax.ShapeDtypeStruct((bsz, 1, LANES), F32),
        jax.ShapeDtypeStruct((bsz, A_KV_HEADS, A_HEAD_DIM, WINDOW), F32),
        jax.ShapeDtypeStruct((bsz, A_KV_HEADS, A_HEAD_DIM, WINDOW), F32),
    ]
    scratch = _set_shapes(tile) + _set_shapes(tile) + [
        pltpu.VMEM((2 * A_KV_HEADS, 2 * WINDOW, LANES), BF16),
        pltpu.VMEM((2 * A_KV_HEADS, 2 * WINDOW, LANES), BF16),
        pltpu.VMEM((M_HEADS, M_DK, 2 * LANES), F32),
        pltpu.VMEM((1, LANES), F32),
        pltpu.VMEM((tile, M_WIDTH), BF16),
        pltpu.VMEM((tile, A_WIDTH), BF16),
        pltpu.VMEM((tile, D_MODEL), BF16),
        pltpu.VMEM((tile, M_WIDTH), BF16),
        pltpu.VMEM((tile, A_WIDTH), BF16),
        pltpu.VMEM((tile, D_MODEL), BF16),
    ]
    outs = pl.pallas_call(
        functools.partial(_prompt_kernel, tile=tile, steps_per_seq=steps_per_seq),
        grid=(n_steps,),
        in_specs=in_specs,
        out_specs=out_specs,
        out_shape=out_shape,
        scratch_shapes=scratch,
        compiler_params=pltpu.CompilerParams(dimension_semantics=("arbitrary",), vmem_limit_bytes=VMEM_LIMIT),
        name="prompt_layer",
    )(sinks, xt, xt, xt, xp, mod3, mod3, mod3, gn, gf, gmn, bias_g, wp, wmo, wao, wo,
      ra, rb, rc, ra, rb, rc, ra, rb, rc)
    return (outs[0].reshape(bsz, seq, d),) + tuple(outs[1:])


def _sample_proj_kernel(x_ref, mod_ref, gn_ref, w_ref, z_ref, hb_s):
    bsz, steps, _ = x_ref.shape

    @pl.when(pl.program_id(0) == 0)
    def _norm():
        mod = mod_ref[...]
        for t in range(steps):
            h = _rms(x_ref[:, t, :]) * gn_ref[...]
            hb_s[t * bsz:(t + 1) * bsz, :] = (h * (1.0 + mod[:, D_MODEL:2 * D_MODEL]) + mod[:, 0:D_MODEL]).astype(BF16)

    z = _dot(hb_s[...], w_ref[...])
    for t in range(steps):
        z_ref[t] = z[t * bsz:(t + 1) * bsz, :]


def _sample_proj(x3, mod_s, gn, wp):
    bsz, steps, d = x3.shape
    tn = 1024
    return pl.pallas_call(
        _sample_proj_kernel,
        grid=(N_PACK // tn,),
        in_specs=[
            _full((bsz, steps, d)),
            pl.BlockSpec((bsz, 2 * d), lambda j: (0, 0)),
            _full((1, d)),
            pl.BlockSpec((d, tn), lambda j: (0, j)),
        ],
        out_specs=pl.BlockSpec((steps, bsz, tn), lambda j: (0, 0, j)),
        out_shape=jax.ShapeDtypeStruct((steps, bsz, N_PACK), F32),
        scratch_shapes=[pltpu.VMEM((steps * bsz, d), BF16)],
        compiler_params=pltpu.CompilerParams(dimension_semantics=("arbitrary",)),
        name="sample_proj",
    )(x3, mod_s, gn, wp)


def _sample_state_kernel(sinks_ref, z_ref, c_ref, n_ref, m_ref, ck_ref, cv_ref, rope_ref, bg_ref, gmn_ref,
                         cn_ref, nn_ref, mn_ref, ym_ref, ya_ref, ko_ref, vo_ref, *, group, steps):
    T, G = steps, group
    toks = range(T)

    def z(t, lo, n):
        return z_ref[t, :, lo:lo + n]

    def rope_t(t, blk):
        return _rope(blk, rope_ref[0, t:t + 1, :], rope_ref[1, t:t + 1, :], rope_ref[2, t:t + 1, :])

    lane = lax.broadcasted_iota(jnp.int32, (G, LANES), 1)
    lo_half = lane < A_HEAD_DIM
    gmn = gmn_ref[...]

    q = [z(t, C_MQ, M_WIDTH) * (M_DK ** -0.5) for t in toks]
    k = [z(t, C_MK, M_WIDTH) for t in toks]
    v = [z(t, C_MV, M_WIDTH) for t in toks]
    og = [_sigmoid(z(t, C_MO, M_WIDTH)) * _silu(z(t, C_MZ, M_WIDTH)) for t in toks]
    qa = [[rope_t(t, z(t, C_AQ + j * LANES, LANES)) * (A_HEAD_DIM ** -0.5) for j in range(A_WIDTH // LANES)]
          for t in toks]
    kn = [rope_t(t, z(t, C_AK, A_KV_WIDTH)) for t in toks]
    vn = [z(t, C_AV, A_KV_WIDTH) for t in toks]
    sz = [_silu(z(t, C_AZ, A_WIDTH)) for t in toks]
    gt = [z(t, C_GI, 2 * LANES) + bg_ref[...] for t in toks]
    ig = [gt[t][:, 0:LANES] for t in toks]
    lf = [_log_sigmoid(gt[t][:, LANES:2 * LANES]) for t in toks]


    m_all = m_ref[...]
    rep = lambda a, h: jnp.broadcast_to(a[:, h:h + 1], (G, LANES))
    gates = []
    for hd in range(M_HEADS):
        ig_h = [rep(ig[t], hd) for t in toks]
        lf_h = [rep(lf[t], hd) for t in toks]
        m_old = rep(m_all, hd)
        bsum = [lf_h[0]]
        for t in range(1, T):
            bsum.append(bsum[-1] + lf_h[t])
        b_last = bsum[-1]
        d_end = [b_last - bsum[s] + ig_h[s] for s in toks]
        m_new = b_last + m_old
        for s in toks:
            m_new = jnp.maximum(m_new, d_end[s])
        mn_ref[:, hd:hd + 1] = m_new[:, 0:1]
        gd = dict(w_end=[jnp.exp(d_end[s] - m_new) for s in toks], decay=jnp.exp(b_last + m_old - m_new),
                  e=[], w_inter=[], emh=[])
        for t in toks:
            inter = bsum[t] + m_old
            dm = [bsum[t] - bsum[s] + ig_h[s] for s in range(t + 1)]
            m_hat = inter
            for s in range(t + 1):
                m_hat = jnp.maximum(m_hat, dm[s])
            gd["e"].append([jnp.exp(dm[s] - m_hat) for s in range(t + 1)])
            gd["w_inter"].append(jnp.exp(inter - m_hat))
            gd["emh"].append(jnp.exp(-m_hat))
        gates.append(gd)

    heads = [slice(hd * M_DK, (hd + 1) * M_DK) for hd in range(M_HEADS)]
    n_old = [n_ref[:, hd, :] for hd in range(M_HEADS)]
    qk = {(hd, t, s): jnp.sum(q[t][:, heads[hd]] * k[s][:, heads[hd]], axis=-1, keepdims=True)
          for hd in range(M_HEADS) for t in toks for s in range(t + 1)}
    qn = {(hd, t): jnp.sum(q[t][:, heads[hd]] * n_old[hd], axis=-1, keepdims=True)
          for hd in range(M_HEADS) for t in toks}

    lo_f = jnp.where(lo_half, 1.0, 0.0)
    hi_f = 1.0 - lo_f
    dup = lambda a, kv: (jnp.where(lo_half, a, pltpu.roll(a, A_HEAD_DIM, 1)) if kv == 0 else
                         jnp.where(lo_half, pltpu.roll(a, A_HEAD_DIM, 1), a))
    kn_x = [[dup(kn[s], kv) for s in toks] for kv in range(A_KV_HEADS)]
    vn_x = [[dup(vn[s], kv) for s in toks] for kv in range(A_KV_HEADS)]
    s_new = {}
    for kv in range(A_KV_HEADS):
        for blk in range(2):
            for t in toks:
                for s in range(t + 1):
                    prod = qa[t][2 * kv + blk] * kn_x[kv][s]
                    s_new[kv, 0, blk, t, s] = jnp.sum(prod * lo_f, axis=-1, keepdims=True)
                    s_new[kv, 1, blk, t, s] = jnp.sum(prod * hi_f, axis=-1, keepdims=True)

    rows_tg = lax.broadcasted_iota(jnp.int32, (T * G, LANES), 0) % G
    own_rows = [rows_tg == b for b in range(G)]
    rows64 = lax.broadcasted_iota(jnp.int32, (2 * T * G, LANES), 0) % G

    qc = []
    for hd in range(M_HEADS):
        hs = heads[hd]
        gd = gates[hd]
        q32 = jnp.concatenate([q[t][:, hs] for t in toks], axis=0)
        kw = [k[s][:, hs] * gd["w_end"][s] for s in toks]
        kw_t = jnp.transpose(jnp.concatenate(kw, axis=0)).astype(BF16)
        v32 = jnp.concatenate([v[s][:, hs] for s in toks], axis=0)
        acc = None
        for b in range(G):
            c_old = c_ref[b, hd]
            part = _dot(jnp.where(own_rows[b], q32, 0.0).astype(BF16), c_old.astype(BF16))
            acc = part if acc is None else acc + part
            upd = _dot(kw_t, jnp.where(own_rows[b], v32, 0.0).astype(BF16))
            cn_ref[b, hd] = gd["decay"][b:b + 1, 0:1] * c_old + upd
        qc.append(acc)
        nn_ref[:, hd, :] = gd["decay"] * n_old[hd] + (kw[0] + kw[1] + kw[2] + kw[3])

    zeros_kt = jnp.zeros((A_HEAD_DIM, WINDOW), BF16)
    sc = {}
    for kv in range(A_KV_HEADS):
        l64 = jnp.concatenate([qa[t][2 * kv + blk] for blk in range(2) for t in toks], axis=0)
        own64 = [jnp.where(rows64 == b, l64, 0.0).astype(BF16) for b in range(G)]
        kt = [ck_ref[b, kv].astype(BF16) for b in range(G)]
        for par in range(2):
            acc = None
            for b in range(G):
                rhs = jnp.concatenate([kt[b], zeros_kt] if par == 0 else [zeros_kt, kt[b]], axis=0)
                part = _dot(own64[b], rhs)
                acc = part if acc is None else acc + part
            sc[kv, par] = acc

    hh = {}
    for hd in range(M_HEADS):
        gd = gates[hd]
        for t in toks:
            sv, ssum = None, None
            for s in range(t + 1):
                s_ts = qk[hd, t, s] * gd["e"][t][s]
                sv = s_ts * v[s][:, heads[hd]] if sv is None else sv + s_ts * v[s][:, heads[hd]]
                ssum = s_ts if ssum is None else ssum + s_ts
            wi = gd["w_inter"][t]
            num = sv + wi * qc[hd][t * G:(t + 1) * G, :]
            den = ssum + wi * qn[hd, t]
            hh[hd, t] = num / jnp.maximum(jnp.abs(den), gd["emh"][t])
    ms = {key: jnp.mean(val * val, axis=-1, keepdims=True) for key, val in hh.items()}
    for (hd, t), val in hh.items():
        hs = heads[hd]
        ym_ref[t, :, hs] = val * lax.rsqrt(ms[hd, t] + NORM_EPS) * gmn[:, hs] * og[t][:, hs]

    items = [(kv, par, blk, t) for kv in range(A_KV_HEADS) for par in range(2) for blk in range(2) for t in toks]
    s_c = {it: jnp.where(lane > it[3], sc[it[0], it[1]][(it[2] * T + it[3]) * G:(it[2] * T + it[3] + 1) * G, :], -jnp.inf)
           for it in items}
    mx_c = {it: jnp.max(s_c[it], axis=-1, keepdims=True) for it in items}
    p_c, p_n, sink_t = {}, {}, {}
    for it in items:
        kv, par, blk, t = it
        sink = sinks_ref[A_GROUP * kv + 2 * blk + par]
        mx = jnp.maximum(mx_c[it], sink)
        for s in range(t + 1):
            mx = jnp.maximum(mx, s_new[kv, par, blk, t, s])
        p_c[it] = jnp.exp(s_c[it] - mx)
        p_n[it] = [jnp.exp(s_new[kv, par, blk, t, s] - mx) for s in range(t + 1)]
        sink_t[it] = jnp.exp(sink - mx)
    sum_c = {it: jnp.sum(p_c[it], axis=-1, keepdims=True) for it in items}
    fresh = {}
    probs = {(kv, par): [] for kv in range(A_KV_HEADS) for par in range(2)}
    for it in items:
        kv, par, blk, t = it
        den = sum_c[it] + sink_t[it]
        for s in range(t + 1):
            den = den + p_n[it][s]
        r = 1.0 / den
        probs[kv, par].append(p_c[it] * r)
        acc = None
        for s in range(t + 1):
            term = (p_n[it][s] * r) * vn_x[kv][s]
            acc = term if acc is None else acc + term
        fresh[it] = acc

    for kv in range(A_KV_HEADS):
        vt = [cv_ref[b, kv].astype(BF16) for b in range(G)]
        o = []
        for par in range(2):
            pstack = jnp.concatenate(probs[kv, par], axis=0)
            acc = None
            for b in range(G):
                rhs = jnp.concatenate([vt[b], zeros_kt] if par == 0 else [zeros_kt, vt[b]], axis=0)
                part = _dot_nt(jnp.where(rows64 == b, pstack, 0.0).astype(BF16), rhs)
                acc = part if acc is None else acc + part
            o.append(acc)
        for blk in range(2):
            cs = slice((2 * kv + blk) * LANES, (2 * kv + blk + 1) * LANES)
            for t in toks:
                rs = slice((blk * T + t) * G, (blk * T + t + 1) * G)
                new_v = jnp.where(lo_half, fresh[kv, 0, blk, t], fresh[kv, 1, blk, t])
                ya_ref[t, :, cs] = (o[0][rs, :] + o[1][rs, :] + new_v) * sz[t][:, cs]

    n_rows = T * G
    src = lax.broadcasted_iota(jnp.int32, (n_rows, n_rows), 1)
    dst = lax.broadcasted_iota(jnp.int32, (n_rows, n_rows), 0)
    perm = jnp.where(src == (dst % T) * G + dst // T, 1.0, 0.0).astype(BF16)
    lane_d = lax.broadcasted_iota(jnp.int32, (A_HEAD_DIM, WINDOW), 1)
    for new, cache_ref, out_ref in ((kn, ck_ref, ko_ref), (vn, cv_ref, vo_ref)):
        stack = _dot_exact01(perm, jnp.concatenate(new, axis=0))
        stack = jnp.concatenate([stack, jnp.zeros((LANES - n_rows, LANES), F32)], axis=0)
        new_t = jnp.transpose(stack)
        for b in range(G):
            placed = pltpu.roll(new_t, (WINDOW - T - T * b) % LANES, 1)
            for kv in range(A_KV_HEADS):
                slid = pltpu.roll(cache_ref[b, kv], WINDOW - T, 1)
                out_ref[b, kv] = jnp.where(lane_d >= WINDOW - T, placed[kv * A_HEAD_DIM:(kv + 1) * A_HEAD_DIM, :], slid)


def _sample_state(zs, state_c, state_n, state_m, ck_t, cv_t, rope, bias_g, gmn, sinks):
    steps, bsz, _ = zs.shape
    group = SAMPLE_GROUP
    assert bsz % group == 0
    in_specs = [
        pl.BlockSpec(memory_space=pltpu.SMEM),
        pl.BlockSpec((steps, group, N_PACK), lambda i: (0, i, 0)),
        pl.BlockSpec((group, M_HEADS, M_DK, M_DV), lambda i: (i, 0, 0, 0)),
        pl.BlockSpec((group, M_HEADS, M_DK), lambda i: (i, 0, 0)),
        pl.BlockSpec((group, M_HEADS), lambda i: (i, 0)),
        pl.BlockSpec((group, A_KV_HEADS, A_HEAD_DIM, WINDOW), lambda i: (i, 0, 0, 0)),
        pl.BlockSpec((group, A_KV_HEADS, A_HEAD_DIM, WINDOW), lambda i: (i, 0, 0, 0)),
        _full((3, steps, LANES)), _full((1, 2 * LANES)), _full((1, M_WIDTH)),
    ]
    out_specs = [
        pl.BlockSpec((group, M_HEADS, M_DK, M_DV), lambda i: (i, 0, 0, 0)),
        pl.BlockSpec((group, M_HEADS, M_DK), lambda i: (i, 0, 0)),
        pl.BlockSpec((group, M_HEADS), lambda i: (i, 0)),
        pl.BlockSpec((steps, group, M_WIDTH), lambda i: (0, i, 0)),
        pl.BlockSpec((steps, group, A_WIDTH), lambda i: (0, i, 0)),
        pl.BlockSpec((group, A_KV_HEADS, A_HEAD_DIM, WINDOW), lambda i: (i, 0, 0, 0)),
        pl.BlockSpec((group, A_KV_HEADS, A_HEAD_DIM, WINDOW), lambda i: (i, 0, 0, 0)),
    ]
    out_shape = [
        jax.ShapeDtypeStruct((bsz, M_HEADS, M_DK, M_DV), F32),
        jax.ShapeDtypeStruct((bsz, M_HEADS, M_DK), F32),
        jax.ShapeDtypeStruct((bsz, M_HEADS), F32),
        jax.ShapeDtypeStruct((steps, bsz, M_WIDTH), F32),
        jax.ShapeDtypeStruct((steps, bsz, A_WIDTH), F32),
        jax.ShapeDtypeStruct((bsz, A_KV_HEADS, A_HEAD_DIM, WINDOW), F32),
        jax.ShapeDtypeStruct((bsz, A_KV_HEADS, A_HEAD_DIM, WINDOW), F32),
    ]
    return pl.pallas_call(
        functools.partial(_sample_state_kernel, group=group, steps=steps),
        grid=(bsz // group,),
        in_specs=in_specs,
        out_specs=out_specs,
        out_shape=out_shape,
        compiler_params=pltpu.CompilerParams(dimension_semantics=("arbitrary",), vmem_limit_bytes=VMEM_LIMIT),
        name="sample_state",
    )(sinks, zs, state_c, state_n, state_m, ck_t, cv_t, rope, bias_g, gmn)


def _sample_out_kernel(x_ref, gate_ref, z_ref, ym_ref, ya_ref, wmo_ref, wao_ref, wo_ref, gf_ref, y_ref):
    bsz, steps, _ = x_ref.shape
    stack = lambda ref, cs: jnp.concatenate([ref[t, :, cs] for t in range(steps)], axis=0)
    full = slice(None)
    x = jnp.concatenate([x_ref[:, t, :] for t in range(steps)], axis=0)
    gate = jnp.concatenate([gate_ref[...]] * steps, axis=0)
    y = _out_stage(x, gate, _sigmoid(stack(z_ref, slice(0, D_MODEL))), _sigmoid(stack(z_ref, slice(D_MODEL, 2 * D_MODEL))),
                   stack(ym_ref, full).astype(BF16), stack(ya_ref, full).astype(BF16), wmo_ref, wao_ref, wo_ref, gf_ref[...])
    for t in range(steps):
        y_ref[:, t, :] = y[t * bsz:(t + 1) * bsz, :]


def _sample_out(x3, mod_s, zs, ym, ya, wmo, wao, wo, gf):
    bsz, steps, d = x3.shape
    return pl.pallas_call(
        _sample_out_kernel,
        grid=(1,),
        in_specs=[
            _full((bsz, steps, d)),
            pl.BlockSpec((bsz, d), lambda i: (0, 2)),
            pl.BlockSpec((steps, bsz, 2 * d), lambda i: (0, 0, 0)),
            _full((steps, bsz, M_WIDTH)), _full((steps, bsz, A_WIDTH)),
            _full((M_WIDTH, d)), _full((A_WIDTH, d)), _full((d, d)), _full((1, d)),
        ],
        out_specs=_full((bsz, steps, d)),
        out_shape=jax.ShapeDtypeStruct((bsz, steps, d), F32),
        compiler_params=pltpu.CompilerParams(dimension_semantics=("arbitrary",), vmem_limit_bytes=VMEM_LIMIT),
        name="sample_out",
    )(x3, mod_s, zs, ym, ya, wmo, wao, wo, gf)


PACK_BLOCK = 512


def _pack_sources():
    names = ("mq", "mk", "mv", "mi", "mf", "mo", "mz", "aq", "ak", "av", "az", "gm", "ga")
    start, pos = {}, 0
    for name, size in zip(names, IN_SIZES):
        start[name] = pos
        pos += size
    assert start["av"] == start["ak"] + A_KV_WIDTH and start["mf"] == start["mi"] + M_HEADS
    order = (("gm", C_GM, D_MODEL), ("ga", C_GA, D_MODEL), ("mq", C_MQ, M_WIDTH), ("mk", C_MK, M_WIDTH),
             ("mv", C_MV, M_WIDTH), ("mo", C_MO, M_WIDTH), ("mz", C_MZ, M_WIDTH), ("aq", C_AQ, A_WIDTH),
             ("az", C_AZ, A_WIDTH))
    src = []
    for name, col, width in order:
        assert col == len(src) * PACK_BLOCK and width % PACK_BLOCK == 0
        src += [start[name] + i * PACK_BLOCK for i in range(width // PACK_BLOCK)]
    assert C_AK == len(src) * PACK_BLOCK and C_AV == C_AK + A_KV_WIDTH and C_GI == C_AV + A_KV_WIDTH
    assert C_GF == C_GI + LANES and N_PACK == C_AK + PACK_BLOCK and start["ak"] + PACK_BLOCK <= pos
    return src + [start["ak"]], start["mi"]


def _pack_kernel(src_ref, wt_ref, gate_ref, o_ref):
    j = pl.program_id(0)
    last = pl.num_programs(0) - 1

    @pl.when(j < last)
    def _block():
        o_ref[...] = jnp.transpose(wt_ref[...]).astype(BF16)

    @pl.when(j == last)
    def _kv_and_gates():
        kv = 2 * A_KV_WIDTH
        o_ref[:, 0:kv] = jnp.transpose(wt_ref[0:kv, :]).astype(BF16)
        g = gate_ref[...]
        pad = jnp.zeros((LANES - M_HEADS, g.shape[1]), F32)
        rows = jnp.concatenate([g[0:M_HEADS], pad, g[M_HEADS:2 * M_HEADS], pad], axis=0)
        o_ref[:, kv:kv + 2 * LANES] = jnp.transpose(rows).astype(BF16)


def _pack_w_in(w_in):
    d = w_in.shape[0]
    src, gate_row = _pack_sources()
    wt = jnp.transpose(w_in)
    return pl.pallas_call(
        _pack_kernel,
        grid_spec=pltpu.PrefetchScalarGridSpec(
            num_scalar_prefetch=1, grid=(N_PACK // PACK_BLOCK,),
            in_specs=[pl.BlockSpec((pl.Element(PACK_BLOCK), pl.Element(d)),
                                   lambda j, src_ref: (pl.multiple_of(src_ref[j], 8), 0)),
                      pl.BlockSpec((pl.Element(2 * M_HEADS), pl.Element(d)), lambda j, src_ref: (gate_row, 0))],
            out_specs=pl.BlockSpec((d, PACK_BLOCK), lambda j, src_ref: (0, j))),
        out_shape=jax.ShapeDtypeStruct((d, N_PACK), BF16),
        compiler_params=pltpu.CompilerParams(dimension_semantics=("arbitrary",)),
        name="pack_w_in",
    )(jnp.asarray(src, jnp.int32), wt, wt)


def kernel(x_prompt, x_sample, state_C, state_n, state_m, cache_k, cache_v, c_prompt, c_sample, w_ada, b_ada, g_norm, w_in, b_igate, b_fgate, g_mnorm, sinks, w_m_out, w_a_out, w_out, g_final):
    assert w_in.shape[0] == 1, "single-layer step"
    bsz_s, steps, d = x_sample.shape
    assert cache_k.shape[2] == WINDOW

    mod_p, mod_s = _adaln(c_prompt, c_sample, w_ada[0], b_ada[0])
    wp = _pack_w_in(w_in[0])
    wmo = w_m_out[0].astype(BF16)
    wao = w_a_out[0].astype(BF16)
    wo = w_out[0].astype(BF16)
    gn = g_norm[0].reshape(1, d)
    gf = g_final.reshape(1, d)
    gmn = g_mnorm[0].reshape(1, M_WIDTH)
    zpad = jnp.zeros((LANES - M_HEADS,), F32)
    bias_g = jnp.concatenate([b_igate[0], zpad, b_fgate[0], zpad]).reshape(1, 2 * LANES)
    sk = sinks[0]

    y_p, c_p, n_p, m_p, k_p, v_p = _prompt_layer(x_prompt, mod_p, sk, gn, gf, gmn, bias_g, wp, wmo, wao, wo)
    m_p = m_p[:, 0, 0:M_HEADS]

    zs = _sample_proj(x_sample, mod_s, gn, wp)
    rope = jnp.stack(_rope_tables(PAST_LEN + jnp.arange(steps, dtype=jnp.int32)))
    dims_keys = lambda c: jnp.transpose(c[0], (0, 2, 3, 1))
    c_s, n_s, m_s, ym, ya, k_t, v_t = _sample_state(zs, state_C[0], state_n[0], state_m[0], dims_keys(cache_k),
                                                    dims_keys(cache_v), rope, bias_g, gmn, sk)
    y_s = _sample_out(x_sample, mod_s, zs, ym, ya, wmo, wao, wo, gf)
    keys_dims = lambda c: jnp.transpose(c, (0, 3, 1, 2))[None]

    return (y_p, y_s, c_p[None], n_p[None], m_p[None], keys_dims(k_p), keys_dims(v_p),
            c_s[None], n_s[None], m_s[None], keys_dims(k_t), keys_dims(v_t))
```

```python
import functools

import jax
import jax.numpy as jnp
from jax import lax
from jax.experimental import pallas as pl
from jax.experimental.pallas import tpu as pltpu

F32 = jnp.float32
BF16 = jnp.bfloat16

D_MODEL = 1024
M_HEADS = 4
M_DK = 128
M_DV = 128
M_WIDTH = M_HEADS * M_DV
M_CHUNK = 128
A_HEADS = 8
A_KV_HEADS = 2
A_GROUP = A_HEADS // A_KV_HEADS
A_HEAD_DIM = 64
A_WIDTH = A_HEADS * A_HEAD_DIM
A_KV_WIDTH = A_KV_HEADS * A_HEAD_DIM
WINDOW = 128
ROT_DIM = A_HEAD_DIM // 4
ROPE_THETA = 500000.0
NORM_EPS = 1e-6
PAST_LEN = 16384
IN_SIZES = (M_HEADS * M_DK, M_HEADS * M_DK, M_WIDTH, M_HEADS, M_HEADS, M_WIDTH, M_WIDTH,
            A_WIDTH, A_KV_WIDTH, A_KV_WIDTH, A_WIDTH, D_MODEL, D_MODEL)

LANES = 128

C_GM = 0
C_GA = C_GM + D_MODEL
C_MQ = C_GA + D_MODEL
C_MK = C_MQ + M_WIDTH
C_MV = C_MK + M_WIDTH
C_MO = C_MV + M_WIDTH
C_MZ = C_MO + M_WIDTH
C_AQ = C_MZ + M_WIDTH
C_AZ = C_AQ + A_WIDTH
C_AK = C_AZ + A_WIDTH
C_AV = C_AK + A_KV_WIDTH
C_GI = C_AV + A_KV_WIDTH
C_GF = C_GI + LANES
N_PACK = C_GF + LANES

PROMPT_TILE = 256
SAMPLE_GROUP = 16
VMEM_LIMIT = 56 * 1024 * 1024


def _sigmoid(x):
    return 1.0 / (1.0 + jnp.exp(-x))


def _silu(x):
    return x * _sigmoid(x)


def _log_sigmoid(x):
    return jnp.minimum(x, 0.0) - jnp.log1p(jnp.exp(-jnp.abs(x)))


def _dot(a, b):
    return jnp.dot(a, b, preferred_element_type=F32)


def _dot_nt(a, b):
    return lax.dot_general(a, b, (((1,), (1,)), ((), ())), preferred_element_type=F32)


def _dot_tn(a, b):
    return lax.dot_general(a, b, (((0,), (0,)), ((), ())), preferred_element_type=F32)


def _dot_exact01(m01, x):
    x1 = x.astype(BF16)
    r1 = x - x1.astype(F32)
    x2 = r1.astype(BF16)
    x3 = (r1 - x2.astype(F32)).astype(BF16)
    return _dot(m01, x1) + _dot(m01, x2) + _dot(m01, x3)


def _rms(x):
    return x * lax.rsqrt(jnp.mean(x * x, axis=-1, keepdims=True) + NORM_EPS)


def _rope(blk, ra, rb, rc):
    return blk * ra + pltpu.roll(blk, 8, 1) * rb + pltpu.roll(blk, LANES - 8, 1) * rc


def _out_stage(x, gate, sgm, sga, ym, ya, wmo_ref, wao_ref, wo_ref, gf):
    pm = _dot(ym, wmo_ref[...])
    pa = _dot(ya, wao_ref[...])
    u = sgm * pm + sga * pa
    r = _dot(u.astype(BF16), wo_ref[...])
    return _rms(x + gate * r) * gf


def _adaln_kernel(cp_ref, cs_ref, w_ref, b_ref, op_ref, os_ref):
    w = w_ref[...].astype(BF16)
    b = b_ref[...]
    op_ref[...] = _dot(_silu(cp_ref[...]).astype(BF16), w) + b
    os_ref[...] = _dot(_silu(cs_ref[...]).astype(BF16), w) + b


def _adaln(c_p, c_s, w_ada, b_ada):
    bp, d = c_p.shape
    bs = c_s.shape[0]
    n = w_ada.shape[1]
    tn = 1024
    return pl.pallas_call(
        _adaln_kernel,
        grid=(n // tn,),
        in_specs=[
            pl.BlockSpec((bp, d), lambda j: (0, 0)),
            pl.BlockSpec((bs, d), lambda j: (0, 0)),
            pl.BlockSpec((d, tn), lambda j: (0, j)),
            pl.BlockSpec((1, tn), lambda j: (0, j)),
        ],
        out_specs=[
            pl.BlockSpec((bp, tn), lambda j: (0, j)),
            pl.BlockSpec((bs, tn), lambda j: (0, j)),
        ],
        out_shape=[jax.ShapeDtypeStruct((bp, n), F32), jax.ShapeDtypeStruct((bs, n), F32)],
        compiler_params=pltpu.CompilerParams(dimension_semantics=("arbitrary",)),
        name="adaln",
    )(c_p, c_s, w_ada, b_ada.reshape(1, n))


_SET_FIELDS = ("hb", "q", "k", "vaug", "og", "qa", "kf", "vf", "sz", "sgm", "sga", "g")


def _set_shapes(tile):
    return [
        pltpu.VMEM((tile, D_MODEL), BF16),
        pltpu.VMEM((tile, M_WIDTH), BF16),
        pltpu.VMEM((tile, M_WIDTH), F32),
        pltpu.VMEM((M_HEADS, tile, 2 * LANES), BF16),
        pltpu.VMEM((tile, M_WIDTH), F32),
        pltpu.VMEM((tile, A_WIDTH), BF16),
        pltpu.VMEM((tile, A_KV_WIDTH), F32),
        pltpu.VMEM((tile, A_KV_WIDTH), F32),
        pltpu.VMEM((tile, A_WIDTH), F32),
        pltpu.VMEM((tile, D_MODEL), F32),
        pltpu.VMEM((tile, D_MODEL), F32),
        pltpu.VMEM((tile, 2 * LANES), F32),
    ]


def _project_pieces(x_ref, mod_ref, rope_refs, st, gn_ref, bg_ref, w_ref):
    def norm():
        mod = mod_ref[0]
        h = _rms(x_ref[0]) * gn_ref[...]
        st["hb"][...] = (h * (1.0 + mod[:, D_MODEL:2 * D_MODEL]) + mod[:, 0:D_MODEL]).astype(BF16)

    def proj(lo, n):
        return _dot(st["hb"][...], w_ref[:, lo:lo + n])

    def rope(v):
        ra_ref, rb_ref, rc_ref = rope_refs
        return _rope(v, ra_ref[...], rb_ref[...], rc_ref[...])

    def gates_q():
        st["g"][...] = proj(C_GI, 2 * LANES) + bg_ref[...]
        st["q"][...] = (proj(C_MQ, M_WIDTH) * (M_DK ** -0.5)).astype(BF16)

    def keys():
        st["k"][...] = proj(C_MK, M_WIDTH)

    def values():
        zv = proj(C_MV, M_WIDTH)
        for hd in range(M_HEADS):
            st["vaug"][hd, :, 0:LANES] = zv[:, hd * M_DV:(hd + 1) * M_DV].astype(BF16)

    def attn_kv():
        st["kf"][...] = rope(proj(C_AK, A_KV_WIDTH))
        st["vf"][...] = proj(C_AV, A_KV_WIDTH)

    def attn_q():
        za = proj(C_AQ, A_WIDTH)
        for j in range(A_WIDTH // LANES):
            blk = rope(za[:, j * LANES:(j + 1) * LANES])
            st["qa"][:, j * LANES:(j + 1) * LANES] = (blk * (A_HEAD_DIM ** -0.5)).astype(BF16)

    def out_gates():
        st["og"][...] = _sigmoid(proj(C_MO, M_WIDTH)) * _silu(proj(C_MZ, M_WIDTH))

    def attn_gate():
        st["sz"][...] = _silu(proj(C_AZ, A_WIDTH))

    def merge_gate(name, lo, part):
        def piece():
            cs = slice(part * M_WIDTH, (part + 1) * M_WIDTH)
            st[name][:, cs] = _sigmoid(proj(lo + part * M_WIDTH, M_WIDTH))
        return piece

    return [norm, gates_q, keys, values, attn_kv, attn_q, out_gates, attn_gate,
            merge_gate("sgm", C_GM, 0), merge_gate("sgm", C_GM, 1),
            merge_gate("sga", C_GA, 0), merge_gate("sga", C_GA, 1)]


def _consume_pieces(st, first_tile, sinks_ref, gmn_ref, kbuf_s, vbuf_s, c_s, m_s, ym_s, ya_s, tile):
    L = M_CHUNK
    heads = [slice(hd * M_DK, (hd + 1) * M_DK) for hd in range(M_HEADS)]
    a_heads = [(kv, bi, half) for kv in range(A_KV_HEADS) for bi in range(2) for half in range(2)]

    def causal_mask():
        row = lax.broadcasted_iota(jnp.int32, (L, L), 0)
        col = lax.broadcasted_iota(jnp.int32, (L, L), 1)
        return row >= col

    def chunk_stages(j):
        rs = slice(j * L, (j + 1) * L)
        cx = {}

        def gates():
            g = st["g"][rs, :]
            ig_all = g[:, 0:LANES]
            lf_all = _log_sigmoid(g[:, LANES:2 * LANES])
            tril = jnp.where(causal_mask(), 1.0, 0.0).astype(BF16)
            b_all = _dot_exact01(tril, lf_all)
            b_last = b_all[L - 1:L, :]
            m_old = m_s[...]
            d_end = b_last - b_all + ig_all
            m_new = jnp.maximum(b_last + m_old, jnp.max(d_end, axis=0, keepdims=True))
            m_s[...] = m_new
            cx.update(b_all=b_all, w_end=jnp.exp(d_end - m_new), decay=jnp.exp(b_last + m_old - m_new),
                      inter=b_all + m_old, r_t=jnp.transpose(ig_all - b_all))

        def kv_buffers():
            srows = slice((j % 2) * L, (j % 2 + 1) * L)
            lo_half = lax.broadcasted_iota(jnp.int32, (L, L), 1) < A_HEAD_DIM
            kc = st["kf"][rs, :]
            vc = st["vf"][rs, :]
            kr = pltpu.roll(kc, A_HEAD_DIM, 1)
            vr = pltpu.roll(vc, A_HEAD_DIM, 1)
            kbuf_s[0, srows, :] = jnp.where(lo_half, kc, 0.0).astype(BF16)
            kbuf_s[1, srows, :] = jnp.where(lo_half, 0.0, kr).astype(BF16)
            kbuf_s[2, srows, :] = jnp.where(lo_half, kr, 0.0).astype(BF16)
            kbuf_s[3, srows, :] = jnp.where(lo_half, 0.0, kc).astype(BF16)
            vbuf_s[0, srows, :] = jnp.where(lo_half, vc, 0.0).astype(BF16)
            vbuf_s[1, srows, :] = jnp.where(lo_half, 0.0, vr).astype(BF16)
            vbuf_s[2, srows, :] = jnp.where(lo_half, vr, 0.0).astype(BF16)
            vbuf_s[3, srows, :] = jnp.where(lo_half, 0.0, vc).astype(BF16)

        def decay_matrices():
            cx["dmat"] = [jnp.where(causal_mask(), cx["b_all"][:, hd:hd + 1] + cx["r_t"][hd:hd + 1, :], -jnp.inf)
                          for hd in range(M_HEADS)]
            cx["m_hat"] = [jnp.maximum(cx["inter"][:, hd:hd + 1], jnp.max(cx["dmat"][hd], axis=-1, keepdims=True))
                           for hd in range(M_HEADS)]
            cx["kw"] = [(st["k"][rs, heads[hd]] * cx["w_end"][:, hd:hd + 1]).astype(BF16) for hd in range(M_HEADS)]

        def state_matmuls():
            cx["qk"], cx["qc"] = [], []
            for hd in range(M_HEADS):
                qh = st["q"][rs, heads[hd]]
                c_old = c_s[hd]
                cx["qk"].append(_dot_nt(qh, st["k"][rs, heads[hd]].astype(BF16)))
                cx["qc"].append(_dot(qh, c_old.astype(BF16)))
                va = st["vaug"][hd, rs, :]
                c_s[hd] = cx["decay"][:, hd:hd + 1] * c_old + _dot_tn(cx["kw"][hd], va)

        def attn_scores():
            row2 = lax.broadcasted_iota(jnp.int32, (L, 2 * L), 0)
            col2 = lax.broadcasted_iota(jnp.int32, (L, 2 * L), 1)
            no_prev = jnp.where(first_tile, 4 * L, 0) if j == 0 else 0
            if j % 2 == 0:
                mask = ((col2 < L) & (col2 <= row2)) | ((col2 >= L) & (col2 - L > row2 + no_prev))
            else:
                mask = ((col2 >= L) & (col2 - L <= row2)) | ((col2 < L) & (col2 > row2 + no_prev))
            cx["sh"] = {}
            for kv in range(A_KV_HEADS):
                qcat = jnp.concatenate([st["qa"][rs, (2 * kv) * LANES:(2 * kv + 1) * LANES],
                                        st["qa"][rs, (2 * kv + 1) * LANES:(2 * kv + 2) * LANES]], axis=0)
                for half in range(2):
                    sc = _dot_nt(qcat, kbuf_s[2 * kv + half])
                    for bi in range(2):
                        cx["sh"][kv, bi, half] = jnp.where(mask, sc[bi * L:(bi + 1) * L, :], -jnp.inf)

        def attn_max():
            cx["mx"] = {hk: jnp.maximum(jnp.max(cx["sh"][hk], axis=-1, keepdims=True),
                                        sinks_ref[A_GROUP * hk[0] + 2 * hk[1] + hk[2]]) for hk in a_heads}

        def stabilised_scores():
            cx["s"] = [(cx["qk"][hd] * jnp.exp(cx["dmat"][hd] - cx["m_hat"][hd])).astype(BF16)
                       for hd in range(M_HEADS)]

        def score_value_matmuls():
            cx["sv"] = [_dot(cx["s"][hd], st["vaug"][hd, rs, :]) for hd in range(M_HEADS)]

        def attn_exp():
            cx["p"] = {hk: jnp.exp(cx["sh"][hk] - cx["mx"][hk]) for hk in a_heads}
            cx["den"] = {hk: jnp.sum(cx["p"][hk], axis=-1, keepdims=True)
                         + jnp.exp(sinks_ref[A_GROUP * hk[0] + 2 * hk[1] + hk[2]] - cx["mx"][hk]) for hk in a_heads}

        def attn_normalise():
            cx["pn"] = {hk: (cx["p"][hk] / cx["den"][hk]).astype(BF16) for hk in a_heads}

        def attn_value_matmuls():
            for kv in range(A_KV_HEADS):
                for bi in range(2):
                    o = (_dot(cx["pn"][kv, bi, 0], vbuf_s[2 * kv]) + _dot(cx["pn"][kv, bi, 1], vbuf_s[2 * kv + 1]))
                    cs = slice((2 * kv + bi) * LANES, (2 * kv + bi + 1) * LANES)
                    ya_s[rs, cs] = (o * st["sz"][rs, cs]).astype(BF16)

        def mlstm_outputs():
            hh = []
            for hd in range(M_HEADS):
                m_hat = cx["m_hat"][hd]
                w_inter = jnp.exp(cx["inter"][:, hd:hd + 1] - m_hat)
                tot = cx["sv"][hd] + w_inter * cx["qc"][hd]
                hh.append(tot[:, 0:M_DV] / jnp.maximum(jnp.abs(tot[:, M_DV:M_DV + 1]), jnp.exp(-m_hat)))
            ms = [jnp.mean(h * h, axis=-1, keepdims=True) for h in hh]
            for hd in range(M_HEADS):
                hs = heads[hd]
                ym_s[rs, hs] = (hh[hd] * lax.rsqrt(ms[hd] + NORM_EPS) * gmn_ref[:, hs] * st["og"][rs, hs]).astype(BF16)

        mlstm = [gates, decay_matrices, state_matmuls, stabilised_scores, score_value_matmuls, mlstm_outputs]
        attn = [kv_buffers, attn_scores, attn_max, attn_exp, attn_normalise, attn_value_matmuls]
        return mlstm, attn

    assert tile == 2 * L
    (m0, a0), (m1, a1) = chunk_stages(0), chunk_stages(1)
    merged = []
    for m_stage, a_stage in zip(m0 + m1, a0 + a1):
        merged += [m_stage, a_stage]
    return merged


def _output_pieces(st, x_ref, rows, gate_ref, ym_s, ya_s, u_s, wmo_ref, wao_ref, wo_ref, gf_ref, y_ref):
    def merge():
        u = st["sgm"][...] * _dot(ym_s[...], wmo_ref[...]) + st["sga"][...] * _dot(ya_s[...], wao_ref[...])
        u_s[...] = u.astype(BF16)

    def out():
        gate = gate_ref[0][:, 2 * D_MODEL:3 * D_MODEL]
        r = _dot(u_s[...], wo_ref[...])
        y_ref[0, rows, :] = _rms(x_ref[0, rows, :] + gate * r) * gf_ref[...]

    return [merge, out]


def _interleave(primary, filler):
    done = 0
    for i, piece in enumerate(primary):
        piece()
        upto = ((i + 1) * len(filler)) // len(primary)
        for f in filler[done:upto]:
            f()
        done = upto


def _prompt_kernel(sinks_ref, x0_ref, xa_ref, xb_ref, xres_ref, mod0_ref, moda_ref, modb_ref,
                   gn_ref, gf_ref, gmn_ref, bg_ref, w_hbm, wmo_hbm, wao_hbm, wo_hbm,
                   ra0_ref, rb0_ref, rc0_ref, raa_ref, rba_ref, rca_ref, rab_ref, rbb_ref, rcb_ref,
                   y_ref, c_out_ref, n_out_ref, m_out_ref, kk_ref, vk_ref, *scratch, tile, steps_per_seq):
    nset = len(_SET_FIELDS)
    set_a = dict(zip(_SET_FIELDS, scratch[0:nset]))
    set_b = dict(zip(_SET_FIELDS, scratch[nset:2 * nset]))
    (kbuf_s, vbuf_s, c_s, m_s, yma_s, yaa_s, ua_s, ymb_s, yab_s, ub_s,
     w_ref, wmo_ref, wao_ref, wo_ref, w_sems) = scratch[2 * nset:]
    k = pl.program_id(0)
    first = (k % steps_per_seq) == 0
    project = lambda x_ref, mod_ref, rope, st: _project_pieces(x_ref, mod_ref, rope, st, gn_ref, bg_ref, w_ref)

    @pl.when(k == 0)
    def _prologue():
        copies = [pltpu.make_async_copy(src, dst, w_sems.at[i]) for i, (src, dst) in enumerate(
            ((w_hbm, w_ref), (wmo_hbm, wmo_ref), (wao_hbm, wao_ref), (wo_hbm, wo_ref)))]
        for copy in copies:
            copy.start()
        for copy in copies:
            copy.wait()
        lane = lax.broadcasted_iota(jnp.int32, (tile, LANES), 1)
        ones_col = jnp.where(lane == 0, 1.0, 0.0).astype(BF16)
        for st in (set_a, set_b):
            for hd in range(M_HEADS):
                st["vaug"][hd, :, LANES:2 * LANES] = ones_col
        for piece in project(x0_ref, mod0_ref, (ra0_ref, rb0_ref, rc0_ref), set_a):
            piece()

    @pl.when(first)
    def _init():
        c_s[...] = jnp.zeros_like(c_s)
        m_s[...] = jnp.zeros_like(m_s)
        kbuf_s[...] = jnp.zeros_like(kbuf_s)
        vbuf_s[...] = jnp.zeros_like(vbuf_s)

    state = (sinks_ref, gmn_ref, kbuf_s, vbuf_s, c_s, m_s)
    out_w = (wmo_ref, wao_ref, wo_ref, gf_ref, y_ref)
    _interleave(_consume_pieces(set_a, first, *state, yma_s, yaa_s, tile),
                project(xa_ref, moda_ref, (raa_ref, rba_ref, rca_ref), set_b))
    _interleave(_consume_pieces(set_b, False, *state, ymb_s, yab_s, tile),
                _output_pieces(set_a, xres_ref, slice(0, tile), moda_ref, yma_s, yaa_s, ua_s, *out_w)
                + project(xb_ref, modb_ref, (rab_ref, rbb_ref, rcb_ref), set_a))
    for piece in _output_pieces(set_b, xres_ref, slice(tile, 2 * tile), moda_ref, ymb_s, yab_s, ub_s, *out_w):
        piece()

    @pl.when((k % steps_per_seq) == steps_per_seq - 1)
    def _final():
        for hd in range(M_HEADS):
            cf = c_s[hd]
            c_out_ref[0, hd] = cf[:, 0:M_DV]
            n_out_ref[0, hd:hd + 1, :] = jnp.transpose(cf[:, M_DV:2 * M_DV])[0:1, :]
        m_out_ref[0] = m_s[...]
        k_t = jnp.transpose(set_b["kf"][tile - WINDOW:tile, :])
        v_t = jnp.transpose(set_b["vf"][tile - WINDOW:tile, :])
        for kv in range(A_KV_HEADS):
            kk_ref[0, kv] = k_t[kv * A_HEAD_DIM:(kv + 1) * A_HEAD_DIM, :]
            vk_ref[0, kv] = v_t[kv * A_HEAD_DIM:(kv + 1) * A_HEAD_DIM, :]


def _rope_tables(pos):
    half = ROT_DIM // 2
    dim = jnp.arange(LANES, dtype=jnp.int32) % A_HEAD_DIM
    inv = ROPE_THETA ** (-(2 * (dim % half)).astype(F32) / ROT_DIM)
    ang = pos.astype(F32)[:, None] * inv[None, :]
    cos, sin = jnp.cos(ang), jnp.sin(ang)
    ra = jnp.where(dim < ROT_DIM, cos, 1.0)
    rb = jnp.where((dim >= half) & (dim < ROT_DIM), sin, 0.0)
    rc = jnp.where(dim < half, -sin, 0.0)
    return ra, rb, rc


def _full(shape):
    return pl.BlockSpec(shape, lambda *_: (0,) * len(shape))


def _prompt_layer(x, mod, sinks, gn, gf, gmn, bias_g, wp, wmo, wao, wo):
    bsz, seq, d = x.shape
    tile = PROMPT_TILE
    assert seq % (2 * tile) == 0 and tile % (2 * M_CHUNK) == 0 and d == D_MODEL
    tiles_per_seq = seq // tile
    steps_per_seq = tiles_per_seq // 2
    n_tiles = bsz * tiles_per_seq
    n_steps = n_tiles // 2
    ra, rb, rc = _rope_tables(jnp.arange(seq, dtype=jnp.int32))
    xt = x.reshape(n_tiles, tile, d)
    xp = x.reshape(n_steps, 2 * tile, d)
    mod3 = mod.reshape(bsz, 1, 3 * d)

    tile_0 = lambda k: 0
    tile_a = lambda k: 2 * k + 1
    tile_b = lambda k: jnp.minimum(2 * k + 2, n_tiles - 1)
    x_spec = lambda f: pl.BlockSpec((1, tile, d), lambda k: (f(k), 0, 0))
    mod_spec = lambda f: pl.BlockSpec((1, 1, 3 * d), lambda k: (f(k) // tiles_per_seq, 0, 0))
    rope_spec = lambda f: pl.BlockSpec((tile, LANES), lambda k: (f(k) % tiles_per_seq, 0))
    in_specs = [
        pl.BlockSpec(memory_space=pltpu.SMEM),
        x_spec(tile_0), x_spec(tile_a), x_spec(tile_b),
        pl.BlockSpec((1, 2 * tile, d), lambda k: (k, 0, 0)),
        mod_spec(tile_0), mod_spec(tile_a), mod_spec(tile_b),
        _full((1, d)), _full((1, d)), _full((1, M_WIDTH)), _full((1, 2 * LANES)),
        pl.BlockSpec(memory_space=pl.ANY), pl.BlockSpec(memory_space=pl.ANY),
        pl.BlockSpec(memory_space=pl.ANY), pl.BlockSpec(memory_space=pl.ANY),
        rope_spec(tile_0), rope_spec(tile_0), rope_spec(tile_0),
        rope_spec(tile_a), rope_spec(tile_a), rope_spec(tile_a),
        rope_spec(tile_b), rope_spec(tile_b), rope_spec(tile_b),
    ]
    seq_of = lambda k: k // steps_per_seq
    out_specs = [
        pl.BlockSpec((1, 2 * tile, d), lambda k: (k, 0, 0)),
        pl.BlockSpec((1, M_HEADS, M_DK, M_DV), lambda k: (seq_of(k), 0, 0, 0)),
        pl.BlockSpec((1, M_HEADS, M_DK), lambda k: (seq_of(k), 0, 0)),
        pl.BlockSpec((1, 1, LANES), lambda k: (seq_of(k), 0, 0)),
        pl.BlockSpec((1, A_KV_HEADS, A_HEAD_DIM, WINDOW), lambda k: (seq_of(k), 0, 0, 0)),
        pl.BlockSpec((1, A_KV_HEADS, A_HEAD_DIM, WINDOW), lambda k: (seq_of(k), 0, 0, 0)),
    ]
    out_shape = [
        jax.ShapeDtypeStruct((n_steps, 2 * tile, d), F32),
        jax.ShapeDtypeStruct((bsz, M_HEADS, M_DK, M_DV), F32),
        jax.ShapeDtypeStruct((bsz, M_HEADS, M_DK), F32),
        jax.ShapeDtypeStruct((bsz, 1, LANES), F32),
        jax.ShapeDtypeStruct((bsz, A_KV_HEADS, A_HEAD_DIM, WINDOW), F32),
        jax.ShapeDtypeStruct((bsz, A_KV_HEADS, A_HEAD_DIM, WINDOW), F32),
    ]
    scratch = _set_shapes(tile) + _set_shapes(tile) + [
        pltpu.VMEM((2 * A_KV_HEADS, 2 * WINDOW, LANES), BF16),
        pltpu.VMEM((2 * A_KV_HEADS, 2 * WINDOW, LANES), BF16),
        pltpu.VMEM((M_HEADS, M_DK, 2 * LANES), F32),
        pltpu.VMEM((1, LANES), F32),
        pltpu.VMEM((tile, M_WIDTH), BF16),
        pltpu.VMEM((tile, A_WIDTH), BF16),
        pltpu.VMEM((tile, D_MODEL), BF16),
        pltpu.VMEM((tile, M_WIDTH), BF16),
        pltpu.VMEM((tile, A_WIDTH), BF16),
        pltpu.VMEM((tile, D_MODEL), BF16),
        pltpu.VMEM((d, N_PACK), BF16),
        pltpu.VMEM((M_WIDTH, d), BF16),
        pltpu.VMEM((A_WIDTH, d), BF16),
        pltpu.VMEM((d, d), BF16),
        pltpu.SemaphoreType.DMA((4,)),
    ]
    outs = pl.pallas_call(
        functools.partial(_prompt_kernel, tile=tile, steps_per_seq=steps_per_seq),
        grid=(n_steps,),
        in_specs=in_specs,
        out_specs=out_specs,
        out_shape=out_shape,
        scratch_shapes=scratch,
        compiler_params=pltpu.CompilerParams(dimension_semantics=("arbitrary",), vmem_limit_bytes=VMEM_LIMIT),
        name="prompt_layer",
    )(sinks, xt, xt, xt, xp, mod3, mod3, mod3, gn, gf, gmn, bias_g, wp, wmo, wao, wo,
      ra, rb, rc, ra, rb, rc, ra, rb, rc)
    return (outs[0].reshape(bsz, seq, d),) + tuple(outs[1:])


def _sample_proj_kernel(x_ref, mod_ref, gn_ref, w_ref, z_ref, hb_s):
    bsz, steps, _ = x_ref.shape

    @pl.when(pl.program_id(0) == 0)
    def _norm():
        mod = mod_ref[...]
        for t in range(steps):
            h = _rms(x_ref[:, t, :]) * gn_ref[...]
            hb_s[t * bsz:(t + 1) * bsz, :] = (h * (1.0 + mod[:, D_MODEL:2 * D_MODEL]) + mod[:, 0:D_MODEL]).astype(BF16)

    z = _dot(hb_s[...], w_ref[...])
    for t in range(steps):
        z_ref[t] = z[t * bsz:(t + 1) * bsz, :]


def _sample_proj(x3, mod_s, gn, wp):
    bsz, steps, d = x3.shape
    tn = 1024
    return pl.pallas_call(
        _sample_proj_kernel,
        grid=(N_PACK // tn,),
        in_specs=[
            _full((bsz, steps, d)),
            pl.BlockSpec((bsz, 2 * d), lambda j: (0, 0)),
            _full((1, d)),
            pl.BlockSpec((d, tn), lambda j: (0, j)),
        ],
        out_specs=pl.BlockSpec((steps, bsz, tn), lambda j: (0, 0, j)),
        out_shape=jax.ShapeDtypeStruct((steps, bsz, N_PACK), F32),
        scratch_shapes=[pltpu.VMEM((steps * bsz, d), BF16)],
        compiler_params=pltpu.CompilerParams(dimension_semantics=("arbitrary",)),
        name="sample_proj",
    )(x3, mod_s, gn, wp)


def _sample_state_kernel(sinks_ref, z_ref, c_ref, n_ref, m_ref, ck_ref, cv_ref, rope_ref, bg_ref, gmn_ref,
                         cn_ref, nn_ref, mn_ref, ym_ref, ya_ref, ko_ref, vo_ref, *, group, steps):
    T, G = steps, group
    toks = range(T)

    def z(t, lo, n):
        return z_ref[t, :, lo:lo + n]

    def rope_t(t, blk):
        return _rope(blk, rope_ref[0, t:t + 1, :], rope_ref[1, t:t + 1, :], rope_ref[2, t:t + 1, :])

    lane = lax.broadcasted_iota(jnp.int32, (G, LANES), 1)
    lo_half = lane < A_HEAD_DIM
    gmn = gmn_ref[...]

    q = [z(t, C_MQ, M_WIDTH) * (M_DK ** -0.5) for t in toks]
    k = [z(t, C_MK, M_WIDTH) for t in toks]
    v = [z(t, C_MV, M_WIDTH) for t in toks]
    og = [_sigmoid(z(t, C_MO, M_WIDTH)) * _silu(z(t, C_MZ, M_WIDTH)) for t in toks]
    qa = [[rope_t(t, z(t, C_AQ + j * LANES, LANES)) * (A_HEAD_DIM ** -0.5) for j in range(A_WIDTH // LANES)]
          for t in toks]
    kn = [rope_t(t, z(t, C_AK, A_KV_WIDTH)) for t in toks]
    vn = [z(t, C_AV, A_KV_WIDTH) for t in toks]
    sz = [_silu(z(t, C_AZ, A_WIDTH)) for t in toks]
    gt = [z(t, C_GI, 2 * LANES) + bg_ref[...] for t in toks]
    ig = [gt[t][:, 0:LANES] for t in toks]
    lf = [_log_sigmoid(gt[t][:, LANES:2 * LANES]) for t in toks]


    m_all = m_ref[...]
    rep = lambda a, h: jnp.broadcast_to(a[:, h:h + 1], (G, LANES))
    gates = []
    for hd in range(M_HEADS):
        ig_h = [rep(ig[t], hd) for t in toks]
        lf_h = [rep(lf[t], hd) for t in toks]
        m_old = rep(m_all, hd)
        bsum = [lf_h[0]]
        for t in range(1, T):
            bsum.append(bsum[-1] + lf_h[t])
        b_last = bsum[-1]
        d_end = [b_last - bsum[s] + ig_h[s] for s in toks]
        m_new = b_last + m_old
        for s in toks:
            m_new = jnp.maximum(m_new, d_end[s])
        mn_ref[:, hd:hd + 1] = m_new[:, 0:1]
        gd = dict(w_end=[jnp.exp(d_end[s] - m_new) for s in toks], decay=jnp.exp(b_last + m_old - m_new),
                  e=[], w_inter=[], emh=[])
        for t in toks:
            inter = bsum[t] + m_old
            dm = [bsum[t] - bsum[s] + ig_h[s] for s in range(t + 1)]
            m_hat = inter
            for s in range(t + 1):
                m_hat = jnp.maximum(m_hat, dm[s])
            gd["e"].append([jnp.exp(dm[s] - m_hat) for s in range(t + 1)])
            gd["w_inter"].append(jnp.exp(inter - m_hat))
            gd["emh"].append(jnp.exp(-m_hat))
        gates.append(gd)

    heads = [slice(hd * M_DK, (hd + 1) * M_DK) for hd in range(M_HEADS)]
    n_old = [n_ref[:, hd, :] for hd in range(M_HEADS)]
    qk = {(hd, t, s): jnp.sum(q[t][:, heads[hd]] * k[s][:, heads[hd]], axis=-1, keepdims=True)
          for hd in range(M_HEADS) for t in toks for s in range(t + 1)}
    qn = {(hd, t): jnp.sum(q[t][:, heads[hd]] * n_old[hd], axis=-1, keepdims=True)
          for hd in range(M_HEADS) for t in toks}

    lo_f = jnp.where(lo_half, 1.0, 0.0)
    hi_f = 1.0 - lo_f
    dup = lambda a, kv: (jnp.where(lo_half, a, pltpu.roll(a, A_HEAD_DIM, 1)) if kv == 0 else
                         jnp.where(lo_half, pltpu.roll(a, A_HEAD_DIM, 1), a))
    kn_x = [[dup(kn[s], kv) for s in toks] for kv in range(A_KV_HEADS)]
    vn_x = [[dup(vn[s], kv) for s in toks] for kv in range(A_KV_HEADS)]
    s_new = {}
    for kv in range(A_KV_HEADS):
        for blk in range(2):
            for t in toks:
                for s in range(t + 1):
                    prod = qa[t][2 * kv + blk] * kn_x[kv][s]
                    s_new[kv, 0, blk, t, s] = jnp.sum(prod * lo_f, axis=-1, keepdims=True)
                    s_new[kv, 1, blk, t, s] = jnp.sum(prod * hi_f, axis=-1, keepdims=True)

    rows_tg = lax.broadcasted_iota(jnp.int32, (T * G, LANES), 0) % G
    own_rows = [rows_tg == b for b in range(G)]
    rows64 = lax.broadcasted_iota(jnp.int32, (2 * T * G, LANES), 0) % G

    qc = []
    for hd in range(M_HEADS):
        hs = heads[hd]
        gd = gates[hd]
        q32 = jnp.concatenate([q[t][:, hs] for t in toks], axis=0)
        kw = [k[s][:, hs] * gd["w_end"][s] for s in toks]
        kw_t = jnp.transpose(jnp.concatenate(kw, axis=0)).astype(BF16)
        v32 = jnp.concatenate([v[s][:, hs] for s in toks], axis=0)
        acc = None
        for b in range(G):
            c_old = c_ref[b, hd]
            part = _dot(jnp.where(own_rows[b], q32, 0.0).astype(BF16), c_old.astype(BF16))
            acc = part if acc is None else acc + part
            upd = _dot(kw_t, jnp.where(own_rows[b], v32, 0.0).astype(BF16))
            cn_ref[b, hd] = gd["decay"][b:b + 1, 0:1] * c_old + upd
        qc.append(acc)
        nn_ref[:, hd, :] = gd["decay"] * n_old[hd] + (kw[0] + kw[1] + kw[2] + kw[3])

    zeros_kt = jnp.zeros((A_HEAD_DIM, WINDOW), BF16)
    sc = {}
    for kv in range(A_KV_HEADS):
        l64 = jnp.concatenate([qa[t][2 * kv + blk] for blk in range(2) for t in toks], axis=0)
        own64 = [jnp.where(rows64 == b, l64, 0.0).astype(BF16) for b in range(G)]
        kt = [ck_ref[b, kv].astype(BF16) for b in range(G)]
        for par in range(2):
            acc = None
            for b in range(G):
                rhs = jnp.concatenate([kt[b], zeros_kt] if par == 0 else [zeros_kt, kt[b]], axis=0)
                part = _dot(own64[b], rhs)
                acc = part if acc is None else acc + part
            sc[kv, par] = acc

    hh = {}
    for hd in range(M_HEADS):
        gd = gates[hd]
        for t in toks:
            sv, ssum = None, None
            for s in range(t + 1):
                s_ts = qk[hd, t, s] * gd["e"][t][s]
                sv = s_ts * v[s][:, heads[hd]] if sv is None else sv + s_ts * v[s][:, heads[hd]]
                ssum = s_ts if ssum is None else ssum + s_ts
            wi = gd["w_inter"][t]
            num = sv + wi * qc[hd][t * G:(t + 1) * G, :]
            den = ssum + wi * qn[hd, t]
            hh[hd, t] = num / jnp.maximum(jnp.abs(den), gd["emh"][t])
    ms = {key: jnp.mean(val * val, axis=-1, keepdims=True) for key, val in hh.items()}
    for (hd, t), val in hh.items():
        hs = heads[hd]
        ym_ref[t, :, hs] = val * lax.rsqrt(ms[hd, t] + NORM_EPS) * gmn[:, hs] * og[t][:, hs]

    items = [(kv, par, blk, t) for kv in range(A_KV_HEADS) for par in range(2) for blk in range(2) for t in toks]
    s_c = {it: jnp.where(lane > it[3], sc[it[0], it[1]][(it[2] * T + it[3]) * G:(it[2] * T + it[3] + 1) * G, :], -jnp.inf)
           for it in items}
    mx_c = {it: jnp.max(s_c[it], axis=-1, keepdims=True) for it in items}
    p_c, p_n, sink_t = {}, {}, {}
    for it in items:
        kv, par, blk, t = it
        sink = sinks_ref[A_GROUP * kv + 2 * blk + par]
        mx = jnp.maximum(mx_c[it], sink)
        for s in range(t + 1):
            mx = jnp.maximum(mx, s_new[kv, par, blk, t, s])
        p_c[it] = jnp.exp(s_c[it] - mx)
        p_n[it] = [jnp.exp(s_new[kv, par, blk, t, s] - mx) for s in range(t + 1)]
        sink_t[it] = jnp.exp(sink - mx)
    sum_c = {it: jnp.sum(p_c[it], axis=-1, keepdims=True) for it in items}
    fresh = {}
    probs = {(kv, par): [] for kv in range(A_KV_HEADS) for par in range(2)}
    for it in items:
        kv, par, blk, t = it
        den = sum_c[it] + sink_t[it]
        for s in range(t + 1):
            den = den + p_n[it][s]
        r = 1.0 / den
        probs[kv, par].append(p_c[it] * r)
        acc = None
        for s in range(t + 1):
            term = (p_n[it][s] * r) * vn_x[kv][s]
            acc = term if acc is None else acc + term
        fresh[it] = acc

    for kv in range(A_KV_HEADS):
        vt = [cv_ref[b, kv].astype(BF16) for b in range(G)]
        o = []
        for par in range(2):
            pstack = jnp.concatenate(probs[kv, par], axis=0)
            acc = None
            for b in range(G):
                rhs = jnp.concatenate([vt[b], zeros_kt] if par == 0 else [zeros_kt, vt[b]], axis=0)
                part = _dot_nt(jnp.where(rows64 == b, pstack, 0.0).astype(BF16), rhs)
                acc = part if acc is None else acc + part
            o.append(acc)
        for blk in range(2):
            cs = slice((2 * kv + blk) * LANES, (2 * kv + blk + 1) * LANES)
            for t in toks:
                rs = slice((blk * T + t) * G, (blk * T + t + 1) * G)
                new_v = jnp.where(lo_half, fresh[kv, 0, blk, t], fresh[kv, 1, blk, t])
                ya_ref[t, :, cs] = (o[0][rs, :] + o[1][rs, :] + new_v) * sz[t][:, cs]

    n_rows = T * G
    src = lax.broadcasted_iota(jnp.int32, (n_rows, n_rows), 1)
    dst = lax.broadcasted_iota(jnp.int32, (n_rows, n_rows), 0)
    perm = jnp.where(src == (dst % T) * G + dst // T, 1.0, 0.0).astype(BF16)
    lane_d = lax.broadcasted_iota(jnp.int32, (A_HEAD_DIM, WINDOW), 1)
    for new, cache_ref, out_ref in ((kn, ck_ref, ko_ref), (vn, cv_ref, vo_ref)):
        stack = _dot_exact01(perm, jnp.concatenate(new, axis=0))
        stack = jnp.concatenate([stack, jnp.zeros((LANES - n_rows, LANES), F32)], axis=0)
        new_t = jnp.transpose(stack)
        for b in range(G):
            placed = pltpu.roll(new_t, (WINDOW - T - T * b) % LANES, 1)
            for kv in range(A_KV_HEADS):
                slid = pltpu.roll(cache_ref[b, kv], WINDOW - T, 1)
                out_ref[b, kv] = jnp.where(lane_d >= WINDOW - T, placed[kv * A_HEAD_DIM:(kv + 1) * A_HEAD_DIM, :], slid)


def _sample_state(zs, state_c, state_n, state_m, ck_t, cv_t, rope, bias_g, gmn, sinks):
    steps, bsz, _ = zs.shape
    group = SAMPLE_GROUP
    assert bsz % group == 0
    in_specs = [
        pl.BlockSpec(memory_space=pltpu.SMEM),
        pl.BlockSpec((steps, group, N_PACK), lambda i: (0, i, 0)),
        pl.BlockSpec((group, M_HEADS, M_DK, M_DV), lambda i: (i, 0, 0, 0)),
        pl.BlockSpec((group, M_HEADS, M_DK), lambda i: (i, 0, 0)),
        pl.BlockSpec((group, M_HEADS), lambda i: (i, 0)),
        pl.BlockSpec((group, A_KV_HEADS, A_HEAD_DIM, WINDOW), lambda i: (i, 0, 0, 0)),
        pl.BlockSpec((group, A_KV_HEADS, A_HEAD_DIM, WINDOW), lambda i: (i, 0, 0, 0)),
        _full((3, steps, LANES)), _full((1, 2 * LANES)), _full((1, M_WIDTH)),
    ]
    out_specs = [
        pl.BlockSpec((group, M_HEADS, M_DK, M_DV), lambda i: (i, 0, 0, 0)),
        pl.BlockSpec((group, M_HEADS, M_DK), lambda i: (i, 0, 0)),
        pl.BlockSpec((group, M_HEADS), lambda i: (i, 0)),
        pl.BlockSpec((steps, group, M_WIDTH), lambda i: (0, i, 0)),
        pl.BlockSpec((steps, group, A_WIDTH), lambda i: (0, i, 0)),
        pl.BlockSpec((group, A_KV_HEADS, A_HEAD_DIM, WINDOW), lambda i: (i, 0, 0, 0)),
        pl.BlockSpec((group, A_KV_HEADS, A_HEAD_DIM, WINDOW), lambda i: (i, 0, 0, 0)),
    ]
    out_shape = [
        jax.ShapeDtypeStruct((bsz, M_HEADS, M_DK, M_DV), F32),
        jax.ShapeDtypeStruct((bsz, M_HEADS, M_DK), F32),
        jax.ShapeDtypeStruct((bsz, M_HEADS), F32),
        jax.ShapeDtypeStruct((steps, bsz, M_WIDTH), F32),
        jax.ShapeDtypeStruct((steps, bsz, A_WIDTH), F32),
        jax.ShapeDtypeStruct((bsz, A_KV_HEADS, A_HEAD_DIM, WINDOW), F32),
        jax.ShapeDtypeStruct((bsz, A_KV_HEADS, A_HEAD_DIM, WINDOW), F32),
    ]
    return pl.pallas_call(
        functools.partial(_sample_state_kernel, group=group, steps=steps),
        grid=(bsz // group,),
        in_specs=in_specs,
        out_specs=out_specs,
        out_shape=out_shape,
        compiler_params=pltpu.CompilerParams(dimension_semantics=("arbitrary",), vmem_limit_bytes=VMEM_LIMIT),
        name="sample_state",
    )(sinks, zs, state_c, state_n, state_m, ck_t, cv_t, rope, bias_g, gmn)


def _sample_out_kernel(x_ref, gate_ref, z_ref, ym_ref, ya_ref, wmo_ref, wao_ref, wo_ref, gf_ref, y_ref):
    bsz, steps, _ = x_ref.shape
    stack = lambda ref, cs: jnp.concatenate([ref[t, :, cs] for t in range(steps)], axis=0)
    full = slice(None)
    x = jnp.concatenate([x_ref[:, t, :] for t in range(steps)], axis=0)
    gate = jnp.concatenate([gate_ref[...]] * steps, axis=0)
    y = _out_stage(x, gate, _sigmoid(stack(z_ref, slice(0, D_MODEL))), _sigmoid(stack(z_ref, slice(D_MODEL, 2 * D_MODEL))),
                   stack(ym_ref, full).astype(BF16), stack(ya_ref, full).astype(BF16), wmo_ref, wao_ref, wo_ref, gf_ref[...])
    for t in range(steps):
        y_ref[:, t, :] = y[t * bsz:(t + 1) * bsz, :]


def _sample_out(x3, mod_s, zs, ym, ya, wmo, wao, wo, gf):
    bsz, steps, d = x3.shape
    return pl.pallas_call(
        _sample_out_kernel,
        grid=(1,),
        in_specs=[
            _full((bsz, steps, d)),
            pl.BlockSpec((bsz, d), lambda i: (0, 2)),
            pl.BlockSpec((steps, bsz, 2 * d), lambda i: (0, 0, 0)),
            _full((steps, bsz, M_WIDTH)), _full((steps, bsz, A_WIDTH)),
            _full((M_WIDTH, d)), _full((A_WIDTH, d)), _full((d, d)), _full((1, d)),
        ],
        out_specs=_full((bsz, steps, d)),
        out_shape=jax.ShapeDtypeStruct((bsz, steps, d), F32),
        compiler_params=pltpu.CompilerParams(dimension_semantics=("arbitrary",), vmem_limit_bytes=VMEM_LIMIT),
        name="sample_out",
    )(x3, mod_s, zs, ym, ya, wmo, wao, wo, gf)


PACK_BLOCK = 512


def _pack_sources():
    names = ("mq", "mk", "mv", "mi", "mf", "mo", "mz", "aq", "ak", "av", "az", "gm", "ga")
    start, pos = {}, 0
    for name, size in zip(names, IN_SIZES):
        start[name] = pos
        pos += size
    assert start["av"] == start["ak"] + A_KV_WIDTH and start["mf"] == start["mi"] + M_HEADS
    order = (("gm", C_GM, D_MODEL), ("ga", C_GA, D_MODEL), ("mq", C_MQ, M_WIDTH), ("mk", C_MK, M_WIDTH),
             ("mv", C_MV, M_WIDTH), ("mo", C_MO, M_WIDTH), ("mz", C_MZ, M_WIDTH), ("aq", C_AQ, A_WIDTH),
             ("az", C_AZ, A_WIDTH))
    src = []
    for name, col, width in order:
        assert col == len(src) * PACK_BLOCK and width % PACK_BLOCK == 0
        src += [start[name] + i * PACK_BLOCK for i in range(width // PACK_BLOCK)]
    assert C_AK == len(src) * PACK_BLOCK and C_AV == C_AK + A_KV_WIDTH and C_GI == C_AV + A_KV_WIDTH
    assert C_GF == C_GI + LANES and N_PACK == C_AK + PACK_BLOCK and start["ak"] + PACK_BLOCK <= pos
    return src + [start["ak"]], start["mi"]


def _pack_kernel(src_ref, wt_ref, gate_ref, o_ref):
    j = pl.program_id(0)
    last = pl.num_programs(0) - 1

    @pl.when(j < last)
    def _block():
        o_ref[...] = jnp.transpose(wt_ref[...]).astype(BF16)

    @pl.when(j == last)
    def _kv_and_gates():
        kv = 2 * A_KV_WIDTH
        o_ref[:, 0:kv] = jnp.transpose(wt_ref[0:kv, :]).astype(BF16)
        g = gate_ref[...]
        pad = jnp.zeros((LANES - M_HEADS, g.shape[1]), F32)
        rows = jnp.concatenate([g[0:M_HEADS], pad, g[M_HEADS:2 * M_HEADS], pad], axis=0)
        o_ref[:, kv:kv + 2 * LANES] = jnp.transpose(rows).astype(BF16)


def _pack_w_in(w_in):
    d = w_in.shape[0]
    src, gate_row = _pack_sources()
    wt = jnp.transpose(w_in)
    return pl.pallas_call(
        _pack_kernel,
        grid_spec=pltpu.PrefetchScalarGridSpec(
            num_scalar_prefetch=1, grid=(N_PACK // PACK_BLOCK,),
            in_specs=[pl.BlockSpec((pl.Element(PACK_BLOCK), pl.Element(d)),
                                   lambda j, src_ref: (pl.multiple_of(src_ref[j], 8), 0)),
                      pl.BlockSpec((pl.Element(2 * M_HEADS), pl.Element(d)), lambda j, src_ref: (gate_row, 0))],
            out_specs=pl.BlockSpec((d, PACK_BLOCK), lambda j, src_ref: (0, j))),
        out_shape=jax.ShapeDtypeStruct((d, N_PACK), BF16),
        compiler_params=pltpu.CompilerParams(dimension_semantics=("arbitrary",)),
        name="pack_w_in",
    )(jnp.asarray(src, jnp.int32), wt, wt)


def kernel(x_prompt, x_sample, state_C, state_n, state_m, cache_k, cache_v, c_prompt, c_sample, w_ada, b_ada, g_norm, w_in, b_igate, b_fgate, g_mnorm, sinks, w_m_out, w_a_out, w_out, g_final):
    assert w_in.shape[0] == 1, "single-layer step"
    bsz_s, steps, d = x_sample.shape
    assert cache_k.shape[2] == WINDOW

    mod_p, mod_s = _adaln(c_prompt, c_sample, w_ada[0], b_ada[0])
    wp = _pack_w_in(w_in[0])
    wmo = w_m_out[0].astype(BF16)
    wao = w_a_out[0].astype(BF16)
    wo = w_out[0].astype(BF16)
    gn = g_norm[0].reshape(1, d)
    gf = g_final.reshape(1, d)
    gmn = g_mnorm[0].reshape(1, M_WIDTH)
    zpad = jnp.zeros((LANES - M_HEADS,), F32)
    bias_g = jnp.concatenate([b_igate[0], zpad, b_fgate[0], zpad]).reshape(1, 2 * LANES)
    sk = sinks[0]

    y_p, c_p, n_p, m_p, k_p, v_p = _prompt_layer(x_prompt, mod_p, sk, gn, gf, gmn, bias_g, wp, wmo, wao, wo)
    m_p = m_p[:, 0, 0:M_HEADS]

    zs = _sample_proj(x_sample, mod_s, gn, wp)
    rope = jnp.stack(_rope_tables(PAST_LEN + jnp.arange(steps, dtype=jnp.int32)))
    dims_keys = lambda c: jnp.transpose(c[0], (0, 2, 3, 1))
    c_s, n_s, m_s, ym, ya, k_t, v_t = _sample_state(zs, state_C[0], state_n[0], state_m[0], dims_keys(cache_k),
                                                    dims_keys(cache_v), rope, bias_g, gmn, sk)
    y_s = _sample_out(x_sample, mod_s, zs, ym, ya, wmo, wao, wo, gf)
    keys_dims = lambda c: jnp.transpose(c, (0, 3, 1, 2))[None]

    return (y_p, y_s, c_p[None], n_p[None], m_p[None], keys_dims(k_p), keys_dims(v_p),
            c_s[None], n_s[None], m_s[None], keys_dims(k_t), keys_dims(v_t))
```

```python
import functools

import jax
import jax.numpy as jnp
from jax import lax
from jax.experimental import pallas as pl
from jax.experimental.pallas import tpu as pltpu

F32 = jnp.float32
BF16 = jnp.bfloat16

D_MODEL = 1024
M_HEADS = 4
M_DK = 128
M_DV = 128
M_WIDTH = M_HEADS * M_DV
M_CHUNK = 128
A_HEADS = 8
A_KV_HEADS = 2
A_GROUP = A_HEADS // A_KV_HEADS
A_HEAD_DIM = 64
A_WIDTH = A_HEADS * A_HEAD_DIM
A_KV_WIDTH = A_KV_HEADS * A_HEAD_DIM
WINDOW = 128
ROT_DIM = A_HEAD_DIM // 4
ROPE_THETA = 500000.0
NORM_EPS = 1e-6
PAST_LEN = 16384
IN_SIZES = (M_HEADS * M_DK, M_HEADS * M_DK, M_WIDTH, M_HEADS, M_HEADS, M_WIDTH, M_WIDTH,
            A_WIDTH, A_KV_WIDTH, A_KV_WIDTH, A_WIDTH, D_MODEL, D_MODEL)

LANES = 128

C_GM = 0
C_GA = C_GM + D_MODEL
C_MQ = C_GA + D_MODEL
C_MK = C_MQ + M_WIDTH
C_MV = C_MK + M_WIDTH
C_MO = C_MV + M_WIDTH
C_MZ = C_MO + M_WIDTH
C_AQ = C_MZ + M_WIDTH
C_AZ = C_AQ + A_WIDTH
C_AK = C_AZ + A_WIDTH
C_AV = C_AK + A_KV_WIDTH
C_GI = C_AV + A_KV_WIDTH
C_GF = C_GI + LANES
N_PACK = C_GF + LANES

PROMPT_TILE = 256
SAMPLE_GROUP = 16
VMEM_LIMIT = 56 * 1024 * 1024


def _sigmoid(x):
    return 1.0 / (1.0 + jnp.exp(-x))


def _silu(x):
    return x * _sigmoid(x)


def _log_sigmoid(x):
    return jnp.minimum(x, 0.0) - jnp.log1p(jnp.exp(-jnp.abs(x)))


def _dot(a, b):
    return jnp.dot(a, b, preferred_element_type=F32)


def _dot_nt(a, b):
    return lax.dot_general(a, b, (((1,), (1,)), ((), ())), preferred_element_type=F32)


def _dot_tn(a, b):
    return lax.dot_general(a, b, (((0,), (0,)), ((), ())), preferred_element_type=F32)


def _dot_exact01(m01, x):
    x1 = x.astype(BF16)
    r1 = x - x1.astype(F32)
    x2 = r1.astype(BF16)
    x3 = (r1 - x2.astype(F32)).astype(BF16)
    return _dot(m01, x1) + _dot(m01, x2) + _dot(m01, x3)


def _rms(x):
    return x * lax.rsqrt(jnp.mean(x * x, axis=-1, keepdims=True) + NORM_EPS)


def _rope(blk, ra, rb, rc):
    return blk * ra + pltpu.roll(blk, 8, 1) * rb + pltpu.roll(blk, LANES - 8, 1) * rc


def _out_stage(x, gate, sgm, sga, ym, ya, wmo_ref, wao_ref, wo_ref, gf):
    pm = _dot(ym, wmo_ref[...])
    pa = _dot(ya, wao_ref[...])
    u = sgm * pm + sga * pa
    r = _dot(u.astype(BF16), wo_ref[...])
    return _rms(x + gate * r) * gf


def _adaln_kernel(cp_ref, cs_ref, w_ref, b_ref, op_ref, os_ref):
    w = w_ref[...].astype(BF16)
    b = b_ref[...]
    op_ref[...] = _dot(_silu(cp_ref[...]).astype(BF16), w) + b
    os_ref[...] = _dot(_silu(cs_ref[...]).astype(BF16), w) + b


def _adaln(c_p, c_s, w_ada, b_ada):
    bp, d = c_p.shape
    bs = c_s.shape[0]
    n = w_ada.shape[1]
    tn = 1024
    return pl.pallas_call(
        _adaln_kernel,
        grid=(n // tn,),
        in_specs=[
            pl.BlockSpec((bp, d), lambda j: (0, 0)),
            pl.BlockSpec((bs, d), lambda j: (0, 0)),
            pl.BlockSpec((d, tn), lambda j: (0, j)),
            pl.BlockSpec((1, tn), lambda j: (0, j)),
        ],
        out_specs=[
            pl.BlockSpec((bp, tn), lambda j: (0, j)),
            pl.BlockSpec((bs, tn), lambda j: (0, j)),
        ],
        out_shape=[jax.ShapeDtypeStruct((bp, n), F32), jax.ShapeDtypeStruct((bs, n), F32)],
        compiler_params=pltpu.CompilerParams(dimension_semantics=("arbitrary",)),
        name="adaln",
    )(c_p, c_s, w_ada, b_ada.reshape(1, n))


_SET_FIELDS = ("hb", "q", "k", "vaug", "og", "qa", "kf", "vf", "sz", "sgm", "sga", "g")


def _set_shapes(tile):
    return [
        pltpu.VMEM((tile, D_MODEL), BF16),
        pltpu.VMEM((tile, M_WIDTH), BF16),
        pltpu.VMEM((tile, M_WIDTH), F32),
        pltpu.VMEM((M_HEADS, tile, 2 * LANES), BF16),
        pltpu.VMEM((tile, M_WIDTH), F32),
        pltpu.VMEM((tile, A_WIDTH), BF16),
        pltpu.VMEM((tile, A_KV_WIDTH), F32),
        pltpu.VMEM((tile, A_KV_WIDTH), F32),
        pltpu.VMEM((tile, A_WIDTH), F32),
        pltpu.VMEM((tile, D_MODEL), F32),
        pltpu.VMEM((tile, D_MODEL), F32),
        pltpu.VMEM((tile, 2 * LANES), F32),
    ]


N_TILE = 256


def _project_pieces(x_ref, mod_ref, rope_refs, st, gn_ref, bg_ref, w_ref):
    def norm():
        mod = mod_ref[0]
        h = _rms(x_ref[0]) * gn_ref[...]
        st["hb"][...] = (h * (1.0 + mod[:, D_MODEL:2 * D_MODEL]) + mod[:, 0:D_MODEL]).astype(BF16)

    def proj(lo):
        return _dot(st["hb"][...], w_ref[:, lo:lo + N_TILE])

    def rope(v):
        ra_ref, rb_ref, rc_ref = rope_refs
        return _rope(v, ra_ref[...], rb_ref[...], rc_ref[...])

    def cols(i):
        return slice(i * N_TILE, (i + 1) * N_TILE)

    def gates():
        st["g"][...] = proj(C_GI) + bg_ref[...]

    def queries(i):
        def piece():
            st["q"][:, cols(i)] = (proj(C_MQ + i * N_TILE) * (M_DK ** -0.5)).astype(BF16)
        return piece

    def keys(i):
        def piece():
            st["k"][:, cols(i)] = proj(C_MK + i * N_TILE)
        return piece

    def values(i):
        def piece():
            zv = proj(C_MV + i * N_TILE)
            for j in range(N_TILE // M_DV):
                st["vaug"][i * (N_TILE // M_DV) + j, :, 0:LANES] = zv[:, j * M_DV:(j + 1) * M_DV].astype(BF16)
        return piece

    def attn_kv():
        z = proj(C_AK)
        st["kf"][...] = rope(z[:, 0:A_KV_WIDTH])
        st["vf"][...] = z[:, A_KV_WIDTH:2 * A_KV_WIDTH]

    def attn_q(i):
        def piece():
            za = proj(C_AQ + i * N_TILE)
            for j in range(N_TILE // LANES):
                blk = rope(za[:, j * LANES:(j + 1) * LANES])
                lo = i * N_TILE + j * LANES
                st["qa"][:, lo:lo + LANES] = (blk * (A_HEAD_DIM ** -0.5)).astype(BF16)
        return piece

    def out_gate(i):
        def piece():
            st["og"][:, cols(i)] = _sigmoid(proj(C_MO + i * N_TILE)) * _silu(proj(C_MZ + i * N_TILE))
        return piece

    def attn_gate(i):
        def piece():
            st["sz"][:, cols(i)] = _silu(proj(C_AZ + i * N_TILE))
        return piece

    def merge_gate(name, lo, i):
        def piece():
            st[name][:, cols(i)] = _sigmoid(proj(lo + i * N_TILE))
        return piece

    assert C_AV == C_AK + A_KV_WIDTH and 2 * A_KV_WIDTH == N_TILE and C_GF == C_GI + LANES
    half, full = range(M_WIDTH // N_TILE), range(D_MODEL // N_TILE)
    pieces = [norm, gates, attn_kv]
    for i in half:
        pieces += [queries(i), keys(i), values(i), attn_q(i)]
    for i in half:
        pieces += [out_gate(i), attn_gate(i)]
    for i in full:
        pieces += [merge_gate("sgm", C_GM, i), merge_gate("sga", C_GA, i)]
    return pieces


def _consume_pieces(st, first_tile, sinks_ref, gmn_ref, kbuf_s, vbuf_s, c_s, m_s, ym_s, ya_s, tile):
    L = M_CHUNK
    heads = [slice(hd * M_DK, (hd + 1) * M_DK) for hd in range(M_HEADS)]
    a_heads = [(kv, bi, half) for kv in range(A_KV_HEADS) for bi in range(2) for half in range(2)]

    def causal_mask():
        row = lax.broadcasted_iota(jnp.int32, (L, L), 0)
        col = lax.broadcasted_iota(jnp.int32, (L, L), 1)
        return row >= col

    def chunk_stages(j):
        rs = slice(j * L, (j + 1) * L)
        cx = {}

        def gates():
            g = st["g"][rs, :]
            ig_all = g[:, 0:LANES]
            lf_all = _log_sigmoid(g[:, LANES:2 * LANES])
            tril = jnp.where(causal_mask(), 1.0, 0.0).astype(BF16)
            b_all = _dot_exact01(tril, lf_all)
            b_last = b_all[L - 1:L, :]
            m_old = m_s[...]
            d_end = b_last - b_all + ig_all
            m_new = jnp.maximum(b_last + m_old, jnp.max(d_end, axis=0, keepdims=True))
            m_s[...] = m_new
            cx.update(b_all=b_all, w_end=jnp.exp(d_end - m_new), decay=jnp.exp(b_last + m_old - m_new),
                      inter=b_all + m_old, r_t=jnp.transpose(ig_all - b_all))

        def kv_buffers():
            srows = slice((j % 2) * L, (j % 2 + 1) * L)
            lo_half = lax.broadcasted_iota(jnp.int32, (L, L), 1) < A_HEAD_DIM
            kc = st["kf"][rs, :]
            vc = st["vf"][rs, :]
            kr = pltpu.roll(kc, A_HEAD_DIM, 1)
            vr = pltpu.roll(vc, A_HEAD_DIM, 1)
            kbuf_s[0, srows, :] = jnp.where(lo_half, kc, 0.0).astype(BF16)
            kbuf_s[1, srows, :] = jnp.where(lo_half, 0.0, kr).astype(BF16)
            kbuf_s[2, srows, :] = jnp.where(lo_half, kr, 0.0).astype(BF16)
            kbuf_s[3, srows, :] = jnp.where(lo_half, 0.0, kc).astype(BF16)
            lane = lax.broadcasted_iota(jnp.int32, (L, L), 1)
            ones_lo = jnp.where(lane == 0, 1.0, 0.0)
            ones_hi = jnp.where(lane == A_HEAD_DIM, 1.0, 0.0)
            vbuf_s[0, srows, :] = jnp.where(lo_half, vc, ones_hi).astype(BF16)
            vbuf_s[1, srows, :] = jnp.where(lo_half, ones_lo, vr).astype(BF16)
            vbuf_s[2, srows, :] = jnp.where(lo_half, vr, ones_hi).astype(BF16)
            vbuf_s[3, srows, :] = jnp.where(lo_half, ones_lo, vc).astype(BF16)

        def decay_matrices():
            cx["dmat"] = [jnp.where(causal_mask(), cx["b_all"][:, hd:hd + 1] + cx["r_t"][hd:hd + 1, :], -jnp.inf)
                          for hd in range(M_HEADS)]
            cx["m_hat"] = [jnp.maximum(cx["inter"][:, hd:hd + 1], jnp.max(cx["dmat"][hd], axis=-1, keepdims=True))
                           for hd in range(M_HEADS)]
            cx["kw"] = [(st["k"][rs, heads[hd]] * cx["w_end"][:, hd:hd + 1]).astype(BF16) for hd in range(M_HEADS)]

        def state_matmuls():
            cx["qk"], cx["qc"] = [], []
            for hd in range(M_HEADS):
                qh = st["q"][rs, heads[hd]]
                c_old = c_s[hd]
                cx["qk"].append(_dot_nt(qh, st["k"][rs, heads[hd]].astype(BF16)))
                cx["qc"].append(_dot(qh, c_old.astype(BF16)))
                va = st["vaug"][hd, rs, :]
                c_s[hd] = cx["decay"][:, hd:hd + 1] * c_old + _dot_tn(cx["kw"][hd], va)

        def attn_scores():
            row2 = lax.broadcasted_iota(jnp.int32, (L, 2 * L), 0)
            col2 = lax.broadcasted_iota(jnp.int32, (L, 2 * L), 1)
            no_prev = jnp.where(first_tile, 4 * L, 0) if j == 0 else 0
            if j % 2 == 0:
                mask = ((col2 < L) & (col2 <= row2)) | ((col2 >= L) & (col2 - L > row2 + no_prev))
            else:
                mask = ((col2 >= L) & (col2 - L <= row2)) | ((col2 < L) & (col2 > row2 + no_prev))
            cx["sh"] = {}
            for kv in range(A_KV_HEADS):
                qcat = jnp.concatenate([st["qa"][rs, (2 * kv) * LANES:(2 * kv + 1) * LANES],
                                        st["qa"][rs, (2 * kv + 1) * LANES:(2 * kv + 2) * LANES]], axis=0)
                for half in range(2):
                    sc = _dot_nt(qcat, kbuf_s[2 * kv + half])
                    for bi in range(2):
                        cx["sh"][kv, bi, half] = jnp.where(mask, sc[bi * L:(bi + 1) * L, :], -jnp.inf)

        def attn_max():
            cx["mx"] = {hk: jnp.maximum(jnp.max(cx["sh"][hk], axis=-1, keepdims=True),
                                        sinks_ref[A_GROUP * hk[0] + 2 * hk[1] + hk[2]]) for hk in a_heads}

        def stabilised_scores():
            cx["s"] = [(cx["qk"][hd] * jnp.exp(cx["dmat"][hd] - cx["m_hat"][hd])).astype(BF16)
                       for hd in range(M_HEADS)]

        def score_value_matmuls():
            cx["sv"] = [_dot(cx["s"][hd], st["vaug"][hd, rs, :]) for hd in range(M_HEADS)]

        def attn_exp():
            cx["p"] = {hk: jnp.exp(cx["sh"][hk] - cx["mx"][hk]).astype(BF16) for hk in a_heads}
            cx["sink"] = {hk: jnp.exp(sinks_ref[A_GROUP * hk[0] + 2 * hk[1] + hk[2]] - cx["mx"][hk]) for hk in a_heads}

        def attn_wait():
            pass

        def attn_value_matmuls():
            lo_half = lax.broadcasted_iota(jnp.int32, (L, L), 1) < A_HEAD_DIM
            for kv in range(A_KV_HEADS):
                for bi in range(2):
                    o_lo = _dot(cx["p"][kv, bi, 0], vbuf_s[2 * kv])
                    o_hi = _dot(cx["p"][kv, bi, 1], vbuf_s[2 * kv + 1])
                    den_lo = o_lo[:, A_HEAD_DIM:A_HEAD_DIM + 1] + cx["sink"][kv, bi, 0]
                    den_hi = o_hi[:, 0:1] + cx["sink"][kv, bi, 1]
                    o = jnp.where(lo_half, o_lo / den_lo, o_hi / den_hi)
                    cs = slice((2 * kv + bi) * LANES, (2 * kv + bi + 1) * LANES)
                    ya_s[rs, cs] = (o * st["sz"][rs, cs]).astype(BF16)

        def mlstm_outputs():
            hh = []
            for hd in range(M_HEADS):
                m_hat = cx["m_hat"][hd]
                w_inter = jnp.exp(cx["inter"][:, hd:hd + 1] - m_hat)
                tot = cx["sv"][hd] + w_inter * cx["qc"][hd]
                hh.append(tot[:, 0:M_DV] / jnp.maximum(jnp.abs(tot[:, M_DV:M_DV + 1]), jnp.exp(-m_hat)))
            ms = [jnp.mean(h * h, axis=-1, keepdims=True) for h in hh]
            for hd in range(M_HEADS):
                hs = heads[hd]
                ym_s[rs, hs] = (hh[hd] * lax.rsqrt(ms[hd] + NORM_EPS) * gmn_ref[:, hs] * st["og"][rs, hs]).astype(BF16)

        mlstm = [gates, decay_matrices, state_matmuls, stabilised_scores, score_value_matmuls, mlstm_outputs]
        attn = [kv_buffers, attn_scores, attn_max, attn_exp, attn_wait, attn_value_matmuls]
        return mlstm, attn

    assert tile == 2 * L
    (m0, a0), (m1, a1) = chunk_stages(0), chunk_stages(1)
    merged = []
    for m_stage, a_stage in zip(m0 + m1, a0 + a1):
        merged += [m_stage, a_stage]
    return merged


def _output_pieces(st, x_ref, rows, gate_ref, ym_s, ya_s, u_s, wmo_ref, wao_ref, wo_ref, gf_ref, y_ref):
    def merge():
        u = st["sgm"][...] * _dot(ym_s[...], wmo_ref[...]) + st["sga"][...] * _dot(ya_s[...], wao_ref[...])
        u_s[...] = u.astype(BF16)

    def out():
        gate = gate_ref[0][:, 2 * D_MODEL:3 * D_MODEL]
        r = _dot(u_s[...], wo_ref[...])
        y_ref[0, rows, :] = _rms(x_ref[0, rows, :] + gate * r) * gf_ref[...]

    return [merge, out]


def _interleave(primary, filler):
    done = 0
    for i, piece in enumerate(primary):
        piece()
        upto = ((i + 1) * len(filler)) // len(primary)
        for f in filler[done:upto]:
            f()
        done = upto


def _prompt_kernel(sinks_ref, x0_ref, xa_ref, xb_ref, xres_ref, mod0_ref, moda_ref, modb_ref,
                   gn_ref, gf_ref, gmn_ref, bg_ref, w_hbm, wmo_hbm, wao_hbm, wo_hbm,
                   ra_ref, rb_ref, rc_ref,
                   y_ref, c_out_ref, n_out_ref, m_out_ref, kk_ref, vk_ref, *scratch, tile, steps_per_seq, n_tiles):
    nset = len(_SET_FIELDS)
    set_a = dict(zip(_SET_FIELDS, scratch[0:nset]))
    set_b = dict(zip(_SET_FIELDS, scratch[nset:2 * nset]))
    (kbuf_s, vbuf_s, c_s, m_s, yma_s, yaa_s, ua_s, ymb_s, yab_s, ub_s,
     w_ref, wmo_ref, wao_ref, wo_ref, w_sems) = scratch[2 * nset:]
    k = pl.program_id(0)
    first = (k % steps_per_seq) == 0
    tiles_per_seq = 2 * steps_per_seq

    def project(x_ref, mod_ref, tile_index, st):
        rows = pl.ds(pl.multiple_of((tile_index % tiles_per_seq) * tile, tile), tile)
        rope = tuple(r.at[rows, :] for r in (ra_ref, rb_ref, rc_ref))
        return _project_pieces(x_ref, mod_ref, rope, st, gn_ref, bg_ref, w_ref)

    @pl.when(k == 0)
    def _prologue():
        copies = [pltpu.make_async_copy(src, dst, w_sems.at[i]) for i, (src, dst) in enumerate(
            ((w_hbm, w_ref), (wmo_hbm, wmo_ref), (wao_hbm, wao_ref), (wo_hbm, wo_ref)))]
        for copy in copies:
            copy.start()
        for copy in copies:
            copy.wait()
        lane = lax.broadcasted_iota(jnp.int32, (tile, LANES), 1)
        ones_col = jnp.where(lane == 0, 1.0, 0.0).astype(BF16)
        for st in (set_a, set_b):
            for hd in range(M_HEADS):
                st["vaug"][hd, :, LANES:2 * LANES] = ones_col
        for piece in project(x0_ref, mod0_ref, 0, set_a):
            piece()

    @pl.when(first)
    def _init():
        c_s[...] = jnp.zeros_like(c_s)
        m_s[...] = jnp.zeros_like(m_s)
        kbuf_s[...] = jnp.zeros_like(kbuf_s)
        vbuf_s[...] = jnp.zeros_like(vbuf_s)

    state = (sinks_ref, gmn_ref, kbuf_s, vbuf_s, c_s, m_s)
    out_w = (wmo_ref, wao_ref, wo_ref, gf_ref, y_ref)
    _interleave(_consume_pieces(set_a, first, *state, yma_s, yaa_s, tile),
                project(xa_ref, moda_ref, 2 * k + 1, set_b))
    _interleave(_consume_pieces(set_b, False, *state, ymb_s, yab_s, tile),
                _output_pieces(set_a, xres_ref, slice(0, tile), moda_ref, yma_s, yaa_s, ua_s, *out_w)
                + project(xb_ref, modb_ref, jnp.minimum(2 * k + 2, n_tiles - 1), set_a))
    for piece in _output_pieces(set_b, xres_ref, slice(tile, 2 * tile), moda_ref, ymb_s, yab_s, ub_s, *out_w):
        piece()

    @pl.when((k % steps_per_seq) == steps_per_seq - 1)
    def _final():
        for hd in range(M_HEADS):
            cf = c_s[hd]
            c_out_ref[0, hd] = cf[:, 0:M_DV]
            n_out_ref[0, hd:hd + 1, :] = jnp.transpose(cf[:, M_DV:2 * M_DV])[0:1, :]
        m_out_ref[0] = m_s[...]
        k_t = jnp.transpose(set_b["kf"][tile - WINDOW:tile, :])
        v_t = jnp.transpose(set_b["vf"][tile - WINDOW:tile, :])
        for kv in range(A_KV_HEADS):
            kk_ref[0, kv] = k_t[kv * A_HEAD_DIM:(kv + 1) * A_HEAD_DIM, :]
            vk_ref[0, kv] = v_t[kv * A_HEAD_DIM:(kv + 1) * A_HEAD_DIM, :]


def _rope_tables(pos):
    half = ROT_DIM // 2
    dim = jnp.arange(LANES, dtype=jnp.int32) % A_HEAD_DIM
    inv = ROPE_THETA ** (-(2 * (dim % half)).astype(F32) / ROT_DIM)
    ang = pos.astype(F32)[:, None] * inv[None, :]
    cos, sin = jnp.cos(ang), jnp.sin(ang)
    ra = jnp.where(dim < ROT_DIM, cos, 1.0)
    rb = jnp.where((dim >= half) & (dim < ROT_DIM), sin, 0.0)
    rc = jnp.where(dim < half, -sin, 0.0)
    return ra, rb, rc


def _full(shape):
    return pl.BlockSpec(shape, lambda *_: (0,) * len(shape))


def _prompt_layer(x, mod, sinks, gn, gf, gmn, bias_g, wp, wmo, wao, wo):
    bsz, seq, d = x.shape
    tile = PROMPT_TILE
    assert seq % (2 * tile) == 0 and tile % (2 * M_CHUNK) == 0 and d == D_MODEL
    tiles_per_seq = seq // tile
    steps_per_seq = tiles_per_seq // 2
    n_tiles = bsz * tiles_per_seq
    n_steps = n_tiles // 2
    ra, rb, rc = _rope_tables(jnp.arange(seq, dtype=jnp.int32))
    xt = x.reshape(n_tiles, tile, d)
    xp = x.reshape(n_steps, 2 * tile, d)
    mod3 = mod.reshape(bsz, 1, 3 * d)

    tile_0 = lambda k: 0
    tile_a = lambda k: 2 * k + 1
    tile_b = lambda k: jnp.minimum(2 * k + 2, n_tiles - 1)
    x_spec = lambda f: pl.BlockSpec((1, tile, d), lambda k: (f(k), 0, 0))
    mod_spec = lambda f: pl.BlockSpec((1, 1, 3 * d), lambda k: (f(k) // tiles_per_seq, 0, 0))
    in_specs = [
        pl.BlockSpec(memory_space=pltpu.SMEM),
        x_spec(tile_0), x_spec(tile_a), x_spec(tile_b),
        pl.BlockSpec((1, 2 * tile, d), lambda k: (k, 0, 0)),
        mod_spec(tile_0), mod_spec(tile_a), mod_spec(tile_b),
        _full((1, d)), _full((1, d)), _full((1, M_WIDTH)), _full((1, 2 * LANES)),
        pl.BlockSpec(memory_space=pl.ANY), pl.BlockSpec(memory_space=pl.ANY),
        pl.BlockSpec(memory_space=pl.ANY), pl.BlockSpec(memory_space=pl.ANY),
        _full((seq, LANES)), _full((seq, LANES)), _full((seq, LANES)),
    ]
    seq_of = lambda k: k // steps_per_seq
    out_specs = [
        pl.BlockSpec((1, 2 * tile, d), lambda k: (k, 0, 0)),
        pl.BlockSpec((1, M_HEADS, M_DK, M_DV), lambda k: (seq_of(k), 0, 0, 0)),
        pl.BlockSpec((1, M_HEADS, M_DK), lambda k: (seq_of(k), 0, 0)),
        pl.BlockSpec((1, 1, LANES), lambda k: (seq_of(k), 0, 0)),
        pl.BlockSpec((1, A_KV_HEADS, A_HEAD_DIM, WINDOW), lambda k: (seq_of(k), 0, 0, 0)),
        pl.BlockSpec((1, A_KV_HEADS, A_HEAD_DIM, WINDOW), lambda k: (seq_of(k), 0, 0, 0)),
    ]
    out_shape = [
        jax.ShapeDtypeStruct((n_steps, 2 * tile, d), F32),
        jax.ShapeDtypeStruct((bsz, M_HEADS, M_DK, M_DV), F32),
        jax.ShapeDtypeStruct((bsz, M_HEADS, M_DK), F32),
        jax.ShapeDtypeStruct((bsz, 1, LANES), F32),
        jax.ShapeDtypeStruct((bsz, A_KV_HEADS, A_HEAD_DIM, WINDOW), F32),
        jax.ShapeDtypeStruct((bsz, A_KV_HEADS, A_HEAD_DIM, WINDOW), F32),
    ]
    scratch = _set_shapes(tile) + _set_shapes(tile) + [
        pltpu.VMEM((2 * A_KV_HEADS, 2 * WINDOW, LANES), BF16),
        pltpu.VMEM((2 * A_KV_HEADS, 2 * WINDOW, LANES), BF16),
        pltpu.VMEM((M_HEADS, M_DK, 2 * LANES), F32),
        pltpu.VMEM((1, LANES), F32),
        pltpu.VMEM((tile, M_WIDTH), BF16),
        pltpu.VMEM((tile, A_WIDTH), BF16),
        pltpu.VMEM((tile, D_MODEL), BF16),
        pltpu.VMEM((tile, M_WIDTH), BF16),
        pltpu.VMEM((tile, A_WIDTH), BF16),
        pltpu.VMEM((tile, D_MODEL), BF16),
        pltpu.VMEM((d, N_PACK), BF16),
        pltpu.VMEM((M_WIDTH, d), BF16),
        pltpu.VMEM((A_WIDTH, d), BF16),
        pltpu.VMEM((d, d), BF16),
        pltpu.SemaphoreType.DMA((4,)),
    ]
    outs = pl.pallas_call(
        functools.partial(_prompt_kernel, tile=tile, steps_per_seq=steps_per_seq, n_tiles=n_tiles),
        grid=(n_steps,),
        in_specs=in_specs,
        out_specs=out_specs,
        out_shape=out_shape,
        scratch_shapes=scratch,
        compiler_params=pltpu.CompilerParams(dimension_semantics=("arbitrary",), vmem_limit_bytes=VMEM_LIMIT),
        name="prompt_layer",
    )(sinks, xt, xt, xt, xp, mod3, mod3, mod3, gn, gf, gmn, bias_g, wp, wmo, wao, wo,
      ra, rb, rc)
    return (outs[0].reshape(bsz, seq, d),) + tuple(outs[1:])


def _sample_proj_kernel(x_ref, mod_ref, gn_ref, w_ref, z_ref, hb_s):
    bsz, steps, _ = x_ref.shape

    @pl.when(pl.program_id(0) == 0)
    def _norm():
        mod = mod_ref[...]
        for t in range(steps):
            h = _rms(x_ref[:, t, :]) * gn_ref[...]
            hb_s[t * bsz:(t + 1) * bsz, :] = (h * (1.0 + mod[:, D_MODEL:2 * D_MODEL]) + mod[:, 0:D_MODEL]).astype(BF16)

    z = _dot(hb_s[...], w_ref[...])
    for t in range(steps):
        z_ref[t] = z[t * bsz:(t + 1) * bsz, :]


def _sample_proj(x3, mod_s, gn, wp):
    bsz, steps, d = x3.shape
    tn = 1024
    return pl.pallas_call(
        _sample_proj_kernel,
        grid=(N_PACK // tn,),
        in_specs=[
            _full((bsz, steps, d)),
            pl.BlockSpec((bsz, 2 * d), lambda j: (0, 0)),
            _full((1, d)),
            pl.BlockSpec((d, tn), lambda j: (0, j)),
        ],
        out_specs=pl.BlockSpec((steps, bsz, tn), lambda j: (0, 0, j)),
        out_shape=jax.ShapeDtypeStruct((steps, bsz, N_PACK), F32),
        scratch_shapes=[pltpu.VMEM((steps * bsz, d), BF16)],
        compiler_params=pltpu.CompilerParams(dimension_semantics=("arbitrary",)),
        name="sample_proj",
    )(x3, mod_s, gn, wp)


def _sample_state_kernel(sinks_ref, z_ref, c_ref, n_ref, m_ref, ck_ref, cv_ref, rope_ref, bg_ref, gmn_ref,
                         cn_ref, nn_ref, mn_ref, ym_ref, ya_ref, ko_ref, vo_ref, *, group, steps):
    T, G = steps, group
    toks = range(T)

    def z(t, lo, n):
        return z_ref[t, :, lo:lo + n]

    def rope_t(t, blk):
        return _rope(blk, rope_ref[0, t:t + 1, :], rope_ref[1, t:t + 1, :], rope_ref[2, t:t + 1, :])

    lane = lax.broadcasted_iota(jnp.int32, (G, LANES), 1)
    lo_half = lane < A_HEAD_DIM
    gmn = gmn_ref[...]

    q = [z(t, C_MQ, M_WIDTH) * (M_DK ** -0.5) for t in toks]
    k = [z(t, C_MK, M_WIDTH) for t in toks]
    v = [z(t, C_MV, M_WIDTH) for t in toks]
    og = [_sigmoid(z(t, C_MO, M_WIDTH)) * _silu(z(t, C_MZ, M_WIDTH)) for t in toks]
    qa = [[rope_t(t, z(t, C_AQ + j * LANES, LANES)) * (A_HEAD_DIM ** -0.5) for j in range(A_WIDTH // LANES)]
          for t in toks]
    kn = [rope_t(t, z(t, C_AK, A_KV_WIDTH)) for t in toks]
    vn = [z(t, C_AV, A_KV_WIDTH) for t in toks]
    sz = [_silu(z(t, C_AZ, A_WIDTH)) for t in toks]
    gt = [z(t, C_GI, 2 * LANES) + bg_ref[...] for t in toks]
    ig = [gt[t][:, 0:LANES] for t in toks]
    lf = [_log_sigmoid(gt[t][:, LANES:2 * LANES]) for t in toks]


    m_all = m_ref[...]
    rep = lambda a, h: jnp.broadcast_to(a[:, h:h + 1], (G, LANES))
    gates = []
    for hd in range(M_HEADS):
        ig_h = [rep(ig[t], hd) for t in toks]
        lf_h = [rep(lf[t], hd) for t in toks]
        m_old = rep(m_all, hd)
        bsum = [lf_h[0]]
        for t in range(1, T):
            bsum.append(bsum[-1] + lf_h[t])
        b_last = bsum[-1]
        d_end = [b_last - bsum[s] + ig_h[s] for s in toks]
        m_new = b_last + m_old
        for s in toks:
            m_new = jnp.maximum(m_new, d_end[s])
        mn_ref[:, hd:hd + 1] = m_new[:, 0:1]
        gd = dict(w_end=[jnp.exp(d_end[s] - m_new) for s in toks], decay=jnp.exp(b_last + m_old - m_new),
                  e=[], w_inter=[], emh=[])
        for t in toks:
            inter = bsum[t] + m_old
            dm = [bsum[t] - bsum[s] + ig_h[s] for s in range(t + 1)]
            m_hat = inter
            for s in range(t + 1):
                m_hat = jnp.maximum(m_hat, dm[s])
            gd["e"].append([jnp.exp(dm[s] - m_hat) for s in range(t + 1)])
            gd["w_inter"].append(jnp.exp(inter - m_hat))
            gd["emh"].append(jnp.exp(-m_hat))
        gates.append(gd)

    heads = [slice(hd * M_DK, (hd + 1) * M_DK) for hd in range(M_HEADS)]
    n_old = [n_ref[:, hd, :] for hd in range(M_HEADS)]
    qk = {(hd, t, s): jnp.sum(q[t][:, heads[hd]] * k[s][:, heads[hd]], axis=-1, keepdims=True)
          for hd in range(M_HEADS) for t in toks for s in range(t + 1)}
    qn = {(hd, t): jnp.sum(q[t][:, heads[hd]] * n_old[hd], axis=-1, keepdims=True)
          for hd in range(M_HEADS) for t in toks}

    lo_f = jnp.where(lo_half, 1.0, 0.0)
    hi_f = 1.0 - lo_f
    dup = lambda a, kv: (jnp.where(lo_half, a, pltpu.roll(a, A_HEAD_DIM, 1)) if kv == 0 else
                         jnp.where(lo_half, pltpu.roll(a, A_HEAD_DIM, 1), a))
    kn_x = [[dup(kn[s], kv) for s in toks] for kv in range(A_KV_HEADS)]
    vn_x = [[dup(vn[s], kv) for s in toks] for kv in range(A_KV_HEADS)]
    s_new = {}
    for kv in range(A_KV_HEADS):
        for blk in range(2):
            for t in toks:
                for s in range(t + 1):
                    prod = qa[t][2 * kv + blk] * kn_x[kv][s]
                    s_new[kv, 0, blk, t, s] = jnp.sum(prod * lo_f, axis=-1, keepdims=True)
                    s_new[kv, 1, blk, t, s] = jnp.sum(prod * hi_f, axis=-1, keepdims=True)

    rows_tg = lax.broadcasted_iota(jnp.int32, (T * G, LANES), 0) % G
    own_rows = [rows_tg == b for b in range(G)]
    rows64 = lax.broadcasted_iota(jnp.int32, (2 * T * G, LANES), 0) % G

    qc = []
    for hd in range(M_HEADS):
        hs = heads[hd]
        gd = gates[hd]
        q32 = jnp.concatenate([q[t][:, hs] for t in toks], axis=0)
        kw = [k[s][:, hs] * gd["w_end"][s] for s in toks]
        kw_t = jnp.transpose(jnp.concatenate(kw, axis=0)).astype(BF16)
        v32 = jnp.concatenate([v[s][:, hs] for s in toks], axis=0)
        acc = None
        for b in range(G):
            c_old = c_ref[b, hd]
            part = _dot(jnp.where(own_rows[b], q32, 0.0).astype(BF16), c_old.astype(BF16))
            acc = part if acc is None else acc + part
            upd = _dot(kw_t, jnp.where(own_rows[b], v32, 0.0).astype(BF16))
            cn_ref[b, hd] = gd["decay"][b:b + 1, 0:1] * c_old + upd
        qc.append(acc)
        nn_ref[:, hd, :] = gd["decay"] * n_old[hd] + (kw[0] + kw[1] + kw[2] + kw[3])

    zeros_kt = jnp.zeros((A_HEAD_DIM, WINDOW), BF16)
    sc = {}
    for kv in range(A_KV_HEADS):
        l64 = jnp.concatenate([qa[t][2 * kv + blk] for blk in range(2) for t in toks], axis=0)
        own64 = [jnp.where(rows64 == b, l64, 0.0).astype(BF16) for b in range(G)]
        kt = [ck_ref[b, kv].astype(BF16) for b in range(G)]
        for par in range(2):
            acc = None
            for b in range(G):
                rhs = jnp.concatenate([kt[b], zeros_kt] if par == 0 else [zeros_kt, kt[b]], axis=0)
                part = _dot(own64[b], rhs)
                acc = part if acc is None else acc + part
            sc[kv, par] = acc

    hh = {}
    for hd in range(M_HEADS):
        gd = gates[hd]
        for t in toks:
            sv, ssum = None, None
            for s in range(t + 1):
                s_ts = qk[hd, t, s] * gd["e"][t][s]
                sv = s_ts * v[s][:, heads[hd]] if sv is None else sv + s_ts * v[s][:, heads[hd]]
                ssum = s_ts if ssum is None else ssum + s_ts
            wi = gd["w_inter"][t]
            num = sv + wi * qc[hd][t * G:(t + 1) * G, :]
            den = ssum + wi * qn[hd, t]
            hh[hd, t] = num / jnp.maximum(jnp.abs(den), gd["emh"][t])
    ms = {key: jnp.mean(val * val, axis=-1, keepdims=True) for key, val in hh.items()}
    for (hd, t), val in hh.items():
        hs = heads[hd]
        ym_ref[t, :, hs] = val * lax.rsqrt(ms[hd, t] + NORM_EPS) * gmn[:, hs] * og[t][:, hs]

    items = [(kv, par, blk, t) for kv in range(A_KV_HEADS) for par in range(2) for blk in range(2) for t in toks]
    s_c = {it: jnp.where(lane > it[3], sc[it[0], it[1]][(it[2] * T + it[3]) * G:(it[2] * T + it[3] + 1) * G, :], -jnp.inf)
           for it in items}
    mx_c = {it: jnp.max(s_c[it], axis=-1, keepdims=True) for it in items}
    p_c, p_n, sink_t = {}, {}, {}
    for it in items:
        kv, par, blk, t = it
        sink = sinks_ref[A_GROUP * kv + 2 * blk + par]
        mx = jnp.maximum(mx_c[it], sink)
        for s in range(t + 1):
            mx = jnp.maximum(mx, s_new[kv, par, blk, t, s])
        p_c[it] = jnp.exp(s_c[it] - mx)
        p_n[it] = [jnp.exp(s_new[kv, par, blk, t, s] - mx) for s in range(t + 1)]
        sink_t[it] = jnp.exp(sink - mx)
    sum_c = {it: jnp.sum(p_c[it], axis=-1, keepdims=True) for it in items}
    fresh = {}
    probs = {(kv, par): [] for kv in range(A_KV_HEADS) for par in range(2)}
    for it in items:
        kv, par, blk, t = it
        den = sum_c[it] + sink_t[it]
        for s in range(t + 1):
            den = den + p_n[it][s]
        r = 1.0 / den
        probs[kv, par].append(p_c[it] * r)
        acc = None
        for s in range(t + 1):
            term = (p_n[it][s] * r) * vn_x[kv][s]
            acc = term if acc is None else acc + term
        fresh[it] = acc

    for kv in range(A_KV_HEADS):
        vt = [cv_ref[b, kv].astype(BF16) for b in range(G)]
        o = []
        for par in range(2):
            pstack = jnp.concatenate(probs[kv, par], axis=0)
            acc = None
            for b in range(G):
                rhs = jnp.concatenate([vt[b], zeros_kt] if par == 0 else [zeros_kt, vt[b]], axis=0)
                part = _dot_nt(jnp.where(rows64 == b, pstack, 0.0).astype(BF16), rhs)
                acc = part if acc is None else acc + part
            o.append(acc)
        for blk in range(2):
            cs = slice((2 * kv + blk) * LANES, (2 * kv + blk + 1) * LANES)
            for t in toks:
                rs = slice((blk * T + t) * G, (blk * T + t + 1) * G)
                new_v = jnp.where(lo_half, fresh[kv, 0, blk, t], fresh[kv, 1, blk, t])
                ya_ref[t, :, cs] = (o[0][rs, :] + o[1][rs, :] + new_v) * sz[t][:, cs]

    n_rows = T * G
    src = lax.broadcasted_iota(jnp.int32, (n_rows, n_rows), 1)
    dst = lax.broadcasted_iota(jnp.int32, (n_rows, n_rows), 0)
    perm = jnp.where(src == (dst % T) * G + dst // T, 1.0, 0.0).astype(BF16)
    lane_d = lax.broadcasted_iota(jnp.int32, (A_HEAD_DIM, WINDOW), 1)
    for new, cache_ref, out_ref in ((kn, ck_ref, ko_ref), (vn, cv_ref, vo_ref)):
        stack = _dot_exact01(perm, jnp.concatenate(new, axis=0))
        stack = jnp.concatenate([stack, jnp.zeros((LANES - n_rows, LANES), F32)], axis=0)
        new_t = jnp.transpose(stack)
        for b in range(G):
            placed = pltpu.roll(new_t, (WINDOW - T - T * b) % LANES, 1)
            for kv in range(A_KV_HEADS):
                slid = pltpu.roll(cache_ref[b, kv], WINDOW - T, 1)
                out_ref[b, kv] = jnp.where(lane_d >= WINDOW - T, placed[kv * A_HEAD_DIM:(kv + 1) * A_HEAD_DIM, :], slid)


def _sample_state(zs, state_c, state_n, state_m, ck_t, cv_t, rope, bias_g, gmn, sinks):
    steps, bsz, _ = zs.shape
    group = SAMPLE_GROUP
    assert bsz % group == 0
    in_specs = [
        pl.BlockSpec(memory_space=pltpu.SMEM),
        pl.BlockSpec((steps, group, N_PACK), lambda i: (0, i, 0)),
        pl.BlockSpec((group, M_HEADS, M_DK, M_DV), lambda i: (i, 0, 0, 0)),
        pl.BlockSpec((group, M_HEADS, M_DK), lambda i: (i, 0, 0)),
        pl.BlockSpec((group, M_HEADS), lambda i: (i, 0)),
        pl.BlockSpec((group, A_KV_HEADS, A_HEAD_DIM, WINDOW), lambda i: (i, 0, 0, 0)),
        pl.BlockSpec((group, A_KV_HEADS, A_HEAD_DIM, WINDOW), lambda i: (i, 0, 0, 0)),
        _full((3, steps, LANES)), _full((1, 2 * LANES)), _full((1, M_WIDTH)),
    ]
    out_specs = [
        pl.BlockSpec((group, M_HEADS, M_DK, M_DV), lambda i: (i, 0, 0, 0)),
        pl.BlockSpec((group, M_HEADS, M_DK), lambda i: (i, 0, 0)),
        pl.BlockSpec((group, M_HEADS), lambda i: (i, 0)),
        pl.BlockSpec((steps, group, M_WIDTH), lambda i: (0, i, 0)),
        pl.BlockSpec((steps, group, A_WIDTH), lambda i: (0, i, 0)),
        pl.BlockSpec((group, A_KV_HEADS, A_HEAD_DIM, WINDOW), lambda i: (i, 0, 0, 0)),
        pl.BlockSpec((group, A_KV_HEADS, A_HEAD_DIM, WINDOW), lambda i: (i, 0, 0, 0)),
    ]
    out_shape = [
        jax.ShapeDtypeStruct((bsz, M_HEADS, M_DK, M_DV), F32),
        jax.ShapeDtypeStruct((bsz, M_HEADS, M_DK), F32),
        jax.ShapeDtypeStruct((bsz, M_HEADS), F32),
        jax.ShapeDtypeStruct((steps, bsz, M_WIDTH), F32),
        jax.ShapeDtypeStruct((steps, bsz, A_WIDTH), F32),
        jax.ShapeDtypeStruct((bsz, A_KV_HEADS, A_HEAD_DIM, WINDOW), F32),
        jax.ShapeDtypeStruct((bsz, A_KV_HEADS, A_HEAD_DIM, WINDOW), F32),
    ]
    return pl.pallas_call(
        functools.partial(_sample_state_kernel, group=group, steps=steps),
        grid=(bsz // group,),
        in_specs=in_specs,
        out_specs=out_specs,
        out_shape=out_shape,
        compiler_params=pltpu.CompilerParams(dimension_semantics=("arbitrary",), vmem_limit_bytes=VMEM_LIMIT),
        name="sample_state",
    )(sinks, zs, state_c, state_n, state_m, ck_t, cv_t, rope, bias_g, gmn)


def _sample_out_kernel(x_ref, gate_ref, z_ref, ym_ref, ya_ref, wmo_ref, wao_ref, wo_ref, gf_ref, y_ref):
    bsz, steps, _ = x_ref.shape
    stack = lambda ref, cs: jnp.concatenate([ref[t, :, cs] for t in range(steps)], axis=0)
    full = slice(None)
    x = jnp.concatenate([x_ref[:, t, :] for t in range(steps)], axis=0)
    gate = jnp.concatenate([gate_ref[...]] * steps, axis=0)
    y = _out_stage(x, gate, _sigmoid(stack(z_ref, slice(0, D_MODEL))), _sigmoid(stack(z_ref, slice(D_MODEL, 2 * D_MODEL))),
                   stack(ym_ref, full).astype(BF16), stack(ya_ref, full).astype(BF16), wmo_ref, wao_ref, wo_ref, gf_ref[...])
    for t in range(steps):
        y_ref[:, t, :] = y[t * bsz:(t + 1) * bsz, :]


def _sample_out(x3, mod_s, zs, ym, ya, wmo, wao, wo, gf):
    bsz, steps, d = x3.shape
    return pl.pallas_call(
        _sample_out_kernel,
        grid=(1,),
        in_specs=[
            _full((bsz, steps, d)),
            pl.BlockSpec((bsz, d), lambda i: (0, 2)),
            pl.BlockSpec((steps, bsz, 2 * d), lambda i: (0, 0, 0)),
            _full((steps, bsz, M_WIDTH)), _full((steps, bsz, A_WIDTH)),
            _full((M_WIDTH, d)), _full((A_WIDTH, d)), _full((d, d)), _full((1, d)),
        ],
        out_specs=_full((bsz, steps, d)),
        out_shape=jax.ShapeDtypeStruct((bsz, steps, d), F32),
        compiler_params=pltpu.CompilerParams(dimension_semantics=("arbitrary",), vmem_limit_bytes=VMEM_LIMIT),
        name="sample_out",
    )(x3, mod_s, zs, ym, ya, wmo, wao, wo, gf)


PACK_BLOCK = 512


def _pack_sources():
    names = ("mq", "mk", "mv", "mi", "mf", "mo", "mz", "aq", "ak", "av", "az", "gm", "ga")
    start, pos = {}, 0
    for name, size in zip(names, IN_SIZES):
        start[name] = pos
        pos += size
    assert start["av"] == start["ak"] + A_KV_WIDTH and start["mf"] == start["mi"] + M_HEADS
    order = (("gm", C_GM, D_MODEL), ("ga", C_GA, D_MODEL), ("mq", C_MQ, M_WIDTH), ("mk", C_MK, M_WIDTH),
             ("mv", C_MV, M_WIDTH), ("mo", C_MO, M_WIDTH), ("mz", C_MZ, M_WIDTH), ("aq", C_AQ, A_WIDTH),
             ("az", C_AZ, A_WIDTH))
    src = []
    for name, col, width in order:
        assert col == len(src) * PACK_BLOCK and width % PACK_BLOCK == 0
        src += [start[name] + i * PACK_BLOCK for i in range(width // PACK_BLOCK)]
    assert C_AK == len(src) * PACK_BLOCK and C_AV == C_AK + A_KV_WIDTH and C_GI == C_AV + A_KV_WIDTH
    assert C_GF == C_GI + LANES and N_PACK == C_AK + PACK_BLOCK and start["ak"] + PACK_BLOCK <= pos
    return src + [start["ak"]], start["mi"]


def _pack_kernel(src_ref, wt_ref, gate_ref, o_ref):
    j = pl.program_id(0)
    last = pl.num_programs(0) - 1

    @pl.when(j < last)
    def _block():
        o_ref[...] = jnp.transpose(wt_ref[...]).astype(BF16)

    @pl.when(j == last)
    def _kv_and_gates():
        kv = 2 * A_KV_WIDTH
        o_ref[:, 0:kv] = jnp.transpose(wt_ref[0:kv, :]).astype(BF16)
        g = gate_ref[...]
        pad = jnp.zeros((LANES - M_HEADS, g.shape[1]), F32)
        rows = jnp.concatenate([g[0:M_HEADS], pad, g[M_HEADS:2 * M_HEADS], pad], axis=0)
        o_ref[:, kv:kv + 2 * LANES] = jnp.transpose(rows).astype(BF16)


def _pack_w_in(w_in):
    d = w_in.shape[0]
    src, gate_row = _pack_sources()
    wt = jnp.transpose(w_in)
    return pl.pallas_call(
        _pack_kernel,
        grid_spec=pltpu.PrefetchScalarGridSpec(
            num_scalar_prefetch=1, grid=(N_PACK // PACK_BLOCK,),
            in_specs=[pl.BlockSpec((pl.Element(PACK_BLOCK), pl.Element(d)),
                                   lambda j, src_ref: (pl.multiple_of(src_ref[j], 8), 0)),
                      pl.BlockSpec((pl.Element(2 * M_HEADS), pl.Element(d)), lambda j, src_ref: (gate_row, 0))],
            out_specs=pl.BlockSpec((d, PACK_BLOCK), lambda j, src_ref: (0, j))),
        out_shape=jax.ShapeDtypeStruct((d, N_PACK), BF16),
        compiler_params=pltpu.CompilerParams(dimension_semantics=("arbitrary",)),
        name="pack_w_in",
    )(jnp.asarray(src, jnp.int32), wt, wt)


def kernel(x_prompt, x_sample, state_C, state_n, state_m, cache_k, cache_v, c_prompt, c_sample, w_ada, b_ada, g_norm, w_in, b_igate, b_fgate, g_mnorm, sinks, w_m_out, w_a_out, w_out, g_final):
    assert w_in.shape[0] == 1, "single-layer step"
    bsz_s, steps, d = x_sample.shape
    assert cache_k.shape[2] == WINDOW

    mod_p, mod_s = _adaln(c_prompt, c_sample, w_ada[0], b_ada[0])
    wp = _pack_w_in(w_in[0])
    wmo = w_m_out[0].astype(BF16)
    wao = w_a_out[0].astype(BF16)
    wo = w_out[0].astype(BF16)
    gn = g_norm[0].reshape(1, d)
    gf = g_final.reshape(1, d)
    gmn = g_mnorm[0].reshape(1, M_WIDTH)
    zpad = jnp.zeros((LANES - M_HEADS,), F32)
    bias_g = jnp.concatenate([b_igate[0], zpad, b_fgate[0], zpad]).reshape(1, 2 * LANES)
    sk = sinks[0]

    y_p, c_p, n_p, m_p, k_p, v_p = _prompt_layer(x_prompt, mod_p, sk, gn, gf, gmn, bias_g, wp, wmo, wao, wo)
    m_p = m_p[:, 0, 0:M_HEADS]

    zs = _sample_proj(x_sample, mod_s, gn, wp)
    rope = jnp.stack(_rope_tables(PAST_LEN + jnp.arange(steps, dtype=jnp.int32)))
    dims_keys = lambda c: jnp.transpose(c[0], (0, 2, 3, 1))
    c_s, n_s, m_s, ym, ya, k_t, v_t = _sample_state(zs, state_C[0], state_n[0], state_m[0], dims_keys(cache_k),
                                                    dims_keys(cache_v), rope, bias_g, gmn, sk)
    y_s = _sample_out(x_sample, mod_s, zs, ym, ya, wmo, wao, wo, gf)
    keys_dims = lambda c: jnp.transpose(c, (0, 3, 1, 2))[None]

    return (y_p, y_s, c_p[None], n_p[None], m_p[None], keys_dims(k_p), keys_dims(v_p),
            c_s[None], n_s[None], m_s[None], keys_dims(k_t), keys_dims(v_t))
```

```python
import functools

import jax
import jax.numpy as jnp
import numpy as np
from jax import lax
from jax.experimental import pallas as pl
from jax.experimental.pallas import tpu as pltpu

F32 = jnp.float32
BF16 = jnp.bfloat16

D_MODEL = 1024
M_HEADS = 4
M_DK = 128
M_DV = 128
M_WIDTH = M_HEADS * M_DV
M_CHUNK = 128
A_HEADS = 8
A_KV_HEADS = 2
A_GROUP = A_HEADS // A_KV_HEADS
A_HEAD_DIM = 64
A_WIDTH = A_HEADS * A_HEAD_DIM
A_KV_WIDTH = A_KV_HEADS * A_HEAD_DIM
WINDOW = 128
ROT_DIM = A_HEAD_DIM // 4
ROPE_THETA = 500000.0
NORM_EPS = 1e-6
PAST_LEN = 16384
IN_SIZES = (M_HEADS * M_DK, M_HEADS * M_DK, M_WIDTH, M_HEADS, M_HEADS, M_WIDTH, M_WIDTH,
            A_WIDTH, A_KV_WIDTH, A_KV_WIDTH, A_WIDTH, D_MODEL, D_MODEL)

LANES = 128

C_GM = 0
C_GA = C_GM + D_MODEL
C_MQ = C_GA + D_MODEL
C_MK = C_MQ + M_WIDTH
C_MV = C_MK + M_WIDTH
C_MO = C_MV + M_WIDTH
C_MZ = C_MO + M_WIDTH
C_AQ = C_MZ + M_WIDTH
C_AZ = C_AQ + A_WIDTH
C_AK = C_AZ + A_WIDTH
C_AV = C_AK + A_KV_WIDTH
C_GI = C_AV + A_KV_WIDTH
C_GF = C_GI + LANES
N_PACK = C_GF + LANES

PROMPT_TILE = 256
SAMPLE_GROUP = 16
VMEM_LIMIT = 56 * 1024 * 1024


def _sigmoid(x):
    return 1.0 / (1.0 + jnp.exp(-x))


def _silu(x):
    return x * _sigmoid(x)


def _log_sigmoid(x):
    return jnp.minimum(x, 0.0) - jnp.log1p(jnp.exp(-jnp.abs(x)))


def _dot(a, b):
    return jnp.dot(a, b, preferred_element_type=F32)


def _dot_nt(a, b):
    return lax.dot_general(a, b, (((1,), (1,)), ((), ())), preferred_element_type=F32)


def _dot_tn(a, b):
    return lax.dot_general(a, b, (((0,), (0,)), ((), ())), preferred_element_type=F32)


def _dot_exact01(m01, x):
    x1 = x.astype(BF16)
    r1 = x - x1.astype(F32)
    x2 = r1.astype(BF16)
    x3 = (r1 - x2.astype(F32)).astype(BF16)
    return _dot(m01, x1) + _dot(m01, x2) + _dot(m01, x3)


def _rms(x):
    return x * lax.rsqrt(jnp.mean(x * x, axis=-1, keepdims=True) + NORM_EPS)


def _rope(blk, ra, rb, rc):
    return blk * ra + pltpu.roll(blk, 8, 1) * rb + pltpu.roll(blk, LANES - 8, 1) * rc


def _out_stage(x, gate, sgm, sga, ym, ya, wmo_ref, wao_ref, wo_ref, gf):
    pm = _dot(ym, wmo_ref[...])
    pa = _dot(ya, wao_ref[...])
    u = sgm * pm + sga * pa
    r = _dot(u.astype(BF16), wo_ref[...])
    return _rms(x + gate * r) * gf


def _adaln_kernel(cp_ref, cs_ref, w_ref, b_ref, op_ref, os_ref):
    w = w_ref[...].astype(BF16)
    b = b_ref[...]
    op_ref[...] = _dot(_silu(cp_ref[...]).astype(BF16), w) + b
    os_ref[...] = _dot(_silu(cs_ref[...]).astype(BF16), w) + b


def _adaln(c_p, c_s, w_ada, b_ada):
    bp, d = c_p.shape
    bs = c_s.shape[0]
    n = w_ada.shape[1]
    tn = 1024
    return pl.pallas_call(
        _adaln_kernel,
        grid=(n // tn,),
        in_specs=[
            pl.BlockSpec((bp, d), lambda j: (0, 0)),
            pl.BlockSpec((bs, d), lambda j: (0, 0)),
            pl.BlockSpec((d, tn), lambda j: (0, j)),
            pl.BlockSpec((1, tn), lambda j: (0, j)),
        ],
        out_specs=[
            pl.BlockSpec((bp, tn), lambda j: (0, j)),
            pl.BlockSpec((bs, tn), lambda j: (0, j)),
        ],
        out_shape=[jax.ShapeDtypeStruct((bp, n), F32), jax.ShapeDtypeStruct((bs, n), F32)],
        compiler_params=pltpu.CompilerParams(dimension_semantics=("arbitrary",)),
        name="adaln",
    )(c_p, c_s, w_ada, b_ada.reshape(1, n))


_SET_FIELDS = ("hb", "q", "k", "vaug", "og", "qa", "kf", "vf", "sz", "sgm", "sga", "g")


def _set_shapes(tile):
    return [
        pltpu.VMEM((tile, D_MODEL), BF16),
        pltpu.VMEM((tile, M_WIDTH), BF16),
        pltpu.VMEM((tile, M_WIDTH), F32),
        pltpu.VMEM((M_HEADS, tile, 2 * LANES), BF16),
        pltpu.VMEM((tile, M_WIDTH), F32),
        pltpu.VMEM((tile, A_WIDTH), BF16),
        pltpu.VMEM((tile, A_KV_WIDTH), F32),
        pltpu.VMEM((tile, A_KV_WIDTH), F32),
        pltpu.VMEM((tile, A_WIDTH), F32),
        pltpu.VMEM((tile, D_MODEL), F32),
        pltpu.VMEM((tile, D_MODEL), F32),
        pltpu.VMEM((tile, 2 * LANES), F32),
    ]


N_TILE = 256


def _project_pieces(x_ref, mod_ref, rope_refs, st, gn_ref, bg_ref, w_ref):
    def norm():
        mod = mod_ref[0]
        h = _rms(x_ref[0]) * gn_ref[...]
        st["hb"][...] = (h * (1.0 + mod[:, D_MODEL:2 * D_MODEL]) + mod[:, 0:D_MODEL]).astype(BF16)

    def proj(lo):
        return _dot(st["hb"][...], w_ref[:, lo:lo + N_TILE])

    def rope(v):
        ra_ref, rb_ref, rc_ref = rope_refs
        return _rope(v, ra_ref[...], rb_ref[...], rc_ref[...])

    def cols(i):
        return slice(i * N_TILE, (i + 1) * N_TILE)

    def gates():
        st["g"][...] = proj(C_GI) + bg_ref[...]

    def queries(i):
        def piece():
            st["q"][:, cols(i)] = (proj(C_MQ + i * N_TILE) * (M_DK ** -0.5)).astype(BF16)
        return piece

    def keys(i):
        def piece():
            st["k"][:, cols(i)] = proj(C_MK + i * N_TILE)
        return piece

    def values(i):
        def piece():
            zv = proj(C_MV + i * N_TILE)
            for j in range(N_TILE // M_DV):
                st["vaug"][i * (N_TILE // M_DV) + j, :, 0:LANES] = zv[:, j * M_DV:(j + 1) * M_DV].astype(BF16)
        return piece

    def attn_kv():
        z = proj(C_AK)
        st["kf"][...] = rope(z[:, 0:A_KV_WIDTH])
        st["vf"][...] = z[:, A_KV_WIDTH:2 * A_KV_WIDTH]

    def attn_q(i):
        def piece():
            za = proj(C_AQ + i * N_TILE)
            for j in range(N_TILE // LANES):
                blk = rope(za[:, j * LANES:(j + 1) * LANES])
                lo = i * N_TILE + j * LANES
                st["qa"][:, lo:lo + LANES] = (blk * (A_HEAD_DIM ** -0.5)).astype(BF16)
        return piece

    def out_gate(i):
        def piece():
            st["og"][:, cols(i)] = _sigmoid(proj(C_MO + i * N_TILE)) * _silu(proj(C_MZ + i * N_TILE))
        return piece

    def attn_gate(i):
        def piece():
            st["sz"][:, cols(i)] = _silu(proj(C_AZ + i * N_TILE))
        return piece

    def merge_gate(name, lo, i):
        def piece():
            st[name][:, cols(i)] = _sigmoid(proj(lo + i * N_TILE))
        return piece

    assert C_AV == C_AK + A_KV_WIDTH and 2 * A_KV_WIDTH == N_TILE and C_GF == C_GI + LANES
    half, full = range(M_WIDTH // N_TILE), range(D_MODEL // N_TILE)
    pieces = [norm, gates, attn_kv]
    for i in half:
        pieces += [queries(i), keys(i), values(i), attn_q(i)]
    for i in half:
        pieces += [out_gate(i), attn_gate(i)]
    for i in full:
        pieces += [merge_gate("sgm", C_GM, i), merge_gate("sga", C_GA, i)]
    return pieces


def _consume_pieces(st, first_tile, sinks_ref, gmn_ref, kbuf_s, vbuf_s, c_s, m_s, ym_s, ya_s, tile):
    L = M_CHUNK
    heads = [slice(hd * M_DK, (hd + 1) * M_DK) for hd in range(M_HEADS)]
    a_heads = [(kv, bi, half) for kv in range(A_KV_HEADS) for bi in range(2) for half in range(2)]

    def causal_mask():
        row = lax.broadcasted_iota(jnp.int32, (L, L), 0)
        col = lax.broadcasted_iota(jnp.int32, (L, L), 1)
        return row >= col

    def chunk_stages(j):
        rs = slice(j * L, (j + 1) * L)
        cx = {}

        def gates():
            g = st["g"][rs, :]
            ig_all = g[:, 0:LANES]
            lf_all = _log_sigmoid(g[:, LANES:2 * LANES])
            tril = jnp.where(causal_mask(), 1.0, 0.0).astype(BF16)
            b_all = _dot_exact01(tril, lf_all)
            b_last = b_all[L - 1:L, :]
            m_old = m_s[...]
            d_end = b_last - b_all + ig_all
            m_new = jnp.maximum(b_last + m_old, jnp.max(d_end, axis=0, keepdims=True))
            m_s[...] = m_new
            cx.update(b_all=b_all, w_end=jnp.exp(d_end - m_new), decay=jnp.exp(b_last + m_old - m_new),
                      inter=b_all + m_old, r_t=jnp.transpose(ig_all - b_all))

        def kv_buffers():
            srows = slice((j % 2) * L, (j % 2 + 1) * L)
            lo_half = lax.broadcasted_iota(jnp.int32, (L, L), 1) < A_HEAD_DIM
            kc = st["kf"][rs, :]
            vc = st["vf"][rs, :]
            kr = pltpu.roll(kc, A_HEAD_DIM, 1)
            vr = pltpu.roll(vc, A_HEAD_DIM, 1)
            kbuf_s[0, srows, :] = jnp.where(lo_half, kc, 0.0).astype(BF16)
            kbuf_s[1, srows, :] = jnp.where(lo_half, 0.0, kr).astype(BF16)
            kbuf_s[2, srows, :] = jnp.where(lo_half, kr, 0.0).astype(BF16)
            kbuf_s[3, srows, :] = jnp.where(lo_half, 0.0, kc).astype(BF16)
            lane = lax.broadcasted_iota(jnp.int32, (L, L), 1)
            ones_lo = jnp.where(lane == 0, 1.0, 0.0)
            ones_hi = jnp.where(lane == A_HEAD_DIM, 1.0, 0.0)
            vbuf_s[0, srows, :] = jnp.where(lo_half, vc, ones_hi).astype(BF16)
            vbuf_s[1, srows, :] = jnp.where(lo_half, ones_lo, vr).astype(BF16)
            vbuf_s[2, srows, :] = jnp.where(lo_half, vr, ones_hi).astype(BF16)
            vbuf_s[3, srows, :] = jnp.where(lo_half, ones_lo, vc).astype(BF16)

        def decay_matrices():
            cx["dmat"] = [jnp.where(causal_mask(), cx["b_all"][:, hd:hd + 1] + cx["r_t"][hd:hd + 1, :], -jnp.inf)
                          for hd in range(M_HEADS)]
            cx["m_hat"] = [jnp.maximum(cx["inter"][:, hd:hd + 1], jnp.max(cx["dmat"][hd], axis=-1, keepdims=True))
                           for hd in range(M_HEADS)]
            cx["kw"] = [(st["k"][rs, heads[hd]] * cx["w_end"][:, hd:hd + 1]).astype(BF16) for hd in range(M_HEADS)]

        def state_matmuls():
            cx["qk"], cx["qc"] = [], []
            for hd in range(M_HEADS):
                qh = st["q"][rs, heads[hd]]
                c_old = c_s[hd]
                cx["qk"].append(_dot_nt(qh, st["k"][rs, heads[hd]].astype(BF16)))
                cx["qc"].append(_dot(qh, c_old.astype(BF16)))
                va = st["vaug"][hd, rs, :]
                c_s[hd] = cx["decay"][:, hd:hd + 1] * c_old + _dot_tn(cx["kw"][hd], va)

        def attn_scores():
            row2 = lax.broadcasted_iota(jnp.int32, (L, 2 * L), 0)
            col2 = lax.broadcasted_iota(jnp.int32, (L, 2 * L), 1)
            no_prev = jnp.where(first_tile, 4 * L, 0) if j == 0 else 0
            if j % 2 == 0:
                mask = ((col2 < L) & (col2 <= row2)) | ((col2 >= L) & (col2 - L > row2 + no_prev))
            else:
                mask = ((col2 >= L) & (col2 - L <= row2)) | ((col2 < L) & (col2 > row2 + no_prev))
            cx["sh"] = {}
            for kv in range(A_KV_HEADS):
                qcat = jnp.concatenate([st["qa"][rs, (2 * kv) * LANES:(2 * kv + 1) * LANES],
                                        st["qa"][rs, (2 * kv + 1) * LANES:(2 * kv + 2) * LANES]], axis=0)
                for half in range(2):
                    sc = _dot_nt(qcat, kbuf_s[2 * kv + half])
                    for bi in range(2):
                        cx["sh"][kv, bi, half] = jnp.where(mask, sc[bi * L:(bi + 1) * L, :], -jnp.inf)

        def attn_max():
            cx["mx"] = {hk: jnp.maximum(jnp.max(cx["sh"][hk], axis=-1, keepdims=True),
                                        sinks_ref[A_GROUP * hk[0] + 2 * hk[1] + hk[2]]) for hk in a_heads}

        def stabilised_scores():
            cx["s"] = [(cx["qk"][hd] * jnp.exp(cx["dmat"][hd] - cx["m_hat"][hd])).astype(BF16)
                       for hd in range(M_HEADS)]

        def score_value_matmuls():
            cx["sv"] = [_dot(cx["s"][hd], st["vaug"][hd, rs, :]) for hd in range(M_HEADS)]

        def attn_exp():
            cx["p"] = {hk: jnp.exp(cx["sh"][hk] - cx["mx"][hk]).astype(BF16) for hk in a_heads}
            cx["sink"] = {hk: jnp.exp(sinks_ref[A_GROUP * hk[0] + 2 * hk[1] + hk[2]] - cx["mx"][hk]) for hk in a_heads}

        def attn_wait():
            pass

        def attn_value_matmuls():
            lo_half = lax.broadcasted_iota(jnp.int32, (L, L), 1) < A_HEAD_DIM
            for kv in range(A_KV_HEADS):
                for bi in range(2):
                    o_lo = _dot(cx["p"][kv, bi, 0], vbuf_s[2 * kv])
                    o_hi = _dot(cx["p"][kv, bi, 1], vbuf_s[2 * kv + 1])
                    den_lo = o_lo[:, A_HEAD_DIM:A_HEAD_DIM + 1] + cx["sink"][kv, bi, 0]
                    den_hi = o_hi[:, 0:1] + cx["sink"][kv, bi, 1]
                    o = jnp.where(lo_half, o_lo / den_lo, o_hi / den_hi)
                    cs = slice((2 * kv + bi) * LANES, (2 * kv + bi + 1) * LANES)
                    ya_s[rs, cs] = (o * st["sz"][rs, cs]).astype(BF16)

        def mlstm_outputs():
            hh = []
            for hd in range(M_HEADS):
                m_hat = cx["m_hat"][hd]
                w_inter = jnp.exp(cx["inter"][:, hd:hd + 1] - m_hat)
                tot = cx["sv"][hd] + w_inter * cx["qc"][hd]
                hh.append(tot[:, 0:M_DV] / jnp.maximum(jnp.abs(tot[:, M_DV:M_DV + 1]), jnp.exp(-m_hat)))
            ms = [jnp.mean(h * h, axis=-1, keepdims=True) for h in hh]
            for hd in range(M_HEADS):
                hs = heads[hd]
                ym_s[rs, hs] = (hh[hd] * lax.rsqrt(ms[hd] + NORM_EPS) * gmn_ref[:, hs] * st["og"][rs, hs]).astype(BF16)

        mlstm = [gates, decay_matrices, state_matmuls, stabilised_scores, score_value_matmuls, mlstm_outputs]
        attn = [kv_buffers, attn_scores, attn_max, attn_exp, attn_wait, attn_value_matmuls]
        return mlstm, attn

    assert tile == 2 * L
    (m0, a0), (m1, a1) = chunk_stages(0), chunk_stages(1)
    merged = []
    for m_stage, a_stage in zip(m0 + m1, a0 + a1):
        merged += [m_stage, a_stage]
    return merged


def _output_pieces(st, x_ref, rows, gate_ref, ym_s, ya_s, u_s, wmo_ref, wao_ref, wo_ref, gf_ref, y_ref):
    def merge():
        u = st["sgm"][...] * _dot(ym_s[...], wmo_ref[...]) + st["sga"][...] * _dot(ya_s[...], wao_ref[...])
        u_s[...] = u.astype(BF16)

    def out():
        gate = gate_ref[0][:, 2 * D_MODEL:3 * D_MODEL]
        r = _dot(u_s[...], wo_ref[...])
        y_ref[0, rows, :] = _rms(x_ref[0, rows, :] + gate * r) * gf_ref[...]

    return [merge, out]


def _interleave(primary, filler):
    done = 0
    for i, piece in enumerate(primary):
        piece()
        upto = ((i + 1) * len(filler)) // len(primary)
        for f in filler[done:upto]:
            f()
        done = upto


def _prompt_kernel(sinks_ref, x0_ref, xa_ref, xb_ref, xres_ref, mod0_ref, moda_ref, modb_ref,
                   gn_ref, gf_ref, gmn_ref, bg_ref, w_hbm, wmo_hbm, wao_hbm, wo_hbm,
                   ra_ref, rb_ref, rc_ref,
                   y_ref, c_out_ref, n_out_ref, m_out_ref, kk_ref, vk_ref, *scratch, tile, steps_per_seq, n_tiles):
    nset = len(_SET_FIELDS)
    set_a = dict(zip(_SET_FIELDS, scratch[0:nset]))
    set_b = dict(zip(_SET_FIELDS, scratch[nset:2 * nset]))
    (kbuf_s, vbuf_s, c_s, m_s, yma_s, yaa_s, ua_s, ymb_s, yab_s, ub_s,
     w_ref, wmo_ref, wao_ref, wo_ref, w_sems) = scratch[2 * nset:]
    k = pl.program_id(0)
    first = (k % steps_per_seq) == 0
    tiles_per_seq = 2 * steps_per_seq

    def project(x_ref, mod_ref, tile_index, st):
        rows = pl.ds(pl.multiple_of((tile_index % tiles_per_seq) * tile, tile), tile)
        rope = tuple(r.at[rows, :] for r in (ra_ref, rb_ref, rc_ref))
        return _project_pieces(x_ref, mod_ref, rope, st, gn_ref, bg_ref, w_ref)

    @pl.when(k == 0)
    def _prologue():
        copies = [pltpu.make_async_copy(src, dst, w_sems.at[i]) for i, (src, dst) in enumerate(
            ((w_hbm, w_ref), (wmo_hbm, wmo_ref), (wao_hbm, wao_ref), (wo_hbm, wo_ref)))]
        for copy in copies:
            copy.start()
        for copy in copies:
            copy.wait()
        lane = lax.broadcasted_iota(jnp.int32, (tile, LANES), 1)
        ones_col = jnp.where(lane == 0, 1.0, 0.0).astype(BF16)
        for st in (set_a, set_b):
            for hd in range(M_HEADS):
                st["vaug"][hd, :, LANES:2 * LANES] = ones_col
        for piece in project(x0_ref, mod0_ref, 0, set_a):
            piece()

    @pl.when(first)
    def _init():
        c_s[...] = jnp.zeros_like(c_s)
        m_s[...] = jnp.zeros_like(m_s)
        kbuf_s[...] = jnp.zeros_like(kbuf_s)
        vbuf_s[...] = jnp.zeros_like(vbuf_s)

    state = (sinks_ref, gmn_ref, kbuf_s, vbuf_s, c_s, m_s)
    out_w = (wmo_ref, wao_ref, wo_ref, gf_ref, y_ref)
    _interleave(_consume_pieces(set_a, first, *state, yma_s, yaa_s, tile),
                project(xa_ref, moda_ref, 2 * k + 1, set_b))
    _interleave(_consume_pieces(set_b, False, *state, ymb_s, yab_s, tile),
                _output_pieces(set_a, xres_ref, slice(0, tile), moda_ref, yma_s, yaa_s, ua_s, *out_w)
                + project(xb_ref, modb_ref, jnp.minimum(2 * k + 2, n_tiles - 1), set_a))
    for piece in _output_pieces(set_b, xres_ref, slice(tile, 2 * tile), moda_ref, ymb_s, yab_s, ub_s, *out_w):
        piece()

    @pl.when((k % steps_per_seq) == steps_per_seq - 1)
    def _final():
        for hd in range(M_HEADS):
            cf = c_s[hd]
            c_out_ref[0, hd] = cf[:, 0:M_DV]
            n_out_ref[0, hd:hd + 1, :] = jnp.transpose(cf[:, M_DV:2 * M_DV])[0:1, :]
        m_out_ref[0] = m_s[...]
        k_t = jnp.transpose(set_b["kf"][tile - WINDOW:tile, :])
        v_t = jnp.transpose(set_b["vf"][tile - WINDOW:tile, :])
        for kv in range(A_KV_HEADS):
            kk_ref[0, kv] = k_t[kv * A_HEAD_DIM:(kv + 1) * A_HEAD_DIM, :]
            vk_ref[0, kv] = v_t[kv * A_HEAD_DIM:(kv + 1) * A_HEAD_DIM, :]


def _rope_tables(first, count):
    half = ROT_DIM // 2
    dim = np.arange(LANES) % A_HEAD_DIM
    inv = ROPE_THETA ** (-(2.0 * (dim % half)) / ROT_DIM)
    ang = (first + np.arange(count, dtype=np.float64))[:, None] * inv[None, :]
    cos, sin = np.cos(ang), np.sin(ang)
    ra = np.where(dim < ROT_DIM, cos, 1.0)
    rb = np.where((dim >= half) & (dim < ROT_DIM), sin, 0.0)
    rc = np.where(dim < half, -sin, 0.0)
    return tuple(jnp.asarray(t, F32) for t in (ra, rb, rc))


def _full(shape):
    return pl.BlockSpec(shape, lambda *_: (0,) * len(shape))


def _prompt_layer(x, mod, sinks, gn, gf, gmn, bias_g, wp, wmo, wao, wo):
    bsz, seq, d = x.shape
    tile = PROMPT_TILE
    assert seq % (2 * tile) == 0 and tile % (2 * M_CHUNK) == 0 and d == D_MODEL
    tiles_per_seq = seq // tile
    steps_per_seq = tiles_per_seq // 2
    n_tiles = bsz * tiles_per_seq
    n_steps = n_tiles // 2
    ra, rb, rc = _rope_tables(0, seq)
    xt = x.reshape(n_tiles, tile, d)
    xp = x.reshape(n_steps, 2 * tile, d)
    mod3 = mod.reshape(bsz, 1, 3 * d)

    tile_0 = lambda k: 0
    tile_a = lambda k: 2 * k + 1
    tile_b = lambda k: jnp.minimum(2 * k + 2, n_tiles - 1)
    x_spec = lambda f: pl.BlockSpec((1, tile, d), lambda k: (f(k), 0, 0))
    mod_spec = lambda f: pl.BlockSpec((1, 1, 3 * d), lambda k: (f(k) // tiles_per_seq, 0, 0))
    in_specs = [
        pl.BlockSpec(memory_space=pltpu.SMEM),
        x_spec(tile_0), x_spec(tile_a), x_spec(tile_b),
        pl.BlockSpec((1, 2 * tile, d), lambda k: (k, 0, 0)),
        mod_spec(tile_0), mod_spec(tile_a), mod_spec(tile_b),
        _full((1, d)), _full((1, d)), _full((1, M_WIDTH)), _full((1, 2 * LANES)),
        pl.BlockSpec(memory_space=pl.ANY), pl.BlockSpec(memory_space=pl.ANY),
        pl.BlockSpec(memory_space=pl.ANY), pl.BlockSpec(memory_space=pl.ANY),
        _full((seq, LANES)), _full((seq, LANES)), _full((seq, LANES)),
    ]
    seq_of = lambda k: k // steps_per_seq
    out_specs = [
        pl.BlockSpec((1, 2 * tile, d), lambda k: (k, 0, 0)),
        pl.BlockSpec((1, M_HEADS, M_DK, M_DV), lambda k: (seq_of(k), 0, 0, 0)),
        pl.BlockSpec((1, M_HEADS, M_DK), lambda k: (seq_of(k), 0, 0)),
        pl.BlockSpec((1, 1, LANES), lambda k: (seq_of(k), 0, 0)),
        pl.BlockSpec((1, A_KV_HEADS, A_HEAD_DIM, WINDOW), lambda k: (seq_of(k), 0, 0, 0)),
        pl.BlockSpec((1, A_KV_HEADS, A_HEAD_DIM, WINDOW), lambda k: (seq_of(k), 0, 0, 0)),
    ]
    out_shape = [
        jax.ShapeDtypeStruct((n_steps, 2 * tile, d), F32),
        jax.ShapeDtypeStruct((bsz, M_HEADS, M_DK, M_DV), F32),
        jax.ShapeDtypeStruct((bsz, M_HEADS, M_DK), F32),
        jax.ShapeDtypeStruct((bsz, 1, LANES), F32),
        jax.ShapeDtypeStruct((bsz, A_KV_HEADS, A_HEAD_DIM, WINDOW), F32),
        jax.ShapeDtypeStruct((bsz, A_KV_HEADS, A_HEAD_DIM, WINDOW), F32),
    ]
    scratch = _set_shapes(tile) + _set_shapes(tile) + [
        pltpu.VMEM((2 * A_KV_HEADS, 2 * WINDOW, LANES), BF16),
        pltpu.VMEM((2 * A_KV_HEADS, 2 * WINDOW, LANES), BF16),
        pltpu.VMEM((M_HEADS, M_DK, 2 * LANES), F32),
        pltpu.VMEM((1, LANES), F32),
        pltpu.VMEM((tile, M_WIDTH), BF16),
        pltpu.VMEM((tile, A_WIDTH), BF16),
        pltpu.VMEM((tile, D_MODEL), BF16),
        pltpu.VMEM((tile, M_WIDTH), BF16),
        pltpu.VMEM((tile, A_WIDTH), BF16),
        pltpu.VMEM((tile, D_MODEL), BF16),
        pltpu.VMEM((d, N_PACK), BF16),
        pltpu.VMEM((M_WIDTH, d), BF16),
        pltpu.VMEM((A_WIDTH, d), BF16),
        pltpu.VMEM((d, d), BF16),
        pltpu.SemaphoreType.DMA((4,)),
    ]
    outs = pl.pallas_call(
        functools.partial(_prompt_kernel, tile=tile, steps_per_seq=steps_per_seq, n_tiles=n_tiles),
        grid=(n_steps,),
        in_specs=in_specs,
        out_specs=out_specs,
        out_shape=out_shape,
        scratch_shapes=scratch,
        compiler_params=pltpu.CompilerParams(dimension_semantics=("arbitrary",), vmem_limit_bytes=VMEM_LIMIT),
        name="prompt_layer",
    )(sinks, xt, xt, xt, xp, mod3, mod3, mod3, gn, gf, gmn, bias_g, wp, wmo, wao, wo,
      ra, rb, rc)
    return (outs[0].reshape(bsz, seq, d),) + tuple(outs[1:])


def _sample_proj_kernel(x_hbm, mod_ref, gn_ref, w_ref, z_ref, hb_s, x_s, x_sems):
    bsz, steps, _ = x_hbm.shape

    @pl.when(pl.program_id(0) == 0)
    def _norm():
        copies = [pltpu.make_async_copy(x_hbm.at[:, t, :], x_s.at[t], x_sems.at[t]) for t in range(steps)]
        for copy in copies:
            copy.start()
        mod = mod_ref[...]
        for t in range(steps):
            copies[t].wait()
            h = _rms(x_s[t]) * gn_ref[...]
            hb_s[t * bsz:(t + 1) * bsz, :] = (h * (1.0 + mod[:, D_MODEL:2 * D_MODEL]) + mod[:, 0:D_MODEL]).astype(BF16)

    z = _dot(hb_s[...], w_ref[...])
    for t in range(steps):
        z_ref[t] = z[t * bsz:(t + 1) * bsz, :]


def _sample_proj(x3, mod_s, gn, wp):
    bsz, steps, d = x3.shape
    tn = 1024
    return pl.pallas_call(
        _sample_proj_kernel,
        grid=(N_PACK // tn,),
        in_specs=[
            pl.BlockSpec(memory_space=pl.ANY),
            pl.BlockSpec((bsz, 2 * d), lambda j: (0, 0)),
            _full((1, d)),
            pl.BlockSpec((d, tn), lambda j: (0, j)),
        ],
        out_specs=pl.BlockSpec((steps, bsz, tn), lambda j: (0, 0, j)),
        out_shape=jax.ShapeDtypeStruct((steps, bsz, N_PACK), F32),
        scratch_shapes=[pltpu.VMEM((steps * bsz, d), BF16), pltpu.VMEM((steps, bsz, d), F32),
                        pltpu.SemaphoreType.DMA((steps,))],
        compiler_params=pltpu.CompilerParams(dimension_semantics=("arbitrary",)),
        name="sample_proj",
    )(x3, mod_s, gn, wp)


def _sample_state_kernel(sinks_ref, z_ref, c_ref, n_ref, m_ref, ck_ref, cv_ref, rope_ref, bg_ref, gmn_ref,
                         cn_ref, nn_ref, mn_ref, ym_ref, ya_ref, ko_ref, vo_ref, *, group, steps):
    T, G = steps, group
    toks = range(T)

    def z(t, lo, n):
        return z_ref[t, :, lo:lo + n]

    def rope_t(t, blk):
        return _rope(blk, rope_ref[0, t:t + 1, :], rope_ref[1, t:t + 1, :], rope_ref[2, t:t + 1, :])

    lane = lax.broadcasted_iota(jnp.int32, (G, LANES), 1)
    lo_half = lane < A_HEAD_DIM
    gmn = gmn_ref[...]

    q = [z(t, C_MQ, M_WIDTH) * (M_DK ** -0.5) for t in toks]
    k = [z(t, C_MK, M_WIDTH) for t in toks]
    v = [z(t, C_MV, M_WIDTH) for t in toks]
    og = [_sigmoid(z(t, C_MO, M_WIDTH)) * _silu(z(t, C_MZ, M_WIDTH)) for t in toks]
    qa = [[rope_t(t, z(t, C_AQ + j * LANES, LANES)) * (A_HEAD_DIM ** -0.5) for j in range(A_WIDTH // LANES)]
          for t in toks]
    kn = [rope_t(t, z(t, C_AK, A_KV_WIDTH)) for t in toks]
    vn = [z(t, C_AV, A_KV_WIDTH) for t in toks]
    sz = [_silu(z(t, C_AZ, A_WIDTH)) for t in toks]
    gt = [z(t, C_GI, 2 * LANES) + bg_ref[...] for t in toks]
    ig = [gt[t][:, 0:LANES] for t in toks]
    lf = [_log_sigmoid(gt[t][:, LANES:2 * LANES]) for t in toks]


    m_all = m_ref[...]
    rep = lambda a, h: jnp.broadcast_to(a[:, h:h + 1], (G, LANES))
    gates = []
    for hd in range(M_HEADS):
        ig_h = [rep(ig[t], hd) for t in toks]
        lf_h = [rep(lf[t], hd) for t in toks]
        m_old = rep(m_all, hd)
        bsum = [lf_h[0]]
        for t in range(1, T):
            bsum.append(bsum[-1] + lf_h[t])
        b_last = bsum[-1]
        d_end = [b_last - bsum[s] + ig_h[s] for s in toks]
        m_new = b_last + m_old
        for s in toks:
            m_new = jnp.maximum(m_new, d_end[s])
        mn_ref[:, hd:hd + 1] = m_new[:, 0:1]
        gd = dict(w_end=[jnp.exp(d_end[s] - m_new) for s in toks], decay=jnp.exp(b_last + m_old - m_new),
                  e=[], w_inter=[], emh=[])
        for t in toks:
            inter = bsum[t] + m_old
            dm = [bsum[t] - bsum[s] + ig_h[s] for s in range(t + 1)]
            m_hat = inter
            for s in range(t + 1):
                m_hat = jnp.maximum(m_hat, dm[s])
            gd["e"].append([jnp.exp(dm[s] - m_hat) for s in range(t + 1)])
            gd["w_inter"].append(jnp.exp(inter - m_hat))
            gd["emh"].append(jnp.exp(-m_hat))
        gates.append(gd)

    heads = [slice(hd * M_DK, (hd + 1) * M_DK) for hd in range(M_HEADS)]
    n_old = [n_ref[:, hd, :] for hd in range(M_HEADS)]
    qk = {(hd, t, s): jnp.sum(q[t][:, heads[hd]] * k[s][:, heads[hd]], axis=-1, keepdims=True)
          for hd in range(M_HEADS) for t in toks for s in range(t + 1)}
    qn = {(hd, t): jnp.sum(q[t][:, heads[hd]] * n_old[hd], axis=-1, keepdims=True)
          for hd in range(M_HEADS) for t in toks}

    lo_f = jnp.where(lo_half, 1.0, 0.0)
    hi_f = 1.0 - lo_f
    dup = lambda a, kv: (jnp.where(lo_half, a, pltpu.roll(a, A_HEAD_DIM, 1)) if kv == 0 else
                         jnp.where(lo_half, pltpu.roll(a, A_HEAD_DIM, 1), a))
    kn_x = [[dup(kn[s], kv) for s in toks] for kv in range(A_KV_HEADS)]
    vn_x = [[dup(vn[s], kv) for s in toks] for kv in range(A_KV_HEADS)]
    s_new = {}
    for kv in range(A_KV_HEADS):
        for blk in range(2):
            for t in toks:
                for s in range(t + 1):
                    prod = qa[t][2 * kv + blk] * kn_x[kv][s]
                    s_new[kv, 0, blk, t, s] = jnp.sum(prod * lo_f, axis=-1, keepdims=True)
                    s_new[kv, 1, blk, t, s] = jnp.sum(prod * hi_f, axis=-1, keepdims=True)

    rows_tg = lax.broadcasted_iota(jnp.int32, (T * G, LANES), 0) % G
    own_rows = [rows_tg == b for b in range(G)]
    rows64 = lax.broadcasted_iota(jnp.int32, (2 * T * G, LANES), 0) % G

    qc = []
    for hd in range(M_HEADS):
        hs = heads[hd]
        gd = gates[hd]
        q32 = jnp.concatenate([q[t][:, hs] for t in toks], axis=0)
        kw = [k[s][:, hs] * gd["w_end"][s] for s in toks]
        kw_t = jnp.transpose(jnp.concatenate(kw, axis=0)).astype(BF16)
        v32 = jnp.concatenate([v[s][:, hs] for s in toks], axis=0)
        acc = None
        for b in range(G):
            c_old = c_ref[b, hd]
            part = _dot(jnp.where(own_rows[b], q32, 0.0).astype(BF16), c_old.astype(BF16))
            acc = part if acc is None else acc + part
            upd = _dot(kw_t, jnp.where(own_rows[b], v32, 0.0).astype(BF16))
            cn_ref[b, hd] = gd["decay"][b:b + 1, 0:1] * c_old + upd
        qc.append(acc)
        nn_ref[:, hd, :] = gd["decay"] * n_old[hd] + (kw[0] + kw[1] + kw[2] + kw[3])

    zeros_kt = jnp.zeros((A_HEAD_DIM, WINDOW), BF16)
    sc = {}
    for kv in range(A_KV_HEADS):
        l64 = jnp.concatenate([qa[t][2 * kv + blk] for blk in range(2) for t in toks], axis=0)
        own64 = [jnp.where(rows64 == b, l64, 0.0).astype(BF16) for b in range(G)]
        kt = [ck_ref[b, kv].astype(BF16) for b in range(G)]
        for par in range(2):
            acc = None
            for b in range(G):
                rhs = jnp.concatenate([kt[b], zeros_kt] if par == 0 else [zeros_kt, kt[b]], axis=0)
                part = _dot(own64[b], rhs)
                acc = part if acc is None else acc + part
            sc[kv, par] = acc

    hh = {}
    for hd in range(M_HEADS):
        gd = gates[hd]
        for t in toks:
            sv, ssum = None, None
            for s in range(t + 1):
                s_ts = qk[hd, t, s] * gd["e"][t][s]
                sv = s_ts * v[s][:, heads[hd]] if sv is None else sv + s_ts * v[s][:, heads[hd]]
                ssum = s_ts if ssum is None else ssum + s_ts
            wi = gd["w_inter"][t]
            num = sv + wi * qc[hd][t * G:(t + 1) * G, :]
            den = ssum + wi * qn[hd, t]
            hh[hd, t] = num / jnp.maximum(jnp.abs(den), gd["emh"][t])
    ms = {key: jnp.mean(val * val, axis=-1, keepdims=True) for key, val in hh.items()}
    for (hd, t), val in hh.items():
        hs = heads[hd]
        ym_ref[t, :, hs] = val * lax.rsqrt(ms[hd, t] + NORM_EPS) * gmn[:, hs] * og[t][:, hs]

    items = [(kv, par, blk, t) for kv in range(A_KV_HEADS) for par in range(2) for blk in range(2) for t in toks]
    s_c = {it: jnp.where(lane > it[3], sc[it[0], it[1]][(it[2] * T + it[3]) * G:(it[2] * T + it[3] + 1) * G, :], -jnp.inf)
           for it in items}
    mx_c = {it: jnp.max(s_c[it], axis=-1, keepdims=True) for it in items}
    p_c, p_n, sink_t = {}, {}, {}
    for it in items:
        kv, par, blk, t = it
        sink = sinks_ref[A_GROUP * kv + 2 * blk + par]
        mx = jnp.maximum(mx_c[it], sink)
        for s in range(t + 1):
            mx = jnp.maximum(mx, s_new[kv, par, blk, t, s])
        p_c[it] = jnp.exp(s_c[it] - mx)
        p_n[it] = [jnp.exp(s_new[kv, par, blk, t, s] - mx) for s in range(t + 1)]
        sink_t[it] = jnp.exp(sink - mx)
    sum_c = {it: jnp.sum(p_c[it], axis=-1, keepdims=True) for it in items}
    fresh = {}
    probs = {(kv, par): [] for kv in range(A_KV_HEADS) for par in range(2)}
    for it in items:
        kv, par, blk, t = it
        den = sum_c[it] + sink_t[it]
        for s in range(t + 1):
            den = den + p_n[it][s]
        r = 1.0 / den
        probs[kv, par].append(p_c[it] * r)
        acc = None
        for s in range(t + 1):
            term = (p_n[it][s] * r) * vn_x[kv][s]
            acc = term if acc is None else acc + term
        fresh[it] = acc

    for kv in range(A_KV_HEADS):
        vt = [cv_ref[b, kv].astype(BF16) for b in range(G)]
        o = []
        for par in range(2):
            pstack = jnp.concatenate(probs[kv, par], axis=0)
            acc = None
            for b in range(G):
                rhs = jnp.concatenate([vt[b], zeros_kt] if par == 0 else [zeros_kt, vt[b]], axis=0)
                part = _dot_nt(jnp.where(rows64 == b, pstack, 0.0).astype(BF16), rhs)
                acc = part if acc is None else acc + part
            o.append(acc)
        for blk in range(2):
            cs = slice((2 * kv + blk) * LANES, (2 * kv + blk + 1) * LANES)
            for t in toks:
                rs = slice((blk * T + t) * G, (blk * T + t + 1) * G)
                new_v = jnp.where(lo_half, fresh[kv, 0, blk, t], fresh[kv, 1, blk, t])
                ya_ref[t, :, cs] = (o[0][rs, :] + o[1][rs, :] + new_v) * sz[t][:, cs]

    n_rows = T * G
    src = lax.broadcasted_iota(jnp.int32, (n_rows, n_rows), 1)
    dst = lax.broadcasted_iota(jnp.int32, (n_rows, n_rows), 0)
    perm = jnp.where(src == (dst % T) * G + dst // T, 1.0, 0.0).astype(BF16)
    lane_d = lax.broadcasted_iota(jnp.int32, (A_HEAD_DIM, WINDOW), 1)
    for new, cache_ref, out_ref in ((kn, ck_ref, ko_ref), (vn, cv_ref, vo_ref)):
        stack = _dot_exact01(perm, jnp.concatenate(new, axis=0))
        stack = jnp.concatenate([stack, jnp.zeros((LANES - n_rows, LANES), F32)], axis=0)
        new_t = jnp.transpose(stack)
        for b in range(G):
            placed = pltpu.roll(new_t, (WINDOW - T - T * b) % LANES, 1)
            for kv in range(A_KV_HEADS):
                slid = pltpu.roll(cache_ref[b, kv], WINDOW - T, 1)
                out_ref[b, kv] = jnp.where(lane_d >= WINDOW - T, placed[kv * A_HEAD_DIM:(kv + 1) * A_HEAD_DIM, :], slid)


def _sample_state(zs, state_c, state_n, state_m, ck_t, cv_t, rope, bias_g, gmn, sinks):
    steps, bsz, _ = zs.shape
    group = SAMPLE_GROUP
    assert bsz % group == 0
    in_specs = [
        pl.BlockSpec(memory_space=pltpu.SMEM),
        pl.BlockSpec((steps, group, N_PACK), lambda i: (0, i, 0)),
        pl.BlockSpec((group, M_HEADS, M_DK, M_DV), lambda i: (i, 0, 0, 0)),
        pl.BlockSpec((group, M_HEADS, M_DK), lambda i: (i, 0, 0)),
        pl.BlockSpec((group, M_HEADS), lambda i: (i, 0)),
        pl.BlockSpec((group, A_KV_HEADS, A_HEAD_DIM, WINDOW), lambda i: (i, 0, 0, 0)),
        pl.BlockSpec((group, A_KV_HEADS, A_HEAD_DIM, WINDOW), lambda i: (i, 0, 0, 0)),
        _full((3, steps, LANES)), _full((1, 2 * LANES)), _full((1, M_WIDTH)),
    ]
    out_specs = [
        pl.BlockSpec((group, M_HEADS, M_DK, M_DV), lambda i: (i, 0, 0, 0)),
        pl.BlockSpec((group, M_HEADS, M_DK), lambda i: (i, 0, 0)),
        pl.BlockSpec((group, M_HEADS), lambda i: (i, 0)),
        pl.BlockSpec((steps, group, M_WIDTH), lambda i: (0, i, 0)),
        pl.BlockSpec((steps, group, A_WIDTH), lambda i: (0, i, 0)),
        pl.BlockSpec((group, A_KV_HEADS, A_HEAD_DIM, WINDOW), lambda i: (i, 0, 0, 0)),
        pl.BlockSpec((group, A_KV_HEADS, A_HEAD_DIM, WINDOW), lambda i: (i, 0, 0, 0)),
    ]
    out_shape = [
        jax.ShapeDtypeStruct((bsz, M_HEADS, M_DK, M_DV), F32),
        jax.ShapeDtypeStruct((bsz, M_HEADS, M_DK), F32),
        jax.ShapeDtypeStruct((bsz, M_HEADS), F32),
        jax.ShapeDtypeStruct((steps, bsz, M_WIDTH), F32),
        jax.ShapeDtypeStruct((steps, bsz, A_WIDTH), F32),
        jax.ShapeDtypeStruct((bsz, A_KV_HEADS, A_HEAD_DIM, WINDOW), F32),
        jax.ShapeDtypeStruct((bsz, A_KV_HEADS, A_HEAD_DIM, WINDOW), F32),
    ]
    return pl.pallas_call(
        functools.partial(_sample_state_kernel, group=group, steps=steps),
        grid=(bsz // group,),
        in_specs=in_specs,
        out_specs=out_specs,
        out_shape=out_shape,
        compiler_params=pltpu.CompilerParams(dimension_semantics=("arbitrary",), vmem_limit_bytes=VMEM_LIMIT),
        name="sample_state",
    )(sinks, zs, state_c, state_n, state_m, ck_t, cv_t, rope, bias_g, gmn)


def _sample_out_kernel(x_ref, gate_ref, z_ref, ym_ref, ya_ref, wmo_ref, wao_ref, wo_ref, gf_ref, y_ref):
    bsz, steps, _ = x_ref.shape
    stack = lambda ref, cs: jnp.concatenate([ref[t, :, cs] for t in range(steps)], axis=0)
    full = slice(None)
    x = jnp.concatenate([x_ref[:, t, :] for t in range(steps)], axis=0)
    gate = jnp.concatenate([gate_ref[...]] * steps, axis=0)
    y = _out_stage(x, gate, _sigmoid(stack(z_ref, slice(0, D_MODEL))), _sigmoid(stack(z_ref, slice(D_MODEL, 2 * D_MODEL))),
                   stack(ym_ref, full).astype(BF16), stack(ya_ref, full).astype(BF16), wmo_ref, wao_ref, wo_ref, gf_ref[...])
    for t in range(steps):
        y_ref[:, t, :] = y[t * bsz:(t + 1) * bsz, :]


def _sample_out(x3, mod_s, zs, ym, ya, wmo, wao, wo, gf):
    bsz, steps, d = x3.shape
    return pl.pallas_call(
        _sample_out_kernel,
        grid=(1,),
        in_specs=[
            _full((bsz, steps, d)),
            pl.BlockSpec((bsz, d), lambda i: (0, 2)),
            pl.BlockSpec((steps, bsz, 2 * d), lambda i: (0, 0, 0)),
            _full((steps, bsz, M_WIDTH)), _full((steps, bsz, A_WIDTH)),
            _full((M_WIDTH, d)), _full((A_WIDTH, d)), _full((d, d)), _full((1, d)),
        ],
        out_specs=_full((bsz, steps, d)),
        out_shape=jax.ShapeDtypeStruct((bsz, steps, d), F32),
        compiler_params=pltpu.CompilerParams(dimension_semantics=("arbitrary",), vmem_limit_bytes=VMEM_LIMIT),
        name="sample_out",
    )(x3, mod_s, zs, ym, ya, wmo, wao, wo, gf)


PACK_BLOCK = 512


def _pack_sources():
    names = ("mq", "mk", "mv", "mi", "mf", "mo", "mz", "aq", "ak", "av", "az", "gm", "ga")
    start, pos = {}, 0
    for name, size in zip(names, IN_SIZES):
        start[name] = pos
        pos += size
    assert start["av"] == start["ak"] + A_KV_WIDTH and start["mf"] == start["mi"] + M_HEADS
    order = (("gm", C_GM, D_MODEL), ("ga", C_GA, D_MODEL), ("mq", C_MQ, M_WIDTH), ("mk", C_MK, M_WIDTH),
             ("mv", C_MV, M_WIDTH), ("mo", C_MO, M_WIDTH), ("mz", C_MZ, M_WIDTH), ("aq", C_AQ, A_WIDTH),
             ("az", C_AZ, A_WIDTH))
    src = []
    for name, col, width in order:
        assert col == len(src) * PACK_BLOCK and width % PACK_BLOCK == 0
        src += [start[name] + i * PACK_BLOCK for i in range(width // PACK_BLOCK)]
    assert C_AK == len(src) * PACK_BLOCK and C_AV == C_AK + A_KV_WIDTH and C_GI == C_AV + A_KV_WIDTH
    assert C_GF == C_GI + LANES and N_PACK == C_AK + PACK_BLOCK and start["ak"] + PACK_BLOCK <= pos
    return src + [start["ak"]], start["mi"]


def _pack_kernel(src_ref, wt_ref, gate_ref, o_ref):
    j = pl.program_id(0)
    last = pl.num_programs(0) - 1

    @pl.when(j < last)
    def _block():
        o_ref[...] = jnp.transpose(wt_ref[...]).astype(BF16)

    @pl.when(j == last)
    def _kv_and_gates():
        kv = 2 * A_KV_WIDTH
        o_ref[:, 0:kv] = jnp.transpose(wt_ref[0:kv, :]).astype(BF16)
        g = gate_ref[...]
        pad = jnp.zeros((LANES - M_HEADS, g.shape[1]), F32)
        rows = jnp.concatenate([g[0:M_HEADS], pad, g[M_HEADS:2 * M_HEADS], pad], axis=0)
        o_ref[:, kv:kv + 2 * LANES] = jnp.transpose(rows).astype(BF16)


def _pack_w_in(w_in):
    d = w_in.shape[0]
    src, gate_row = _pack_sources()
    wt = jnp.transpose(w_in)
    return pl.pallas_call(
        _pack_kernel,
        grid_spec=pltpu.PrefetchScalarGridSpec(
            num_scalar_prefetch=1, grid=(N_PACK // PACK_BLOCK,),
            in_specs=[pl.BlockSpec((pl.Element(PACK_BLOCK), pl.Element(d)),
                                   lambda j, src_ref: (pl.multiple_of(src_ref[j], 8), 0)),
                      pl.BlockSpec((pl.Element(2 * M_HEADS), pl.Element(d)), lambda j, src_ref: (gate_row, 0))],
            out_specs=pl.BlockSpec((d, PACK_BLOCK), lambda j, src_ref: (0, j))),
        out_shape=jax.ShapeDtypeStruct((d, N_PACK), BF16),
        compiler_params=pltpu.CompilerParams(dimension_semantics=("arbitrary",)),
        name="pack_w_in",
    )(jnp.asarray(src, jnp.int32), wt, wt)


def kernel(x_prompt, x_sample, state_C, state_n, state_m, cache_k, cache_v, c_prompt, c_sample, w_ada, b_ada, g_norm, w_in, b_igate, b_fgate, g_mnorm, sinks, w_m_out, w_a_out, w_out, g_final):
    assert w_in.shape[0] == 1, "single-layer step"
    bsz_s, steps, d = x_sample.shape
    assert cache_k.shape[2] == WINDOW

    mod_p, mod_s = _adaln(c_prompt, c_sample, w_ada[0], b_ada[0])
    wp = _pack_w_in(w_in[0])
    wmo = w_m_out[0].astype(BF16)
    wao = w_a_out[0].astype(BF16)
    wo = w_out[0].astype(BF16)
    gn = g_norm[0].reshape(1, d)
    gf = g_final.reshape(1, d)
    gmn = g_mnorm[0].reshape(1, M_WIDTH)
    zpad = jnp.zeros((LANES - M_HEADS,), F32)
    bias_g = jnp.concatenate([b_igate[0], zpad, b_fgate[0], zpad]).reshape(1, 2 * LANES)
    sk = sinks[0]

    y_p, c_p, n_p, m_p, k_p, v_p = _prompt_layer(x_prompt, mod_p, sk, gn, gf, gmn, bias_g, wp, wmo, wao, wo)
    m_p = m_p[:, 0, 0:M_HEADS]

    zs = _sample_proj(x_sample, mod_s, gn, wp)
    rope = jnp.stack(_rope_tables(PAST_LEN, steps))
    dims_keys = lambda c: jnp.transpose(c[0], (0, 2, 3, 1))
    c_s, n_s, m_s, ym, ya, k_t, v_t = _sample_state(zs, state_C[0], state_n[0], state_m[0], dims_keys(cache_k),
                                                    dims_keys(cache_v), rope, bias_g, gmn, sk)
    y_s = _sample_out(x_sample, mod_s, zs, ym, ya, wmo, wao, wo, gf)
    keys_dims = lambda c: jnp.transpose(c, (0, 3, 1, 2))[None]

    return (y_p, y_s, c_p[None], n_p[None], m_p[None], keys_dims(k_p), keys_dims(v_p),
            c_s[None], n_s[None], m_s[None], keys_dims(k_t), keys_dims(v_t))
```

```python
import functools

import jax
import jax.numpy as jnp
import numpy as np
from jax import lax
from jax.experimental import pallas as pl
from jax.experimental.pallas import tpu as pltpu

F32 = jnp.float32
BF16 = jnp.bfloat16

D_MODEL = 1024
M_HEADS = 4
M_DK = 128
M_DV = 128
M_WIDTH = M_HEADS * M_DV
M_CHUNK = 128
A_HEADS = 8
A_KV_HEADS = 2
A_GROUP = A_HEADS // A_KV_HEADS
A_HEAD_DIM = 64
A_WIDTH = A_HEADS * A_HEAD_DIM
A_KV_WIDTH = A_KV_HEADS * A_HEAD_DIM
WINDOW = 128
ROT_DIM = A_HEAD_DIM // 4
ROPE_THETA = 500000.0
NORM_EPS = 1e-6
PAST_LEN = 16384
IN_SIZES = (M_HEADS * M_DK, M_HEADS * M_DK, M_WIDTH, M_HEADS, M_HEADS, M_WIDTH, M_WIDTH,
            A_WIDTH, A_KV_WIDTH, A_KV_WIDTH, A_WIDTH, D_MODEL, D_MODEL)

LANES = 128

C_GM = 0
C_GA = C_GM + D_MODEL
C_MQ = C_GA + D_MODEL
C_MK = C_MQ + M_WIDTH
C_MV = C_MK + M_WIDTH
C_MO = C_MV + M_WIDTH
C_MZ = C_MO + M_WIDTH
C_AQ = C_MZ + M_WIDTH
C_AZ = C_AQ + A_WIDTH
C_AK = C_AZ + A_WIDTH
C_AV = C_AK + A_KV_WIDTH
C_GI = C_AV + A_KV_WIDTH
C_GF = C_GI + LANES
N_PACK = C_GF + LANES

PROMPT_TILE = 256
SAMPLE_GROUP = 16
VMEM_LIMIT = 56 * 1024 * 1024


def _sigmoid(x):
    return 1.0 / (1.0 + jnp.exp(-x))


def _silu(x):
    return x * _sigmoid(x)


def _log_sigmoid(x):
    return jnp.minimum(x, 0.0) - jnp.log1p(jnp.exp(-jnp.abs(x)))


def _dot(a, b):
    return jnp.dot(a, b, preferred_element_type=F32)


def _dot_nt(a, b):
    return lax.dot_general(a, b, (((1,), (1,)), ((), ())), preferred_element_type=F32)


def _dot_tn(a, b):
    return lax.dot_general(a, b, (((0,), (0,)), ((), ())), preferred_element_type=F32)


def _dot_exact01(m01, x):
    x1 = x.astype(BF16)
    r1 = x - x1.astype(F32)
    x2 = r1.astype(BF16)
    x3 = (r1 - x2.astype(F32)).astype(BF16)
    return _dot(m01, x1) + _dot(m01, x2) + _dot(m01, x3)


def _rms(x):
    return x * lax.rsqrt(jnp.mean(x * x, axis=-1, keepdims=True) + NORM_EPS)


def _rope(blk, ra, rb, rc):
    return blk * ra + pltpu.roll(blk, 8, 1) * rb + pltpu.roll(blk, LANES - 8, 1) * rc


def _out_stage(x, gate, sgm, sga, ym, ya, wmo_ref, wao_ref, wo_ref, gf):
    pm = _dot(ym, wmo_ref[...])
    pa = _dot(ya, wao_ref[...])
    u = sgm * pm + sga * pa
    r = _dot(u.astype(BF16), wo_ref[...])
    return _rms(x + gate * r) * gf


def _adaln_kernel(cp_ref, cs_ref, w_ref, b_ref, op_ref, os_ref):
    w = w_ref[...].astype(BF16)
    b = b_ref[...]
    op_ref[:, 0, :] = _dot(_silu(cp_ref[...]).astype(BF16), w) + b
    os_ref[...] = _dot(_silu(cs_ref[...]).astype(BF16), w) + b


def _adaln(c_p, c_s, w_ada, b_ada):
    bp, d = c_p.shape
    bs = c_s.shape[0]
    n = w_ada.shape[1]
    tn = 1024
    return pl.pallas_call(
        _adaln_kernel,
        grid=(n // tn,),
        in_specs=[
            pl.BlockSpec((bp, d), lambda j: (0, 0)),
            pl.BlockSpec((bs, d), lambda j: (0, 0)),
            pl.BlockSpec((d, tn), lambda j: (0, j)),
            pl.BlockSpec((1, tn), lambda j: (0, j)),
        ],
        out_specs=[
            pl.BlockSpec((bp, 1, tn), lambda j: (0, 0, j)),
            pl.BlockSpec((bs, tn), lambda j: (0, j)),
        ],
        out_shape=[jax.ShapeDtypeStruct((bp, 1, n), F32), jax.ShapeDtypeStruct((bs, n), F32)],
        compiler_params=pltpu.CompilerParams(dimension_semantics=("arbitrary",)),
        name="adaln",
    )(c_p, c_s, w_ada, b_ada.reshape(1, n))


_SET_FIELDS = ("hb", "q", "k", "vaug", "og", "qa", "kf", "vf", "sz", "sgm", "sga", "g")


def _set_shapes(tile):
    return [
        pltpu.VMEM((tile, D_MODEL), BF16),
        pltpu.VMEM((tile, M_WIDTH), BF16),
        pltpu.VMEM((tile, M_WIDTH), F32),
        pltpu.VMEM((M_HEADS, tile, 2 * LANES), BF16),
        pltpu.VMEM((tile, M_WIDTH), F32),
        pltpu.VMEM((tile, A_WIDTH), BF16),
        pltpu.VMEM((tile, A_KV_WIDTH), F32),
        pltpu.VMEM((tile, A_KV_WIDTH), F32),
        pltpu.VMEM((tile, A_WIDTH), F32),
        pltpu.VMEM((tile, D_MODEL), F32),
        pltpu.VMEM((tile, D_MODEL), F32),
        pltpu.VMEM((tile, 2 * LANES), F32),
    ]


N_TILE = 256


def _project_pieces(x_ref, mod_ref, rope_refs, st, gn_ref, bg_ref, w_ref):
    def norm():
        mod = mod_ref[...]
        h = _rms(x_ref[0]) * gn_ref[...]
        st["hb"][...] = (h * (1.0 + mod[:, D_MODEL:2 * D_MODEL]) + mod[:, 0:D_MODEL]).astype(BF16)

    def proj(lo):
        return _dot(st["hb"][...], w_ref[:, lo:lo + N_TILE])

    def rope(v):
        ra_ref, rb_ref, rc_ref = rope_refs
        return _rope(v, ra_ref[...], rb_ref[...], rc_ref[...])

    def cols(i):
        return slice(i * N_TILE, (i + 1) * N_TILE)

    def gates():
        st["g"][...] = proj(C_GI) + bg_ref[...]

    def queries(i):
        def piece():
            st["q"][:, cols(i)] = (proj(C_MQ + i * N_TILE) * (M_DK ** -0.5)).astype(BF16)
        return piece

    def keys(i):
        def piece():
            st["k"][:, cols(i)] = proj(C_MK + i * N_TILE)
        return piece

    def values(i):
        def piece():
            zv = proj(C_MV + i * N_TILE)
            for j in range(N_TILE // M_DV):
                st["vaug"][i * (N_TILE // M_DV) + j, :, 0:LANES] = zv[:, j * M_DV:(j + 1) * M_DV].astype(BF16)
        return piece

    def attn_kv():
        z = proj(C_AK)
        st["kf"][...] = rope(z[:, 0:A_KV_WIDTH])
        st["vf"][...] = z[:, A_KV_WIDTH:2 * A_KV_WIDTH]

    def attn_q(i):
        def piece():
            za = proj(C_AQ + i * N_TILE)
            for j in range(N_TILE // LANES):
                blk = rope(za[:, j * LANES:(j + 1) * LANES])
                lo = i * N_TILE + j * LANES
                st["qa"][:, lo:lo + LANES] = (blk * (A_HEAD_DIM ** -0.5)).astype(BF16)
        return piece

    def out_gate(i):
        def piece():
            st["og"][:, cols(i)] = _sigmoid(proj(C_MO + i * N_TILE)) * _silu(proj(C_MZ + i * N_TILE))
        return piece

    def attn_gate(i):
        def piece():
            st["sz"][:, cols(i)] = _silu(proj(C_AZ + i * N_TILE))
        return piece

    def merge_gate(name, lo, i):
        def piece():
            st[name][:, cols(i)] = _sigmoid(proj(lo + i * N_TILE))
        return piece

    assert C_AV == C_AK + A_KV_WIDTH and 2 * A_KV_WIDTH == N_TILE and C_GF == C_GI + LANES
    half, full = range(M_WIDTH // N_TILE), range(D_MODEL // N_TILE)
    pieces = [norm, gates, attn_kv]
    for i in half:
        pieces += [queries(i), keys(i), values(i), attn_q(i)]
    for i in half:
        pieces += [out_gate(i), attn_gate(i)]
    for i in full:
        pieces += [merge_gate("sgm", C_GM, i), merge_gate("sga", C_GA, i)]
    return pieces


def _consume_pieces(st, first_tile, sinks_ref, gmn_ref, kbuf_s, vbuf_s, c_s, m_s, ym_s, ya_s, tile):
    L = M_CHUNK
    heads = [slice(hd * M_DK, (hd + 1) * M_DK) for hd in range(M_HEADS)]
    a_heads = [(kv, bi, half) for kv in range(A_KV_HEADS) for bi in range(2) for half in range(2)]

    def causal_mask():
        row = lax.broadcasted_iota(jnp.int32, (L, L), 0)
        col = lax.broadcasted_iota(jnp.int32, (L, L), 1)
        return row >= col

    def chunk_stages(j):
        rs = slice(j * L, (j + 1) * L)
        cx = {}

        def gates():
            g = st["g"][rs, :]
            ig_all = g[:, 0:LANES]
            lf_all = _log_sigmoid(g[:, LANES:2 * LANES])
            tril = jnp.where(causal_mask(), 1.0, 0.0).astype(BF16)
            b_all = _dot_exact01(tril, lf_all)
            b_last = b_all[L - 1:L, :]
            m_old = m_s[...]
            d_end = b_last - b_all + ig_all
            m_new = jnp.maximum(b_last + m_old, jnp.max(d_end, axis=0, keepdims=True))
            m_s[...] = m_new
            cx.update(b_all=b_all, w_end=jnp.exp(d_end - m_new), decay=jnp.exp(b_last + m_old - m_new),
                      inter=b_all + m_old, r_t=jnp.transpose(ig_all - b_all))

        def kv_buffers():
            srows = slice((j % 2) * L, (j % 2 + 1) * L)
            lo_half = lax.broadcasted_iota(jnp.int32, (L, L), 1) < A_HEAD_DIM
            kc = st["kf"][rs, :]
            vc = st["vf"][rs, :]
            kr = pltpu.roll(kc, A_HEAD_DIM, 1)
            vr = pltpu.roll(vc, A_HEAD_DIM, 1)
            kbuf_s[0, srows, :] = jnp.where(lo_half, kc, 0.0).astype(BF16)
            kbuf_s[1, srows, :] = jnp.where(lo_half, 0.0, kr).astype(BF16)
            kbuf_s[2, srows, :] = jnp.where(lo_half, kr, 0.0).astype(BF16)
            kbuf_s[3, srows, :] = jnp.where(lo_half, 0.0, kc).astype(BF16)
            lane = lax.broadcasted_iota(jnp.int32, (L, L), 1)
            ones_lo = jnp.where(lane == 0, 1.0, 0.0)
            ones_hi = jnp.where(lane == A_HEAD_DIM, 1.0, 0.0)
            vbuf_s[0, srows, :] = jnp.where(lo_half, vc, ones_hi).astype(BF16)
            vbuf_s[1, srows, :] = jnp.where(lo_half, ones_lo, vr).astype(BF16)
            vbuf_s[2, srows, :] = jnp.where(lo_half, vr, ones_hi).astype(BF16)
            vbuf_s[3, srows, :] = jnp.where(lo_half, ones_lo, vc).astype(BF16)

        def decay_matrices():
            cx["dmat"] = [jnp.where(causal_mask(), cx["b_all"][:, hd:hd + 1] + cx["r_t"][hd:hd + 1, :], -jnp.inf)
                          for hd in range(M_HEADS)]
            cx["m_hat"] = [jnp.maximum(cx["inter"][:, hd:hd + 1], jnp.max(cx["dmat"][hd], axis=-1, keepdims=True))
                           for hd in range(M_HEADS)]
            cx["kw"] = [(st["k"][rs, heads[hd]] * cx["w_end"][:, hd:hd + 1]).astype(BF16) for hd in range(M_HEADS)]

        def state_matmuls():
            cx["qk"], cx["qc"] = [], []
            for hd in range(M_HEADS):
                qh = st["q"][rs, heads[hd]]
                c_old = c_s[hd]
                cx["qk"].append(_dot_nt(qh, st["k"][rs, heads[hd]].astype(BF16)))
                cx["qc"].append(_dot(qh, c_old.astype(BF16)))
                va = st["vaug"][hd, rs, :]
                c_s[hd] = cx["decay"][:, hd:hd + 1] * c_old + _dot_tn(cx["kw"][hd], va)

        def attn_scores():
            row2 = lax.broadcasted_iota(jnp.int32, (L, 2 * L), 0)
            col2 = lax.broadcasted_iota(jnp.int32, (L, 2 * L), 1)
            no_prev = jnp.where(first_tile, 4 * L, 0) if j == 0 else 0
            if j % 2 == 0:
                mask = ((col2 < L) & (col2 <= row2)) | ((col2 >= L) & (col2 - L > row2 + no_prev))
            else:
                mask = ((col2 >= L) & (col2 - L <= row2)) | ((col2 < L) & (col2 > row2 + no_prev))
            cx["sh"] = {}
            for kv in range(A_KV_HEADS):
                qcat = jnp.concatenate([st["qa"][rs, (2 * kv) * LANES:(2 * kv + 1) * LANES],
                                        st["qa"][rs, (2 * kv + 1) * LANES:(2 * kv + 2) * LANES]], axis=0)
                for half in range(2):
                    sc = _dot_nt(qcat, kbuf_s[2 * kv + half])
                    for bi in range(2):
                        cx["sh"][kv, bi, half] = jnp.where(mask, sc[bi * L:(bi + 1) * L, :], -jnp.inf)

        def attn_max():
            cx["mx"] = {hk: jnp.maximum(jnp.max(cx["sh"][hk], axis=-1, keepdims=True),
                                        sinks_ref[A_GROUP * hk[0] + 2 * hk[1] + hk[2]]) for hk in a_heads}

        def stabilised_scores():
            cx["s"] = [(cx["qk"][hd] * jnp.exp(cx["dmat"][hd] - cx["m_hat"][hd])).astype(BF16)
                       for hd in range(M_HEADS)]

        def score_value_matmuls():
            cx["sv"] = [_dot(cx["s"][hd], st["vaug"][hd, rs, :]) for hd in range(M_HEADS)]

        def attn_exp():
            cx["p"] = {hk: jnp.exp(cx["sh"][hk] - cx["mx"][hk]).astype(BF16) for hk in a_heads}
            cx["sink"] = {hk: jnp.exp(sinks_ref[A_GROUP * hk[0] + 2 * hk[1] + hk[2]] - cx["mx"][hk]) for hk in a_heads}

        def attn_wait():
            pass

        def attn_value_matmuls():
            lo_half = lax.broadcasted_iota(jnp.int32, (L, L), 1) < A_HEAD_DIM
            for kv in range(A_KV_HEADS):
                for bi in range(2):
                    o_lo = _dot(cx["p"][kv, bi, 0], vbuf_s[2 * kv])
                    o_hi = _dot(cx["p"][kv, bi, 1], vbuf_s[2 * kv + 1])
                    den_lo = o_lo[:, A_HEAD_DIM:A_HEAD_DIM + 1] + cx["sink"][kv, bi, 0]
                    den_hi = o_hi[:, 0:1] + cx["sink"][kv, bi, 1]
                    o = jnp.where(lo_half, o_lo / den_lo, o_hi / den_hi)
                    cs = slice((2 * kv + bi) * LANES, (2 * kv + bi + 1) * LANES)
                    ya_s[rs, cs] = (o * st["sz"][rs, cs]).astype(BF16)

        def mlstm_outputs():
            hh = []
            for hd in range(M_HEADS):
                m_hat = cx["m_hat"][hd]
                w_inter = jnp.exp(cx["inter"][:, hd:hd + 1] - m_hat)
                tot = cx["sv"][hd] + w_inter * cx["qc"][hd]
                hh.append(tot[:, 0:M_DV] / jnp.maximum(jnp.abs(tot[:, M_DV:M_DV + 1]), jnp.exp(-m_hat)))
            ms = [jnp.mean(h * h, axis=-1, keepdims=True) for h in hh]
            for hd in range(M_HEADS):
                hs = heads[hd]
                ym_s[rs, hs] = (hh[hd] * lax.rsqrt(ms[hd] + NORM_EPS) * gmn_ref[:, hs] * st["og"][rs, hs]).astype(BF16)

        mlstm = [gates, decay_matrices, state_matmuls, stabilised_scores, score_value_matmuls, mlstm_outputs]
        attn = [kv_buffers, attn_scores, attn_max, attn_exp, attn_wait, attn_value_matmuls]
        return mlstm, attn

    assert tile == 2 * L
    (m0, a0), (m1, a1) = chunk_stages(0), chunk_stages(1)
    merged = []
    for m_stage, a_stage in zip(m0 + m1, a0 + a1):
        merged += [m_stage, a_stage]
    return merged


def _output_pieces(st, x_ref, rows, gate_ref, ym_s, ya_s, u_s, wmo_ref, wao_ref, wo_ref, gf_ref, y_ref):
    def merge(i):
        def piece():
            cs = slice(i * N_TILE, (i + 1) * N_TILE)
            u = st["sgm"][:, cs] * _dot(ym_s[...], wmo_ref[:, cs]) + st["sga"][:, cs] * _dot(ya_s[...], wao_ref[:, cs])
            u_s[:, cs] = u.astype(BF16)
        return piece

    def out():
        gate = gate_ref[:, 2 * D_MODEL:3 * D_MODEL]
        r = _dot(u_s[...], wo_ref[...])
        y_ref[0, rows, :] = _rms(x_ref[0, rows, :] + gate * r) * gf_ref[...]

    return [merge(i) for i in range(D_MODEL // N_TILE)] + [out]


def _interleave(primary, filler):
    done = 0
    for i, piece in enumerate(primary):
        piece()
        upto = ((i + 1) * len(filler)) // len(primary)
        for f in filler[done:upto]:
            f()
        done = upto


def _prompt_kernel(sinks_ref, x0_ref, xa_ref, xb_ref, xres_ref, mod_ref,
                   gn_ref, gf_ref, gmn_ref, bg_ref, w_hbm, wmo_hbm, wao_hbm, wo_hbm,
                   ra_ref, rb_ref, rc_ref,
                   y_ref, c_out_ref, n_out_ref, m_out_ref, kk_ref, vk_ref, *scratch, tile, steps_per_seq, n_tiles):
    nset = len(_SET_FIELDS)
    set_a = dict(zip(_SET_FIELDS, scratch[0:nset]))
    set_b = dict(zip(_SET_FIELDS, scratch[nset:2 * nset]))
    (kbuf_s, vbuf_s, c_s, m_s, yma_s, yaa_s, ua_s, ymb_s, yab_s, ub_s,
     w_ref, wmo_ref, wao_ref, wo_ref, w_sems) = scratch[2 * nset:]
    k = pl.program_id(0)
    first = (k % steps_per_seq) == 0
    tiles_per_seq = 2 * steps_per_seq

    def project(x_ref, tile_index, st):
        rows = pl.ds(pl.multiple_of((tile_index % tiles_per_seq) * tile, tile), tile)
        rope = tuple(r.at[rows, :] for r in (ra_ref, rb_ref, rc_ref))
        return _project_pieces(x_ref, mod_ref.at[tile_index // tiles_per_seq], rope, st, gn_ref, bg_ref, w_ref)

    @pl.when(k == 0)
    def _prologue():
        copies = [pltpu.make_async_copy(src, dst, w_sems.at[i]) for i, (src, dst) in enumerate(
            ((w_hbm, w_ref), (wmo_hbm, wmo_ref), (wao_hbm, wao_ref), (wo_hbm, wo_ref)))]
        for copy in copies:
            copy.start()
        for copy in copies:
            copy.wait()
        lane = lax.broadcasted_iota(jnp.int32, (tile, LANES), 1)
        ones_col = jnp.where(lane == 0, 1.0, 0.0).astype(BF16)
        for st in (set_a, set_b):
            for hd in range(M_HEADS):
                st["vaug"][hd, :, LANES:2 * LANES] = ones_col
        for piece in project(x0_ref, 0, set_a):
            piece()

    @pl.when(first)
    def _init():
        c_s[...] = jnp.zeros_like(c_s)
        m_s[...] = jnp.zeros_like(m_s)
        kbuf_s[...] = jnp.zeros_like(kbuf_s)
        vbuf_s[...] = jnp.zeros_like(vbuf_s)

    state = (sinks_ref, gmn_ref, kbuf_s, vbuf_s, c_s, m_s)
    out_w = (wmo_ref, wao_ref, wo_ref, gf_ref, y_ref)
    mod_k = mod_ref.at[k // steps_per_seq]
    _interleave(_consume_pieces(set_a, first, *state, yma_s, yaa_s, tile),
                project(xa_ref, 2 * k + 1, set_b))
    _interleave(_consume_pieces(set_b, False, *state, ymb_s, yab_s, tile),
                _output_pieces(set_a, xres_ref, slice(0, tile), mod_k, yma_s, yaa_s, ua_s, *out_w)
                + project(xb_ref, jnp.minimum(2 * k + 2, n_tiles - 1), set_a))
    for piece in _output_pieces(set_b, xres_ref, slice(tile, 2 * tile), mod_k, ymb_s, yab_s, ub_s, *out_w):
        piece()

    @pl.when((k % steps_per_seq) == steps_per_seq - 1)
    def _final():
        for hd in range(M_HEADS):
            cf = c_s[hd]
            c_out_ref[0, hd] = cf[:, 0:M_DV]
            n_out_ref[0, hd:hd + 1, :] = jnp.transpose(cf[:, M_DV:2 * M_DV])[0:1, :]
        m_out_ref[0] = m_s[...]
        k_t = jnp.transpose(set_b["kf"][tile - WINDOW:tile, :])
        v_t = jnp.transpose(set_b["vf"][tile - WINDOW:tile, :])
        for kv in range(A_KV_HEADS):
            kk_ref[0, kv] = k_t[kv * A_HEAD_DIM:(kv + 1) * A_HEAD_DIM, :]
            vk_ref[0, kv] = v_t[kv * A_HEAD_DIM:(kv + 1) * A_HEAD_DIM, :]


def _rope_tables(first, count):
    half = ROT_DIM // 2
    dim = np.arange(LANES) % A_HEAD_DIM
    inv = ROPE_THETA ** (-(2.0 * (dim % half)) / ROT_DIM)
    ang = (first + np.arange(count, dtype=np.float64))[:, None] * inv[None, :]
    cos, sin = np.cos(ang), np.sin(ang)
    ra = np.where(dim < ROT_DIM, cos, 1.0)
    rb = np.where((dim >= half) & (dim < ROT_DIM), sin, 0.0)
    rc = np.where(dim < half, -sin, 0.0)
    return tuple(jnp.asarray(t, F32) for t in (ra, rb, rc))


def _full(shape):
    return pl.BlockSpec(shape, lambda *_: (0,) * len(shape))


def _prompt_layer(x, mod, sinks, gn, gf, gmn, bias_g, wp, wmo, wao, wo):
    bsz, seq, d = x.shape
    tile = PROMPT_TILE
    assert seq % (2 * tile) == 0 and tile % (2 * M_CHUNK) == 0 and d == D_MODEL
    tiles_per_seq = seq // tile
    steps_per_seq = tiles_per_seq // 2
    n_tiles = bsz * tiles_per_seq
    n_steps = n_tiles // 2
    ra, rb, rc = _rope_tables(0, seq)
    xt = x.reshape(n_tiles, tile, d)
    xp = x.reshape(n_steps, 2 * tile, d)

    tile_0 = lambda k: 0
    tile_a = lambda k: 2 * k + 1
    tile_b = lambda k: jnp.minimum(2 * k + 2, n_tiles - 1)
    x_spec = lambda f: pl.BlockSpec((1, tile, d), lambda k: (f(k), 0, 0))
    in_specs = [
        pl.BlockSpec(memory_space=pltpu.SMEM),
        x_spec(tile_0), x_spec(tile_a), x_spec(tile_b),
        pl.BlockSpec((1, 2 * tile, d), lambda k: (k, 0, 0)),
        _full((bsz, 1, 3 * d)),
        _full((1, d)), _full((1, d)), _full((1, M_WIDTH)), _full((1, 2 * LANES)),
        pl.BlockSpec(memory_space=pl.ANY), pl.BlockSpec(memory_space=pl.ANY),
        pl.BlockSpec(memory_space=pl.ANY), pl.BlockSpec(memory_space=pl.ANY),
        _full((seq, LANES)), _full((seq, LANES)), _full((seq, LANES)),
    ]
    seq_of = lambda k: k // steps_per_seq
    out_specs = [
        pl.BlockSpec((1, 2 * tile, d), lambda k: (k, 0, 0)),
        pl.BlockSpec((1, M_HEADS, M_DK, M_DV), lambda k: (seq_of(k), 0, 0, 0)),
        pl.BlockSpec((1, M_HEADS, M_DK), lambda k: (seq_of(k), 0, 0)),
        pl.BlockSpec((1, 1, LANES), lambda k: (seq_of(k), 0, 0)),
        pl.BlockSpec((1, A_KV_HEADS, A_HEAD_DIM, WINDOW), lambda k: (seq_of(k), 0, 0, 0)),
        pl.BlockSpec((1, A_KV_HEADS, A_HEAD_DIM, WINDOW), lambda k: (seq_of(k), 0, 0, 0)),
    ]
    out_shape = [
        jax.ShapeDtypeStruct((n_steps, 2 * tile, d), F32),
        jax.ShapeDtypeStruct((bsz, M_HEADS, M_DK, M_DV), F32),
        jax.ShapeDtypeStruct((bsz, M_HEADS, M_DK), F32),
        jax.ShapeDtypeStruct((bsz, 1, LANES), F32),
        jax.ShapeDtypeStruct((bsz, A_KV_HEADS, A_HEAD_DIM, WINDOW), F32),
        jax.ShapeDtypeStruct((bsz, A_KV_HEADS, A_HEAD_DIM, WINDOW), F32),
    ]
    scratch = _set_shapes(tile) + _set_shapes(tile) + [
        pltpu.VMEM((2 * A_KV_HEADS, 2 * WINDOW, LANES), BF16),
        pltpu.VMEM((2 * A_KV_HEADS, 2 * WINDOW, LANES), BF16),
        pltpu.VMEM((M_HEADS, M_DK, 2 * LANES), F32),
        pltpu.VMEM((1, LANES), F32),
        pltpu.VMEM((tile, M_WIDTH), BF16),
        pltpu.VMEM((tile, A_WIDTH), BF16),
        pltpu.VMEM((tile, D_MODEL), BF16),
        pltpu.VMEM((tile, M_WIDTH), BF16),
        pltpu.VMEM((tile, A_WIDTH), BF16),
        pltpu.VMEM((tile, D_MODEL), BF16),
        pltpu.VMEM((d, N_PACK), BF16),
        pltpu.VMEM((M_WIDTH, d), BF16),
        pltpu.VMEM((A_WIDTH, d), BF16),
        pltpu.VMEM((d, d), BF16),
        pltpu.SemaphoreType.DMA((4,)),
    ]
    outs = pl.pallas_call(
        functools.partial(_prompt_kernel, tile=tile, steps_per_seq=steps_per_seq, n_tiles=n_tiles),
        grid=(n_steps,),
        in_specs=in_specs,
        out_specs=out_specs,
        out_shape=out_shape,
        scratch_shapes=scratch,
        compiler_params=pltpu.CompilerParams(dimension_semantics=("arbitrary",), vmem_limit_bytes=VMEM_LIMIT),
        name="prompt_layer",
    )(sinks, xt, xt, xt, xp, mod, gn, gf, gmn, bias_g, wp, wmo, wao, wo,
      ra, rb, rc)
    return (outs[0].reshape(bsz, seq, d),) + tuple(outs[1:])


def _sample_proj_kernel(x_hbm, mod_ref, gn_ref, w_ref, z_ref, hb_s, x_s, x_sems):
    bsz, steps, _ = x_hbm.shape

    @pl.when(pl.program_id(0) == 0)
    def _norm():
        copies = [pltpu.make_async_copy(x_hbm.at[:, t, :], x_s.at[t], x_sems.at[t]) for t in range(steps)]
        for copy in copies:
            copy.start()
        mod = mod_ref[...]
        for t in range(steps):
            copies[t].wait()
            h = _rms(x_s[t]) * gn_ref[...]
            hb_s[t * bsz:(t + 1) * bsz, :] = (h * (1.0 + mod[:, D_MODEL:2 * D_MODEL]) + mod[:, 0:D_MODEL]).astype(BF16)

    z = _dot(hb_s[...], w_ref[...])
    for t in range(steps):
        z_ref[t] = z[t * bsz:(t + 1) * bsz, :]


def _sample_proj(x3, mod_s, gn, wp):
    bsz, steps, d = x3.shape
    tn = 1024
    return pl.pallas_call(
        _sample_proj_kernel,
        grid=(N_PACK // tn,),
        in_specs=[
            pl.BlockSpec(memory_space=pl.ANY),
            pl.BlockSpec((bsz, 2 * d), lambda j: (0, 0)),
            _full((1, d)),
            pl.BlockSpec((d, tn), lambda j: (0, j)),
        ],
        out_specs=pl.BlockSpec((steps, bsz, tn), lambda j: (0, 0, j)),
        out_shape=jax.ShapeDtypeStruct((steps, bsz, N_PACK), F32),
        scratch_shapes=[pltpu.VMEM((steps * bsz, d), BF16), pltpu.VMEM((steps, bsz, d), F32),
                        pltpu.SemaphoreType.DMA((steps,))],
        compiler_params=pltpu.CompilerParams(dimension_semantics=("arbitrary",)),
        name="sample_proj",
    )(x3, mod_s, gn, wp)


def _sample_state_kernel(sinks_ref, z_ref, c_ref, n_ref, m_ref, ck_ref, cv_ref, rope_ref, bg_ref, gmn_ref,
                         cn_ref, nn_ref, mn_ref, ym_ref, ya_ref, ko_ref, vo_ref, *, group, steps):
    T, G = steps, group
    toks = range(T)

    def z(t, lo, n):
        return z_ref[t, :, lo:lo + n]

    def rope_t(t, blk):
        return _rope(blk, rope_ref[0, t:t + 1, :], rope_ref[1, t:t + 1, :], rope_ref[2, t:t + 1, :])

    lane = lax.broadcasted_iota(jnp.int32, (G, LANES), 1)
    lo_half = lane < A_HEAD_DIM
    gmn = gmn_ref[...]

    q = [z(t, C_MQ, M_WIDTH) * (M_DK ** -0.5) for t in toks]
    k = [z(t, C_MK, M_WIDTH) for t in toks]
    v = [z(t, C_MV, M_WIDTH) for t in toks]
    og = [_sigmoid(z(t, C_MO, M_WIDTH)) * _silu(z(t, C_MZ, M_WIDTH)) for t in toks]
    qa = [[rope_t(t, z(t, C_AQ + j * LANES, LANES)) * (A_HEAD_DIM ** -0.5) for j in range(A_WIDTH // LANES)]
          for t in toks]
    kn = [rope_t(t, z(t, C_AK, A_KV_WIDTH)) for t in toks]
    vn = [z(t, C_AV, A_KV_WIDTH) for t in toks]
    sz = [_silu(z(t, C_AZ, A_WIDTH)) for t in toks]
    gt = [z(t, C_GI, 2 * LANES) + bg_ref[...] for t in toks]
    ig = [gt[t][:, 0:LANES] for t in toks]
    lf = [_log_sigmoid(gt[t][:, LANES:2 * LANES]) for t in toks]


    m_all = m_ref[...]
    rep = lambda a, h: jnp.broadcast_to(a[:, h:h + 1], (G, LANES))
    gates = []
    for hd in range(M_HEADS):
        ig_h = [rep(ig[t], hd) for t in toks]
        lf_h = [rep(lf[t], hd) for t in toks]
        m_old = rep(m_all, hd)
        bsum = [lf_h[0]]
        for t in range(1, T):
            bsum.append(bsum[-1] + lf_h[t])
        b_last = bsum[-1]
        d_end = [b_last - bsum[s] + ig_h[s] for s in toks]
        m_new = b_last + m_old
        for s in toks:
            m_new = jnp.maximum(m_new, d_end[s])
        mn_ref[:, hd:hd + 1] = m_new[:, 0:1]
        gd = dict(w_end=[jnp.exp(d_end[s] - m_new) for s in toks], decay=jnp.exp(b_last + m_old - m_new),
                  e=[], w_inter=[], emh=[])
        for t in toks:
            inter = bsum[t] + m_old
            dm = [bsum[t] - bsum[s] + ig_h[s] for s in range(t + 1)]
            m_hat = inter
            for s in range(t + 1):
                m_hat = jnp.maximum(m_hat, dm[s])
            gd["e"].append([jnp.exp(dm[s] - m_hat) for s in range(t + 1)])
            gd["w_inter"].append(jnp.exp(inter - m_hat))
            gd["emh"].append(jnp.exp(-m_hat))
        gates.append(gd)

    heads = [slice(hd * M_DK, (hd + 1) * M_DK) for hd in range(M_HEADS)]
    n_old = [n_ref[:, hd, :] for hd in range(M_HEADS)]
    qk = {(hd, t, s): jnp.sum(q[t][:, heads[hd]] * k[s][:, heads[hd]], axis=-1, keepdims=True)
          for hd in range(M_HEADS) for t in toks for s in range(t + 1)}
    qn = {(hd, t): jnp.sum(q[t][:, heads[hd]] * n_old[hd], axis=-1, keepdims=True)
          for hd in range(M_HEADS) for t in toks}

    lo_f = jnp.where(lo_half, 1.0, 0.0)
    hi_f = 1.0 - lo_f
    dup = lambda a, kv: (jnp.where(lo_half, a, pltpu.roll(a, A_HEAD_DIM, 1)) if kv == 0 else
                         jnp.where(lo_half, pltpu.roll(a, A_HEAD_DIM, 1), a))
    kn_x = [[dup(kn[s], kv) for s in toks] for kv in range(A_KV_HEADS)]
    vn_x = [[dup(vn[s], kv) for s in toks] for kv in range(A_KV_HEADS)]
    s_new = {}
    for kv in range(A_KV_HEADS):
        for blk in range(2):
            for t in toks:
                for s in range(t + 1):
                    prod = qa[t][2 * kv + blk] * kn_x[kv][s]
                    s_new[kv, 0, blk, t, s] = jnp.sum(prod * lo_f, axis=-1, keepdims=True)
                    s_new[kv, 1, blk, t, s] = jnp.sum(prod * hi_f, axis=-1, keepdims=True)

    rows_tg = lax.broadcasted_iota(jnp.int32, (T * G, LANES), 0) % G
    own_rows = [rows_tg == b for b in range(G)]
    rows64 = lax.broadcasted_iota(jnp.int32, (2 * T * G, LANES), 0) % G

    qc = []
    for hd in range(M_HEADS):
        hs = heads[hd]
        gd = gates[hd]
        q32 = jnp.concatenate([q[t][:, hs] for t in toks], axis=0)
        kw = [k[s][:, hs] * gd["w_end"][s] for s in toks]
        kw_t = jnp.transpose(jnp.concatenate(kw, axis=0)).astype(BF16)
        v32 = jnp.concatenate([v[s][:, hs] for s in toks], axis=0)
        acc = None
        for b in range(G):
            c_old = c_ref[b, hd]
            part = _dot(jnp.where(own_rows[b], q32, 0.0).astype(BF16), c_old.astype(BF16))
            acc = part if acc is None else acc + part
            upd = _dot(kw_t, jnp.where(own_rows[b], v32, 0.0).astype(BF16))
            cn_ref[b, hd] = gd["decay"][b:b + 1, 0:1] * c_old + upd
        qc.append(acc)
        nn_ref[:, hd, :] = gd["decay"] * n_old[hd] + (kw[0] + kw[1] + kw[2] + kw[3])

    zeros_kt = jnp.zeros((A_HEAD_DIM, WINDOW), BF16)
    sc = {}
    for kv in range(A_KV_HEADS):
        l64 = jnp.concatenate([qa[t][2 * kv + blk] for blk in range(2) for t in toks], axis=0)
        own64 = [jnp.where(rows64 == b, l64, 0.0).astype(BF16) for b in range(G)]
        kt = [ck_ref[b, kv].astype(BF16) for b in range(G)]
        for par in range(2):
            acc = None
            for b in range(G):
                rhs = jnp.concatenate([kt[b], zeros_kt] if par == 0 else [zeros_kt, kt[b]], axis=0)
                part = _dot(own64[b], rhs)
                acc = part if acc is None else acc + part
            sc[kv, par] = acc

    hh = {}
    for hd in range(M_HEADS):
        gd = gates[hd]
        for t in toks:
            sv, ssum = None, None
            for s in range(t + 1):
                s_ts = qk[hd, t, s] * gd["e"][t][s]
                sv = s_ts * v[s][:, heads[hd]] if sv is None else sv + s_ts * v[s][:, heads[hd]]
                ssum = s_ts if ssum is None else ssum + s_ts
            wi = gd["w_inter"][t]
            num = sv + wi * qc[hd][t * G:(t + 1) * G, :]
            den = ssum + wi * qn[hd, t]
            hh[hd, t] = num / jnp.maximum(jnp.abs(den), gd["emh"][t])
    ms = {key: jnp.mean(val * val, axis=-1, keepdims=True) for key, val in hh.items()}
    for (hd, t), val in hh.items():
        hs = heads[hd]
        ym_ref[t, :, hs] = val * lax.rsqrt(ms[hd, t] + NORM_EPS) * gmn[:, hs] * og[t][:, hs]

    items = [(kv, par, blk, t) for kv in range(A_KV_HEADS) for par in range(2) for blk in range(2) for t in toks]
    s_c = {it: jnp.where(lane > it[3], sc[it[0], it[1]][(it[2] * T + it[3]) * G:(it[2] * T + it[3] + 1) * G, :], -jnp.inf)
           for it in items}
    mx_c = {it: jnp.max(s_c[it], axis=-1, keepdims=True) for it in items}
    p_c, p_n, sink_t = {}, {}, {}
    for it in items:
        kv, par, blk, t = it
        sink = sinks_ref[A_GROUP * kv + 2 * blk + par]
        mx = jnp.maximum(mx_c[it], sink)
        for s in range(t + 1):
            mx = jnp.maximum(mx, s_new[kv, par, blk, t, s])
        p_c[it] = jnp.exp(s_c[it] - mx)
        p_n[it] = [jnp.exp(s_new[kv, par, blk, t, s] - mx) for s in range(t + 1)]
        sink_t[it] = jnp.exp(sink - mx)
    sum_c = {it: jnp.sum(p_c[it], axis=-1, keepdims=True) for it in items}
    fresh = {}
    probs = {(kv, par): [] for kv in range(A_KV_HEADS) for par in range(2)}
    for it in items:
        kv, par, blk, t = it
        den = sum_c[it] + sink_t[it]
        for s in range(t + 1):
            den = den + p_n[it][s]
        r = 1.0 / den
        probs[kv, par].append(p_c[it] * r)
        acc = None
        for s in range(t + 1):
            term = (p_n[it][s] * r) * vn_x[kv][s]
            acc = term if acc is None else acc + term
        fresh[it] = acc

    for kv in range(A_KV_HEADS):
        vt = [cv_ref[b, kv].astype(BF16) for b in range(G)]
        o = []
        for par in range(2):
            pstack = jnp.concatenate(probs[kv, par], axis=0)
            acc = None
            for b in range(G):
                rhs = jnp.concatenate([vt[b], zeros_kt] if par == 0 else [zeros_kt, vt[b]], axis=0)
                part = _dot_nt(jnp.where(rows64 == b, pstack, 0.0).astype(BF16), rhs)
                acc = part if acc is None else acc + part
            o.append(acc)
        for blk in range(2):
            cs = slice((2 * kv + blk) * LANES, (2 * kv + blk + 1) * LANES)
            for t in toks:
                rs = slice((blk * T + t) * G, (blk * T + t + 1) * G)
                new_v = jnp.where(lo_half, fresh[kv, 0, blk, t], fresh[kv, 1, blk, t])
                ya_ref[t, :, cs] = (o[0][rs, :] + o[1][rs, :] + new_v) * sz[t][:, cs]

    n_rows = T * G
    src = lax.broadcasted_iota(jnp.int32, (n_rows, n_rows), 1)
    dst = lax.broadcasted_iota(jnp.int32, (n_rows, n_rows), 0)
    perm = jnp.where(src == (dst % T) * G + dst // T, 1.0, 0.0).astype(BF16)
    lane_d = lax.broadcasted_iota(jnp.int32, (A_HEAD_DIM, WINDOW), 1)
    for new, cache_ref, out_ref in ((kn, ck_ref, ko_ref), (vn, cv_ref, vo_ref)):
        stack = _dot_exact01(perm, jnp.concatenate(new, axis=0))
        stack = jnp.concatenate([stack, jnp.zeros((LANES - n_rows, LANES), F32)], axis=0)
        new_t = jnp.transpose(stack)
        for b in range(G):
            placed = pltpu.roll(new_t, (WINDOW - T - T * b) % LANES, 1)
            for kv in range(A_KV_HEADS):
                slid = pltpu.roll(cache_ref[b, kv], WINDOW - T, 1)
                out_ref[b, kv] = jnp.where(lane_d >= WINDOW - T, placed[kv * A_HEAD_DIM:(kv + 1) * A_HEAD_DIM, :], slid)


def _sample_state(zs, state_c, state_n, state_m, ck_t, cv_t, rope, bias_g, gmn, sinks):
    steps, bsz, _ = zs.shape
    group = SAMPLE_GROUP
    assert bsz % group == 0
    in_specs = [
        pl.BlockSpec(memory_space=pltpu.SMEM),
        pl.BlockSpec((steps, group, N_PACK), lambda i: (0, i, 0)),
        pl.BlockSpec((group, M_HEADS, M_DK, M_DV), lambda i: (i, 0, 0, 0)),
        pl.BlockSpec((group, M_HEADS, M_DK), lambda i: (i, 0, 0)),
        pl.BlockSpec((group, M_HEADS), lambda i: (i, 0)),
        pl.BlockSpec((group, A_KV_HEADS, A_HEAD_DIM, WINDOW), lambda i: (i, 0, 0, 0)),
        pl.BlockSpec((group, A_KV_HEADS, A_HEAD_DIM, WINDOW), lambda i: (i, 0, 0, 0)),
        _full((3, steps, LANES)), _full((1, 2 * LANES)), _full((1, M_WIDTH)),
    ]
    out_specs = [
        pl.BlockSpec((group, M_HEADS, M_DK, M_DV), lambda i: (i, 0, 0, 0)),
        pl.BlockSpec((group, M_HEADS, M_DK), lambda i: (i, 0, 0)),
        pl.BlockSpec((group, M_HEADS), lambda i: (i, 0)),
        pl.BlockSpec((steps, group, M_WIDTH), lambda i: (0, i, 0)),
        pl.BlockSpec((steps, group, A_WIDTH), lambda i: (0, i, 0)),
        pl.BlockSpec((group, A_KV_HEADS, A_HEAD_DIM, WINDOW), lambda i: (i, 0, 0, 0)),
        pl.BlockSpec((group, A_KV_HEADS, A_HEAD_DIM, WINDOW), lambda i: (i, 0, 0, 0)),
    ]
    out_shape = [
        jax.ShapeDtypeStruct((bsz, M_HEADS, M_DK, M_DV), F32),
        jax.ShapeDtypeStruct((bsz, M_HEADS, M_DK), F32),
        jax.ShapeDtypeStruct((bsz, M_HEADS), F32),
        jax.ShapeDtypeStruct((steps, bsz, M_WIDTH), F32),
        jax.ShapeDtypeStruct((steps, bsz, A_WIDTH), F32),
        jax.ShapeDtypeStruct((bsz, A_KV_HEADS, A_HEAD_DIM, WINDOW), F32),
        jax.ShapeDtypeStruct((bsz, A_KV_HEADS, A_HEAD_DIM, WINDOW), F32),
    ]
    return pl.pallas_call(
        functools.partial(_sample_state_kernel, group=group, steps=steps),
        grid=(bsz // group,),
        in_specs=in_specs,
        out_specs=out_specs,
        out_shape=out_shape,
        compiler_params=pltpu.CompilerParams(dimension_semantics=("arbitrary",), vmem_limit_bytes=VMEM_LIMIT),
        name="sample_state",
    )(sinks, zs, state_c, state_n, state_m, ck_t, cv_t, rope, bias_g, gmn)


def _sample_out_kernel(x_ref, gate_ref, z_ref, ym_ref, ya_ref, wmo_ref, wao_ref, wo_ref, gf_ref, y_ref):
    bsz, steps, _ = x_ref.shape
    stack = lambda ref, cs: jnp.concatenate([ref[t, :, cs] for t in range(steps)], axis=0)
    full = slice(None)
    x = jnp.concatenate([x_ref[:, t, :] for t in range(steps)], axis=0)
    gate = jnp.concatenate([gate_ref[...]] * steps, axis=0)
    y = _out_stage(x, gate, _sigmoid(stack(z_ref, slice(0, D_MODEL))), _sigmoid(stack(z_ref, slice(D_MODEL, 2 * D_MODEL))),
                   stack(ym_ref, full).astype(BF16), stack(ya_ref, full).astype(BF16), wmo_ref, wao_ref, wo_ref, gf_ref[...])
    for t in range(steps):
        y_ref[:, t, :] = y[t * bsz:(t + 1) * bsz, :]


def _sample_out(x3, mod_s, zs, ym, ya, wmo, wao, wo, gf):
    bsz, steps, d = x3.shape
    return pl.pallas_call(
        _sample_out_kernel,
        grid=(1,),
        in_specs=[
            _full((bsz, steps, d)),
            pl.BlockSpec((bsz, d), lambda i: (0, 2)),
            pl.BlockSpec((steps, bsz, 2 * d), lambda i: (0, 0, 0)),
            _full((steps, bsz, M_WIDTH)), _full((steps, bsz, A_WIDTH)),
            _full((M_WIDTH, d)), _full((A_WIDTH, d)), _full((d, d)), _full((1, d)),
        ],
        out_specs=_full((bsz, steps, d)),
        out_shape=jax.ShapeDtypeStruct((bsz, steps, d), F32),
        compiler_params=pltpu.CompilerParams(dimension_semantics=("arbitrary",), vmem_limit_bytes=VMEM_LIMIT),
        name="sample_out",
    )(x3, mod_s, zs, ym, ya, wmo, wao, wo, gf)


PACK_BLOCK = 512


def _pack_sources():
    names = ("mq", "mk", "mv", "mi", "mf", "mo", "mz", "aq", "ak", "av", "az", "gm", "ga")
    start, pos = {}, 0
    for name, size in zip(names, IN_SIZES):
        start[name] = pos
        pos += size
    assert start["av"] == start["ak"] + A_KV_WIDTH and start["mf"] == start["mi"] + M_HEADS
    order = (("gm", C_GM, D_MODEL), ("ga", C_GA, D_MODEL), ("mq", C_MQ, M_WIDTH), ("mk", C_MK, M_WIDTH),
             ("mv", C_MV, M_WIDTH), ("mo", C_MO, M_WIDTH), ("mz", C_MZ, M_WIDTH), ("aq", C_AQ, A_WIDTH),
             ("az", C_AZ, A_WIDTH))
    src = []
    for name, col, width in order:
        assert col == len(src) * PACK_BLOCK and width % PACK_BLOCK == 0
        src += [start[name] + i * PACK_BLOCK for i in range(width // PACK_BLOCK)]
    assert C_AK == len(src) * PACK_BLOCK and C_AV == C_AK + A_KV_WIDTH and C_GI == C_AV + A_KV_WIDTH
    assert C_GF == C_GI + LANES and N_PACK == C_AK + PACK_BLOCK and start["ak"] + PACK_BLOCK <= pos
    return src + [start["ak"]], start["mi"]


def _pack_kernel(src_ref, wt_ref, gate_ref, o_ref):
    j = pl.program_id(0)
    last = pl.num_programs(0) - 1

    @pl.when(j < last)
    def _block():
        o_ref[...] = jnp.transpose(wt_ref[...]).astype(BF16)

    @pl.when(j == last)
    def _kv_and_gates():
        kv = 2 * A_KV_WIDTH
        o_ref[:, 0:kv] = jnp.transpose(wt_ref[0:kv, :]).astype(BF16)
        g = gate_ref[...]
        pad = jnp.zeros((LANES - M_HEADS, g.shape[1]), F32)
        rows = jnp.concatenate([g[0:M_HEADS], pad, g[M_HEADS:2 * M_HEADS], pad], axis=0)
        o_ref[:, kv:kv + 2 * LANES] = jnp.transpose(rows).astype(BF16)


def _pack_w_in(w_in):
    d = w_in.shape[0]
    src, gate_row = _pack_sources()
    wt = jnp.transpose(w_in)
    return pl.pallas_call(
        _pack_kernel,
        grid_spec=pltpu.PrefetchScalarGridSpec(
            num_scalar_prefetch=1, grid=(N_PACK // PACK_BLOCK,),
            in_specs=[pl.BlockSpec((pl.Element(PACK_BLOCK), pl.Element(d)),
                                   lambda j, src_ref: (pl.multiple_of(src_ref[j], 8), 0)),
                      pl.BlockSpec((pl.Element(2 * M_HEADS), pl.Element(d)), lambda j, src_ref: (gate_row, 0))],
            out_specs=pl.BlockSpec((d, PACK_BLOCK), lambda j, src_ref: (0, j))),
        out_shape=jax.ShapeDtypeStruct((d, N_PACK), BF16),
        compiler_params=pltpu.CompilerParams(dimension_semantics=("arbitrary",)),
        name="pack_w_in",
    )(jnp.asarray(src, jnp.int32), wt, wt)


def kernel(x_prompt, x_sample, state_C, state_n, state_m, cache_k, cache_v, c_prompt, c_sample, w_ada, b_ada, g_norm, w_in, b_igate, b_fgate, g_mnorm, sinks, w_m_out, w_a_out, w_out, g_final):
    assert w_in.shape[0] == 1, "single-layer step"
    bsz_s, steps, d = x_sample.shape
    assert cache_k.shape[2] == WINDOW

    mod_p, mod_s = _adaln(c_prompt, c_sample, w_ada[0], b_ada[0])
    wp = _pack_w_in(w_in[0])
    wmo = w_m_out[0].astype(BF16)
    wao = w_a_out[0].astype(BF16)
    wo = w_out[0].astype(BF16)
    gn = g_norm[0].reshape(1, d)
    gf = g_final.reshape(1, d)
    gmn = g_mnorm[0].reshape(1, M_WIDTH)
    zpad = jnp.zeros((LANES - M_HEADS,), F32)
    bias_g = jnp.concatenate([b_igate[0], zpad, b_fgate[0], zpad]).reshape(1, 2 * LANES)
    sk = sinks[0]

    y_p, c_p, n_p, m_p, k_p, v_p = _prompt_layer(x_prompt, mod_p, sk, gn, gf, gmn, bias_g, wp, wmo, wao, wo)
    m_p = m_p[:, 0, 0:M_HEADS]

    zs = _sample_proj(x_sample, mod_s, gn, wp)
    rope = jnp.stack(_rope_tables(PAST_LEN, steps))
    dims_keys = lambda c: jnp.transpose(c[0], (0, 2, 3, 1))
    c_s, n_s, m_s, ym, ya, k_t, v_t = _sample_state(zs, state_C[0], state_n[0], state_m[0], dims_keys(cache_k),
                                                    dims_keys(cache_v), rope, bias_g, gmn, sk)
    y_s = _sample_out(x_sample, mod_s, zs, ym, ya, wmo, wao, wo, gf)
    keys_dims = lambda c: jnp.transpose(c, (0, 3, 1, 2))[None]

    return (y_p, y_s, c_p[None], n_p[None], m_p[None], keys_dims(k_p), keys_dims(v_p),
            c_s[None], n_s[None], m_s[None], keys_dims(k_t), keys_dims(v_t))
```

```python
import functools

import jax
import jax.numpy as jnp
import numpy as np
from jax import lax
from jax.experimental import pallas as pl
from jax.experimental.pallas import tpu as pltpu

F32 = jnp.float32
BF16 = jnp.bfloat16

D_MODEL = 1024
M_HEADS = 4
M_DK = 128
M_DV = 128
M_WIDTH = M_HEADS * M_DV
M_CHUNK = 128
A_HEADS = 8
A_KV_HEADS = 2
A_GROUP = A_HEADS // A_KV_HEADS
A_HEAD_DIM = 64
A_WIDTH = A_HEADS * A_HEAD_DIM
A_KV_WIDTH = A_KV_HEADS * A_HEAD_DIM
WINDOW = 128
ROT_DIM = A_HEAD_DIM // 4
ROPE_THETA = 500000.0
NORM_EPS = 1e-6
PAST_LEN = 16384
IN_SIZES = (M_HEADS * M_DK, M_HEADS * M_DK, M_WIDTH, M_HEADS, M_HEADS, M_WIDTH, M_WIDTH,
            A_WIDTH, A_KV_WIDTH, A_KV_WIDTH, A_WIDTH, D_MODEL, D_MODEL)

LANES = 128

C_GM = 0
C_GA = C_GM + D_MODEL
C_MQ = C_GA + D_MODEL
C_MK = C_MQ + M_WIDTH
C_MV = C_MK + M_WIDTH
C_MO = C_MV + M_WIDTH
C_MZ = C_MO + M_WIDTH
C_AQ = C_MZ + M_WIDTH
C_AZ = C_AQ + A_WIDTH
C_AK = C_AZ + A_WIDTH
C_AV = C_AK + A_KV_WIDTH
C_GI = C_AV + A_KV_WIDTH
C_GF = C_GI + LANES
N_PACK = C_GF + LANES

TAIL_PIECES = 4
PROMPT_TILE = 256
SAMPLE_GROUP = 16
VMEM_LIMIT = 56 * 1024 * 1024


def _sigmoid(x):
    return 1.0 / (1.0 + jnp.exp(-x))


def _silu(x):
    return x * _sigmoid(x)


def _log_sigmoid(x):
    return jnp.minimum(x, 0.0) - jnp.log1p(jnp.exp(-jnp.abs(x)))


def _dot(a, b):
    return jnp.dot(a, b, preferred_element_type=F32)


def _dot_nt(a, b):
    return lax.dot_general(a, b, (((1,), (1,)), ((), ())), preferred_element_type=F32)


def _dot_tn(a, b):
    return lax.dot_general(a, b, (((0,), (0,)), ((), ())), preferred_element_type=F32)


def _dot_exact01(m01, x):
    x1 = x.astype(BF16)
    r1 = x - x1.astype(F32)
    x2 = r1.astype(BF16)
    x3 = (r1 - x2.astype(F32)).astype(BF16)
    return _dot(m01, x1) + _dot(m01, x2) + _dot(m01, x3)


def _rms(x):
    return x * lax.rsqrt(jnp.mean(x * x, axis=-1, keepdims=True) + NORM_EPS)


def _rope(blk, ra, rb, rc):
    return blk * ra + pltpu.roll(blk, 8, 1) * rb + pltpu.roll(blk, LANES - 8, 1) * rc


def _out_stage(x, gate, sgm, sga, ym, ya, wmo_ref, wao_ref, wo_ref, gf):
    pm = _dot(ym, wmo_ref[...])
    pa = _dot(ya, wao_ref[...])
    u = sgm * pm + sga * pa
    r = _dot(u.astype(BF16), wo_ref[...])
    return _rms(x + gate * r) * gf


def _adaln_kernel(cp_ref, cs_ref, w_ref, b_ref, op_ref, os_ref):
    w = w_ref[...].astype(BF16)
    b = b_ref[...]
    op_ref[...] = _dot(_silu(cp_ref[...]).astype(BF16), w) + b
    os_ref[...] = _dot(_silu(cs_ref[...]).astype(BF16), w) + b


def _adaln(c_p, c_s, w_ada, b_ada):
    bp, d = c_p.shape
    bs = c_s.shape[0]
    n = w_ada.shape[1]
    tn = 1024
    return pl.pallas_call(
        _adaln_kernel,
        grid=(n // tn,),
        in_specs=[
            pl.BlockSpec((bp, d), lambda j: (0, 0)),
            pl.BlockSpec((bs, d), lambda j: (0, 0)),
            pl.BlockSpec((d, tn), lambda j: (0, j)),
            pl.BlockSpec((1, tn), lambda j: (0, j)),
        ],
        out_specs=[
            pl.BlockSpec((bp, tn), lambda j: (0, j)),
            pl.BlockSpec((bs, tn), lambda j: (0, j)),
        ],
        out_shape=[jax.ShapeDtypeStruct((bp, n), F32), jax.ShapeDtypeStruct((bs, n), F32)],
        compiler_params=pltpu.CompilerParams(dimension_semantics=("arbitrary",)),
        name="adaln",
    )(c_p, c_s, w_ada, b_ada.reshape(1, n))


_SET_FIELDS = ("hb", "q", "k", "vaug", "og", "qa", "kf", "vf", "sz", "sgm", "sga", "g")


def _set_shapes(tile):
    return [
        pltpu.VMEM((tile, D_MODEL), BF16),
        pltpu.VMEM((tile, M_WIDTH), BF16),
        pltpu.VMEM((tile, M_WIDTH), F32),
        pltpu.VMEM((M_HEADS, tile, 2 * LANES), BF16),
        pltpu.VMEM((tile, M_WIDTH), F32),
        pltpu.VMEM((tile, A_WIDTH), BF16),
        pltpu.VMEM((tile, A_KV_WIDTH), F32),
        pltpu.VMEM((tile, A_KV_WIDTH), F32),
        pltpu.VMEM((tile, A_WIDTH), F32),
        pltpu.VMEM((tile, D_MODEL), F32),
        pltpu.VMEM((tile, D_MODEL), F32),
        pltpu.VMEM((tile, 2 * LANES), F32),
    ]


N_TILE = 256


def _project_pieces(x_ref, mod_ref, rope_refs, st, gn_ref, bg_ref, w_ref):
    def norm():
        mod = mod_ref[0]
        h = _rms(x_ref[0]) * gn_ref[...]
        st["hb"][...] = (h * (1.0 + mod[:, D_MODEL:2 * D_MODEL]) + mod[:, 0:D_MODEL]).astype(BF16)

    def proj(lo):
        return _dot(st["hb"][...], w_ref[:, lo:lo + N_TILE])

    def rope(v):
        ra_ref, rb_ref, rc_ref = rope_refs
        return _rope(v, ra_ref[...], rb_ref[...], rc_ref[...])

    def cols(i):
        return slice(i * N_TILE, (i + 1) * N_TILE)

    def gates():
        st["g"][...] = proj(C_GI) + bg_ref[...]

    def queries(i):
        def piece():
            st["q"][:, cols(i)] = (proj(C_MQ + i * N_TILE) * (M_DK ** -0.5)).astype(BF16)
        return piece

    def keys(i):
        def piece():
            st["k"][:, cols(i)] = proj(C_MK + i * N_TILE)
        return piece

    def values(i):
        def piece():
            zv = proj(C_MV + i * N_TILE)
            for j in range(N_TILE // M_DV):
                st["vaug"][i * (N_TILE // M_DV) + j, :, 0:LANES] = zv[:, j * M_DV:(j + 1) * M_DV].astype(BF16)
        return piece

    def attn_kv():
        z = proj(C_AK)
        st["kf"][...] = rope(z[:, 0:A_KV_WIDTH])
        st["vf"][...] = z[:, A_KV_WIDTH:2 * A_KV_WIDTH]

    def attn_q(i):
        def piece():
            za = proj(C_AQ + i * N_TILE)
            for j in range(N_TILE // LANES):
                blk = rope(za[:, j * LANES:(j + 1) * LANES])
                lo = i * N_TILE + j * LANES
                st["qa"][:, lo:lo + LANES] = (blk * (A_HEAD_DIM ** -0.5)).astype(BF16)
        return piece

    def out_gate(i):
        def piece():
            st["og"][:, cols(i)] = _sigmoid(proj(C_MO + i * N_TILE)) * _silu(proj(C_MZ + i * N_TILE))
        return piece

    def attn_gate(i):
        def piece():
            st["sz"][:, cols(i)] = _silu(proj(C_AZ + i * N_TILE))
        return piece

    def merge_gate(name, lo, i):
        def piece():
            st[name][:, cols(i)] = _sigmoid(proj(lo + i * N_TILE))
        return piece

    assert C_AV == C_AK + A_KV_WIDTH and 2 * A_KV_WIDTH == N_TILE and C_GF == C_GI + LANES
    half, full = range(M_WIDTH // N_TILE), range(D_MODEL // N_TILE)
    pieces = [norm, gates, attn_kv]
    for i in half:
        pieces += [queries(i), keys(i), values(i), attn_q(i)]
    for i in half:
        pieces += [out_gate(i), attn_gate(i)]
    for i in full:
        pieces += [merge_gate("sgm", C_GM, i), merge_gate("sga", C_GA, i)]
    return pieces


def _consume_pieces(st, first_tile, sinks_ref, gmn_ref, kbuf_s, vbuf_s, c_s, m_s, ym_s, ya_s, tile):
    L = M_CHUNK
    heads = [slice(hd * M_DK, (hd + 1) * M_DK) for hd in range(M_HEADS)]
    a_heads = [(kv, bi, half) for kv in range(A_KV_HEADS) for bi in range(2) for half in range(2)]

    def causal_mask():
        row = lax.broadcasted_iota(jnp.int32, (L, L), 0)
        col = lax.broadcasted_iota(jnp.int32, (L, L), 1)
        return row >= col

    def chunk_stages(j):
        rs = slice(j * L, (j + 1) * L)
        cx = {}

        def gates():
            g = st["g"][rs, :]
            ig_all = g[:, 0:LANES]
            lf_all = _log_sigmoid(g[:, LANES:2 * LANES])
            tril = jnp.where(causal_mask(), 1.0, 0.0).astype(BF16)
            b_all = _dot_exact01(tril, lf_all)
            b_last = b_all[L - 1:L, :]
            m_old = m_s[...]
            d_end = b_last - b_all + ig_all
            m_new = jnp.maximum(b_last + m_old, jnp.max(d_end, axis=0, keepdims=True))
            m_s[...] = m_new
            cx.update(b_all=b_all, w_end=jnp.exp(d_end - m_new), decay=jnp.exp(b_last + m_old - m_new),
                      inter=b_all + m_old, r_t=jnp.transpose(ig_all - b_all))

        def kv_buffers():
            srows = slice((j % 2) * L, (j % 2 + 1) * L)
            lo_half = lax.broadcasted_iota(jnp.int32, (L, L), 1) < A_HEAD_DIM
            kc = st["kf"][rs, :]
            vc = st["vf"][rs, :]
            kr = pltpu.roll(kc, A_HEAD_DIM, 1)
            vr = pltpu.roll(vc, A_HEAD_DIM, 1)
            kbuf_s[0, srows, :] = jnp.where(lo_half, kc, 0.0).astype(BF16)
            kbuf_s[1, srows, :] = jnp.where(lo_half, 0.0, kr).astype(BF16)
            kbuf_s[2, srows, :] = jnp.where(lo_half, kr, 0.0).astype(BF16)
            kbuf_s[3, srows, :] = jnp.where(lo_half, 0.0, kc).astype(BF16)
            lane = lax.broadcasted_iota(jnp.int32, (L, L), 1)
            ones_lo = jnp.where(lane == 0, 1.0, 0.0)
            ones_hi = jnp.where(lane == A_HEAD_DIM, 1.0, 0.0)
            vbuf_s[0, srows, :] = jnp.where(lo_half, vc, ones_hi).astype(BF16)
            vbuf_s[1, srows, :] = jnp.where(lo_half, ones_lo, vr).astype(BF16)
            vbuf_s[2, srows, :] = jnp.where(lo_half, vr, ones_hi).astype(BF16)
            vbuf_s[3, srows, :] = jnp.where(lo_half, ones_lo, vc).astype(BF16)

        def decay_matrices():
            cx["dmat"] = [jnp.where(causal_mask(), cx["b_all"][:, hd:hd + 1] + cx["r_t"][hd:hd + 1, :], -jnp.inf)
                          for hd in range(M_HEADS)]
            cx["m_hat"] = [jnp.maximum(cx["inter"][:, hd:hd + 1], jnp.max(cx["dmat"][hd], axis=-1, keepdims=True))
                           for hd in range(M_HEADS)]
            cx["kw"] = [(st["k"][rs, heads[hd]] * cx["w_end"][:, hd:hd + 1]).astype(BF16) for hd in range(M_HEADS)]

        def state_matmuls():
            cx["qk"], cx["qc"] = [], []
            for hd in range(M_HEADS):
                qh = st["q"][rs, heads[hd]]
                c_old = c_s[hd]
                cx["qk"].append(_dot_nt(qh, st["k"][rs, heads[hd]].astype(BF16)))
                cx["qc"].append(_dot(qh, c_old.astype(BF16)))
                va = st["vaug"][hd, rs, :]
                c_s[hd] = cx["decay"][:, hd:hd + 1] * c_old + _dot_tn(cx["kw"][hd], va)

        def attn_scores():
            row2 = lax.broadcasted_iota(jnp.int32, (L, 2 * L), 0)
            col2 = lax.broadcasted_iota(jnp.int32, (L, 2 * L), 1)
            no_prev = jnp.where(first_tile, 4 * L, 0) if j == 0 else 0
            if j % 2 == 0:
                mask = ((col2 < L) & (col2 <= row2)) | ((col2 >= L) & (col2 - L > row2 + no_prev))
            else:
                mask = ((col2 >= L) & (col2 - L <= row2)) | ((col2 < L) & (col2 > row2 + no_prev))
            cx["sh"] = {}
            for kv in range(A_KV_HEADS):
                qcat = jnp.concatenate([st["qa"][rs, (2 * kv) * LANES:(2 * kv + 1) * LANES],
                                        st["qa"][rs, (2 * kv + 1) * LANES:(2 * kv + 2) * LANES]], axis=0)
                for half in range(2):
                    sc = _dot_nt(qcat, kbuf_s[2 * kv + half])
                    for bi in range(2):
                        cx["sh"][kv, bi, half] = jnp.where(mask, sc[bi * L:(bi + 1) * L, :], -jnp.inf)

        def attn_max():
            cx["mx"] = {hk: jnp.maximum(jnp.max(cx["sh"][hk], axis=-1, keepdims=True),
                                        sinks_ref[A_GROUP * hk[0] + 2 * hk[1] + hk[2]]) for hk in a_heads}

        def stabilised_scores():
            cx["s"] = [(cx["qk"][hd] * jnp.exp(cx["dmat"][hd] - cx["m_hat"][hd])).astype(BF16)
                       for hd in range(M_HEADS)]

        def score_value_matmuls():
            cx["sv"] = [_dot(cx["s"][hd], st["vaug"][hd, rs, :]) for hd in range(M_HEADS)]

        def attn_exp():
            cx["p"] = {hk: jnp.exp(cx["sh"][hk] - cx["mx"][hk]).astype(BF16) for hk in a_heads}
            cx["sink"] = {hk: jnp.exp(sinks_ref[A_GROUP * hk[0] + 2 * hk[1] + hk[2]] - cx["mx"][hk]) for hk in a_heads}

        def attn_wait():
            pass

        def attn_value_matmuls():
            lo_half = lax.broadcasted_iota(jnp.int32, (L, L), 1) < A_HEAD_DIM
            for kv in range(A_KV_HEADS):
                for bi in range(2):
                    o_lo = _dot(cx["p"][kv, bi, 0], vbuf_s[2 * kv])
                    o_hi = _dot(cx["p"][kv, bi, 1], vbuf_s[2 * kv + 1])
                    den_lo = o_lo[:, A_HEAD_DIM:A_HEAD_DIM + 1] + cx["sink"][kv, bi, 0]
                    den_hi = o_hi[:, 0:1] + cx["sink"][kv, bi, 1]
                    o = jnp.where(lo_half, o_lo / den_lo, o_hi / den_hi)
                    cs = slice((2 * kv + bi) * LANES, (2 * kv + bi + 1) * LANES)
                    ya_s[rs, cs] = (o * st["sz"][rs, cs]).astype(BF16)

        def mlstm_outputs():
            hh = []
            for hd in range(M_HEADS):
                m_hat = cx["m_hat"][hd]
                w_inter = jnp.exp(cx["inter"][:, hd:hd + 1] - m_hat)
                tot = cx["sv"][hd] + w_inter * cx["qc"][hd]
                hh.append(tot[:, 0:M_DV] / jnp.maximum(jnp.abs(tot[:, M_DV:M_DV + 1]), jnp.exp(-m_hat)))
            ms = [jnp.mean(h * h, axis=-1, keepdims=True) for h in hh]
            for hd in range(M_HEADS):
                hs = heads[hd]
                ym_s[rs, hs] = (hh[hd] * lax.rsqrt(ms[hd] + NORM_EPS) * gmn_ref[:, hs] * st["og"][rs, hs]).astype(BF16)

        mlstm = [gates, decay_matrices, state_matmuls, stabilised_scores, score_value_matmuls, mlstm_outputs]
        attn = [kv_buffers, attn_scores, attn_max, attn_exp, attn_wait, attn_value_matmuls]
        return mlstm, attn

    assert tile == 2 * L
    (m0, a0), (m1, a1) = chunk_stages(0), chunk_stages(1)
    merged = []
    for m_stage, a_stage in zip(m0 + m1, a0 + a1):
        merged += [m_stage, a_stage]
    return merged


def _output_pieces(st, x_ref, rows, gate_ref, ym_s, ya_s, u_s, wmo_ref, wao_ref, wo_ref, gf_ref, y_ref):
    def merge():
        u = st["sgm"][...] * _dot(ym_s[...], wmo_ref[...]) + st["sga"][...] * _dot(ya_s[...], wao_ref[...])
        u_s[...] = u.astype(BF16)

    def out():
        gate = gate_ref[0][:, 2 * D_MODEL:3 * D_MODEL]
        r = _dot(u_s[...], wo_ref[...])
        y_ref[0, rows, :] = _rms(x_ref[0, rows, :] + gate * r) * gf_ref[...]

    return [merge, out]


def _interleave(primary, filler):
    done = 0
    for i, piece in enumerate(primary):
        piece()
        upto = ((i + 1) * len(filler)) // len(primary)
        for f in filler[done:upto]:
            f()
        done = upto


def _prompt_kernel(sinks_ref, x0_ref, xa_ref, xb_ref, xres_ref, mod0_ref, moda_ref, modb_ref,
                   gn_ref, gf_ref, gmn_ref, bg_ref, w_hbm, wmo_hbm, wao_hbm, wo_hbm,
                   ra_ref, rb_ref, rc_ref,
                   y_ref, c_out_ref, n_out_ref, m_out_ref, kk_ref, vk_ref, *scratch, tile, steps_per_seq, n_tiles):
    nset = len(_SET_FIELDS)
    set_a = dict(zip(_SET_FIELDS, scratch[0:nset]))
    set_b = dict(zip(_SET_FIELDS, scratch[nset:2 * nset]))
    (kbuf_s, vbuf_s, c_s, m_s, yma_s, yaa_s, ua_s, ymb_s, yab_s, ub_s,
     w_ref, wmo_ref, wao_ref, wo_ref, w_sems) = scratch[2 * nset:]
    k = pl.program_id(0)
    first = (k % steps_per_seq) == 0
    tiles_per_seq = 2 * steps_per_seq

    def project(x_ref, mod_ref, tile_index, st):
        rows = pl.ds(pl.multiple_of((tile_index % tiles_per_seq) * tile, tile), tile)
        rope = tuple(r.at[rows, :] for r in (ra_ref, rb_ref, rc_ref))
        return _project_pieces(x_ref, mod_ref, rope, st, gn_ref, bg_ref, w_ref)

    @pl.when(k == 0)
    def _prologue():
        copies = [pltpu.make_async_copy(src, dst, w_sems.at[i]) for i, (src, dst) in enumerate(
            ((w_hbm, w_ref), (wmo_hbm, wmo_ref), (wao_hbm, wao_ref), (wo_hbm, wo_ref)))]
        for copy in copies:
            copy.start()
        for copy in copies:
            copy.wait()
        lane = lax.broadcasted_iota(jnp.int32, (tile, LANES), 1)
        ones_col = jnp.where(lane == 0, 1.0, 0.0).astype(BF16)
        for st in (set_a, set_b):
            for hd in range(M_HEADS):
                st["vaug"][hd, :, LANES:2 * LANES] = ones_col
        for piece in project(x0_ref, mod0_ref, 0, set_a):
            piece()

    @pl.when(first)
    def _init():
        c_s[...] = jnp.zeros_like(c_s)
        m_s[...] = jnp.zeros_like(m_s)
        kbuf_s[...] = jnp.zeros_like(kbuf_s)
        vbuf_s[...] = jnp.zeros_like(vbuf_s)

    state = (sinks_ref, gmn_ref, kbuf_s, vbuf_s, c_s, m_s)
    out_w = (wmo_ref, wao_ref, wo_ref, gf_ref, y_ref)
    _interleave(_consume_pieces(set_a, first, *state, yma_s, yaa_s, tile),
                project(xa_ref, moda_ref, 2 * k + 1, set_b))
    next_a = project(xb_ref, modb_ref, jnp.minimum(2 * k + 2, n_tiles - 1), set_a)
    late = len(next_a) - TAIL_PIECES
    _interleave(_consume_pieces(set_b, False, *state, ymb_s, yab_s, tile),
                _output_pieces(set_a, xres_ref, slice(0, tile), moda_ref, yma_s, yaa_s, ua_s, *out_w) + next_a[:late])
    _interleave(_output_pieces(set_b, xres_ref, slice(tile, 2 * tile), moda_ref, ymb_s, yab_s, ub_s, *out_w),
                next_a[late:])

    @pl.when((k % steps_per_seq) == steps_per_seq - 1)
    def _final():
        for hd in range(M_HEADS):
            cf = c_s[hd]
            c_out_ref[0, hd] = cf[:, 0:M_DV]
            n_out_ref[0, hd:hd + 1, :] = jnp.transpose(cf[:, M_DV:2 * M_DV])[0:1, :]
        m_out_ref[0] = m_s[...]
        k_t = jnp.transpose(set_b["kf"][tile - WINDOW:tile, :])
        v_t = jnp.transpose(set_b["vf"][tile - WINDOW:tile, :])
        for kv in range(A_KV_HEADS):
            kk_ref[0, kv] = k_t[kv * A_HEAD_DIM:(kv + 1) * A_HEAD_DIM, :]
            vk_ref[0, kv] = v_t[kv * A_HEAD_DIM:(kv + 1) * A_HEAD_DIM, :]


def _rope_tables(first, count):
    half = ROT_DIM // 2
    dim = np.arange(LANES) % A_HEAD_DIM
    inv = ROPE_THETA ** (-(2.0 * (dim % half)) / ROT_DIM)
    ang = (first + np.arange(count, dtype=np.float64))[:, None] * inv[None, :]
    cos, sin = np.cos(ang), np.sin(ang)
    ra = np.where(dim < ROT_DIM, cos, 1.0)
    rb = np.where((dim >= half) & (dim < ROT_DIM), sin, 0.0)
    rc = np.where(dim < half, -sin, 0.0)
    return tuple(jnp.asarray(t, F32) for t in (ra, rb, rc))


def _full(shape):
    return pl.BlockSpec(shape, lambda *_: (0,) * len(shape))


def _prompt_layer(x, mod, sinks, gn, gf, gmn, bias_g, wp, wmo, wao, wo):
    bsz, seq, d = x.shape
    tile = PROMPT_TILE
    assert seq % (2 * tile) == 0 and tile % (2 * M_CHUNK) == 0 and d == D_MODEL
    tiles_per_seq = seq // tile
    steps_per_seq = tiles_per_seq // 2
    n_tiles = bsz * tiles_per_seq
    n_steps = n_tiles // 2
    ra, rb, rc = _rope_tables(0, seq)
    xt = x.reshape(n_tiles, tile, d)
    xp = x.reshape(n_steps, 2 * tile, d)
    mod3 = mod.reshape(bsz, 1, 3 * d)

    tile_0 = lambda k: 0
    tile_a = lambda k: 2 * k + 1
    tile_b = lambda k: jnp.minimum(2 * k + 2, n_tiles - 1)
    x_spec = lambda f: pl.BlockSpec((1, tile, d), lambda k: (f(k), 0, 0))
    mod_spec = lambda f: pl.BlockSpec((1, 1, 3 * d), lambda k: (f(k) // tiles_per_seq, 0, 0))
    in_specs = [
        pl.BlockSpec(memory_space=pltpu.SMEM),
        x_spec(tile_0), x_spec(tile_a), x_spec(tile_b),
        pl.BlockSpec((1, 2 * tile, d), lambda k: (k, 0, 0)),
        mod_spec(tile_0), mod_spec(tile_a), mod_spec(tile_b),
        _full((1, d)), _full((1, d)), _full((1, M_WIDTH)), _full((1, 2 * LANES)),
        pl.BlockSpec(memory_space=pl.ANY), pl.BlockSpec(memory_space=pl.ANY),
        pl.BlockSpec(memory_space=pl.ANY), pl.BlockSpec(memory_space=pl.ANY),
        _full((seq, LANES)), _full((seq, LANES)), _full((seq, LANES)),
    ]
    seq_of = lambda k: k // steps_per_seq
    out_specs = [
        pl.BlockSpec((1, 2 * tile, d), lambda k: (k, 0, 0)),
        pl.BlockSpec((1, M_HEADS, M_DK, M_DV), lambda k: (seq_of(k), 0, 0, 0)),
        pl.BlockSpec((1, M_HEADS, M_DK), lambda k: (seq_of(k), 0, 0)),
        pl.BlockSpec((1, 1, LANES), lambda k: (seq_of(k), 0, 0)),
        pl.BlockSpec((1, A_KV_HEADS, A_HEAD_DIM, WINDOW), lambda k: (seq_of(k), 0, 0, 0)),
        pl.BlockSpec((1, A_KV_HEADS, A_HEAD_DIM, WINDOW), lambda k: (seq_of(k), 0, 0, 0)),
    ]
    out_shape = [
        jax.ShapeDtypeStruct((n_steps, 2 * tile, d), F32),
        jax.ShapeDtypeStruct((bsz, M_HEADS, M_DK, M_DV), F32),
        jax.ShapeDtypeStruct((bsz, M_HEADS, M_DK), F32),
        jax.ShapeDtypeStruct((bsz, 1, LANES), F32),
        jax.ShapeDtypeStruct((bsz, A_KV_HEADS, A_HEAD_DIM, WINDOW), F32),
        jax.ShapeDtypeStruct((bsz, A_KV_HEADS, A_HEAD_DIM, WINDOW), F32),
    ]
    scratch = _set_shapes(tile) + _set_shapes(tile) + [
        pltpu.VMEM((2 * A_KV_HEADS, 2 * WINDOW, LANES), BF16),
        pltpu.VMEM((2 * A_KV_HEADS, 2 * WINDOW, LANES), BF16),
        pltpu.VMEM((M_HEADS, M_DK, 2 * LANES), F32),
        pltpu.VMEM((1, LANES), F32),
        pltpu.VMEM((tile, M_WIDTH), BF16),
        pltpu.VMEM((tile, A_WIDTH), BF16),
        pltpu.VMEM((tile, D_MODEL), BF16),
        pltpu.VMEM((tile, M_WIDTH), BF16),
        pltpu.VMEM((tile, A_WIDTH), BF16),
        pltpu.VMEM((tile, D_MODEL), BF16),
        pltpu.VMEM((d, N_PACK), BF16),
        pltpu.VMEM((M_WIDTH, d), BF16),
        pltpu.VMEM((A_WIDTH, d), BF16),
        pltpu.VMEM((d, d), BF16),
        pltpu.SemaphoreType.DMA((4,)),
    ]
    outs = pl.pallas_call(
        functools.partial(_prompt_kernel, tile=tile, steps_per_seq=steps_per_seq, n_tiles=n_tiles),
        grid=(n_steps,),
        in_specs=in_specs,
        out_specs=out_specs,
        out_shape=out_shape,
        scratch_shapes=scratch,
        compiler_params=pltpu.CompilerParams(dimension_semantics=("arbitrary",), vmem_limit_bytes=VMEM_LIMIT),
        name="prompt_layer",
    )(sinks, xt, xt, xt, xp, mod3, mod3, mod3, gn, gf, gmn, bias_g, wp, wmo, wao, wo,
      ra, rb, rc)
    return (outs[0].reshape(bsz, seq, d),) + tuple(outs[1:])


def _sample_proj_kernel(x_hbm, mod_ref, gn_ref, w_ref, z_ref, hb_s, x_s, x_sems):
    bsz, steps, _ = x_hbm.shape

    @pl.when(pl.program_id(0) == 0)
    def _norm():
        copies = [pltpu.make_async_copy(x_hbm.at[:, t, :], x_s.at[t], x_sems.at[t]) for t in range(steps)]
        for copy in copies:
            copy.start()
        mod = mod_ref[...]
        for t in range(steps):
            copies[t].wait()
            h = _rms(x_s[t]) * gn_ref[...]
            hb_s[t * bsz:(t + 1) * bsz, :] = (h * (1.0 + mod[:, D_MODEL:2 * D_MODEL]) + mod[:, 0:D_MODEL]).astype(BF16)

    z = _dot(hb_s[...], w_ref[...])
    for t in range(steps):
        z_ref[t] = z[t * bsz:(t + 1) * bsz, :]


def _sample_proj(x3, mod_s, gn, wp):
    bsz, steps, d = x3.shape
    tn = 1024
    return pl.pallas_call(
        _sample_proj_kernel,
        grid=(N_PACK // tn,),
        in_specs=[
            pl.BlockSpec(memory_space=pl.ANY),
            pl.BlockSpec((bsz, 2 * d), lambda j: (0, 0)),
            _full((1, d)),
            pl.BlockSpec((d, tn), lambda j: (0, j)),
        ],
        out_specs=pl.BlockSpec((steps, bsz, tn), lambda j: (0, 0, j)),
        out_shape=jax.ShapeDtypeStruct((steps, bsz, N_PACK), F32),
        scratch_shapes=[pltpu.VMEM((steps * bsz, d), BF16), pltpu.VMEM((steps, bsz, d), F32),
                        pltpu.SemaphoreType.DMA((steps,))],
        compiler_params=pltpu.CompilerParams(dimension_semantics=("arbitrary",)),
        name="sample_proj",
    )(x3, mod_s, gn, wp)


def _sample_state_kernel(sinks_ref, z_ref, c_ref, n_ref, m_ref, ck_ref, cv_ref, rope_ref, bg_ref, gmn_ref,
                         cn_ref, nn_ref, mn_ref, ym_ref, ya_ref, ko_ref, vo_ref, *, group, steps):
    T, G = steps, group
    toks = range(T)

    def z(t, lo, n):
        return z_ref[t, :, lo:lo + n]

    def rope_t(t, blk):
        return _rope(blk, rope_ref[0, t:t + 1, :], rope_ref[1, t:t + 1, :], rope_ref[2, t:t + 1, :])

    lane = lax.broadcasted_iota(jnp.int32, (G, LANES), 1)
    lo_half = lane < A_HEAD_DIM
    gmn = gmn_ref[...]

    q = [z(t, C_MQ, M_WIDTH) * (M_DK ** -0.5) for t in toks]
    k = [z(t, C_MK, M_WIDTH) for t in toks]
    v = [z(t, C_MV, M_WIDTH) for t in toks]
    og = [_sigmoid(z(t, C_MO, M_WIDTH)) * _silu(z(t, C_MZ, M_WIDTH)) for t in toks]
    qa = [[rope_t(t, z(t, C_AQ + j * LANES, LANES)) * (A_HEAD_DIM ** -0.5) for j in range(A_WIDTH // LANES)]
          for t in toks]
    kn = [rope_t(t, z(t, C_AK, A_KV_WIDTH)) for t in toks]
    vn = [z(t, C_AV, A_KV_WIDTH) for t in toks]
    sz = [_silu(z(t, C_AZ, A_WIDTH)) for t in toks]
    gt = [z(t, C_GI, 2 * LANES) + bg_ref[...] for t in toks]
    ig = [gt[t][:, 0:LANES] for t in toks]
    lf = [_log_sigmoid(gt[t][:, LANES:2 * LANES]) for t in toks]


    m_all = m_ref[...]
    rep = lambda a, h: jnp.broadcast_to(a[:, h:h + 1], (G, LANES))
    gates = []
    for hd in range(M_HEADS):
        ig_h = [rep(ig[t], hd) for t in toks]
        lf_h = [rep(lf[t], hd) for t in toks]
        m_old = rep(m_all, hd)
        bsum = [lf_h[0]]
        for t in range(1, T):
            bsum.append(bsum[-1] + lf_h[t])
        b_last = bsum[-1]
        d_end = [b_last - bsum[s] + ig_h[s] for s in toks]
        m_new = b_last + m_old
        for s in toks:
            m_new = jnp.maximum(m_new, d_end[s])
        mn_ref[:, hd:hd + 1] = m_new[:, 0:1]
        gd = dict(w_end=[jnp.exp(d_end[s] - m_new) for s in toks], decay=jnp.exp(b_last + m_old - m_new),
                  e=[], w_inter=[], emh=[])
        for t in toks:
            inter = bsum[t] + m_old
            dm = [bsum[t] - bsum[s] + ig_h[s] for s in range(t + 1)]
            m_hat = inter
            for s in range(t + 1):
                m_hat = jnp.maximum(m_hat, dm[s])
            gd["e"].append([jnp.exp(dm[s] - m_hat) for s in range(t + 1)])
            gd["w_inter"].append(jnp.exp(inter - m_hat))
            gd["emh"].append(jnp.exp(-m_hat))
        gates.append(gd)

    heads = [slice(hd * M_DK, (hd + 1) * M_DK) for hd in range(M_HEADS)]
    n_old = [n_ref[:, hd, :] for hd in range(M_HEADS)]
    qk = {(hd, t, s): jnp.sum(q[t][:, heads[hd]] * k[s][:, heads[hd]], axis=-1, keepdims=True)
          for hd in range(M_HEADS) for t in toks for s in range(t + 1)}
    qn = {(hd, t): jnp.sum(q[t][:, heads[hd]] * n_old[hd], axis=-1, keepdims=True)
          for hd in range(M_HEADS) for t in toks}

    lo_f = jnp.where(lo_half, 1.0, 0.0)
    hi_f = 1.0 - lo_f
    dup = lambda a, kv: (jnp.where(lo_half, a, pltpu.roll(a, A_HEAD_DIM, 1)) if kv == 0 else
                         jnp.where(lo_half, pltpu.roll(a, A_HEAD_DIM, 1), a))
    kn_x = [[dup(kn[s], kv) for s in toks] for kv in range(A_KV_HEADS)]
    vn_x = [[dup(vn[s], kv) for s in toks] for kv in range(A_KV_HEADS)]
    s_new = {}
    for kv in range(A_KV_HEADS):
        for blk in range(2):
            for t in toks:
                for s in range(t + 1):
                    prod = qa[t][2 * kv + blk] * kn_x[kv][s]
                    s_new[kv, 0, blk, t, s] = jnp.sum(prod * lo_f, axis=-1, keepdims=True)
                    s_new[kv, 1, blk, t, s] = jnp.sum(prod * hi_f, axis=-1, keepdims=True)

    rows_tg = lax.broadcasted_iota(jnp.int32, (T * G, LANES), 0) % G
    own_rows = [rows_tg == b for b in range(G)]
    rows64 = lax.broadcasted_iota(jnp.int32, (2 * T * G, LANES), 0) % G

    qc = []
    for hd in range(M_HEADS):
        hs = heads[hd]
        gd = gates[hd]
        q32 = jnp.concatenate([q[t][:, hs] for t in toks], axis=0)
        kw = [k[s][:, hs] * gd["w_end"][s] for s in toks]
        kw_t = jnp.transpose(jnp.concatenate(kw, axis=0)).astype(BF16)
        v32 = jnp.concatenate([v[s][:, hs] for s in toks], axis=0)
        acc = None
        for b in range(G):
            c_old = c_ref[b, hd]
            part = _dot(jnp.where(own_rows[b], q32, 0.0).astype(BF16), c_old.astype(BF16))
            acc = part if acc is None else acc + part
            upd = _dot(kw_t, jnp.where(own_rows[b], v32, 0.0).astype(BF16))
            cn_ref[b, hd] = gd["decay"][b:b + 1, 0:1] * c_old + upd
        qc.append(acc)
        nn_ref[:, hd, :] = gd["decay"] * n_old[hd] + (kw[0] + kw[1] + kw[2] + kw[3])

    zeros_kt = jnp.zeros((A_HEAD_DIM, WINDOW), BF16)
    sc = {}
    for kv in range(A_KV_HEADS):
        l64 = jnp.concatenate([qa[t][2 * kv + blk] for blk in range(2) for t in toks], axis=0)
        own64 = [jnp.where(rows64 == b, l64, 0.0).astype(BF16) for b in range(G)]
        kt = [ck_ref[b, kv].astype(BF16) for b in range(G)]
        for par in range(2):
            acc = None
            for b in range(G):
                rhs = jnp.concatenate([kt[b], zeros_kt] if par == 0 else [zeros_kt, kt[b]], axis=0)
                part = _dot(own64[b], rhs)
                acc = part if acc is None else acc + part
            sc[kv, par] = acc

    hh = {}
    for hd in range(M_HEADS):
        gd = gates[hd]
        for t in toks:
            sv, ssum = None, None
            for s in range(t + 1):
                s_ts = qk[hd, t, s] * gd["e"][t][s]
                sv = s_ts * v[s][:, heads[hd]] if sv is None else sv + s_ts * v[s][:, heads[hd]]
                ssum = s_ts if ssum is None else ssum + s_ts
            wi = gd["w_inter"][t]
            num = sv + wi * qc[hd][t * G:(t + 1) * G, :]
            den = ssum + wi * qn[hd, t]
            hh[hd, t] = num / jnp.maximum(jnp.abs(den), gd["emh"][t])
    ms = {key: jnp.mean(val * val, axis=-1, keepdims=True) for key, val in hh.items()}
    for (hd, t), val in hh.items():
        hs = heads[hd]
        ym_ref[t, :, hs] = val * lax.rsqrt(ms[hd, t] + NORM_EPS) * gmn[:, hs] * og[t][:, hs]

    items = [(kv, par, blk, t) for kv in range(A_KV_HEADS) for par in range(2) for blk in range(2) for t in toks]
    s_c = {it: jnp.where(lane > it[3], sc[it[0], it[1]][(it[2] * T + it[3]) * G:(it[2] * T + it[3] + 1) * G, :], -jnp.inf)
           for it in items}
    mx_c = {it: jnp.max(s_c[it], axis=-1, keepdims=True) for it in items}
    p_c, p_n, sink_t = {}, {}, {}
    for it in items:
        kv, par, blk, t = it
        sink = sinks_ref[A_GROUP * kv + 2 * blk + par]
        mx = jnp.maximum(mx_c[it], sink)
        for s in range(t + 1):
            mx = jnp.maximum(mx, s_new[kv, par, blk, t, s])
        p_c[it] = jnp.exp(s_c[it] - mx)
        p_n[it] = [jnp.exp(s_new[kv, par, blk, t, s] - mx) for s in range(t + 1)]
        sink_t[it] = jnp.exp(sink - mx)
    sum_c = {it: jnp.sum(p_c[it], axis=-1, keepdims=True) for it in items}
    fresh = {}
    probs = {(kv, par): [] for kv in range(A_KV_HEADS) for par in range(2)}
    for it in items:
        kv, par, blk, t = it
        den = sum_c[it] + sink_t[it]
        for s in range(t + 1):
            den = den + p_n[it][s]
        r = 1.0 / den
        probs[kv, par].append(p_c[it] * r)
        acc = None
        for s in range(t + 1):
            term = (p_n[it][s] * r) * vn_x[kv][s]
            acc = term if acc is None else acc + term
        fresh[it] = acc

    for kv in range(A_KV_HEADS):
        vt = [cv_ref[b, kv].astype(BF16) for b in range(G)]
        o = []
        for par in range(2):
            pstack = jnp.concatenate(probs[kv, par], axis=0)
            acc = None
            for b in range(G):
                rhs = jnp.concatenate([vt[b], zeros_kt] if par == 0 else [zeros_kt, vt[b]], axis=0)
                part = _dot_nt(jnp.where(rows64 == b, pstack, 0.0).astype(BF16), rhs)
                acc = part if acc is None else acc + part
            o.append(acc)
        for blk in range(2):
            cs = slice((2 * kv + blk) * LANES, (2 * kv + blk + 1) * LANES)
            for t in toks:
                rs = slice((blk * T + t) * G, (blk * T + t + 1) * G)
                new_v = jnp.where(lo_half, fresh[kv, 0, blk, t], fresh[kv, 1, blk, t])
                ya_ref[t, :, cs] = (o[0][rs, :] + o[1][rs, :] + new_v) * sz[t][:, cs]

    n_rows = T * G
    src = lax.broadcasted_iota(jnp.int32, (n_rows, n_rows), 1)
    dst = lax.broadcasted_iota(jnp.int32, (n_rows, n_rows), 0)
    perm = jnp.where(src == (dst % T) * G + dst // T, 1.0, 0.0).astype(BF16)
    lane_d = lax.broadcasted_iota(jnp.int32, (A_HEAD_DIM, WINDOW), 1)
    for new, cache_ref, out_ref in ((kn, ck_ref, ko_ref), (vn, cv_ref, vo_ref)):
        stack = _dot_exact01(perm, jnp.concatenate(new, axis=0))
        stack = jnp.concatenate([stack, jnp.zeros((LANES - n_rows, LANES), F32)], axis=0)
        new_t = jnp.transpose(stack)
        for b in range(G):
            placed = pltpu.roll(new_t, (WINDOW - T - T * b) % LANES, 1)
            for kv in range(A_KV_HEADS):
                slid = pltpu.roll(cache_ref[b, kv], WINDOW - T, 1)
                out_ref[b, kv] = jnp.where(lane_d >= WINDOW - T, placed[kv * A_HEAD_DIM:(kv + 1) * A_HEAD_DIM, :], slid)


def _sample_state(zs, state_c, state_n, state_m, ck_t, cv_t, rope, bias_g, gmn, sinks):
    steps, bsz, _ = zs.shape
    group = SAMPLE_GROUP
    assert bsz % group == 0
    in_specs = [
        pl.BlockSpec(memory_space=pltpu.SMEM),
        pl.BlockSpec((steps, group, N_PACK), lambda i: (0, i, 0)),
        pl.BlockSpec((group, M_HEADS, M_DK, M_DV), lambda i: (i, 0, 0, 0)),
        pl.BlockSpec((group, M_HEADS, M_DK), lambda i: (i, 0, 0)),
        pl.BlockSpec((group, M_HEADS), lambda i: (i, 0)),
        pl.BlockSpec((group, A_KV_HEADS, A_HEAD_DIM, WINDOW), lambda i: (i, 0, 0, 0)),
        pl.BlockSpec((group, A_KV_HEADS, A_HEAD_DIM, WINDOW), lambda i: (i, 0, 0, 0)),
        _full((3, steps, LANES)), _full((1, 2 * LANES)), _full((1, M_WIDTH)),
    ]
    out_specs = [
        pl.BlockSpec((group, M_HEADS, M_DK, M_DV), lambda i: (i, 0, 0, 0)),
        pl.BlockSpec((group, M_HEADS, M_DK), lambda i: (i, 0, 0)),
        pl.BlockSpec((group, M_HEADS), lambda i: (i, 0)),
        pl.BlockSpec((steps, group, M_WIDTH), lambda i: (0, i, 0)),
        pl.BlockSpec((steps, group, A_WIDTH), lambda i: (0, i, 0)),
        pl.BlockSpec((group, A_KV_HEADS, A_HEAD_DIM, WINDOW), lambda i: (i, 0, 0, 0)),
        pl.BlockSpec((group, A_KV_HEADS, A_HEAD_DIM, WINDOW), lambda i: (i, 0, 0, 0)),
    ]
    out_shape = [
        jax.ShapeDtypeStruct((bsz, M_HEADS, M_DK, M_DV), F32),
        jax.ShapeDtypeStruct((bsz, M_HEADS, M_DK), F32),
        jax.ShapeDtypeStruct((bsz, M_HEADS), F32),
        jax.ShapeDtypeStruct((steps, bsz, M_WIDTH), F32),
        jax.ShapeDtypeStruct((steps, bsz, A_WIDTH), F32),
        jax.ShapeDtypeStruct((bsz, A_KV_HEADS, A_HEAD_DIM, WINDOW), F32),
        jax.ShapeDtypeStruct((bsz, A_KV_HEADS, A_HEAD_DIM, WINDOW), F32),
    ]
    return pl.pallas_call(
        functools.partial(_sample_state_kernel, group=group, steps=steps),
        grid=(bsz // group,),
        in_specs=in_specs,
        out_specs=out_specs,
        out_shape=out_shape,
        compiler_params=pltpu.CompilerParams(dimension_semantics=("arbitrary",), vmem_limit_bytes=VMEM_LIMIT),
        name="sample_state",
    )(sinks, zs, state_c, state_n, state_m, ck_t, cv_t, rope, bias_g, gmn)


def _sample_out_kernel(x_ref, gate_ref, z_ref, ym_ref, ya_ref, wmo_ref, wao_ref, wo_ref, gf_ref, y_ref):
    bsz, steps, _ = x_ref.shape
    stack = lambda ref, cs: jnp.concatenate([ref[t, :, cs] for t in range(steps)], axis=0)
    full = slice(None)
    x = jnp.concatenate([x_ref[:, t, :] for t in range(steps)], axis=0)
    gate = jnp.concatenate([gate_ref[...]] * steps, axis=0)
    y = _out_stage(x, gate, _sigmoid(stack(z_ref, slice(0, D_MODEL))), _sigmoid(stack(z_ref, slice(D_MODEL, 2 * D_MODEL))),
                   stack(ym_ref, full).astype(BF16), stack(ya_ref, full).astype(BF16), wmo_ref, wao_ref, wo_ref, gf_ref[...])
    for t in range(steps):
        y_ref[:, t, :] = y[t * bsz:(t + 1) * bsz, :]


def _sample_out(x3, mod_s, zs, ym, ya, wmo, wao, wo, gf):
    bsz, steps, d = x3.shape
    return pl.pallas_call(
        _sample_out_kernel,
        grid=(1,),
        in_specs=[
            _full((bsz, steps, d)),
            pl.BlockSpec((bsz, d), lambda i: (0, 2)),
            pl.BlockSpec((steps, bsz, 2 * d), lambda i: (0, 0, 0)),
            _full((steps, bsz, M_WIDTH)), _full((steps, bsz, A_WIDTH)),
            _full((M_WIDTH, d)), _full((A_WIDTH, d)), _full((d, d)), _full((1, d)),
        ],
        out_specs=_full((bsz, steps, d)),
        out_shape=jax.ShapeDtypeStruct((bsz, steps, d), F32),
        compiler_params=pltpu.CompilerParams(dimension_semantics=("arbitrary",), vmem_limit_bytes=VMEM_LIMIT),
        name="sample_out",
    )(x3, mod_s, zs, ym, ya, wmo, wao, wo, gf)


PACK_BLOCK = 512


def _pack_sources():
    names = ("mq", "mk", "mv", "mi", "mf", "mo", "mz", "aq", "ak", "av", "az", "gm", "ga")
    start, pos = {}, 0
    for name, size in zip(names, IN_SIZES):
        start[name] = pos
        pos += size
    assert start["av"] == start["ak"] + A_KV_WIDTH and start["mf"] == start["mi"] + M_HEADS
    order = (("gm", C_GM, D_MODEL), ("ga", C_GA, D_MODEL), ("mq", C_MQ, M_WIDTH), ("mk", C_MK, M_WIDTH),
             ("mv", C_MV, M_WIDTH), ("mo", C_MO, M_WIDTH), ("mz", C_MZ, M_WIDTH), ("aq", C_AQ, A_WIDTH),
             ("az", C_AZ, A_WIDTH))
    src = []
    for name, col, width in order:
        assert col == len(src) * PACK_BLOCK and width % PACK_BLOCK == 0
        src += [start[name] + i * PACK_BLOCK for i in range(width // PACK_BLOCK)]
    assert C_AK == len(src) * PACK_BLOCK and C_AV == C_AK + A_KV_WIDTH and C_GI == C_AV + A_KV_WIDTH
    assert C_GF == C_GI + LANES and N_PACK == C_AK + PACK_BLOCK and start["ak"] + PACK_BLOCK <= pos
    return src + [start["ak"]], start["mi"]


def _pack_kernel(src_ref, wt_ref, gate_ref, o_ref):
    j = pl.program_id(0)
    last = pl.num_programs(0) - 1

    @pl.when(j < last)
    def _block():
        o_ref[...] = jnp.transpose(wt_ref[...]).astype(BF16)

    @pl.when(j == last)
    def _kv_and_gates():
        kv = 2 * A_KV_WIDTH
        o_ref[:, 0:kv] = jnp.transpose(wt_ref[0:kv, :]).astype(BF16)
        g = gate_ref[...]
        pad = jnp.zeros((LANES - M_HEADS, g.shape[1]), F32)
        rows = jnp.concatenate([g[0:M_HEADS], pad, g[M_HEADS:2 * M_HEADS], pad], axis=0)
        o_ref[:, kv:kv + 2 * LANES] = jnp.transpose(rows).astype(BF16)


def _pack_w_in(w_in):
    d = w_in.shape[0]
    src, gate_row = _pack_sources()
    wt = jnp.transpose(w_in)
    return pl.pallas_call(
        _pack_kernel,
        grid_spec=pltpu.PrefetchScalarGridSpec(
            num_scalar_prefetch=1, grid=(N_PACK // PACK_BLOCK,),
            in_specs=[pl.BlockSpec((pl.Element(PACK_BLOCK), pl.Element(d)),
                                   lambda j, src_ref: (pl.multiple_of(src_ref[j], 8), 0)),
                      pl.BlockSpec((pl.Element(2 * M_HEADS), pl.Element(d)), lambda j, src_ref: (gate_row, 0))],
            out_specs=pl.BlockSpec((d, PACK_BLOCK), lambda j, src_ref: (0, j))),
        out_shape=jax.ShapeDtypeStruct((d, N_PACK), BF16),
        compiler_params=pltpu.CompilerParams(dimension_semantics=("arbitrary",)),
        name="pack_w_in",
    )(jnp.asarray(src, jnp.int32), wt, wt)


def kernel(x_prompt, x_sample, state_C, state_n, state_m, cache_k, cache_v, c_prompt, c_sample, w_ada, b_ada, g_norm, w_in, b_igate, b_fgate, g_mnorm, sinks, w_m_out, w_a_out, w_out, g_final):
    assert w_in.shape[0] == 1, "single-layer step"
    bsz_s, steps, d = x_sample.shape
    assert cache_k.shape[2] == WINDOW

    mod_p, mod_s = _adaln(c_prompt, c_sample, w_ada[0], b_ada[0])
    wp = _pack_w_in(w_in[0])
    wmo = w_m_out[0].astype(BF16)
    wao = w_a_out[0].astype(BF16)
    wo = w_out[0].astype(BF16)
    gn = g_norm[0].reshape(1, d)
    gf = g_final.reshape(1, d)
    gmn = g_mnorm[0].reshape(1, M_WIDTH)
    zpad = jnp.zeros((LANES - M_HEADS,), F32)
    bias_g = jnp.concatenate([b_igate[0], zpad, b_fgate[0], zpad]).reshape(1, 2 * LANES)
    sk = sinks[0]

    y_p, c_p, n_p, m_p, k_p, v_p = _prompt_layer(x_prompt, mod_p, sk, gn, gf, gmn, bias_g, wp, wmo, wao, wo)
    m_p = m_p[:, 0, 0:M_HEADS]

    zs = _sample_proj(x_sample, mod_s, gn, wp)
    rope = jnp.stack(_rope_tables(PAST_LEN, steps))
    dims_keys = lambda c: jnp.transpose(c[0], (0, 2, 3, 1))
    c_s, n_s, m_s, ym, ya, k_t, v_t = _sample_state(zs, state_C[0], state_n[0], state_m[0], dims_keys(cache_k),
                                                    dims_keys(cache_v), rope, bias_g, gmn, sk)
    y_s = _sample_out(x_sample, mod_s, zs, ym, ya, wmo, wao, wo, gf)
    keys_dims = lambda c: jnp.transpose(c, (0, 3, 1, 2))[None]

    return (y_p, y_s, c_p[None], n_p[None], m_p[None], keys_dims(k_p), keys_dims(v_p),
            c_s[None], n_s[None], m_s[None], keys_dims(k_t), keys_dims(v_t))
```

```python
import functools

import jax
import jax.numpy as jnp
import numpy as np
from jax import lax
from jax.experimental import pallas as pl
from jax.experimental.pallas import tpu as pltpu

F32 = jnp.float32
BF16 = jnp.bfloat16

D_MODEL = 1024
M_HEADS = 4
M_DK = 128
M_DV = 128
M_WIDTH = M_HEADS * M_DV
M_CHUNK = 128
A_HEADS = 8
A_KV_HEADS = 2
A_GROUP = A_HEADS // A_KV_HEADS
A_HEAD_DIM = 64
A_WIDTH = A_HEADS * A_HEAD_DIM
A_KV_WIDTH = A_KV_HEADS * A_HEAD_DIM
WINDOW = 128
ROT_DIM = A_HEAD_DIM // 4
ROPE_THETA = 500000.0
NORM_EPS = 1e-6
PAST_LEN = 16384
IN_SIZES = (M_HEADS * M_DK, M_HEADS * M_DK, M_WIDTH, M_HEADS, M_HEADS, M_WIDTH, M_WIDTH,
            A_WIDTH, A_KV_WIDTH, A_KV_WIDTH, A_WIDTH, D_MODEL, D_MODEL)

LANES = 128

C_GM = 0
C_GA = C_GM + D_MODEL
C_MQ = C_GA + D_MODEL
C_MK = C_MQ + M_WIDTH
C_MV = C_MK + M_WIDTH
C_MO = C_MV + M_WIDTH
C_MZ = C_MO + M_WIDTH
C_AQ = C_MZ + M_WIDTH
C_AZ = C_AQ + A_WIDTH
C_AK = C_AZ + A_WIDTH
C_AV = C_AK + A_KV_WIDTH
C_GI = C_AV + A_KV_WIDTH
C_GF = C_GI + LANES
N_PACK = C_GF + LANES

PROMPT_TILE = 256
SAMPLE_GROUP = 16
STATE_RING = 3
VMEM_LIMIT = 56 * 1024 * 1024


def _sigmoid(x):
    return 1.0 / (1.0 + jnp.exp(-x))


def _silu(x):
    return x * _sigmoid(x)


def _log_sigmoid(x):
    return jnp.minimum(x, 0.0) - jnp.log1p(jnp.exp(-jnp.abs(x)))


def _dot(a, b):
    return jnp.dot(a, b, preferred_element_type=F32)


def _dot_nt(a, b):
    return lax.dot_general(a, b, (((1,), (1,)), ((), ())), preferred_element_type=F32)


def _dot_tn(a, b):
    return lax.dot_general(a, b, (((0,), (0,)), ((), ())), preferred_element_type=F32)


def _dot_exact01(m01, x):
    x1 = x.astype(BF16)
    r1 = x - x1.astype(F32)
    x2 = r1.astype(BF16)
    x3 = (r1 - x2.astype(F32)).astype(BF16)
    return _dot(m01, x1) + _dot(m01, x2) + _dot(m01, x3)


def _rms(x):
    return x * lax.rsqrt(jnp.mean(x * x, axis=-1, keepdims=True) + NORM_EPS)


def _rope(blk, ra, rb, rc):
    return blk * ra + pltpu.roll(blk, 8, 1) * rb + pltpu.roll(blk, LANES - 8, 1) * rc


def _out_stage(x, gate, sgm, sga, ym, ya, wmo_ref, wao_ref, wo_ref, gf):
    pm = _dot(ym, wmo_ref[...])
    pa = _dot(ya, wao_ref[...])
    u = sgm * pm + sga * pa
    r = _dot(u.astype(BF16), wo_ref[...])
    return _rms(x + gate * r) * gf


def _adaln_kernel(cp_ref, cs_ref, w_ref, b_ref, op_ref, os_ref):
    w = w_ref[...].astype(BF16)
    b = b_ref[...]
    op_ref[...] = _dot(_silu(cp_ref[...]).astype(BF16), w) + b
    os_ref[...] = _dot(_silu(cs_ref[...]).astype(BF16), w) + b


def _adaln(c_p, c_s, w_ada, b_ada):
    bp, d = c_p.shape
    bs = c_s.shape[0]
    n = w_ada.shape[1]
    tn = 1024
    return pl.pallas_call(
        _adaln_kernel,
        grid=(n // tn,),
        in_specs=[
            pl.BlockSpec((bp, d), lambda j: (0, 0)),
            pl.BlockSpec((bs, d), lambda j: (0, 0)),
            pl.BlockSpec((d, tn), lambda j: (0, j)),
            pl.BlockSpec((1, tn), lambda j: (0, j)),
        ],
        out_specs=[
            pl.BlockSpec((bp, tn), lambda j: (0, j)),
            pl.BlockSpec((bs, tn), lambda j: (0, j)),
        ],
        out_shape=[jax.ShapeDtypeStruct((bp, n), F32), jax.ShapeDtypeStruct((bs, n), F32)],
        compiler_params=pltpu.CompilerParams(dimension_semantics=("arbitrary",)),
        name="adaln",
    )(c_p, c_s, w_ada, b_ada.reshape(1, n))


_SET_FIELDS = ("hb", "q", "k", "vaug", "og", "qa", "kf", "vf", "sz", "sgm", "sga", "g")


def _set_shapes(tile):
    return [
        pltpu.VMEM((tile, D_MODEL), BF16),
        pltpu.VMEM((tile, M_WIDTH), BF16),
        pltpu.VMEM((tile, M_WIDTH), F32),
        pltpu.VMEM((M_HEADS, tile, 2 * LANES), BF16),
        pltpu.VMEM((tile, M_WIDTH), F32),
        pltpu.VMEM((tile, A_WIDTH), BF16),
        pltpu.VMEM((tile, A_KV_WIDTH), F32),
        pltpu.VMEM((tile, A_KV_WIDTH), F32),
        pltpu.VMEM((tile, A_WIDTH), F32),
        pltpu.VMEM((tile, D_MODEL), F32),
        pltpu.VMEM((tile, D_MODEL), F32),
        pltpu.VMEM((tile, 2 * LANES), F32),
    ]


N_TILE = 256


def _project_pieces(x_ref, mod_ref, rope_refs, st, gn_ref, bg_ref, w_ref):
    def norm():
        mod = mod_ref[0]
        h = _rms(x_ref[0]) * gn_ref[...]
        st["hb"][...] = (h * (1.0 + mod[:, D_MODEL:2 * D_MODEL]) + mod[:, 0:D_MODEL]).astype(BF16)

    def proj(lo):
        return _dot(st["hb"][...], w_ref[:, lo:lo + N_TILE])

    def rope(v):
        ra_ref, rb_ref, rc_ref = rope_refs
        return _rope(v, ra_ref[...], rb_ref[...], rc_ref[...])

    def cols(i):
        return slice(i * N_TILE, (i + 1) * N_TILE)

    def gates():
        st["g"][...] = proj(C_GI) + bg_ref[...]

    def queries(i):
        def piece():
            st["q"][:, cols(i)] = (proj(C_MQ + i * N_TILE) * (M_DK ** -0.5)).astype(BF16)
        return piece

    def keys(i):
        def piece():
            st["k"][:, cols(i)] = proj(C_MK + i * N_TILE)
        return piece

    def values(i):
        def piece():
            zv = proj(C_MV + i * N_TILE)
            for j in range(N_TILE // M_DV):
                st["vaug"][i * (N_TILE // M_DV) + j, :, 0:LANES] = zv[:, j * M_DV:(j + 1) * M_DV].astype(BF16)
        return piece

    def attn_kv():
        z = proj(C_AK)
        st["kf"][...] = rope(z[:, 0:A_KV_WIDTH])
        st["vf"][...] = z[:, A_KV_WIDTH:2 * A_KV_WIDTH]

    def attn_q(i):
        def piece():
            za = proj(C_AQ + i * N_TILE)
            for j in range(N_TILE // LANES):
                blk = rope(za[:, j * LANES:(j + 1) * LANES])
                lo = i * N_TILE + j * LANES
                st["qa"][:, lo:lo + LANES] = (blk * (A_HEAD_DIM ** -0.5)).astype(BF16)
        return piece

    def out_gate(i):
        def piece():
            st["og"][:, cols(i)] = _sigmoid(proj(C_MO + i * N_TILE)) * _silu(proj(C_MZ + i * N_TILE))
        return piece

    def attn_gate(i):
        def piece():
            st["sz"][:, cols(i)] = _silu(proj(C_AZ + i * N_TILE))
        return piece

    def merge_gate(name, lo, i):
        def piece():
            st[name][:, cols(i)] = _sigmoid(proj(lo + i * N_TILE))
        return piece

    assert C_AV == C_AK + A_KV_WIDTH and 2 * A_KV_WIDTH == N_TILE and C_GF == C_GI + LANES
    half, full = range(M_WIDTH // N_TILE), range(D_MODEL // N_TILE)
    pieces = [norm, gates, attn_kv]
    for i in half:
        pieces += [queries(i), keys(i), values(i), attn_q(i)]
    for i in half:
        pieces += [out_gate(i), attn_gate(i)]
    for i in full:
        pieces += [merge_gate("sgm", C_GM, i), merge_gate("sga", C_GA, i)]
    return pieces


def _consume_pieces(st, first_tile, sinks_ref, gmn_ref, kbuf_s, vbuf_s, c_s, m_s, ym_s, ya_s, tile):
    L = M_CHUNK
    heads = [slice(hd * M_DK, (hd + 1) * M_DK) for hd in range(M_HEADS)]
    a_heads = [(kv, bi, half) for kv in range(A_KV_HEADS) for bi in range(2) for half in range(2)]

    def causal_mask():
        row = lax.broadcasted_iota(jnp.int32, (L, L), 0)
        col = lax.broadcasted_iota(jnp.int32, (L, L), 1)
        return row >= col

    def chunk_stages(j):
        rs = slice(j * L, (j + 1) * L)
        cx = {}

        def gates():
            g = st["g"][rs, :]
            ig_all = g[:, 0:LANES]
            lf_all = _log_sigmoid(g[:, LANES:2 * LANES])
            tril = jnp.where(causal_mask(), 1.0, 0.0).astype(BF16)
            b_all = _dot_exact01(tril, lf_all)
            b_last = b_all[L - 1:L, :]
            m_old = m_s[...]
            d_end = b_last - b_all + ig_all
            m_new = jnp.maximum(b_last + m_old, jnp.max(d_end, axis=0, keepdims=True))
            m_s[...] = m_new
            cx.update(b_all=b_all, w_end=jnp.exp(d_end - m_new), decay=jnp.exp(b_last + m_old - m_new),
                      inter=b_all + m_old, r_t=jnp.transpose(ig_all - b_all))

        def kv_buffers():
            srows = slice((j % 2) * L, (j % 2 + 1) * L)
            lo_half = lax.broadcasted_iota(jnp.int32, (L, L), 1) < A_HEAD_DIM
            kc = st["kf"][rs, :]
            vc = st["vf"][rs, :]
            kr = pltpu.roll(kc, A_HEAD_DIM, 1)
            vr = pltpu.roll(vc, A_HEAD_DIM, 1)
            kbuf_s[0, srows, :] = jnp.where(lo_half, kc, 0.0).astype(BF16)
            kbuf_s[1, srows, :] = jnp.where(lo_half, 0.0, kr).astype(BF16)
            kbuf_s[2, srows, :] = jnp.where(lo_half, kr, 0.0).astype(BF16)
            kbuf_s[3, srows, :] = jnp.where(lo_half, 0.0, kc).astype(BF16)
            lane = lax.broadcasted_iota(jnp.int32, (L, L), 1)
            ones_lo = jnp.where(lane == 0, 1.0, 0.0)
            ones_hi = jnp.where(lane == A_HEAD_DIM, 1.0, 0.0)
            vbuf_s[0, srows, :] = jnp.where(lo_half, vc, ones_hi).astype(BF16)
            vbuf_s[1, srows, :] = jnp.where(lo_half, ones_lo, vr).astype(BF16)
            vbuf_s[2, srows, :] = jnp.where(lo_half, vr, ones_hi).astype(BF16)
            vbuf_s[3, srows, :] = jnp.where(lo_half, ones_lo, vc).astype(BF16)

        def decay_matrices():
            cx["dmat"] = [jnp.where(causal_mask(), cx["b_all"][:, hd:hd + 1] + cx["r_t"][hd:hd + 1, :], -jnp.inf)
                          for hd in range(M_HEADS)]
            cx["m_hat"] = [jnp.maximum(cx["inter"][:, hd:hd + 1], jnp.max(cx["dmat"][hd], axis=-1, keepdims=True))
                           for hd in range(M_HEADS)]
            cx["kw"] = [(st["k"][rs, heads[hd]] * cx["w_end"][:, hd:hd + 1]).astype(BF16) for hd in range(M_HEADS)]

        def state_matmuls():
            cx["qk"], cx["qc"] = [], []
            for hd in range(M_HEADS):
                qh = st["q"][rs, heads[hd]]
                c_old = c_s[hd]
                cx["qk"].append(_dot_nt(qh, st["k"][rs, heads[hd]].astype(BF16)))
                cx["qc"].append(_dot(qh, c_old.astype(BF16)))
                va = st["vaug"][hd, rs, :]
                c_s[hd] = cx["decay"][:, hd:hd + 1] * c_old + _dot_tn(cx["kw"][hd], va)

        def attn_scores():
            row2 = lax.broadcasted_iota(jnp.int32, (L, 2 * L), 0)
            col2 = lax.broadcasted_iota(jnp.int32, (L, 2 * L), 1)
            no_prev = jnp.where(first_tile, 4 * L, 0) if j == 0 else 0
            if j % 2 == 0:
                mask = ((col2 < L) & (col2 <= row2)) | ((col2 >= L) & (col2 - L > row2 + no_prev))
            else:
                mask = ((col2 >= L) & (col2 - L <= row2)) | ((col2 < L) & (col2 > row2 + no_prev))
            cx["sh"] = {}
            for kv in range(A_KV_HEADS):
                qcat = jnp.concatenate([st["qa"][rs, (2 * kv) * LANES:(2 * kv + 1) * LANES],
                                        st["qa"][rs, (2 * kv + 1) * LANES:(2 * kv + 2) * LANES]], axis=0)
                for half in range(2):
                    sc = _dot_nt(qcat, kbuf_s[2 * kv + half])
                    for bi in range(2):
                        cx["sh"][kv, bi, half] = jnp.where(mask, sc[bi * L:(bi + 1) * L, :], -jnp.inf)

        def attn_max():
            cx["mx"] = {hk: jnp.maximum(jnp.max(cx["sh"][hk], axis=-1, keepdims=True),
                                        sinks_ref[A_GROUP * hk[0] + 2 * hk[1] + hk[2]]) for hk in a_heads}

        def stabilised_scores():
            cx["s"] = [(cx["qk"][hd] * jnp.exp(cx["dmat"][hd] - cx["m_hat"][hd])).astype(BF16)
                       for hd in range(M_HEADS)]

        def score_value_matmuls():
            cx["sv"] = [_dot(cx["s"][hd], st["vaug"][hd, rs, :]) for hd in range(M_HEADS)]

        def attn_exp():
            cx["p"] = {hk: jnp.exp(cx["sh"][hk] - cx["mx"][hk]).astype(BF16) for hk in a_heads}
            cx["sink"] = {hk: jnp.exp(sinks_ref[A_GROUP * hk[0] + 2 * hk[1] + hk[2]] - cx["mx"][hk]) for hk in a_heads}

        def attn_wait():
            pass

        def attn_value_matmuls():
            lo_half = lax.broadcasted_iota(jnp.int32, (L, L), 1) < A_HEAD_DIM
            for kv in range(A_KV_HEADS):
                for bi in range(2):
                    o_lo = _dot(cx["p"][kv, bi, 0], vbuf_s[2 * kv])
                    o_hi = _dot(cx["p"][kv, bi, 1], vbuf_s[2 * kv + 1])
                    den_lo = o_lo[:, A_HEAD_DIM:A_HEAD_DIM + 1] + cx["sink"][kv, bi, 0]
                    den_hi = o_hi[:, 0:1] + cx["sink"][kv, bi, 1]
                    o = jnp.where(lo_half, o_lo / den_lo, o_hi / den_hi)
                    cs = slice((2 * kv + bi) * LANES, (2 * kv + bi + 1) * LANES)
                    ya_s[rs, cs] = (o * st["sz"][rs, cs]).astype(BF16)

        def mlstm_outputs():
            hh = []
            for hd in range(M_HEADS):
                m_hat = cx["m_hat"][hd]
                w_inter = jnp.exp(cx["inter"][:, hd:hd + 1] - m_hat)
                tot = cx["sv"][hd] + w_inter * cx["qc"][hd]
                hh.append(tot[:, 0:M_DV] / jnp.maximum(jnp.abs(tot[:, M_DV:M_DV + 1]), jnp.exp(-m_hat)))
            ms = [jnp.mean(h * h, axis=-1, keepdims=True) for h in hh]
            for hd in range(M_HEADS):
                hs = heads[hd]
                ym_s[rs, hs] = (hh[hd] * lax.rsqrt(ms[hd] + NORM_EPS) * gmn_ref[:, hs] * st["og"][rs, hs]).astype(BF16)

        mlstm = [gates, decay_matrices, state_matmuls, stabilised_scores, score_value_matmuls, mlstm_outputs]
        attn = [kv_buffers, attn_scores, attn_max, attn_exp, attn_wait, attn_value_matmuls]
        return mlstm, attn

    assert tile == 2 * L
    (m0, a0), (m1, a1) = chunk_stages(0), chunk_stages(1)
    merged = []
    for m_stage, a_stage in zip(m0 + m1, a0 + a1):
        merged += [m_stage, a_stage]
    return merged


def _output_pieces(st, x_ref, rows, gate_ref, ym_s, ya_s, u_s, wmo_ref, wao_ref, wo_ref, gf_ref, y_ref):
    def merge():
        u = st["sgm"][...] * _dot(ym_s[...], wmo_ref[...]) + st["sga"][...] * _dot(ya_s[...], wao_ref[...])
        u_s[...] = u.astype(BF16)

    def out():
        gate = gate_ref[0][:, 2 * D_MODEL:3 * D_MODEL]
        r = _dot(u_s[...], wo_ref[...])
        y_ref[0, rows, :] = _rms(x_ref[0, rows, :] + gate * r) * gf_ref[...]

    return [merge, out]


def _interleave(primary, filler):
    done = 0
    for i, piece in enumerate(primary):
        piece()
        upto = ((i + 1) * len(filler)) // len(primary)
        for f in filler[done:upto]:
            f()
        done = upto


def _prompt_kernel(sinks_ref, x0_ref, xa_ref, xb_ref, xres_ref, mod0_ref, moda_ref, modb_ref,
                   gn_ref, gf_ref, gmn_ref, bg_ref, w_hbm, wmo_hbm, wao_hbm, wo_hbm,
                   ra_ref, rb_ref, rc_ref,
                   y_ref, c_out_ref, n_out_ref, m_out_ref, kk_ref, vk_ref, *scratch, tile, steps_per_seq, n_tiles):
    nset = len(_SET_FIELDS)
    set_a = dict(zip(_SET_FIELDS, scratch[0:nset]))
    set_b = dict(zip(_SET_FIELDS, scratch[nset:2 * nset]))
    (kbuf_s, vbuf_s, c_s, m_s, yma_s, yaa_s, ua_s, ymb_s, yab_s, ub_s,
     w_ref, wmo_ref, wao_ref, wo_ref, w_sems) = scratch[2 * nset:]
    k = pl.program_id(0)
    first = (k % steps_per_seq) == 0
    tiles_per_seq = 2 * steps_per_seq

    def project(x_ref, mod_ref, tile_index, st):
        rows = pl.ds(pl.multiple_of((tile_index % tiles_per_seq) * tile, tile), tile)
        rope = tuple(r.at[rows, :] for r in (ra_ref, rb_ref, rc_ref))
        return _project_pieces(x_ref, mod_ref, rope, st, gn_ref, bg_ref, w_ref)

    @pl.when(k == 0)
    def _prologue():
        copies = [pltpu.make_async_copy(src, dst, w_sems.at[i]) for i, (src, dst) in enumerate(
            ((w_hbm, w_ref), (wmo_hbm, wmo_ref), (wao_hbm, wao_ref), (wo_hbm, wo_ref)))]
        for copy in copies:
            copy.start()
        for copy in copies:
            copy.wait()
        lane = lax.broadcasted_iota(jnp.int32, (tile, LANES), 1)
        ones_col = jnp.where(lane == 0, 1.0, 0.0).astype(BF16)
        for st in (set_a, set_b):
            for hd in range(M_HEADS):
                st["vaug"][hd, :, LANES:2 * LANES] = ones_col
        for piece in project(x0_ref, mod0_ref, 0, set_a):
            piece()

    @pl.when(first)
    def _init():
        c_s[...] = jnp.zeros_like(c_s)
        m_s[...] = jnp.zeros_like(m_s)
        kbuf_s[...] = jnp.zeros_like(kbuf_s)
        vbuf_s[...] = jnp.zeros_like(vbuf_s)

    state = (sinks_ref, gmn_ref, kbuf_s, vbuf_s, c_s, m_s)
    out_w = (wmo_ref, wao_ref, wo_ref, gf_ref, y_ref)
    _interleave(_consume_pieces(set_a, first, *state, yma_s, yaa_s, tile),
                project(xa_ref, moda_ref, 2 * k + 1, set_b))
    _interleave(_consume_pieces(set_b, False, *state, ymb_s, yab_s, tile),
                _output_pieces(set_a, xres_ref, slice(0, tile), moda_ref, yma_s, yaa_s, ua_s, *out_w)
                + project(xb_ref, modb_ref, jnp.minimum(2 * k + 2, n_tiles - 1), set_a))
    for piece in _output_pieces(set_b, xres_ref, slice(tile, 2 * tile), moda_ref, ymb_s, yab_s, ub_s, *out_w):
        piece()

    @pl.when((k % steps_per_seq) == steps_per_seq - 1)
    def _final():
        for hd in range(M_HEADS):
            cf = c_s[hd]
            c_out_ref[0, hd] = cf[:, 0:M_DV]
            n_out_ref[0, hd:hd + 1, :] = jnp.transpose(cf[:, M_DV:2 * M_DV])[0:1, :]
        m_out_ref[0] = m_s[...]
        k_t = jnp.transpose(set_b["kf"][tile - WINDOW:tile, :])
        v_t = jnp.transpose(set_b["vf"][tile - WINDOW:tile, :])
        for kv in range(A_KV_HEADS):
            kk_ref[0, kv] = k_t[kv * A_HEAD_DIM:(kv + 1) * A_HEAD_DIM, :]
            vk_ref[0, kv] = v_t[kv * A_HEAD_DIM:(kv + 1) * A_HEAD_DIM, :]


def _rope_tables(first, count):
    half = ROT_DIM // 2
    dim = np.arange(LANES) % A_HEAD_DIM
    inv = ROPE_THETA ** (-(2.0 * (dim % half)) / ROT_DIM)
    ang = (first + np.arange(count, dtype=np.float64))[:, None] * inv[None, :]
    cos, sin = np.cos(ang), np.sin(ang)
    ra = np.where(dim < ROT_DIM, cos, 1.0)
    rb = np.where((dim >= half) & (dim < ROT_DIM), sin, 0.0)
    rc = np.where(dim < half, -sin, 0.0)
    return tuple(jnp.asarray(t, F32) for t in (ra, rb, rc))


def _full(shape):
    return pl.BlockSpec(shape, lambda *_: (0,) * len(shape))


def _prompt_layer(x, mod, sinks, gn, gf, gmn, bias_g, wp, wmo, wao, wo):
    bsz, seq, d = x.shape
    tile = PROMPT_TILE
    assert seq % (2 * tile) == 0 and tile % (2 * M_CHUNK) == 0 and d == D_MODEL
    tiles_per_seq = seq // tile
    steps_per_seq = tiles_per_seq // 2
    n_tiles = bsz * tiles_per_seq
    n_steps = n_tiles // 2
    ra, rb, rc = _rope_tables(0, seq)
    xt = x.reshape(n_tiles, tile, d)
    xp = x.reshape(n_steps, 2 * tile, d)
    mod3 = mod.reshape(bsz, 1, 3 * d)

    tile_0 = lambda k: 0
    tile_a = lambda k: 2 * k + 1
    tile_b = lambda k: jnp.minimum(2 * k + 2, n_tiles - 1)
    x_spec = lambda f: pl.BlockSpec((1, tile, d), lambda k: (f(k), 0, 0))
    mod_spec = lambda f: pl.BlockSpec((1, 1, 3 * d), lambda k: (f(k) // tiles_per_seq, 0, 0))
    in_specs = [
        pl.BlockSpec(memory_space=pltpu.SMEM),
        x_spec(tile_0), x_spec(tile_a), x_spec(tile_b),
        pl.BlockSpec((1, 2 * tile, d), lambda k: (k, 0, 0)),
        mod_spec(tile_0), mod_spec(tile_a), mod_spec(tile_b),
        _full((1, d)), _full((1, d)), _full((1, M_WIDTH)), _full((1, 2 * LANES)),
        pl.BlockSpec(memory_space=pl.ANY), pl.BlockSpec(memory_space=pl.ANY),
        pl.BlockSpec(memory_space=pl.ANY), pl.BlockSpec(memory_space=pl.ANY),
        _full((seq, LANES)), _full((seq, LANES)), _full((seq, LANES)),
    ]
    seq_of = lambda k: k // steps_per_seq
    out_specs = [
        pl.BlockSpec((1, 2 * tile, d), lambda k: (k, 0, 0)),
        pl.BlockSpec((1, M_HEADS, M_DK, M_DV), lambda k: (seq_of(k), 0, 0, 0)),
        pl.BlockSpec((1, M_HEADS, M_DK), lambda k: (seq_of(k), 0, 0)),
        pl.BlockSpec((1, 1, LANES), lambda k: (seq_of(k), 0, 0)),
        pl.BlockSpec((1, A_KV_HEADS, A_HEAD_DIM, WINDOW), lambda k: (seq_of(k), 0, 0, 0)),
        pl.BlockSpec((1, A_KV_HEADS, A_HEAD_DIM, WINDOW), lambda k: (seq_of(k), 0, 0, 0)),
    ]
    out_shape = [
        jax.ShapeDtypeStruct((n_steps, 2 * tile, d), F32),
        jax.ShapeDtypeStruct((bsz, M_HEADS, M_DK, M_DV), F32),
        jax.ShapeDtypeStruct((bsz, M_HEADS, M_DK), F32),
        jax.ShapeDtypeStruct((bsz, 1, LANES), F32),
        jax.ShapeDtypeStruct((bsz, A_KV_HEADS, A_HEAD_DIM, WINDOW), F32),
        jax.ShapeDtypeStruct((bsz, A_KV_HEADS, A_HEAD_DIM, WINDOW), F32),
    ]
    scratch = _set_shapes(tile) + _set_shapes(tile) + [
        pltpu.VMEM((2 * A_KV_HEADS, 2 * WINDOW, LANES), BF16),
        pltpu.VMEM((2 * A_KV_HEADS, 2 * WINDOW, LANES), BF16),
        pltpu.VMEM((M_HEADS, M_DK, 2 * LANES), F32),
        pltpu.VMEM((1, LANES), F32),
        pltpu.VMEM((tile, M_WIDTH), BF16),
        pltpu.VMEM((tile, A_WIDTH), BF16),
        pltpu.VMEM((tile, D_MODEL), BF16),
        pltpu.VMEM((tile, M_WIDTH), BF16),
        pltpu.VMEM((tile, A_WIDTH), BF16),
        pltpu.VMEM((tile, D_MODEL), BF16),
        pltpu.VMEM((d, N_PACK), BF16),
        pltpu.VMEM((M_WIDTH, d), BF16),
        pltpu.VMEM((A_WIDTH, d), BF16),
        pltpu.VMEM((d, d), BF16),
        pltpu.SemaphoreType.DMA((4,)),
    ]
    outs = pl.pallas_call(
        functools.partial(_prompt_kernel, tile=tile, steps_per_seq=steps_per_seq, n_tiles=n_tiles),
        grid=(n_steps,),
        in_specs=in_specs,
        out_specs=out_specs,
        out_shape=out_shape,
        scratch_shapes=scratch,
        compiler_params=pltpu.CompilerParams(dimension_semantics=("arbitrary",), vmem_limit_bytes=VMEM_LIMIT),
        name="prompt_layer",
    )(sinks, xt, xt, xt, xp, mod3, mod3, mod3, gn, gf, gmn, bias_g, wp, wmo, wao, wo,
      ra, rb, rc)
    return (outs[0].reshape(bsz, seq, d),) + tuple(outs[1:])


def _sample_proj_kernel(x_hbm, mod_ref, gn_ref, w_ref, z_ref, hb_s, x_s, x_sems):
    bsz, steps, _ = x_hbm.shape

    @pl.when(pl.program_id(0) == 0)
    def _norm():
        copies = [pltpu.make_async_copy(x_hbm.at[:, t, :], x_s.at[t], x_sems.at[t]) for t in range(steps)]
        for copy in copies:
            copy.start()
        mod = mod_ref[...]
        for t in range(steps):
            copies[t].wait()
            h = _rms(x_s[t]) * gn_ref[...]
            hb_s[t * bsz:(t + 1) * bsz, :] = (h * (1.0 + mod[:, D_MODEL:2 * D_MODEL]) + mod[:, 0:D_MODEL]).astype(BF16)

    z = _dot(hb_s[...], w_ref[...])
    for t in range(steps):
        z_ref[t] = z[t * bsz:(t + 1) * bsz, :]


def _sample_proj(x3, mod_s, gn, wp):
    bsz, steps, d = x3.shape
    tn = 1024
    return pl.pallas_call(
        _sample_proj_kernel,
        grid=(N_PACK // tn,),
        in_specs=[
            pl.BlockSpec(memory_space=pl.ANY),
            pl.BlockSpec((bsz, 2 * d), lambda j: (0, 0)),
            _full((1, d)),
            pl.BlockSpec((d, tn), lambda j: (0, j)),
        ],
        out_specs=pl.BlockSpec((steps, bsz, tn), lambda j: (0, 0, j)),
        out_shape=jax.ShapeDtypeStruct((steps, bsz, N_PACK), F32),
        scratch_shapes=[pltpu.VMEM((steps * bsz, d), BF16), pltpu.VMEM((steps, bsz, d), F32),
                        pltpu.SemaphoreType.DMA((steps,))],
        compiler_params=pltpu.CompilerParams(dimension_semantics=("arbitrary",)),
        name="sample_proj",
    )(x3, mod_s, gn, wp)


def _sample_state_kernel(sinks_ref, z_ref, c_hbm, n_ref, m_ref, ck_ref, cv_ref, rope_ref, bg_ref, gmn_ref,
                         cn_ref, nn_ref, mn_ref, ym_ref, ya_ref, ko_ref, vo_ref, c_ring, c_sems, *, group, steps):
    T, G = steps, group
    toks = range(T)

    i = pl.program_id(0)
    n_steps = pl.num_programs(0)

    def c_copy(step):
        slot = step % STATE_RING
        return pltpu.make_async_copy(c_hbm.at[pl.ds(step * G, G)], c_ring.at[slot], c_sems.at[slot])

    @pl.when(i == 0)
    def _prefill():
        for step in range(STATE_RING - 1):
            @pl.when(step < n_steps)
            def _start():
                c_copy(step).start()

    @pl.when(i + STATE_RING - 1 < n_steps)
    def _ahead():
        c_copy(i + STATE_RING - 1).start()

    c_copy(i).wait()
    c_ref = c_ring.at[i % STATE_RING]

    def z(t, lo, n):
        return z_ref[t, :, lo:lo + n]

    def rope_t(t, blk):
        return _rope(blk, rope_ref[0, t:t + 1, :], rope_ref[1, t:t + 1, :], rope_ref[2, t:t + 1, :])

    lane = lax.broadcasted_iota(jnp.int32, (G, LANES), 1)
    lo_half = lane < A_HEAD_DIM
    gmn = gmn_ref[...]

    q = [z(t, C_MQ, M_WIDTH) * (M_DK ** -0.5) for t in toks]
    k = [z(t, C_MK, M_WIDTH) for t in toks]
    v = [z(t, C_MV, M_WIDTH) for t in toks]
    og = [_sigmoid(z(t, C_MO, M_WIDTH)) * _silu(z(t, C_MZ, M_WIDTH)) for t in toks]
    qa = [[rope_t(t, z(t, C_AQ + j * LANES, LANES)) * (A_HEAD_DIM ** -0.5) for j in range(A_WIDTH // LANES)]
          for t in toks]
    kn = [rope_t(t, z(t, C_AK, A_KV_WIDTH)) for t in toks]
    vn = [z(t, C_AV, A_KV_WIDTH) for t in toks]
    sz = [_silu(z(t, C_AZ, A_WIDTH)) for t in toks]
    gt = [z(t, C_GI, 2 * LANES) + bg_ref[...] for t in toks]
    ig = [gt[t][:, 0:LANES] for t in toks]
    lf = [_log_sigmoid(gt[t][:, LANES:2 * LANES]) for t in toks]


    m_all = m_ref[...]
    rep = lambda a, h: jnp.broadcast_to(a[:, h:h + 1], (G, LANES))
    gates = []
    for hd in range(M_HEADS):
        ig_h = [rep(ig[t], hd) for t in toks]
        lf_h = [rep(lf[t], hd) for t in toks]
        m_old = rep(m_all, hd)
        bsum = [lf_h[0]]
        for t in range(1, T):
            bsum.append(bsum[-1] + lf_h[t])
        b_last = bsum[-1]
        d_end = [b_last - bsum[s] + ig_h[s] for s in toks]
        m_new = b_last + m_old
        for s in toks:
            m_new = jnp.maximum(m_new, d_end[s])
        mn_ref[:, hd:hd + 1] = m_new[:, 0:1]
        gd = dict(w_end=[jnp.exp(d_end[s] - m_new) for s in toks], decay=jnp.exp(b_last + m_old - m_new),
                  e=[], w_inter=[], emh=[])
        for t in toks:
            inter = bsum[t] + m_old
            dm = [bsum[t] - bsum[s] + ig_h[s] for s in range(t + 1)]
            m_hat = inter
            for s in range(t + 1):
                m_hat = jnp.maximum(m_hat, dm[s])
            gd["e"].append([jnp.exp(dm[s] - m_hat) for s in range(t + 1)])
            gd["w_inter"].append(jnp.exp(inter - m_hat))
            gd["emh"].append(jnp.exp(-m_hat))
        gates.append(gd)

    heads = [slice(hd * M_DK, (hd + 1) * M_DK) for hd in range(M_HEADS)]
    n_old = [n_ref[:, hd, :] for hd in range(M_HEADS)]
    qk = {(hd, t, s): jnp.sum(q[t][:, heads[hd]] * k[s][:, heads[hd]], axis=-1, keepdims=True)
          for hd in range(M_HEADS) for t in toks for s in range(t + 1)}
    qn = {(hd, t): jnp.sum(q[t][:, heads[hd]] * n_old[hd], axis=-1, keepdims=True)
          for hd in range(M_HEADS) for t in toks}

    lo_f = jnp.where(lo_half, 1.0, 0.0)
    hi_f = 1.0 - lo_f
    dup = lambda a, kv: (jnp.where(lo_half, a, pltpu.roll(a, A_HEAD_DIM, 1)) if kv == 0 else
                         jnp.where(lo_half, pltpu.roll(a, A_HEAD_DIM, 1), a))
    kn_x = [[dup(kn[s], kv) for s in toks] for kv in range(A_KV_HEADS)]
    vn_x = [[dup(vn[s], kv) for s in toks] for kv in range(A_KV_HEADS)]
    s_new = {}
    for kv in range(A_KV_HEADS):
        for blk in range(2):
            for t in toks:
                for s in range(t + 1):
                    prod = qa[t][2 * kv + blk] * kn_x[kv][s]
                    s_new[kv, 0, blk, t, s] = jnp.sum(prod * lo_f, axis=-1, keepdims=True)
                    s_new[kv, 1, blk, t, s] = jnp.sum(prod * hi_f, axis=-1, keepdims=True)

    rows_tg = lax.broadcasted_iota(jnp.int32, (T * G, LANES), 0) % G
    own_rows = [rows_tg == b for b in range(G)]
    rows64 = lax.broadcasted_iota(jnp.int32, (2 * T * G, LANES), 0) % G

    qc = []
    for hd in range(M_HEADS):
        hs = heads[hd]
        gd = gates[hd]
        q32 = jnp.concatenate([q[t][:, hs] for t in toks], axis=0)
        kw = [k[s][:, hs] * gd["w_end"][s] for s in toks]
        kw_t = jnp.transpose(jnp.concatenate(kw, axis=0)).astype(BF16)
        v32 = jnp.concatenate([v[s][:, hs] for s in toks], axis=0)
        acc = None
        for b in range(G):
            c_old = c_ref[b, hd]
            part = _dot(jnp.where(own_rows[b], q32, 0.0).astype(BF16), c_old.astype(BF16))
            acc = part if acc is None else acc + part
            upd = _dot(kw_t, jnp.where(own_rows[b], v32, 0.0).astype(BF16))
            cn_ref[b, hd] = gd["decay"][b:b + 1, 0:1] * c_old + upd
        qc.append(acc)
        nn_ref[:, hd, :] = gd["decay"] * n_old[hd] + (kw[0] + kw[1] + kw[2] + kw[3])

    zeros_kt = jnp.zeros((A_HEAD_DIM, WINDOW), BF16)
    sc = {}
    for kv in range(A_KV_HEADS):
        l64 = jnp.concatenate([qa[t][2 * kv + blk] for blk in range(2) for t in toks], axis=0)
        own64 = [jnp.where(rows64 == b, l64, 0.0).astype(BF16) for b in range(G)]
        kt = [ck_ref[b, kv].astype(BF16) for b in range(G)]
        for par in range(2):
            acc = None
            for b in range(G):
                rhs = jnp.concatenate([kt[b], zeros_kt] if par == 0 else [zeros_kt, kt[b]], axis=0)
                part = _dot(own64[b], rhs)
                acc = part if acc is None else acc + part
            sc[kv, par] = acc

    hh = {}
    for hd in range(M_HEADS):
        gd = gates[hd]
        for t in toks:
            sv, ssum = None, None
            for s in range(t + 1):
                s_ts = qk[hd, t, s] * gd["e"][t][s]
                sv = s_ts * v[s][:, heads[hd]] if sv is None else sv + s_ts * v[s][:, heads[hd]]
                ssum = s_ts if ssum is None else ssum + s_ts
            wi = gd["w_inter"][t]
            num = sv + wi * qc[hd][t * G:(t + 1) * G, :]
            den = ssum + wi * qn[hd, t]
            hh[hd, t] = num / jnp.maximum(jnp.abs(den), gd["emh"][t])
    ms = {key: jnp.mean(val * val, axis=-1, keepdims=True) for key, val in hh.items()}
    for (hd, t), val in hh.items():
        hs = heads[hd]
        ym_ref[t, :, hs] = val * lax.rsqrt(ms[hd, t] + NORM_EPS) * gmn[:, hs] * og[t][:, hs]

    items = [(kv, par, blk, t) for kv in range(A_KV_HEADS) for par in range(2) for blk in range(2) for t in toks]
    s_c = {it: jnp.where(lane > it[3], sc[it[0], it[1]][(it[2] * T + it[3]) * G:(it[2] * T + it[3] + 1) * G, :], -jnp.inf)
           for it in items}
    mx_c = {it: jnp.max(s_c[it], axis=-1, keepdims=True) for it in items}
    p_c, p_n, sink_t = {}, {}, {}
    for it in items:
        kv, par, blk, t = it
        sink = sinks_ref[A_GROUP * kv + 2 * blk + par]
        mx = jnp.maximum(mx_c[it], sink)
        for s in range(t + 1):
            mx = jnp.maximum(mx, s_new[kv, par, blk, t, s])
        p_c[it] = jnp.exp(s_c[it] - mx)
        p_n[it] = [jnp.exp(s_new[kv, par, blk, t, s] - mx) for s in range(t + 1)]
        sink_t[it] = jnp.exp(sink - mx)
    sum_c = {it: jnp.sum(p_c[it], axis=-1, keepdims=True) for it in items}
    fresh = {}
    probs = {(kv, par): [] for kv in range(A_KV_HEADS) for par in range(2)}
    for it in items:
        kv, par, blk, t = it
        den = sum_c[it] + sink_t[it]
        for s in range(t + 1):
            den = den + p_n[it][s]
        r = 1.0 / den
        probs[kv, par].append(p_c[it] * r)
        acc = None
        for s in range(t + 1):
            term = (p_n[it][s] * r) * vn_x[kv][s]
            acc = term if acc is None else acc + term
        fresh[it] = acc

    for kv in range(A_KV_HEADS):
        vt = [cv_ref[b, kv].astype(BF16) for b in range(G)]
        o = []
        for par in range(2):
            pstack = jnp.concatenate(probs[kv, par], axis=0)
            acc = None
            for b in range(G):
                rhs = jnp.concatenate([vt[b], zeros_kt] if par == 0 else [zeros_kt, vt[b]], axis=0)
                part = _dot_nt(jnp.where(rows64 == b, pstack, 0.0).astype(BF16), rhs)
                acc = part if acc is None else acc + part
            o.append(acc)
        for blk in range(2):
            cs = slice((2 * kv + blk) * LANES, (2 * kv + blk + 1) * LANES)
            for t in toks:
                rs = slice((blk * T + t) * G, (blk * T + t + 1) * G)
                new_v = jnp.where(lo_half, fresh[kv, 0, blk, t], fresh[kv, 1, blk, t])
                ya_ref[t, :, cs] = (o[0][rs, :] + o[1][rs, :] + new_v) * sz[t][:, cs]

    n_rows = T * G
    src = lax.broadcasted_iota(jnp.int32, (n_rows, n_rows), 1)
    dst = lax.broadcasted_iota(jnp.int32, (n_rows, n_rows), 0)
    perm = jnp.where(src == (dst % T) * G + dst // T, 1.0, 0.0).astype(BF16)
    lane_d = lax.broadcasted_iota(jnp.int32, (A_HEAD_DIM, WINDOW), 1)
    for new, cache_ref, out_ref in ((kn, ck_ref, ko_ref), (vn, cv_ref, vo_ref)):
        stack = _dot_exact01(perm, jnp.concatenate(new, axis=0))
        stack = jnp.concatenate([stack, jnp.zeros((LANES - n_rows, LANES), F32)], axis=0)
        new_t = jnp.transpose(stack)
        for b in range(G):
            placed = pltpu.roll(new_t, (WINDOW - T - T * b) % LANES, 1)
            for kv in range(A_KV_HEADS):
                slid = pltpu.roll(cache_ref[b, kv], WINDOW - T, 1)
                out_ref[b, kv] = jnp.where(lane_d >= WINDOW - T, placed[kv * A_HEAD_DIM:(kv + 1) * A_HEAD_DIM, :], slid)


def _sample_state(zs, state_c, state_n, state_m, ck_t, cv_t, rope, bias_g, gmn, sinks):
    steps, bsz, _ = zs.shape
    group = SAMPLE_GROUP
    assert bsz % group == 0
    in_specs = [
        pl.BlockSpec(memory_space=pltpu.SMEM),
        pl.BlockSpec((steps, group, N_PACK), lambda i: (0, i, 0)),
        pl.BlockSpec(memory_space=pl.ANY),
        pl.BlockSpec((group, M_HEADS, M_DK), lambda i: (i, 0, 0)),
        pl.BlockSpec((group, M_HEADS), lambda i: (i, 0)),
        pl.BlockSpec((group, A_KV_HEADS, A_HEAD_DIM, WINDOW), lambda i: (i, 0, 0, 0)),
        pl.BlockSpec((group, A_KV_HEADS, A_HEAD_DIM, WINDOW), lambda i: (i, 0, 0, 0)),
        _full((3, steps, LANES)), _full((1, 2 * LANES)), _full((1, M_WIDTH)),
    ]
    out_specs = [
        pl.BlockSpec((group, M_HEADS, M_DK, M_DV), lambda i: (i, 0, 0, 0)),
        pl.BlockSpec((group, M_HEADS, M_DK), lambda i: (i, 0, 0)),
        pl.BlockSpec((group, M_HEADS), lambda i: (i, 0)),
        pl.BlockSpec((steps, group, M_WIDTH), lambda i: (0, i, 0)),
        pl.BlockSpec((steps, group, A_WIDTH), lambda i: (0, i, 0)),
        pl.BlockSpec((group, A_KV_HEADS, A_HEAD_DIM, WINDOW), lambda i: (i, 0, 0, 0)),
        pl.BlockSpec((group, A_KV_HEADS, A_HEAD_DIM, WINDOW), lambda i: (i, 0, 0, 0)),
    ]
    out_shape = [
        jax.ShapeDtypeStruct((bsz, M_HEADS, M_DK, M_DV), F32),
        jax.ShapeDtypeStruct((bsz, M_HEADS, M_DK), F32),
        jax.ShapeDtypeStruct((bsz, M_HEADS), F32),
        jax.ShapeDtypeStruct((steps, bsz, M_WIDTH), F32),
        jax.ShapeDtypeStruct((steps, bsz, A_WIDTH), F32),
        jax.ShapeDtypeStruct((bsz, A_KV_HEADS, A_HEAD_DIM, WINDOW), F32),
        jax.ShapeDtypeStruct((bsz, A_KV_HEADS, A_HEAD_DIM, WINDOW), F32),
    ]
    return pl.pallas_call(
        functools.partial(_sample_state_kernel, group=group, steps=steps),
        grid=(bsz // group,),
        in_specs=in_specs,
        out_specs=out_specs,
        out_shape=out_shape,
        scratch_shapes=[pltpu.VMEM((STATE_RING, group, M_HEADS, M_DK, M_DV), F32),
                        pltpu.SemaphoreType.DMA((STATE_RING,))],
        compiler_params=pltpu.CompilerParams(dimension_semantics=("arbitrary",), vmem_limit_bytes=VMEM_LIMIT),
        name="sample_state",
    )(sinks, zs, state_c, state_n, state_m, ck_t, cv_t, rope, bias_g, gmn)


def _sample_out_kernel(x_ref, gate_ref, z_ref, ym_ref, ya_ref, wmo_ref, wao_ref, wo_ref, gf_ref, y_ref):
    bsz, steps, _ = x_ref.shape
    stack = lambda ref, cs: jnp.concatenate([ref[t, :, cs] for t in range(steps)], axis=0)
    full = slice(None)
    x = jnp.concatenate([x_ref[:, t, :] for t in range(steps)], axis=0)
    gate = jnp.concatenate([gate_ref[...]] * steps, axis=0)
    y = _out_stage(x, gate, _sigmoid(stack(z_ref, slice(0, D_MODEL))), _sigmoid(stack(z_ref, slice(D_MODEL, 2 * D_MODEL))),
                   stack(ym_ref, full).astype(BF16), stack(ya_ref, full).astype(BF16), wmo_ref, wao_ref, wo_ref, gf_ref[...])
    for t in range(steps):
        y_ref[:, t, :] = y[t * bsz:(t + 1) * bsz, :]


def _sample_out(x3, mod_s, zs, ym, ya, wmo, wao, wo, gf):
    bsz, steps, d = x3.shape
    return pl.pallas_call(
        _sample_out_kernel,
        grid=(1,),
        in_specs=[
            _full((bsz, steps, d)),
            pl.BlockSpec((bsz, d), lambda i: (0, 2)),
            pl.BlockSpec((steps, bsz, 2 * d), lambda i: (0, 0, 0)),
            _full((steps, bsz, M_WIDTH)), _full((steps, bsz, A_WIDTH)),
            _full((M_WIDTH, d)), _full((A_WIDTH, d)), _full((d, d)), _full((1, d)),
        ],
        out_specs=_full((bsz, steps, d)),
        out_shape=jax.ShapeDtypeStruct((bsz, steps, d), F32),
        compiler_params=pltpu.CompilerParams(dimension_semantics=("arbitrary",), vmem_limit_bytes=VMEM_LIMIT),
        name="sample_out",
    )(x3, mod_s, zs, ym, ya, wmo, wao, wo, gf)


PACK_BLOCK = 512


def _pack_sources():
    names = ("mq", "mk", "mv", "mi", "mf", "mo", "mz", "aq", "ak", "av", "az", "gm", "ga")
    start, pos = {}, 0
    for name, size in zip(names, IN_SIZES):
        start[name] = pos
        pos += size
    assert start["av"] == start["ak"] + A_KV_WIDTH and start["mf"] == start["mi"] + M_HEADS
    order = (("gm", C_GM, D_MODEL), ("ga", C_GA, D_MODEL), ("mq", C_MQ, M_WIDTH), ("mk", C_MK, M_WIDTH),
             ("mv", C_MV, M_WIDTH), ("mo", C_MO, M_WIDTH), ("mz", C_MZ, M_WIDTH), ("aq", C_AQ, A_WIDTH),
             ("az", C_AZ, A_WIDTH))
    src = []
    for name, col, width in order:
        assert col == len(src) * PACK_BLOCK and width % PACK_BLOCK == 0
        src += [start[name] + i * PACK_BLOCK for i in range(width // PACK_BLOCK)]
    assert C_AK == len(src) * PACK_BLOCK and C_AV == C_AK + A_KV_WIDTH and C_GI == C_AV + A_KV_WIDTH
    assert C_GF == C_GI + LANES and N_PACK == C_AK + PACK_BLOCK and start["ak"] + PACK_BLOCK <= pos
    return src + [start["ak"]], start["mi"]


def _pack_kernel(src_ref, wt_ref, gate_ref, o_ref):
    j = pl.program_id(0)
    last = pl.num_programs(0) - 1

    @pl.when(j < last)
    def _block():
        o_ref[...] = jnp.transpose(wt_ref[...]).astype(BF16)

    @pl.when(j == last)
    def _kv_and_gates():
        kv = 2 * A_KV_WIDTH
        o_ref[:, 0:kv] = jnp.transpose(wt_ref[0:kv, :]).astype(BF16)
        g = gate_ref[...]
        pad = jnp.zeros((LANES - M_HEADS, g.shape[1]), F32)
        rows = jnp.concatenate([g[0:M_HEADS], pad, g[M_HEADS:2 * M_HEADS], pad], axis=0)
        o_ref[:, kv:kv + 2 * LANES] = jnp.transpose(rows).astype(BF16)


def _pack_w_in(w_in):
    d = w_in.shape[0]
    src, gate_row = _pack_sources()
    wt = jnp.transpose(w_in)
    return pl.pallas_call(
        _pack_kernel,
        grid_spec=pltpu.PrefetchScalarGridSpec(
            num_scalar_prefetch=1, grid=(N_PACK // PACK_BLOCK,),
            in_specs=[pl.BlockSpec((pl.Element(PACK_BLOCK), pl.Element(d)),
                                   lambda j, src_ref: (pl.multiple_of(src_ref[j], 8), 0)),
                      pl.BlockSpec((pl.Element(2 * M_HEADS), pl.Element(d)), lambda j, src_ref: (gate_row, 0))],
            out_specs=pl.BlockSpec((d, PACK_BLOCK), lambda j, src_ref: (0, j))),
        out_shape=jax.ShapeDtypeStruct((d, N_PACK), BF16),
        compiler_params=pltpu.CompilerParams(dimension_semantics=("arbitrary",)),
        name="pack_w_in",
    )(jnp.asarray(src, jnp.int32), wt, wt)


def kernel(x_prompt, x_sample, state_C, state_n, state_m, cache_k, cache_v, c_prompt, c_sample, w_ada, b_ada, g_norm, w_in, b_igate, b_fgate, g_mnorm, sinks, w_m_out, w_a_out, w_out, g_final):
    assert w_in.shape[0] == 1, "single-layer step"
    bsz_s, steps, d = x_sample.shape
    assert cache_k.shape[2] == WINDOW

    mod_p, mod_s = _adaln(c_prompt, c_sample, w_ada[0], b_ada[0])
    wp = _pack_w_in(w_in[0])
    wmo = w_m_out[0].astype(BF16)
    wao = w_a_out[0].astype(BF16)
    wo = w_out[0].astype(BF16)
    gn = g_norm[0].reshape(1, d)
    gf = g_final.reshape(1, d)
    gmn = g_mnorm[0].reshape(1, M_WIDTH)
    zpad = jnp.zeros((LANES - M_HEADS,), F32)
    bias_g = jnp.concatenate([b_igate[0], zpad, b_fgate[0], zpad]).reshape(1, 2 * LANES)
    sk = sinks[0]

    y_p, c_p, n_p, m_p, k_p, v_p = _prompt_layer(x_prompt, mod_p, sk, gn, gf, gmn, bias_g, wp, wmo, wao, wo)
    m_p = m_p[:, 0, 0:M_HEADS]

    zs = _sample_proj(x_sample, mod_s, gn, wp)
    rope = jnp.stack(_rope_tables(PAST_LEN, steps))
    dims_keys = lambda c: jnp.transpose(c[0], (0, 2, 3, 1))
    c_s, n_s, m_s, ym, ya, k_t, v_t = _sample_state(zs, state_C[0], state_n[0], state_m[0], dims_keys(cache_k),
                                                    dims_keys(cache_v), rope, bias_g, gmn, sk)
    y_s = _sample_out(x_sample, mod_s, zs, ym, ya, wmo, wao, wo, gf)
    keys_dims = lambda c: jnp.transpose(c, (0, 3, 1, 2))[None]

    return (y_p, y_s, c_p[None], n_p[None], m_p[None], keys_dims(k_p), keys_dims(v_p),
            c_s[None], n_s[None], m_s[None], keys_dims(k_t), keys_dims(v_t))
```
